```python
import math, functools
import jax, jax.numpy as jnp
from jax import lax
import numpy as np

D_MODEL = 2048
BATCH = 2
SEQ = 4096
DEPTH = 2
DEC_BATCH = 32
DEC_SEQ = 16
PAST_LEN = 4096

CHUNK = 64
N_A = DEPTH // 2
N_B = DEPTH - N_A
HA = 16
DK = 128
DV = 128
DQK = HA * DK
DVW = HA * DV
CONV_W = 4
CONV_CH = 2 * DQK + DVW
N_IN_A = 2 * DQK + 2 * DVW + 2 * HA
N_Q = 32
N_KV = 4
HD = 64
GROUP = N_Q // N_KV
WINDOW = 128
W_CH = WINDOW // CHUNK
ROPE_THETA = 10000.0
D_FF = 4 * D_MODEL
EPS = 1e-6

kernel_name = 'yoco_gdn_swa_sink_stream_step'


def rmsnorm(x, w):
    xf = x.astype(jnp.float32)
    y = xf * lax.rsqrt(jnp.mean(xf * xf, axis=-1, keepdims=True) + EPS)
    return (y * w.astype(jnp.float32)).astype(x.dtype)


def l2norm(x):
    return x * lax.rsqrt(jnp.sum(x * x, axis=-1, keepdims=True) + EPS)


def rope(x, pos):
    half = HD // 2
    inv = 1.0 / (ROPE_THETA ** (jnp.arange(half, dtype=jnp.float32) / half))
    ang = pos.astype(jnp.float32)[:, None] * inv[None, :]
    cos = jnp.cos(ang)[None, :, None, :]
    sin = jnp.sin(ang)[None, :, None, :]
    xf = x.astype(jnp.float32)
    x1, x2 = xf[..., :half], xf[..., half:]
    return jnp.concatenate([x1 * cos - x2 * sin, x2 * cos + x1 * sin], axis=-1).astype(x.dtype)


def gated_delta_rule(q, k, v, g, beta, s0):
    b, l, h, _ = q.shape
    c = min(CHUNK, l)
    n = l // c

    def blocks(t):
        t = t.reshape((b, n, c, h) + t.shape[3:])
        return jnp.moveaxis(t, 3, 1)

    q = blocks(q * (DK ** -0.5))
    k = blocks(k)
    v = blocks(v)
    g = blocks(g)
    beta = blocks(beta)
    gc = jnp.cumsum(g, axis=-1)
    incl = jnp.tril(jnp.ones((c, c), dtype=bool))
    strict = jnp.tril(jnp.ones((c, c), dtype=bool), -1)
    decay = jnp.exp(jnp.where(incl, gc[..., :, None] - gc[..., None, :], -jnp.inf))
    kb = k * beta[..., None]
    m = jnp.where(strict, jnp.einsum('bhncd,bhnsd->bhncs', kb, k) * decay, 0.0)
    a = m + jnp.eye(c, dtype=m.dtype)
    rhs = jnp.concatenate([v * beta[..., None], kb * jnp.exp(gc)[..., None]], axis=-1)
    sol = lax.linalg.triangular_solve(a, rhs, left_side=True, lower=True, unit_diagonal=True)
    u, w = sol[..., :DV], sol[..., DV:]
    qk = jnp.einsum('bhncd,bhnsd->bhncs', q, k) * decay

    def step(s, xs):
        qi, ki, ui, wi, gi, qki = xs
        v_new = ui - jnp.einsum('bhck,bhkv->bhcv', wi, s)
        o = (jnp.einsum('bhck,bhkv->bhcv', qi * jnp.exp(gi)[..., None], s)
             + jnp.einsum('bhcs,bhsv->bhcv', qki, v_new))
        g_last = gi[..., -1]
        s = (s * jnp.exp(g_last)[..., None, None]
             + jnp.einsum('bhck,bhcv->bhkv', ki * jnp.exp(g_last[..., None] - gi)[..., None], v_new))
        return s, o

    xs = tuple(jnp.moveaxis(t, 2, 0) for t in (q, k, u, w, gc, qk))
    s_last, o = lax.scan(step, s0, xs)
    o = jnp.transpose(o, (1, 0, 3, 2, 4)).reshape(b, l, h, DV)
    return o, s_last


def gdn_mixer(h, conv_prev, s_prev, w_in, conv_w, a_log, dt_bias, o_norm, w_out):
    b, l, _ = h.shape
    proj = h @ w_in
    qkv = proj[..., :CONV_CH]
    z = proj[..., CONV_CH:CONV_CH + DVW]
    beta_in = proj[..., CONV_CH + DVW:CONV_CH + DVW + HA]
    a_in = proj[..., CONV_CH + DVW + HA:]
    xpad = jnp.concatenate([conv_prev.astype(h.dtype), qkv], axis=1)
    conv = lax.conv_general_dilated(
        xpad, conv_w[:, None, :].astype(h.dtype), window_strides=(1,), padding='VALID',
        dimension_numbers=('NWC', 'WIO', 'NWC'), feature_group_count=CONV_CH)
    conv = jax.nn.silu(conv.astype(jnp.float32))
    q = l2norm(conv[..., :DQK].reshape(b, l, HA, DK))
    k = l2norm(conv[..., DQK:2 * DQK].reshape(b, l, HA, DK))
    v = conv[..., 2 * DQK:].reshape(b, l, HA, DV)
    beta = jax.nn.sigmoid(beta_in.astype(jnp.float32))
    g = -jnp.exp(a_log.astype(jnp.float32)) * jax.nn.softplus(
        a_in.astype(jnp.float32) + dt_bias.astype(jnp.float32))
    o, s_new = gated_delta_rule(q, k, v, g, beta, s_prev.astype(jnp.float32))
    o = o * lax.rsqrt(jnp.mean(o * o, axis=-1, keepdims=True) + EPS) * o_norm.astype(jnp.float32)
    o = o * jax.nn.silu(z.astype(jnp.float32).reshape(b, l, HA, DV))
    out = o.reshape(b, l, DVW).astype(h.dtype) @ w_out
    return out, xpad[:, -(CONV_W - 1):], s_new.astype(h.dtype)


def shared_kv(x, kv_norm, w_kv, pos):
    b, l, _ = x.shape
    kv = rmsnorm(x, kv_norm) @ w_kv
    k = rope(kv[..., :N_KV * HD].reshape(b, l, N_KV, HD), pos)
    v = kv[..., N_KV * HD:].reshape(b, l, N_KV, HD)
    return k, v


def sink_softmax(s, sk):
    m = jnp.maximum(jnp.max(s, axis=-1, keepdims=True), sk)
    e = jnp.exp(s - m)
    return e / (jnp.sum(e, axis=-1, keepdims=True) + jnp.exp(sk - m))


def window_attention_prompt(q, k, v, sk):
    b, l = q.shape[:2]
    nc = l // CHUNK
    padw = ((0, 0), (W_CH * CHUNK, 0), (0, 0), (0, 0))
    kp = jnp.pad(k, padw).reshape(b, nc + W_CH, CHUNK, N_KV, HD)
    vp = jnp.pad(v, padw).reshape(b, nc + W_CH, CHUNK, N_KV, HD)
    kb = jnp.concatenate([kp[:, j:j + nc] for j in range(W_CH + 1)], axis=2)
    vb = jnp.concatenate([vp[:, j:j + nc] for j in range(W_CH + 1)], axis=2)
    qc = q.reshape(b, nc, CHUNK, N_KV, GROUP, HD)
    s = jnp.einsum('bnqhgd,bnkhd->bnhgqk', qc, kb, preferred_element_type=jnp.float32) * (HD ** -0.5)
    key_chunk = jnp.arange(nc)[:, None] + jnp.arange(W_CH + 1)[None, :] - W_CH
    valid = jnp.repeat(key_chunk >= 0, CHUNK, axis=1)
    s = jnp.where(valid[None, :, None, None, None, :], s, -jnp.inf)
    p = sink_softmax(s, sk)
    o = jnp.einsum('bnhgqk,bnkhd->bnqhgd', p, vb.astype(jnp.float32))
    return o.reshape(b, l, N_Q * HD)


def window_attention_sample(q, k, v, past_k, past_v, sk):
    b, l = q.shape[:2]
    kk = jnp.concatenate([past_k.astype(k.dtype), k], axis=1)
    vv = jnp.concatenate([past_v.astype(v.dtype), v], axis=1)
    qg = q.reshape(b, l, N_KV, GROUP, HD)
    s = jnp.einsum('bqhgd,bkhd->bhgqk', qg, kk, preferred_element_type=jnp.float32) * (HD ** -0.5)
    p = sink_softmax(s, sk)
    o = jnp.einsum('bhgqk,bkhd->bqhgd', p, vv.astype(jnp.float32))
    return o.reshape(b, l, N_Q * HD)


def swa_mixer(h, pos, k_sh, v_sh, past_k, past_v, w_q, sinks, w_o):
    b, l, _ = h.shape
    q = rope((h @ w_q).reshape(b, l, N_Q, HD), pos)
    sk = sinks.astype(jnp.float32).reshape(N_KV, GROUP, 1, 1)
    if past_k is None:
        o = window_attention_prompt(q, k_sh, v_sh, sk)
    else:
        o = window_attention_sample(q, k_sh, v_sh, past_k, past_v, sk)
    return o.astype(h.dtype) @ w_o


def trunk(x, pos, conv_prev, s_prev, past_k, past_v, attn_norm, mlp_norm, final_norm,
          a_w_in, a_conv_w, a_log, a_dt_bias, a_o_norm, a_w_out, kv_norm, w_kv,
          b_w_q, b_sinks, b_w_o, w_up, w_down):
    new_conv, new_s = [], []
    k_sh, v_sh = None, None
    for layer in range(DEPTH):
        h = rmsnorm(x, attn_norm[layer])
        if layer < N_A:
            o, c_new, s_new = gdn_mixer(h, conv_prev[layer], s_prev[layer], a_w_in[layer],
                                        a_conv_w[layer], a_log[layer], a_dt_bias[layer],
                                        a_o_norm[layer], a_w_out[layer])
            new_conv.append(c_new)
            new_s.append(s_new)
        else:
            if layer == N_A:
                k_sh, v_sh = shared_kv(x, kv_norm, w_kv, pos)
            j = layer - N_A
            o = swa_mixer(h, pos, k_sh, v_sh, past_k, past_v, b_w_q[j], b_sinks[j], b_w_o[j])
        x = x + o
        hm = rmsnorm(x, mlp_norm[layer])
        x = x + jnp.square(jax.nn.relu(hm @ w_up[layer])) @ w_down[layer]
    return rmsnorm(x, final_norm), jnp.stack(new_conv), jnp.stack(new_s), k_sh, v_sh


def setup_inputs(seed: int = 0) -> dict:
    key = jax.random.key(seed)
    ks = jax.random.split(key, 24)
    f32 = jnp.float32

    def nrm(k, shape, scale):
        return scale * jax.random.normal(k, shape, f32)

    dt = jnp.exp(jax.random.uniform(ks[10], (N_A, HA), f32, math.log(1e-3), math.log(1e-1)))
    dt_bias = dt + jnp.log(-jnp.expm1(-dt))
    a_log = jnp.log(jax.random.uniform(ks[11], (N_A, HA), f32, 1.0, 16.0))
    return {
        'x_prompt': nrm(ks[0], (BATCH, SEQ, D_MODEL), 1.0),
        'x_sample': nrm(ks[1], (DEC_BATCH, DEC_SEQ, D_MODEL), 1.0),
        'cache_conv': nrm(ks[2], (N_A, DEC_BATCH, CONV_W - 1, CONV_CH), 1.0),
        'state_gdn': nrm(ks[3], (N_A, DEC_BATCH, HA, DK, DV), 0.05),
        'cache_k': nrm(ks[4], (DEC_BATCH, WINDOW, N_KV, HD), 1.0),
        'cache_v': nrm(ks[5], (DEC_BATCH, WINDOW, N_KV, HD), 1.0),
        'attn_norm': 1.0 + nrm(ks[6], (DEPTH, D_MODEL), 0.02),
        'mlp_norm': 1.0 + nrm(ks[7], (DEPTH, D_MODEL), 0.02),
        'final_norm': 1.0 + nrm(ks[8], (D_MODEL,), 0.02),
        'a_w_in': nrm(ks[9], (N_A, D_MODEL, N_IN_A), D_MODEL ** -0.5),
        'a_conv_w': nrm(ks[12], (N_A, CONV_W, CONV_CH), CONV_W ** -0.5),
        'a_log': a_log,
        'a_dt_bias': dt_bias,
        'a_o_norm': 1.0 + nrm(ks[13], (N_A, DV), 0.02),
        'a_w_out': nrm(ks[14], (N_A, DVW, D_MODEL), DVW ** -0.5),
        'kv_norm': 1.0 + nrm(ks[15], (D_MODEL,), 0.02),
        'w_kv': nrm(ks[16], (D_MODEL, 2 * N_KV * HD), D_MODEL ** -0.5),
        'b_w_q': nrm(ks[17], (N_B, D_MODEL, N_Q * HD), D_MODEL ** -0.5),
        'b_sinks': nrm(ks[18], (N_B, N_Q), 0.5),
        'b_w_o': nrm(ks[19], (N_B, N_Q * HD, D_MODEL), (N_Q * HD) ** -0.5),
        'w_up': nrm(ks[20], (DEPTH, D_MODEL, D_FF), D_MODEL ** -0.5),
        'w_down': nrm(ks[21], (DEPTH, D_FF, D_MODEL), D_FF ** -0.5),
    }


def reference(x_prompt, x_sample, cache_conv, state_gdn, cache_k, cache_v, attn_norm, mlp_norm,
              final_norm, a_w_in, a_conv_w, a_log, a_dt_bias, a_o_norm, a_w_out, kv_norm, w_kv,
              b_w_q, b_sinks, b_w_o, w_up, w_down):
    run = functools.partial(
        trunk, attn_norm=attn_norm, mlp_norm=mlp_norm, final_norm=final_norm, a_w_in=a_w_in,
        a_conv_w=a_conv_w, a_log=a_log, a_dt_bias=a_dt_bias, a_o_norm=a_o_norm, a_w_out=a_w_out,
        kv_norm=kv_norm, w_kv=w_kv, b_w_q=b_w_q, b_sinks=b_sinks, b_w_o=b_w_o, w_up=w_up,
        w_down=w_down)
    bp, lp = x_prompt.shape[0], x_prompt.shape[1]
    conv0 = jnp.zeros((N_A, bp, CONV_W - 1, CONV_CH), x_prompt.dtype)
    s0 = jnp.zeros((N_A, bp, HA, DK, DV), jnp.float32)
    pos_p = jnp.arange(lp)
    y_prompt, conv_p, gdn_p, k_p, v_p = run(x_prompt, pos_p, conv0, s0, None, None)
    pos_s = PAST_LEN + jnp.arange(x_sample.shape[1])
    y_sample, conv_s, gdn_s, k_s, v_s = run(x_sample, pos_s, cache_conv, state_gdn, cache_k, cache_v)
    k_p_win = k_p[:, -WINDOW:]
    v_p_win = v_p[:, -WINDOW:]
    return (y_prompt, y_sample, conv_p, gdn_p, k_p_win, v_p_win, conv_s, gdn_s, k_s, v_s)
```

```python
import functools
import math

import jax
import jax.numpy as jnp
from jax import lax
from jax.experimental import pallas as pl
from jax.experimental.pallas import tpu as pltpu

F32 = jnp.float32
BF16 = jnp.bfloat16

EPS = 1e-6
CHUNK = 64
WINDOW = 128
PAST_LEN = 4096
ROPE_THETA = 10000.0
HA, DK, DV = 16, 128, 128
N_Q, N_KV, HD = 32, 4, 64
CONV_W = 4
LANES = 128
SUBLANES = 8
VMEM_LIMIT = 56 * 1024 * 1024
GDN_HEADS_PER_STEP = 4


def _pick_tile(n, candidates):
    for c in candidates:
        if n % c == 0:
            return c
    return n


def _dot(a, b, dims=(((1,), (0,)), ((), ()))):
    return lax.dot_general(a.astype(BF16), b.astype(BF16), dims, preferred_element_type=F32)


_NT = (((1,), (1,)), ((), ()))
_TN = (((0,), (0,)), ((), ()))


def _rms_scale(x):
    return lax.rsqrt(jnp.mean(x * x, axis=-1, keepdims=True) + EPS)


def _silu(x):
    return x / (1.0 + jnp.exp(-x))


def _rope_slab(y, cos, sin_signed):
    lane = lax.broadcasted_iota(jnp.int32, y.shape, 1)
    up = pltpu.roll(y, 32, 1)
    down = pltpu.roll(y, LANES - 32, 1)
    swapped = jnp.where((lane % HD) < HD // 2, down, up)
    return y * cos + swapped * sin_signed


def _matmul_kernel(*refs, has_norm, has_res, rope_slabs, n_slabs):
    it = iter(refs)
    x_ref = next(it)
    g_ref = next(it) if has_norm else None
    w_ref = next(it)
    res_ref = next(it) if has_res else None
    cos_ref = next(it) if rope_slabs else None
    sin_ref = next(it) if rope_slabs else None
    o_ref = next(it)
    xn_ref = next(it) if has_norm else None

    if has_norm:
        @pl.when(pl.program_id(1) == 0)
        def _():
            x = x_ref[...]
            xn_ref[...] = (x * _rms_scale(x) * g_ref[...]).astype(BF16)
        lhs = xn_ref[...]
    else:
        lhs = x_ref[...]
    y = jnp.dot(lhs, w_ref[...], preferred_element_type=F32)
    if has_res:
        y = y + res_ref[...]
    if rope_slabs:
        cos = cos_ref[...]
        sin = sin_ref[...]
        for s in range(n_slabs):
            sl = slice(s * LANES, (s + 1) * LANES)
            ys = y[:, sl]
            if s < rope_slabs:
                ys = _rope_slab(ys, cos, sin)
            o_ref[:, sl] = ys.astype(o_ref.dtype)
    else:
        o_ref[...] = y.astype(o_ref.dtype)


def _matmul(x, w, *, gain=None, res=None, rope=None, rope_cols=0, out_dtype=F32, tm, tn, name):
    m, k = x.shape
    n = w.shape[1]
    assert m % tm == 0 and n % tn == 0
    has_norm = gain is not None
    has_res = res is not None
    rope_slabs = 0
    if rope is not None:
        assert rope_cols == n or tn == n
        rope_slabs = min(rope_cols, tn) // LANES
    in_specs = [pl.BlockSpec((tm, k), lambda i, j: (i, 0))]
    args = [x]
    if has_norm:
        in_specs.append(pl.BlockSpec((1, k), lambda i, j: (0, 0)))
        args.append(gain)
    in_specs.append(pl.BlockSpec((k, tn), lambda i, j: (0, j)))
    args.append(w)
    if has_res:
        in_specs.append(pl.BlockSpec((tm, tn), lambda i, j: (i, j)))
        args.append(res)
    if rope_slabs:
        in_specs += [pl.BlockSpec((tm, LANES), lambda i, j: (i, 0))] * 2
        args += list(rope)
    kern = functools.partial(_matmul_kernel, has_norm=has_norm, has_res=has_res,
                             rope_slabs=rope_slabs, n_slabs=tn // LANES)
    return pl.pallas_call(
        kern,
        grid=(m // tm, n // tn),
        in_specs=in_specs,
        out_specs=pl.BlockSpec((tm, tn), lambda i, j: (i, j)),
        out_shape=jax.ShapeDtypeStruct((m, n), out_dtype),
        scratch_shapes=[pltpu.VMEM((tm, k), BF16)] if has_norm else [],
        compiler_params=pltpu.CompilerParams(
            dimension_semantics=("parallel", "arbitrary"), vmem_limit_bytes=VMEM_LIMIT),
        name=name,
    )(*args)


def _mlp_kernel(*refs, final_norm):
    if final_norm:
        x_ref, g_ref, wu_ref, wd_ref, fg_ref, o_ref, hn_ref, acc_ref = refs
    else:
        x_ref, g_ref, wu_ref, wd_ref, o_ref, hn_ref, acc_ref = refs
    f = pl.program_id(1)

    @pl.when(f == 0)
    def _():
        x = x_ref[...]
        hn_ref[...] = (x * _rms_scale(x) * g_ref[...]).astype(BF16)
        acc_ref[...] = jnp.zeros_like(acc_ref)

    u = jnp.dot(hn_ref[...], wu_ref[...], preferred_element_type=F32)
    a = jnp.square(jnp.maximum(u, 0.0)).astype(BF16)
    acc_ref[...] += jnp.dot(a, wd_ref[...], preferred_element_type=F32)

    @pl.when(f == pl.num_programs(1) - 1)
    def _():
        y = x_ref[...] + acc_ref[...]
        if final_norm:
            y = y * _rms_scale(y) * fg_ref[...]
        o_ref[...] = y


def _mlp(x, gain, w_up, w_down, *, final_gain=None, tm, tf, name):
    m, d = x.shape
    dff = w_up.shape[1]
    assert m % tm == 0 and dff % tf == 0
    final_norm = final_gain is not None
    in_specs = [
        pl.BlockSpec((tm, d), lambda i, f: (i, 0)),
        pl.BlockSpec((1, d), lambda i, f: (0, 0)),
        pl.BlockSpec((d, tf), lambda i, f: (0, f)),
        pl.BlockSpec((tf, d), lambda i, f: (f, 0)),
    ]
    args = [x, gain, w_up, w_down]
    if final_norm:
        in_specs.append(pl.BlockSpec((1, d), lambda i, f: (0, 0)))
        args.append(final_gain)
    return pl.pallas_call(
        functools.partial(_mlp_kernel, final_norm=final_norm),
        grid=(m // tm, dff // tf),
        in_specs=in_specs,
        out_specs=pl.BlockSpec((tm, d), lambda i, f: (i, 0)),
        out_shape=jax.ShapeDtypeStruct((m, d), F32),
        scratch_shapes=[pltpu.VMEM((tm, d), BF16), pltpu.VMEM((tm, d), F32)],
        compiler_params=pltpu.CompilerParams(
            dimension_semantics=("parallel", "arbitrary"), vmem_limit_bytes=VMEM_LIMIT),
        name=name,
    )(*args)


def _unit_lower_inverse(a, c):
    row = lax.broadcasted_iota(jnp.int32, (c, c), 0)
    col = lax.broadcasted_iota(jnp.int32, (c, c), 1)
    eye = jnp.where(row == col, 1.0, 0.0).astype(F32)
    p = -a
    t = eye + p
    for _ in range(max(int(math.ceil(math.log2(c))) - 1, 0)):
        p = _dot(p, p)
        t = t + _dot(t, p)
    return t


def _gdn_kernel(xq_ref, xk_ref, xv_ref, z_ref, gates_ref, alog_ref, dt_ref,
                pq_ref, pk_ref, pv_ref, wq_ref, wk_ref, wv_ref, onorm_ref, s0_ref,
                o_ref, s_ref, padq_ref, padk_ref, padv_ref, *, c, hb):
    n = pl.program_id(2)
    halo = SUBLANES

    @pl.when(n == 0)
    def _():
        padq_ref[0:halo, :] = pq_ref[0]
        padk_ref[0:halo, :] = pk_ref[0]
        padv_ref[0:halo, :] = pv_ref[0]
        s_ref[...] = s0_ref[...]

    def conv(x_ref, pad_ref, w_ref):
        pad_ref[halo:halo + c, :] = x_ref[...]
        acc = None
        for j in range(CONV_W):
            off = halo - (CONV_W - 1) + j
            term = pad_ref[off:off + c, :] * w_ref[j:j + 1, :]
            acc = term if acc is None else acc + term
        pad_ref[0:halo, :] = pad_ref[c:c + halo, :]
        return _silu(acc)

    cq = conv(xq_ref, padq_ref, wq_ref)
    ck = conv(xk_ref, padk_ref, wk_ref)
    cv = conv(xv_ref, padv_ref, wv_ref)

    gates = gates_ref[...]
    beta_all = 1.0 / (1.0 + jnp.exp(-gates))
    ga = gates + dt_ref[...]
    softplus = jnp.maximum(ga, 0.0) + jnp.log(1.0 + jnp.exp(-jnp.abs(ga)))
    g_all = -jnp.exp(alog_ref[...]) * softplus
    row = lax.broadcasted_iota(jnp.int32, (c, c), 0)
    col = lax.broadcasted_iota(jnp.int32, (c, c), 1)
    incl = row >= col
    strict = row > col
    tril = jnp.where(incl, 1.0, 0.0).astype(F32)
    gc_all = jnp.dot(tril, g_all, preferred_element_type=F32,
                     precision=lax.Precision.HIGHEST)
    r128 = lax.broadcasted_iota(jnp.int32, (LANES, LANES), 0)
    c128 = lax.broadcasted_iota(jnp.int32, (LANES, LANES), 1)
    eye128 = jnp.where(r128 == c128, 1.0, 0.0).astype(F32)
    gc_t = lax.dot_general(eye128, gc_all, _NT, preferred_element_type=F32,
                           precision=lax.Precision.HIGHEST)

    onorm = onorm_ref[...]
    for hh in range(hb):
        sl = slice(hh * LANES, (hh + 1) * LANES)
        q = cq[:, sl]
        k = ck[:, sl]
        v = cv[:, sl]
        q = q * (lax.rsqrt(jnp.sum(q * q, axis=-1, keepdims=True) + EPS) * (DK ** -0.5))
        k = k * lax.rsqrt(jnp.sum(k * k, axis=-1, keepdims=True) + EPS)
        bcol = beta_all[:, hh:hh + 1]
        gcol = gc_all[:, 64 + hh:64 + hh + 1]
        grow = gc_t[64 + hh:64 + hh + 1, :]
        glast = gc_all[c - 1:c, 64 + hh:64 + hh + 1]
        decay = jnp.exp(jnp.where(incl, gcol - grow, -jnp.inf))
        kb = k * bcol
        a = jnp.where(strict, _dot(kb, k, _NT) * decay, 0.0)
        t = _unit_lower_inverse(a, c)
        egc = jnp.exp(gcol)
        rhs = jnp.concatenate([v * bcol, kb * egc], axis=-1)
        sol = _dot(t, rhs)
        u = sol[:, :DV]
        w = sol[:, DV:]
        qk = _dot(q, k, _NT) * decay
        s = s_ref[0, hh]
        v_new = u - _dot(w, s)
        o = _dot(q * egc, s) + _dot(qk, v_new)
        s_ref[0, hh] = s * jnp.exp(glast) + _dot(k * jnp.exp(glast - gcol), v_new, _TN)
        o = o * _rms_scale(o) * onorm
        o = o * _silu(z_ref[:, sl])
        o_ref[:, sl] = o.astype(o_ref.dtype)


def _gdn(qkvz, gates, alog_row, dt_row, conv_prev_pad, conv_w, onorm_row, s0,
         *, row0, batch, length, c, hb, name):
    nblk = length // c
    ng = HA // hb
    hw = hb * LANES
    rb0 = row0 // c
    assert row0 % c == 0 and length % c == 0

    def rows(b, g, n):
        return rb0 + b * nblk + n

    x_spec = lambda part: pl.BlockSpec((c, hw), lambda b, g, n: (rows(b, g, n), part * ng + g))
    prev_spec = lambda part: pl.BlockSpec((1, SUBLANES, hw), lambda b, g, n: (b, 0, part * ng + g))
    w_spec = lambda part: pl.BlockSpec((CONV_W, hw), lambda b, g, n: (0, part * ng + g))
    gate_row_spec = pl.BlockSpec((1, LANES), lambda b, g, n: (0, g))
    in_specs = [
        x_spec(0), x_spec(1), x_spec(2), x_spec(3),
        pl.BlockSpec((c, LANES), lambda b, g, n: (rows(b, g, n), g)),
        gate_row_spec, gate_row_spec,
        prev_spec(0), prev_spec(1), prev_spec(2),
        w_spec(0), w_spec(1), w_spec(2),
        pl.BlockSpec((1, LANES), lambda b, g, n: (0, 0)),
        pl.BlockSpec((1, hb, DK, DV), lambda b, g, n: (b, g, 0, 0)),
    ]
    out_specs = [
        pl.BlockSpec((c, hw), lambda b, g, n: (b * nblk + n, g)),
        pl.BlockSpec((1, hb, DK, DV), lambda b, g, n: (b, g, 0, 0)),
    ]
    return pl.pallas_call(
        functools.partial(_gdn_kernel, c=c, hb=hb),
        grid=(batch, ng, nblk),
        in_specs=in_specs,
        out_specs=out_specs,
        out_shape=[jax.ShapeDtypeStruct((batch * length, HA * DV), BF16),
                   jax.ShapeDtypeStruct((batch, HA, DK, DV), F32)],
        scratch_shapes=[pltpu.VMEM((c + SUBLANES, hw), F32)] * 3,
        compiler_params=pltpu.CompilerParams(
            dimension_semantics=("parallel", "parallel", "arbitrary"),
            vmem_limit_bytes=VMEM_LIMIT),
        name=name,
    )(qkvz, qkvz, qkvz, qkvz, gates, alog_row, dt_row,
      conv_prev_pad, conv_prev_pad, conv_prev_pad, conv_w, conv_w, conv_w, onorm_row, s0)


def _attn_kernel(sinks_ref, q_ref, kp_ref, vp_ref, kc_ref, vc_ref, o_ref, *, tq, chunk_mask):
    i = pl.program_id(1)
    nk = WINDOW + tq
    kall = jnp.concatenate([kp_ref[...], kc_ref[...]], axis=0)
    vall = jnp.concatenate([vp_ref[...], vc_ref[...]], axis=0)
    lane_k = lax.broadcasted_iota(jnp.int32, (nk, LANES), 1)
    lane_q = lax.broadcasted_iota(jnp.int32, (tq, LANES), 1)
    if chunk_mask:
        qc = lax.broadcasted_iota(jnp.int32, (tq, nk), 0) // CHUNK
        kc = lax.broadcasted_iota(jnp.int32, (tq, nk), 1) // CHUNK
        w_ch = WINDOW // CHUNK
        first_kc = jnp.where(i > 0, 0, w_ch)
        valid = (kc >= jnp.maximum(qc, first_kc)) & (kc <= qc + w_ch)
    scale = HD ** -0.5
    heads_per_slab = LANES // HD
    group = N_Q // N_KV
    for h in range(N_KV):
        ksl = slice((h // heads_per_slab) * LANES, (h // heads_per_slab + 1) * LANES)
        kslab = kall[:, ksl]
        vslab = vall[:, ksl]
        first = (lane_k < HD) == (h % heads_per_slab == 0)
        kdup = jnp.where(first, kslab, pltpu.roll(kslab, HD, 1)).astype(BF16)
        vdup = jnp.where(first, vslab, pltpu.roll(vslab, HD, 1)).astype(BF16)
        for s in range(group // heads_per_slab):
            slab = h * (group // heads_per_slab) + s
            sl = slice(slab * LANES, (slab + 1) * LANES)
            q2 = q_ref[:, sl]
            outs = []
            for half in range(heads_per_slab):
                hq = slab * heads_per_slab + half
                in_half = (lane_q >= half * HD) & (lane_q < (half + 1) * HD)
                qh = jnp.where(in_half, q2, jnp.zeros_like(q2))
                sc = lax.dot_general(qh, kdup, _NT, preferred_element_type=F32) * scale
                if chunk_mask:
                    sc = jnp.where(valid, sc, -jnp.inf)
                sk = sinks_ref[hq]
                mx = jnp.maximum(jnp.max(sc, axis=-1, keepdims=True), sk)
                e = jnp.exp(sc - mx)
                den = jnp.sum(e, axis=-1, keepdims=True) + jnp.exp(sk - mx)
                p = e / den
                outs.append(jnp.dot(p.astype(BF16), vdup, preferred_element_type=F32))
            o_ref[:, sl] = jnp.where(lane_q < HD, outs[0], outs[1]).astype(o_ref.dtype)


def _attention(sinks, q, kprev, vprev, kprev_col, vprev_col, prev_rb0, kv, *, row0, batch, length,
               tq, chunk_mask, name):
    nq = length // tq
    rb0 = row0 // tq
    kvw = N_KV * HD
    assert row0 % tq == 0 and length % tq == 0

    def cur(b, i):
        return rb0 + b * nq + i

    def prev(b, i):
        return prev_rb0 + b * nq + jnp.maximum(i - 1, 0)

    in_specs = [
        pl.BlockSpec(memory_space=pltpu.SMEM),
        pl.BlockSpec((tq, N_Q * HD), lambda b, i: (cur(b, i), 0)),
        pl.BlockSpec((WINDOW, kvw), lambda b, i: (prev(b, i), kprev_col)),
        pl.BlockSpec((WINDOW, kvw), lambda b, i: (prev(b, i), vprev_col)),
        pl.BlockSpec((tq, kvw), lambda b, i: (cur(b, i), 0)),
        pl.BlockSpec((tq, kvw), lambda b, i: (cur(b, i), 1)),
    ]
    return pl.pallas_call(
        functools.partial(_attn_kernel, tq=tq, chunk_mask=chunk_mask),
        grid=(batch, nq),
        in_specs=in_specs,
        out_specs=pl.BlockSpec((tq, N_Q * HD), lambda b, i: (b * nq + i, 0)),
        out_shape=jax.ShapeDtypeStruct((batch * length, N_Q * HD), BF16),
        compiler_params=pltpu.CompilerParams(
            dimension_semantics=("parallel", "arbitrary"), vmem_limit_bytes=VMEM_LIMIT),
        name=name,
    )(sinks, q, kprev, vprev, kv, kv)


def _rope_tables(pos):
    half = HD // 2
    inv = 1.0 / (ROPE_THETA ** (jnp.arange(half, dtype=F32) / half))
    ang = pos.astype(F32)[:, None] * inv[None, :]
    cos = jnp.cos(ang)
    sin = jnp.sin(ang)
    return (jnp.concatenate([cos, cos, cos, cos], axis=-1),
            jnp.concatenate([-sin, sin, -sin, sin], axis=-1))


def _gate_layout(t, hb):
    lead = t.shape[:-1]
    ng = HA // hb
    beta = t[..., :HA].reshape(lead + (ng, hb))
    dec = t[..., HA:].reshape(lead + (ng, hb))
    pad = jnp.zeros(lead + (ng, LANES // 2 - hb), t.dtype)
    return jnp.concatenate([beta, pad, dec, pad], axis=-1).reshape(lead + (ng * LANES,))


def kernel(x_prompt, x_sample, cache_conv, state_gdn, cache_k, cache_v, attn_norm, mlp_norm, final_norm, a_w_in, a_conv_w, a_log, a_dt_bias, a_o_norm, a_w_out, kv_norm, w_kv, b_w_q, b_sinks, b_w_o, w_up, w_down):
    bp, lp, d = x_prompt.shape
    bs, ls, _ = x_sample.shape
    mp, ms = bp * lp, bs * ls
    m = mp + ms
    hb = GDN_HEADS_PER_STEP
    dqk, dvw = HA * DK, HA * DV
    conv_ch = 2 * dqk + dvw
    assert a_w_in.shape[0] == 1 and b_w_q.shape[0] == 1 and lp % WINDOW == 0 and mp % WINDOW == 0

    x = jnp.concatenate([x_prompt.reshape(mp, d), x_sample.reshape(ms, d)], axis=0)
    tm = _pick_tile(m, (512, 256, 128, 64, 32, 16))

    w_in = a_w_in[0]
    w_main = w_in[:, :conv_ch + dvw].astype(BF16)
    w_gate = _gate_layout(w_in[:, conv_ch + dvw:], hb).astype(BF16)
    g0 = attn_norm[0].reshape(1, d)
    qkvz = _matmul(x, w_main, gain=g0, tm=tm, tn=_pick_tile(conv_ch + dvw, (1024, 512, 256, 128)),
                   name="gdn_in_proj")
    gates = _matmul(x, w_gate, gain=g0, tm=tm, tn=w_gate.shape[1], name="gdn_gate_proj")
    zeros_gate = jnp.zeros((HA,), F32)
    alog_row = _gate_layout(jnp.concatenate([zeros_gate, a_log[0]]), hb).reshape(1, -1)
    dt_row = _gate_layout(jnp.concatenate([zeros_gate, a_dt_bias[0]]), hb).reshape(1, -1)
    onorm_row = a_o_norm[0].reshape(1, DV)
    conv_w = a_conv_w[0]
    pad_rows = ((0, 0), (SUBLANES - (CONV_W - 1), 0), (0, 0))
    prev_p = jnp.zeros((bp, SUBLANES, conv_ch), F32)
    prev_s = jnp.pad(cache_conv[0], pad_rows)
    o_p, gdn_p = _gdn(qkvz, gates, alog_row, dt_row, prev_p, conv_w, onorm_row,
                      jnp.zeros((bp, HA, DK, DV), F32), row0=0, batch=bp, length=lp,
                      c=min(CHUNK, lp), hb=hb, name="gdn_prompt")
    o_s, gdn_s = _gdn(qkvz, gates, alog_row, dt_row, prev_s, conv_w, onorm_row,
                      state_gdn[0], row0=mp, batch=bs, length=ls,
                      c=min(CHUNK, ls), hb=hb, name="gdn_sample")
    o = jnp.concatenate([o_p, o_s], axis=0)
    tn_d = _pick_tile(d, (1024, 512, 256, 128))
    x = _matmul(o, a_w_out[0].astype(BF16), res=x, tm=tm, tn=tn_d, name="gdn_out_proj")
    tf = _pick_tile(w_up.shape[2], (512, 256, 128))
    x = _mlp(x, mlp_norm[0].reshape(1, d), w_up[0].astype(BF16), w_down[0].astype(BF16),
             tm=tm, tf=tf, name="mlp0")

    pos = jnp.concatenate([jnp.tile(jnp.arange(lp), bp), jnp.tile(PAST_LEN + jnp.arange(ls), bs)])
    rope = _rope_tables(pos)
    kvw = N_KV * HD
    kv = _matmul(x, w_kv.astype(BF16), gain=kv_norm.reshape(1, d), rope=rope, rope_cols=kvw,
                 tm=tm, tn=2 * kvw, name="kv_proj")
    q = _matmul(x, b_w_q[0].astype(BF16), gain=attn_norm[1].reshape(1, d), rope=rope,
                rope_cols=N_Q * HD, out_dtype=BF16, tm=tm, tn=tn_d, name="q_proj")
    sinks = b_sinks[0].astype(F32)
    a_p = _attention(sinks, q, kv, kv, 0, 1, 0, kv, row0=0, batch=bp, length=lp, tq=WINDOW,
                     chunk_mask=True, name="attn_prompt")
    a_s = _attention(sinks, q, cache_k.reshape(bs * WINDOW, kvw), cache_v.reshape(bs * WINDOW, kvw),
                     0, 0, 0, kv, row0=mp, batch=bs, length=ls, tq=ls,
                     chunk_mask=False, name="attn_sample")
    a = jnp.concatenate([a_p, a_s], axis=0)
    x = _matmul(a, b_w_o[0].astype(BF16), res=x, tm=tm, tn=tn_d, name="attn_out_proj")
    y = _mlp(x, mlp_norm[1].reshape(1, d), w_up[1].astype(BF16), w_down[1].astype(BF16),
             final_gain=final_norm.reshape(1, d), tm=tm, tf=tf, name="mlp1")

    y_prompt = y[:mp].reshape(bp, lp, d)
    y_sample = y[mp:].reshape(bs, ls, d)
    keep = CONV_W - 1
    qkv_p = qkvz[:mp, :conv_ch].reshape(bp, lp, conv_ch)
    qkv_s = qkvz[mp:, :conv_ch].reshape(bs, ls, conv_ch)
    conv_p = qkv_p[:, lp - keep:][None]
    conv_s = qkv_s[:, ls - keep:][None]
    kv_p = kv[:mp].reshape(bp, lp, 2 * kvw)[:, lp - WINDOW:]
    kv_s = kv[mp:].reshape(bs, ls, 2 * kvw)
    k_p = kv_p[..., :kvw].reshape(bp, WINDOW, N_KV, HD)
    v_p = kv_p[..., kvw:].reshape(bp, WINDOW, N_KV, HD)
    k_s = kv_s[..., :kvw].reshape(bs, ls, N_KV, HD)
    v_s = kv_s[..., kvw:].reshape(bs, ls, N_KV, HD)
    return (y_prompt, y_sample, conv_p, gdn_p[None], k_p, v_p, conv_s, gdn_s[None], k_s, v_s)
```

```python
import functools
import math

import jax
import jax.numpy as jnp
from jax import lax
from jax.experimental import pallas as pl
from jax.experimental.pallas import tpu as pltpu

F32 = jnp.float32
BF16 = jnp.bfloat16

EPS = 1e-6
CHUNK = 64
WINDOW = 128
PAST_LEN = 4096
ROPE_THETA = 10000.0
HA, DK, DV = 16, 128, 128
N_Q, N_KV, HD = 32, 4, 64
CONV_W = 4
LANES = 128
SUBLANES = 8
VMEM_LIMIT = 56 * 1024 * 1024
GDN_HEADS_PER_STEP = 16


def _pick_tile(n, candidates):
    for c in candidates:
        if n % c == 0:
            return c
    return n


def _dot(a, b, dims=(((1,), (0,)), ((), ()))):
    return lax.dot_general(a.astype(BF16), b.astype(BF16), dims, preferred_element_type=F32)


_NT = (((1,), (1,)), ((), ()))
_TN = (((0,), (0,)), ((), ()))


def _rms_scale(x):
    return lax.rsqrt(jnp.mean(x * x, axis=-1, keepdims=True) + EPS)


def _silu(x):
    return x / (1.0 + jnp.exp(-x))


def _rope_slab(y, cos, sin_signed):
    lane = lax.broadcasted_iota(jnp.int32, y.shape, 1)
    up = pltpu.roll(y, 32, 1)
    down = pltpu.roll(y, LANES - 32, 1)
    swapped = jnp.where((lane % HD) < HD // 2, down, up)
    return y * cos + swapped * sin_signed


def _matmul_kernel(*refs, has_norm, has_res, rope_slabs, n_slabs):
    it = iter(refs)
    x_ref = next(it)
    g_ref = next(it) if has_norm else None
    w_ref = next(it)
    res_ref = next(it) if has_res else None
    cos_ref = next(it) if rope_slabs else None
    sin_ref = next(it) if rope_slabs else None
    o_ref = next(it)
    xn_ref = next(it) if has_norm else None

    if has_norm:
        @pl.when(pl.program_id(1) == 0)
        def _():
            x = x_ref[...]
            xn_ref[...] = (x * _rms_scale(x) * g_ref[...]).astype(BF16)
        lhs = xn_ref[...]
    else:
        lhs = x_ref[...]
    y = jnp.dot(lhs, w_ref[...], preferred_element_type=F32)
    if has_res:
        y = y + res_ref[...]
    if rope_slabs:
        cos = cos_ref[...]
        sin = sin_ref[...]
        for s in range(n_slabs):
            sl = slice(s * LANES, (s + 1) * LANES)
            ys = y[:, sl]
            if s < rope_slabs:
                ys = _rope_slab(ys, cos, sin)
            o_ref[:, sl] = ys.astype(o_ref.dtype)
    else:
        o_ref[...] = y.astype(o_ref.dtype)


def _matmul(x, w, *, gain=None, res=None, rope=None, rope_cols=0, out_dtype=F32, tm, tn, name):
    m, k = x.shape
    n = w.shape[1]
    assert m % tm == 0 and n % tn == 0
    has_norm = gain is not None
    has_res = res is not None
    rope_slabs = 0
    if rope is not None:
        assert rope_cols == n or tn == n
        rope_slabs = min(rope_cols, tn) // LANES
    in_specs = [pl.BlockSpec((tm, k), lambda i, j: (i, 0))]
    args = [x]
    if has_norm:
        in_specs.append(pl.BlockSpec((1, k), lambda i, j: (0, 0)))
        args.append(gain)
    in_specs.append(pl.BlockSpec((k, tn), lambda i, j: (0, j)))
    args.append(w)
    if has_res:
        in_specs.append(pl.BlockSpec((tm, tn), lambda i, j: (i, j)))
        args.append(res)
    if rope_slabs:
        in_specs += [pl.BlockSpec((tm, LANES), lambda i, j: (i, 0))] * 2
        args += list(rope)
    kern = functools.partial(_matmul_kernel, has_norm=has_norm, has_res=has_res,
                             rope_slabs=rope_slabs, n_slabs=tn // LANES)
    return pl.pallas_call(
        kern,
        grid=(m // tm, n // tn),
        in_specs=in_specs,
        out_specs=pl.BlockSpec((tm, tn), lambda i, j: (i, j)),
        out_shape=jax.ShapeDtypeStruct((m, n), out_dtype),
        scratch_shapes=[pltpu.VMEM((tm, k), BF16)] if has_norm else [],
        compiler_params=pltpu.CompilerParams(
            dimension_semantics=("parallel", "arbitrary"), vmem_limit_bytes=VMEM_LIMIT),
        name=name,
    )(*args)


def _mlp_kernel(*refs, final_norm):
    if final_norm:
        x_ref, g_ref, wu_ref, wd_ref, fg_ref, o_ref, hn_ref, acc_ref = refs
    else:
        x_ref, g_ref, wu_ref, wd_ref, o_ref, hn_ref, acc_ref = refs
    f = pl.program_id(1)

    @pl.when(f == 0)
    def _():
        x = x_ref[...]
        hn_ref[...] = (x * _rms_scale(x) * g_ref[...]).astype(BF16)
        acc_ref[...] = jnp.zeros_like(acc_ref)

    u = jnp.dot(hn_ref[...], wu_ref[...], preferred_element_type=F32)
    a = jnp.square(jnp.maximum(u, 0.0)).astype(BF16)
    acc_ref[...] += jnp.dot(a, wd_ref[...], preferred_element_type=F32)

    @pl.when(f == pl.num_programs(1) - 1)
    def _():
        y = x_ref[...] + acc_ref[...]
        if final_norm:
            y = y * _rms_scale(y) * fg_ref[...]
        o_ref[...] = y


def _mlp(x, gain, w_up, w_down, *, final_gain=None, tm, tf, name):
    m, d = x.shape
    dff = w_up.shape[1]
    assert m % tm == 0 and dff % tf == 0
    final_norm = final_gain is not None
    in_specs = [
        pl.BlockSpec((tm, d), lambda i, f: (i, 0)),
        pl.BlockSpec((1, d), lambda i, f: (0, 0)),
        pl.BlockSpec((d, tf), lambda i, f: (0, f)),
        pl.BlockSpec((tf, d), lambda i, f: (f, 0)),
    ]
    args = [x, gain, w_up, w_down]
    if final_norm:
        in_specs.append(pl.BlockSpec((1, d), lambda i, f: (0, 0)))
        args.append(final_gain)
    return pl.pallas_call(
        functools.partial(_mlp_kernel, final_norm=final_norm),
        grid=(m // tm, dff // tf),
        in_specs=in_specs,
        out_specs=pl.BlockSpec((tm, d), lambda i, f: (i, 0)),
        out_shape=jax.ShapeDtypeStruct((m, d), F32),
        scratch_shapes=[pltpu.VMEM((tm, d), BF16), pltpu.VMEM((tm, d), F32)],
        compiler_params=pltpu.CompilerParams(
            dimension_semantics=("parallel", "arbitrary"), vmem_limit_bytes=VMEM_LIMIT),
        name=name,
    )(*args)


def _gdn_kernel(xq_ref, xk_ref, xv_ref, z_ref, gates_ref, alog_ref, dt_ref,
                pq_ref, pk_ref, pv_ref, wq_ref, wk_ref, wv_ref, onorm_ref, s0_ref,
                o_ref, s_ref, padq_ref, padk_ref, padv_ref, *, c, hb):
    n = pl.program_id(2)
    halo = SUBLANES

    @pl.when(n == 0)
    def _():
        padq_ref[0:halo, :] = pq_ref[0]
        padk_ref[0:halo, :] = pk_ref[0]
        padv_ref[0:halo, :] = pv_ref[0]
        s_ref[...] = s0_ref[...]

    def conv(x_ref, pad_ref, w_ref):
        pad_ref[halo:halo + c, :] = x_ref[...]
        acc = None
        for j in range(CONV_W):
            off = halo - (CONV_W - 1) + j
            term = pad_ref[off:off + c, :] * w_ref[j:j + 1, :]
            acc = term if acc is None else acc + term
        pad_ref[0:halo, :] = pad_ref[c:c + halo, :]
        return _silu(acc)

    cq = conv(xq_ref, padq_ref, wq_ref)
    ck = conv(xk_ref, padk_ref, wk_ref)
    cv = conv(xv_ref, padv_ref, wv_ref)

    gates = gates_ref[...]
    beta_all = 1.0 / (1.0 + jnp.exp(-gates))
    ga = gates + dt_ref[...]
    softplus = jnp.maximum(ga, 0.0) + jnp.log(1.0 + jnp.exp(-jnp.abs(ga)))
    g_all = -jnp.exp(alog_ref[...]) * softplus
    row = lax.broadcasted_iota(jnp.int32, (c, c), 0)
    col = lax.broadcasted_iota(jnp.int32, (c, c), 1)
    incl = row >= col
    strict = row > col
    tril = jnp.where(incl, 1.0, 0.0).astype(F32)
    gc_all = jnp.dot(tril, g_all, preferred_element_type=F32,
                     precision=lax.Precision.HIGHEST)
    r128 = lax.broadcasted_iota(jnp.int32, (LANES, LANES), 0)
    c128 = lax.broadcasted_iota(jnp.int32, (LANES, LANES), 1)
    eye128 = jnp.where(r128 == c128, 1.0, 0.0).astype(F32)
    gc_t = lax.dot_general(eye128, gc_all, _NT, preferred_element_type=F32,
                           precision=lax.Precision.HIGHEST)

    heads = range(hb)
    sls = [slice(h * LANES, (h + 1) * LANES) for h in heads]
    q = [cq[:, sl] for sl in sls]
    k = [ck[:, sl] for sl in sls]
    v = [cv[:, sl] for sl in sls]
    q = [x * (lax.rsqrt(jnp.sum(x * x, axis=-1, keepdims=True) + EPS) * (DK ** -0.5)) for x in q]
    k = [x * lax.rsqrt(jnp.sum(x * x, axis=-1, keepdims=True) + EPS) for x in k]
    bcol = [beta_all[:, h:h + 1] for h in heads]
    gcol = [gc_all[:, 64 + h:64 + h + 1] for h in heads]
    grow = [gc_t[64 + h:64 + h + 1, :] for h in heads]
    glast = [gc_all[c - 1:c, 64 + h:64 + h + 1] for h in heads]
    decay = [jnp.exp(jnp.where(incl, gcol[h] - grow[h], -jnp.inf)) for h in heads]
    kb = [k[h] * bcol[h] for h in heads]
    eye = jnp.where(row == col, 1.0, 0.0).astype(F32)
    p = [-jnp.where(strict, _dot(kb[h], k[h], _NT) * decay[h], 0.0) for h in heads]
    t = [eye + x for x in p]
    for _ in range(max(int(math.ceil(math.log2(c))) - 1, 0)):
        p = [_dot(x, x) for x in p]
        t = [t[h] + _dot(t[h], p[h]) for h in heads]
    egc = [jnp.exp(x) for x in gcol]
    rhs = [jnp.concatenate([v[h] * bcol[h], kb[h] * egc[h]], axis=-1) for h in heads]
    sol = [_dot(t[h], rhs[h]) for h in heads]
    qk = [_dot(q[h], k[h], _NT) * decay[h] for h in heads]
    s = [s_ref[0, h] for h in heads]
    v_new = [sol[h][:, :DV] - _dot(sol[h][:, DV:], s[h]) for h in heads]
    o = [_dot(q[h] * egc[h], s[h]) + _dot(qk[h], v_new[h]) for h in heads]
    s_new = [s[h] * jnp.exp(glast[h]) + _dot(k[h] * jnp.exp(glast[h] - gcol[h]), v_new[h], _TN)
             for h in heads]
    onorm = onorm_ref[...]
    for h in heads:
        s_ref[0, h] = s_new[h]
        oh = o[h] * _rms_scale(o[h]) * onorm
        o_ref[:, sls[h]] = (oh * _silu(z_ref[:, sls[h]])).astype(o_ref.dtype)


def _gdn(qkvz, gates, alog_row, dt_row, conv_prev_pad, conv_w, onorm_row, s0,
         *, row0, batch, length, c, hb, name):
    nblk = length // c
    ng = HA // hb
    hw = hb * LANES
    rb0 = row0 // c
    assert row0 % c == 0 and length % c == 0

    def rows(b, g, n):
        return rb0 + b * nblk + n

    x_spec = lambda part: pl.BlockSpec((c, hw), lambda b, g, n: (rows(b, g, n), part * ng + g))
    prev_spec = lambda part: pl.BlockSpec((1, SUBLANES, hw), lambda b, g, n: (b, 0, part * ng + g))
    w_spec = lambda part: pl.BlockSpec((CONV_W, hw), lambda b, g, n: (0, part * ng + g))
    gate_row_spec = pl.BlockSpec((1, LANES), lambda b, g, n: (0, g))
    in_specs = [
        x_spec(0), x_spec(1), x_spec(2), x_spec(3),
        pl.BlockSpec((c, LANES), lambda b, g, n: (rows(b, g, n), g)),
        gate_row_spec, gate_row_spec,
        prev_spec(0), prev_spec(1), prev_spec(2),
        w_spec(0), w_spec(1), w_spec(2),
        pl.BlockSpec((1, LANES), lambda b, g, n: (0, 0)),
        pl.BlockSpec((1, hb, DK, DV), lambda b, g, n: (b, g, 0, 0)),
    ]
    out_specs = [
        pl.BlockSpec((c, hw), lambda b, g, n: (b * nblk + n, g)),
        pl.BlockSpec((1, hb, DK, DV), lambda b, g, n: (b, g, 0, 0)),
    ]
    return pl.pallas_call(
        functools.partial(_gdn_kernel, c=c, hb=hb),
        grid=(batch, ng, nblk),
        in_specs=in_specs,
        out_specs=out_specs,
        out_shape=[jax.ShapeDtypeStruct((batch * length, HA * DV), BF16),
                   jax.ShapeDtypeStruct((batch, HA, DK, DV), F32)],
        scratch_shapes=[pltpu.VMEM((c + SUBLANES, hw), F32)] * 3,
        compiler_params=pltpu.CompilerParams(
            dimension_semantics=("parallel", "parallel", "arbitrary"),
            vmem_limit_bytes=VMEM_LIMIT),
        name=name,
    )(qkvz, qkvz, qkvz, qkvz, gates, alog_row, dt_row,
      conv_prev_pad, conv_prev_pad, conv_prev_pad, conv_w, conv_w, conv_w, onorm_row, s0)


def _attn_kernel(sinks_ref, q_ref, kp_ref, vp_ref, kc_ref, vc_ref, o_ref, *, tq, chunk_mask):
    i = pl.program_id(1)
    nk = WINDOW + tq
    kall = jnp.concatenate([kp_ref[...], kc_ref[...]], axis=0)
    vall = jnp.concatenate([vp_ref[...], vc_ref[...]], axis=0)
    lane_k = lax.broadcasted_iota(jnp.int32, (nk, LANES), 1)
    lane_q = lax.broadcasted_iota(jnp.int32, (tq, LANES), 1)
    if chunk_mask:
        qc = lax.broadcasted_iota(jnp.int32, (tq, nk), 0) // CHUNK
        kc = lax.broadcasted_iota(jnp.int32, (tq, nk), 1) // CHUNK
        w_ch = WINDOW // CHUNK
        first_kc = jnp.where(i > 0, 0, w_ch)
        valid = (kc >= jnp.maximum(qc, first_kc)) & (kc <= qc + w_ch)
    scale = HD ** -0.5
    heads_per_slab = LANES // HD
    group = N_Q // N_KV
    for h in range(N_KV):
        ksl = slice((h // heads_per_slab) * LANES, (h // heads_per_slab + 1) * LANES)
        kslab = kall[:, ksl]
        vslab = vall[:, ksl]
        first = (lane_k < HD) == (h % heads_per_slab == 0)
        kdup = jnp.where(first, kslab, pltpu.roll(kslab, HD, 1)).astype(BF16)
        vdup = jnp.where(first, vslab, pltpu.roll(vslab, HD, 1)).astype(BF16)
        for s in range(group // heads_per_slab):
            slab = h * (group // heads_per_slab) + s
            sl = slice(slab * LANES, (slab + 1) * LANES)
            q2 = q_ref[:, sl]
            outs = []
            for half in range(heads_per_slab):
                hq = slab * heads_per_slab + half
                in_half = (lane_q >= half * HD) & (lane_q < (half + 1) * HD)
                qh = jnp.where(in_half, q2, jnp.zeros_like(q2))
                sc = lax.dot_general(qh, kdup, _NT, preferred_element_type=F32) * scale
                if chunk_mask:
                    sc = jnp.where(valid, sc, -jnp.inf)
                sk = sinks_ref[hq]
                mx = jnp.maximum(jnp.max(sc, axis=-1, keepdims=True), sk)
                e = jnp.exp(sc - mx)
                den = jnp.sum(e, axis=-1, keepdims=True) + jnp.exp(sk - mx)
                p = e / den
                outs.append(jnp.dot(p.astype(BF16), vdup, preferred_element_type=F32))
            o_ref[:, sl] = jnp.where(lane_q < HD, outs[0], outs[1]).astype(o_ref.dtype)


def _attention(sinks, q, kprev, vprev, kprev_col, vprev_col, prev_rb0, kv, *, row0, batch, length,
               tq, chunk_mask, name):
    nq = length // tq
    rb0 = row0 // tq
    kvw = N_KV * HD
    assert row0 % tq == 0 and length % tq == 0

    def cur(b, i):
        return rb0 + b * nq + i

    def prev(b, i):
        return prev_rb0 + b * nq + jnp.maximum(i - 1, 0)

    in_specs = [
        pl.BlockSpec(memory_space=pltpu.SMEM),
        pl.BlockSpec((tq, N_Q * HD), lambda b, i: (cur(b, i), 0)),
        pl.BlockSpec((WINDOW, kvw), lambda b, i: (prev(b, i), kprev_col)),
        pl.BlockSpec((WINDOW, kvw), lambda b, i: (prev(b, i), vprev_col)),
        pl.BlockSpec((tq, kvw), lambda b, i: (cur(b, i), 0)),
        pl.BlockSpec((tq, kvw), lambda b, i: (cur(b, i), 1)),
    ]
    return pl.pallas_call(
        functools.partial(_attn_kernel, tq=tq, chunk_mask=chunk_mask),
        grid=(batch, nq),
        in_specs=in_specs,
        out_specs=pl.BlockSpec((tq, N_Q * HD), lambda b, i: (b * nq + i, 0)),
        out_shape=jax.ShapeDtypeStruct((batch * length, N_Q * HD), BF16),
        compiler_params=pltpu.CompilerParams(
            dimension_semantics=("parallel", "arbitrary"), vmem_limit_bytes=VMEM_LIMIT),
        name=name,
    )(sinks, q, kprev, vprev, kv, kv)


def _rope_tables(pos):
    half = HD // 2
    inv = 1.0 / (ROPE_THETA ** (jnp.arange(half, dtype=F32) / half))
    ang = pos.astype(F32)[:, None] * inv[None, :]
    cos = jnp.cos(ang)
    sin = jnp.sin(ang)
    return (jnp.concatenate([cos, cos, cos, cos], axis=-1),
            jnp.concatenate([-sin, sin, -sin, sin], axis=-1))


def _gate_layout(t, hb):
    lead = t.shape[:-1]
    ng = HA // hb
    beta = t[..., :HA].reshape(lead + (ng, hb))
    dec = t[..., HA:].reshape(lead + (ng, hb))
    pad = jnp.zeros(lead + (ng, LANES // 2 - hb), t.dtype)
    return jnp.concatenate([beta, pad, dec, pad], axis=-1).reshape(lead + (ng * LANES,))


def kernel(x_prompt, x_sample, cache_conv, state_gdn, cache_k, cache_v, attn_norm, mlp_norm, final_norm, a_w_in, a_conv_w, a_log, a_dt_bias, a_o_norm, a_w_out, kv_norm, w_kv, b_w_q, b_sinks, b_w_o, w_up, w_down):
    bp, lp, d = x_prompt.shape
    bs, ls, _ = x_sample.shape
    mp, ms = bp * lp, bs * ls
    m = mp + ms
    hb = GDN_HEADS_PER_STEP
    dqk, dvw = HA * DK, HA * DV
    conv_ch = 2 * dqk + dvw
    assert a_w_in.shape[0] == 1 and b_w_q.shape[0] == 1 and lp % WINDOW == 0 and mp % WINDOW == 0

    x = jnp.concatenate([x_prompt.reshape(mp, d), x_sample.reshape(ms, d)], axis=0)
    tm = _pick_tile(m, (512, 256, 128, 64, 32, 16))

    w_in = a_w_in[0]
    w_main = w_in[:, :conv_ch + dvw].astype(BF16)
    w_gate = _gate_layout(w_in[:, conv_ch + dvw:], hb).astype(BF16)
    g0 = attn_norm[0].reshape(1, d)
    qkvz = _matmul(x, w_main, gain=g0, tm=tm, tn=_pick_tile(conv_ch + dvw, (1024, 512, 256, 128)),
                   name="gdn_in_proj")
    gates = _matmul(x, w_gate, gain=g0, tm=tm, tn=w_gate.shape[1], name="gdn_gate_proj")
    zeros_gate = jnp.zeros((HA,), F32)
    alog_row = _gate_layout(jnp.concatenate([zeros_gate, a_log[0]]), hb).reshape(1, -1)
    dt_row = _gate_layout(jnp.concatenate([zeros_gate, a_dt_bias[0]]), hb).reshape(1, -1)
    onorm_row = a_o_norm[0].reshape(1, DV)
    conv_w = a_conv_w[0]
    pad_rows = ((0, 0), (SUBLANES - (CONV_W - 1), 0), (0, 0))
    prev_p = jnp.zeros((bp, SUBLANES, conv_ch), F32)
    prev_s = jnp.pad(cache_conv[0], pad_rows)
    o_p, gdn_p = _gdn(qkvz, gates, alog_row, dt_row, prev_p, conv_w, onorm_row,
                      jnp.zeros((bp, HA, DK, DV), F32), row0=0, batch=bp, length=lp,
                      c=min(CHUNK, lp), hb=hb, name="gdn_prompt")
    o_s, gdn_s = _gdn(qkvz, gates, alog_row, dt_row, prev_s, conv_w, onorm_row,
                      state_gdn[0], row0=mp, batch=bs, length=ls,
                      c=min(CHUNK, ls), hb=hb, name="gdn_sample")
    o = jnp.concatenate([o_p, o_s], axis=0)
    tn_d = _pick_tile(d, (1024, 512, 256, 128))
    x = _matmul(o, a_w_out[0].astype(BF16), res=x, tm=tm, tn=tn_d, name="gdn_out_proj")
    tf = _pick_tile(w_up.shape[2], (512, 256, 128))
    x = _mlp(x, mlp_norm[0].reshape(1, d), w_up[0].astype(BF16), w_down[0].astype(BF16),
             tm=tm, tf=tf, name="mlp0")

    pos = jnp.concatenate([jnp.tile(jnp.arange(lp), bp), jnp.tile(PAST_LEN + jnp.arange(ls), bs)])
    rope = _rope_tables(pos)
    kvw = N_KV * HD
    kv = _matmul(x, w_kv.astype(BF16), gain=kv_norm.reshape(1, d), rope=rope, rope_cols=kvw,
                 tm=tm, tn=2 * kvw, name="kv_proj")
    q = _matmul(x, b_w_q[0].astype(BF16), gain=attn_norm[1].reshape(1, d), rope=rope,
                rope_cols=N_Q * HD, out_dtype=BF16, tm=tm, tn=tn_d, name="q_proj")
    sinks = b_sinks[0].astype(F32)
    a_p = _attention(sinks, q, kv, kv, 0, 1, 0, kv, row0=0, batch=bp, length=lp, tq=WINDOW,
                     chunk_mask=True, name="attn_prompt")
    a_s = _attention(sinks, q, cache_k.reshape(bs * WINDOW, kvw), cache_v.reshape(bs * WINDOW, kvw),
                     0, 0, 0, kv, row0=mp, batch=bs, length=ls, tq=ls,
                     chunk_mask=False, name="attn_sample")
    a = jnp.concatenate([a_p, a_s], axis=0)
    x = _matmul(a, b_w_o[0].astype(BF16), res=x, tm=tm, tn=tn_d, name="attn_out_proj")
    y = _mlp(x, mlp_norm[1].reshape(1, d), w_up[1].astype(BF16), w_down[1].astype(BF16),
             final_gain=final_norm.reshape(1, d), tm=tm, tf=tf, name="mlp1")

    y_prompt = y[:mp].reshape(bp, lp, d)
    y_sample = y[mp:].reshape(bs, ls, d)
    keep = CONV_W - 1
    qkv_p = qkvz[:mp, :conv_ch].reshape(bp, lp, conv_ch)
    qkv_s = qkvz[mp:, :conv_ch].reshape(bs, ls, conv_ch)
    conv_p = qkv_p[:, lp - keep:][None]
    conv_s = qkv_s[:, ls - keep:][None]
    kv_p = kv[:mp].reshape(bp, lp, 2 * kvw)[:, lp - WINDOW:]
    kv_s = kv[mp:].reshape(bs, ls, 2 * kvw)
    k_p = kv_p[..., :kvw].reshape(bp, WINDOW, N_KV, HD)
    v_p = kv_p[..., kvw:].reshape(bp, WINDOW, N_KV, HD)
    k_s = kv_s[..., :kvw].reshape(bs, ls, N_KV, HD)
    v_s = kv_s[..., kvw:].reshape(bs, ls, N_KV, HD)
    return (y_prompt, y_sample, conv_p, gdn_p[None], k_p, v_p, conv_s, gdn_s[None], k_s, v_s)
```

```python
import functools
import math

import jax
import jax.numpy as jnp
from jax import lax
from jax.experimental import pallas as pl
from jax.experimental.pallas import tpu as pltpu

F32 = jnp.float32
BF16 = jnp.bfloat16

EPS = 1e-6
CHUNK = 64
WINDOW = 128
PAST_LEN = 4096
ROPE_THETA = 10000.0
HA, DK, DV = 16, 128, 128
N_Q, N_KV, HD = 32, 4, 64
CONV_W = 4
LANES = 128
SUBLANES = 8
VMEM_LIMIT = 56 * 1024 * 1024
GDN_HEADS_PER_STEP = 16
ATTN_SAMPLE_BATCHES_PER_STEP = 4
ROW_TILES = (1088, 512, 256, 128, 64, 32, 16)
COL_TILES = (1024, 512, 256, 128)


def _pick_tile(n, candidates):
    for c in candidates:
        if n % c == 0:
            return c
    return n


def _dot(a, b, dims=(((1,), (0,)), ((), ()))):
    return lax.dot_general(a.astype(BF16), b.astype(BF16), dims, preferred_element_type=F32)


_NT = (((1,), (1,)), ((), ()))
_TN = (((0,), (0,)), ((), ()))


def _rms_scale(x):
    return lax.rsqrt(jnp.mean(x * x, axis=-1, keepdims=True) + EPS)


def _silu(x):
    return x / (1.0 + jnp.exp(-x))


def _rope_slab(y, cos, sin_signed):
    lane = lax.broadcasted_iota(jnp.int32, y.shape, 1)
    up = pltpu.roll(y, 32, 1)
    down = pltpu.roll(y, LANES - 32, 1)
    swapped = jnp.where((lane % HD) < HD // 2, down, up)
    return y * cos + swapped * sin_signed


def _matmul_kernel(*refs, n_x, split_tile, has_norm, has_res, rope_slabs, n_slabs):
    it = iter(refs)
    x_refs = [next(it) for _ in range(n_x)]
    g_ref = next(it) if has_norm else None
    w_ref = next(it)
    res_ref = next(it) if has_res else None
    cos_ref = next(it) if rope_slabs else None
    sin_ref = next(it) if rope_slabs else None
    o_ref = next(it)
    xn_ref = next(it) if has_norm else None

    if has_norm:
        @pl.when(pl.program_id(1) == 0)
        def _():
            x = x_refs[0][...]
            xn_ref[...] = (x * _rms_scale(x) * g_ref[...]).astype(BF16)

    def emit(lhs_ref):
        y = jnp.dot(lhs_ref[...], w_ref[...].astype(BF16), preferred_element_type=F32)
        if has_res:
            y = y + res_ref[...]
        if rope_slabs:
            cos = cos_ref[...]
            sin = sin_ref[...]
            for s in range(n_slabs):
                sl = slice(s * LANES, (s + 1) * LANES)
                ys = y[:, sl]
                if s < rope_slabs:
                    ys = _rope_slab(ys, cos, sin)
                o_ref[:, sl] = ys.astype(o_ref.dtype)
        else:
            o_ref[...] = y.astype(o_ref.dtype)

    if has_norm:
        emit(xn_ref)
    elif n_x == 1:
        emit(x_refs[0])
    else:
        i = pl.program_id(0)
        pl.when(i < split_tile)(functools.partial(emit, x_refs[0]))
        pl.when(i >= split_tile)(functools.partial(emit, x_refs[1]))


def _matmul(xs, w, *, gain=None, res=None, rope=None, rope_cols=0, out_dtype=F32, tm, tn, name):
    xs = xs if isinstance(xs, (tuple, list)) else (xs,)
    k = xs[0].shape[1]
    m = sum(x.shape[0] for x in xs)
    n = (w.shape[1] // tn) * tn
    assert all(x.shape[0] % tm == 0 for x in xs) and w.shape[0] == k
    has_norm = gain is not None
    has_res = res is not None
    assert not (has_norm and len(xs) > 1)
    split_tile = xs[0].shape[0] // tm
    rope_slabs = 0
    if rope is not None:
        assert rope_cols == n or tn == n
        rope_slabs = min(rope_cols, tn) // LANES
    if len(xs) == 1:
        in_specs = [pl.BlockSpec((tm, k), lambda i, j: (i, 0))]
    else:
        in_specs = [pl.BlockSpec((tm, k), lambda i, j: (jnp.minimum(i, split_tile - 1), 0)),
                    pl.BlockSpec((tm, k), lambda i, j: (jnp.maximum(i - split_tile, 0), 0))]
    args = list(xs)
    if has_norm:
        in_specs.append(pl.BlockSpec((1, k), lambda i, j: (0, 0)))
        args.append(gain)
    in_specs.append(pl.BlockSpec((k, tn), lambda i, j: (0, j)))
    args.append(w)
    if has_res:
        in_specs.append(pl.BlockSpec((tm, tn), lambda i, j: (i, j)))
        args.append(res)
    if rope_slabs:
        in_specs += [pl.BlockSpec((tm, LANES), lambda i, j: (i, 0))] * 2
        args += list(rope)
    kern = functools.partial(_matmul_kernel, n_x=len(xs), split_tile=split_tile, has_norm=has_norm,
                             has_res=has_res, rope_slabs=rope_slabs, n_slabs=tn // LANES)
    return pl.pallas_call(
        kern,
        grid=(m // tm, n // tn),
        in_specs=in_specs,
        out_specs=pl.BlockSpec((tm, tn), lambda i, j: (i, j)),
        out_shape=jax.ShapeDtypeStruct((m, n), out_dtype),
        scratch_shapes=[pltpu.VMEM((tm, k), BF16)] if has_norm else [],
        compiler_params=pltpu.CompilerParams(
            dimension_semantics=("parallel", "arbitrary"), vmem_limit_bytes=VMEM_LIMIT),
        name=name,
    )(*args)


def _mlp_kernel(*refs, final_norm):
    if final_norm:
        x_ref, g_ref, wu_ref, wd_ref, fg_ref, o_ref, hn_ref = refs
    else:
        x_ref, g_ref, wu_ref, wd_ref, o_ref, hn_ref = refs
    f = pl.program_id(1)

    @pl.when(f == 0)
    def _():
        x = x_ref[...]
        hn_ref[...] = (x * _rms_scale(x) * g_ref[...]).astype(BF16)
        o_ref[...] = x

    u = jnp.dot(hn_ref[...], wu_ref[...].astype(BF16), preferred_element_type=F32)
    a = jnp.square(jnp.maximum(u, 0.0)).astype(BF16)
    o_ref[...] += jnp.dot(a, wd_ref[...].astype(BF16), preferred_element_type=F32)

    if final_norm:
        @pl.when(f == pl.num_programs(1) - 1)
        def _():
            y = o_ref[...]
            o_ref[...] = y * _rms_scale(y) * fg_ref[...]


def _mlp(x, gain, w_up, w_down, layer, *, final_gain=None, tm, tf, name):
    m, d = x.shape
    dff = w_up.shape[2]
    assert m % tm == 0 and dff % tf == 0
    final_norm = final_gain is not None
    once = pl.Buffered(1)
    in_specs = [
        pl.BlockSpec((tm, d), lambda i, f: (i, 0), pipeline_mode=once),
        pl.BlockSpec((1, d), lambda i, f: (0, 0)),
        pl.BlockSpec((None, d, tf), lambda i, f: (layer, 0, f)),
        pl.BlockSpec((None, tf, d), lambda i, f: (layer, f, 0)),
    ]
    args = [x, gain, w_up, w_down]
    if final_norm:
        in_specs.append(pl.BlockSpec((1, d), lambda i, f: (0, 0)))
        args.append(final_gain)
    return pl.pallas_call(
        functools.partial(_mlp_kernel, final_norm=final_norm),
        grid=(m // tm, dff // tf),
        in_specs=in_specs,
        out_specs=pl.BlockSpec((tm, d), lambda i, f: (i, 0), pipeline_mode=once),
        out_shape=jax.ShapeDtypeStruct((m, d), F32),
        scratch_shapes=[pltpu.VMEM((tm, d), BF16)],
        compiler_params=pltpu.CompilerParams(
            dimension_semantics=("parallel", "arbitrary"), vmem_limit_bytes=VMEM_LIMIT),
        name=name,
    )(*args)


def _gdn_kernel(xq_ref, xk_ref, xv_ref, z_ref, gates_ref, alog_ref, dt_ref,
                pq_ref, pk_ref, pv_ref, wq_ref, wk_ref, wv_ref, onorm_ref, s0_ref,
                o_ref, s_ref, hq_ref, hk_ref, hv_ref, padq_ref, padk_ref, padv_ref, *, c, hb):
    n = pl.program_id(2)
    halo = SUBLANES

    @pl.when(n == 0)
    def _():
        padq_ref[0:halo, :] = pq_ref[0]
        padk_ref[0:halo, :] = pk_ref[0]
        padv_ref[0:halo, :] = pv_ref[0]
        s_ref[...] = s0_ref[...]

    def conv(x_ref, pad_ref, w_ref, hist_ref):
        pad_ref[halo:halo + c, :] = x_ref[...]
        acc = None
        for j in range(CONV_W):
            off = halo - (CONV_W - 1) + j
            term = pad_ref[off:off + c, :] * w_ref[j:j + 1, :]
            acc = term if acc is None else acc + term
        tail = pad_ref[c:c + halo, :]
        pad_ref[0:halo, :] = tail
        hist_ref[0] = tail
        return _silu(acc)

    cq = conv(xq_ref, padq_ref, wq_ref, hq_ref)
    ck = conv(xk_ref, padk_ref, wk_ref, hk_ref)
    cv = conv(xv_ref, padv_ref, wv_ref, hv_ref)

    gates = gates_ref[...]
    beta_all = 1.0 / (1.0 + jnp.exp(-gates))
    ga = gates + dt_ref[...]
    softplus = jnp.maximum(ga, 0.0) + jnp.log(1.0 + jnp.exp(-jnp.abs(ga)))
    g_all = -jnp.exp(alog_ref[...]) * softplus
    row = lax.broadcasted_iota(jnp.int32, (c, c), 0)
    col = lax.broadcasted_iota(jnp.int32, (c, c), 1)
    incl = row >= col
    strict = row > col
    tril = jnp.where(incl, 1.0, 0.0).astype(F32)
    gc_all = jnp.dot(tril, g_all, preferred_element_type=F32,
                     precision=lax.Precision.HIGHEST)
    r128 = lax.broadcasted_iota(jnp.int32, (LANES, LANES), 0)
    c128 = lax.broadcasted_iota(jnp.int32, (LANES, LANES), 1)
    eye128 = jnp.where(r128 == c128, 1.0, 0.0).astype(F32)
    gc_t = lax.dot_general(eye128, gc_all, _NT, preferred_element_type=F32,
                           precision=lax.Precision.HIGHEST)

    heads = range(hb)
    sls = [slice(h * LANES, (h + 1) * LANES) for h in heads]
    q = [cq[:, sl] for sl in sls]
    k = [ck[:, sl] for sl in sls]
    v = [cv[:, sl] for sl in sls]
    q = [x * (lax.rsqrt(jnp.sum(x * x, axis=-1, keepdims=True) + EPS) * (DK ** -0.5)) for x in q]
    k = [x * lax.rsqrt(jnp.sum(x * x, axis=-1, keepdims=True) + EPS) for x in k]
    bcol = [beta_all[:, h:h + 1] for h in heads]
    gcol = [gc_all[:, 64 + h:64 + h + 1] for h in heads]
    grow = [gc_t[64 + h:64 + h + 1, :] for h in heads]
    glast = [gc_all[c - 1:c, 64 + h:64 + h + 1] for h in heads]
    decay = [jnp.exp(jnp.where(incl, gcol[h] - grow[h], -jnp.inf)) for h in heads]
    kb = [k[h] * bcol[h] for h in heads]
    eye = jnp.where(row == col, 1.0, 0.0).astype(F32)
    p = [-jnp.where(strict, _dot(kb[h], k[h], _NT) * decay[h], 0.0) for h in heads]
    t = [eye + x for x in p]
    for _ in range(max(int(math.ceil(math.log2(c))) - 1, 0)):
        p = [_dot(x, x) for x in p]
        t = [t[h] + _dot(t[h], p[h]) for h in heads]
    egc = [jnp.exp(x) for x in gcol]
    rhs = [jnp.concatenate([v[h] * bcol[h], kb[h] * egc[h]], axis=-1) for h in heads]
    sol = [_dot(t[h], rhs[h]) for h in heads]
    qk = [_dot(q[h], k[h], _NT) * decay[h] for h in heads]
    s = [s_ref[0, h] for h in heads]
    v_new = [sol[h][:, :DV] - _dot(sol[h][:, DV:], s[h]) for h in heads]
    o = [_dot(q[h] * egc[h], s[h]) + _dot(qk[h], v_new[h]) for h in heads]
    s_new = [s[h] * jnp.exp(glast[h]) + _dot(k[h] * jnp.exp(glast[h] - gcol[h]), v_new[h], _TN)
             for h in heads]
    onorm = onorm_ref[...]
    for h in heads:
        s_ref[0, h] = s_new[h]
        oh = o[h] * _rms_scale(o[h]) * onorm
        o_ref[:, sls[h]] = (oh * _silu(z_ref[:, sls[h]])).astype(o_ref.dtype)


def _gdn(qkvz, gates, alog_row, dt_row, conv_prev_pad, conv_w, onorm_row, s0,
         *, row0, batch, length, c, hb, name):
    nblk = length // c
    ng = HA // hb
    hw = hb * LANES
    rb0 = row0 // c
    assert row0 % c == 0 and length % c == 0 and c >= SUBLANES

    def rows(b, g, n):
        return rb0 + b * nblk + n

    x_spec = lambda part: pl.BlockSpec((c, hw), lambda b, g, n: (rows(b, g, n), part * ng + g))
    prev_spec = lambda part: pl.BlockSpec((1, SUBLANES, hw), lambda b, g, n: (b, 0, part * ng + g))
    w_spec = lambda part: pl.BlockSpec((CONV_W, hw), lambda b, g, n: (0, part * ng + g))
    gate_row_spec = pl.BlockSpec((1, LANES), lambda b, g, n: (0, g))
    hist_spec = pl.BlockSpec((1, SUBLANES, hw), lambda b, g, n: (b, 0, g))
    hist_shape = jax.ShapeDtypeStruct((batch, SUBLANES, HA * LANES), F32)
    in_specs = [
        x_spec(0), x_spec(1), x_spec(2), x_spec(3),
        pl.BlockSpec((c, LANES), lambda b, g, n: (rows(b, g, n), g)),
        gate_row_spec, gate_row_spec,
        prev_spec(0), prev_spec(1), prev_spec(2),
        w_spec(0), w_spec(1), w_spec(2),
        pl.BlockSpec((1, LANES), lambda b, g, n: (0, 0)),
        pl.BlockSpec((1, hb, DK, DV), lambda b, g, n: (b, g, 0, 0)),
    ]
    out_specs = [
        pl.BlockSpec((c, hw), lambda b, g, n: (b * nblk + n, g)),
        pl.BlockSpec((1, hb, DK, DV), lambda b, g, n: (b, g, 0, 0)),
        hist_spec, hist_spec, hist_spec,
    ]
    return pl.pallas_call(
        functools.partial(_gdn_kernel, c=c, hb=hb),
        grid=(batch, ng, nblk),
        in_specs=in_specs,
        out_specs=out_specs,
        out_shape=[jax.ShapeDtypeStruct((batch * length, HA * DV), BF16),
                   jax.ShapeDtypeStruct((batch, HA, DK, DV), F32),
                   hist_shape, hist_shape, hist_shape],
        scratch_shapes=[pltpu.VMEM((c + SUBLANES, hw), F32)] * 3,
        compiler_params=pltpu.CompilerParams(
            dimension_semantics=("parallel", "parallel", "arbitrary"),
            vmem_limit_bytes=VMEM_LIMIT),
        name=name,
    )(qkvz, qkvz, qkvz, qkvz, gates, alog_row, dt_row,
      conv_prev_pad, conv_prev_pad, conv_prev_pad, conv_w, conv_w, conv_w, onorm_row, s0)


def _attn_kernel(sinks_ref, q_ref, kp_ref, vp_ref, kc_ref, vc_ref, o_ref, *, tq, nb, chunk_mask):
    i = pl.program_id(1)
    nk = WINDOW + tq
    lane_k = lax.broadcasted_iota(jnp.int32, (nk, LANES), 1)
    lane_q = lax.broadcasted_iota(jnp.int32, (tq, LANES), 1)
    if chunk_mask:
        qc = lax.broadcasted_iota(jnp.int32, (tq, nk), 0) // CHUNK
        kc = lax.broadcasted_iota(jnp.int32, (tq, nk), 1) // CHUNK
        w_ch = WINDOW // CHUNK
        first_kc = jnp.where(i > 0, 0, w_ch)
        valid = (kc >= jnp.maximum(qc, first_kc)) & (kc <= qc + w_ch)
    scale = HD ** -0.5
    per_slab = LANES // HD
    group = N_Q // N_KV
    slabs_per_kv = group // per_slab
    half_masks = [(lane_q >= r * HD) & (lane_q < (r + 1) * HD) for r in range(per_slab)]

    units = [(b, h) for b in range(nb) for h in range(N_KV)]
    kdup, vdup, qs = [], [], []
    for b, h in units:
        ksl = slice((h // per_slab) * LANES, (h // per_slab + 1) * LANES)
        kslab = jnp.concatenate([kp_ref[b * WINDOW:(b + 1) * WINDOW, ksl],
                                 kc_ref[b * tq:(b + 1) * tq, ksl]], axis=0)
        vslab = jnp.concatenate([vp_ref[b * WINDOW:(b + 1) * WINDOW, ksl],
                                 vc_ref[b * tq:(b + 1) * tq, ksl]], axis=0)
        first = (lane_k < HD) == (h % per_slab == 0)
        kdup.append(jnp.where(first, kslab, pltpu.roll(kslab, HD, 1)).astype(BF16))
        vdup.append(jnp.where(first, vslab, pltpu.roll(vslab, HD, 1)).astype(BF16))
        pieces = []
        for s in range(slabs_per_kv):
            slab = h * slabs_per_kv + s
            q2 = q_ref[b * tq:(b + 1) * tq, slab * LANES:(slab + 1) * LANES]
            pieces += [jnp.where(mk, q2, jnp.zeros_like(q2)) for mk in half_masks]
        qs.append(jnp.concatenate(pieces, axis=0))
    sc = [lax.dot_general(qs[u], kdup[u], _NT, preferred_element_type=F32) * scale
          for u in range(len(units))]
    ps = []
    for u, (b, h) in enumerate(units):
        blocks = []
        for r in range(group):
            sr = sc[u][r * tq:(r + 1) * tq]
            if chunk_mask:
                sr = jnp.where(valid, sr, -jnp.inf)
            sk = sinks_ref[h * group + r]
            mx = jnp.maximum(jnp.max(sr, axis=-1, keepdims=True), sk)
            e = jnp.exp(sr - mx)
            den = jnp.sum(e, axis=-1, keepdims=True) + jnp.exp(sk - mx)
            blocks.append((e / den).astype(BF16))
        ps.append(jnp.concatenate(blocks, axis=0))
    pv = [jnp.dot(ps[u], vdup[u], preferred_element_type=F32) for u in range(len(units))]
    for u, (b, h) in enumerate(units):
        for s in range(slabs_per_kv):
            slab = h * slabs_per_kv + s
            halves = [pv[u][(s * per_slab + r) * tq:(s * per_slab + r + 1) * tq] for r in range(per_slab)]
            out = halves[-1]
            for r in range(per_slab - 2, -1, -1):
                out = jnp.where(half_masks[r], halves[r], out)
            o_ref[b * tq:(b + 1) * tq, slab * LANES:(slab + 1) * LANES] = out.astype(o_ref.dtype)


def _attention(sinks, q, kprev, vprev, kprev_col, vprev_col, kv, *, row0, batch, length,
               tq, nb, chunk_mask, name):
    nq = length // tq
    kvw = N_KV * HD
    assert row0 % (nb * tq) == 0 and length % tq == 0 and batch % nb == 0 and (nb == 1 or nq == 1)
    rb0 = row0 // (nb * tq)

    def cur(b, i):
        return rb0 + b * nq + i

    def prev(b, i):
        return b * nq + jnp.maximum(i - 1, 0)

    in_specs = [
        pl.BlockSpec(memory_space=pltpu.SMEM),
        pl.BlockSpec((nb * tq, N_Q * HD), lambda b, i: (cur(b, i), 0)),
        pl.BlockSpec((nb * WINDOW, kvw), lambda b, i: (prev(b, i), kprev_col)),
        pl.BlockSpec((nb * WINDOW, kvw), lambda b, i: (prev(b, i), vprev_col)),
        pl.BlockSpec((nb * tq, kvw), lambda b, i: (cur(b, i), 0)),
        pl.BlockSpec((nb * tq, kvw), lambda b, i: (cur(b, i), 1)),
    ]
    return pl.pallas_call(
        functools.partial(_attn_kernel, tq=tq, nb=nb, chunk_mask=chunk_mask),
        grid=(batch // nb, nq),
        in_specs=in_specs,
        out_specs=pl.BlockSpec((nb * tq, N_Q * HD), lambda b, i: (b * nq + i, 0)),
        out_shape=jax.ShapeDtypeStruct((batch * length, N_Q * HD), BF16),
        compiler_params=pltpu.CompilerParams(
            dimension_semantics=("parallel", "arbitrary"), vmem_limit_bytes=VMEM_LIMIT),
        name=name,
    )(sinks, q, kprev, vprev, kv, kv)


def _rope_tables(pos):
    half = HD // 2
    inv = 1.0 / (ROPE_THETA ** (jnp.arange(half, dtype=F32) / half))
    ang = pos.astype(F32)[:, None] * inv[None, :]
    cos = jnp.cos(ang)
    sin = jnp.sin(ang)
    return (jnp.concatenate([cos, cos, cos, cos], axis=-1),
            jnp.concatenate([-sin, sin, -sin, sin], axis=-1))


def _gate_layout(t, hb):
    lead = t.shape[:-1]
    ng = HA // hb
    beta = t[..., :HA].reshape(lead + (ng, hb))
    dec = t[..., HA:].reshape(lead + (ng, hb))
    pad = jnp.zeros(lead + (ng, LANES // 2 - hb), t.dtype)
    return jnp.concatenate([beta, pad, dec, pad], axis=-1).reshape(lead + (ng * LANES,))


def kernel(x_prompt, x_sample, cache_conv, state_gdn, cache_k, cache_v, attn_norm, mlp_norm, final_norm, a_w_in, a_conv_w, a_log, a_dt_bias, a_o_norm, a_w_out, kv_norm, w_kv, b_w_q, b_sinks, b_w_o, w_up, w_down):
    bp, lp, d = x_prompt.shape
    bs, ls, _ = x_sample.shape
    mp, ms = bp * lp, bs * ls
    m = mp + ms
    hb = GDN_HEADS_PER_STEP
    dqk, dvw = HA * DK, HA * DV
    conv_ch = 2 * dqk + dvw
    keep = CONV_W - 1
    assert a_w_in.shape[0] == 1 and b_w_q.shape[0] == 1 and lp % WINDOW == 0 and ls >= keep

    x = jnp.concatenate([x_prompt.reshape(mp, d), x_sample.reshape(ms, d)], axis=0)
    tm = _pick_tile(m, ROW_TILES)
    tm2 = _pick_tile(math.gcd(mp, ms), ROW_TILES[1:])
    tn_d = _pick_tile(d, COL_TILES[1:])

    w_in = a_w_in[0].astype(BF16)
    w_gate = _gate_layout(w_in[:, conv_ch + dvw:], hb)
    g0 = attn_norm[0].reshape(1, d)
    qkvz = _matmul(x, w_in, gain=g0, tm=tm, tn=_pick_tile(conv_ch + dvw, COL_TILES), name="gdn_in_proj")
    gates = _matmul(x, w_gate, gain=g0, tm=tm, tn=w_gate.shape[1], name="gdn_gate_proj")
    zeros_gate = jnp.zeros((HA,), F32)
    alog_row = _gate_layout(jnp.concatenate([zeros_gate, a_log[0]]), hb).reshape(1, -1)
    dt_row = _gate_layout(jnp.concatenate([zeros_gate, a_dt_bias[0]]), hb).reshape(1, -1)
    onorm_row = a_o_norm[0].reshape(1, DV)
    conv_w = a_conv_w[0]
    prev_p = jnp.zeros((bp, SUBLANES, conv_ch), F32)
    prev_s = jnp.pad(cache_conv[0], ((0, 0), (SUBLANES - keep, 0), (0, 0)))
    o_p, gdn_p, *hist_p = _gdn(qkvz, gates, alog_row, dt_row, prev_p, conv_w, onorm_row,
                               jnp.zeros((bp, HA, DK, DV), F32), row0=0, batch=bp, length=lp,
                               c=min(CHUNK, lp), hb=hb, name="gdn_prompt")
    o_s, gdn_s, *hist_s = _gdn(qkvz, gates, alog_row, dt_row, prev_s, conv_w, onorm_row,
                               state_gdn[0], row0=mp, batch=bs, length=ls,
                               c=min(CHUNK, ls), hb=hb, name="gdn_sample")
    x = _matmul((o_p, o_s), a_w_out[0], res=x, tm=tm2, tn=tn_d, name="gdn_out_proj")
    tf = _pick_tile(w_up.shape[2], COL_TILES[1:])
    x = _mlp(x, mlp_norm[0].reshape(1, d), w_up, w_down, 0, tm=tm, tf=tf, name="mlp0")

    pos = jnp.concatenate([jnp.tile(jnp.arange(lp), bp), jnp.tile(PAST_LEN + jnp.arange(ls), bs)])
    rope = _rope_tables(pos)
    kvw = N_KV * HD
    kv = _matmul(x, w_kv, gain=kv_norm.reshape(1, d), rope=rope, rope_cols=kvw,
                 tm=tm, tn=2 * kvw, name="kv_proj")
    q = _matmul(x, b_w_q[0], gain=attn_norm[1].reshape(1, d), rope=rope,
                rope_cols=N_Q * HD, out_dtype=BF16, tm=tm, tn=tn_d, name="q_proj")
    sinks = b_sinks[0].astype(F32)
    a_p = _attention(sinks, q, kv, kv, 0, 1, kv, row0=0, batch=bp, length=lp, tq=WINDOW, nb=1,
                     chunk_mask=True, name="attn_prompt")
    nb = _pick_tile(bs, (ATTN_SAMPLE_BATCHES_PER_STEP, 2, 1))
    a_s = _attention(sinks, q, cache_k.reshape(bs * WINDOW, kvw), cache_v.reshape(bs * WINDOW, kvw),
                     0, 0, kv, row0=mp, batch=bs, length=ls, tq=ls, nb=nb,
                     chunk_mask=False, name="attn_sample")
    x = _matmul((a_p, a_s), b_w_o[0], res=x, tm=tm2, tn=tn_d, name="attn_out_proj")
    y = _mlp(x, mlp_norm[1].reshape(1, d), w_up, w_down, 1, final_gain=final_norm.reshape(1, d),
             tm=tm, tf=tf, name="mlp1")

    y_prompt = y[:mp].reshape(bp, lp, d)
    y_sample = y[mp:].reshape(bs, ls, d)
    conv_p = jnp.concatenate([t[:, SUBLANES - keep:] for t in hist_p], axis=-1)[None]
    conv_s = jnp.concatenate([t[:, SUBLANES - keep:] for t in hist_s], axis=-1)[None]
    kv_p = kv[:mp].reshape(bp, lp, 2 * kvw)[:, lp - WINDOW:]
    kv_s = kv[mp:].reshape(bs, ls, 2 * kvw)
    k_p = kv_p[..., :kvw].reshape(bp, WINDOW, N_KV, HD)
    v_p = kv_p[..., kvw:].reshape(bp, WINDOW, N_KV, HD)
    k_s = kv_s[..., :kvw].reshape(bs, ls, N_KV, HD)
    v_s = kv_s[..., kvw:].reshape(bs, ls, N_KV, HD)
    return (y_prompt, y_sample, conv_p, gdn_p[None], k_p, v_p, conv_s, gdn_s[None], k_s, v_s)
```

```python
import functools
import math

import jax
import jax.numpy as jnp
from jax import lax
from jax.experimental import pallas as pl
from jax.experimental.pallas import tpu as pltpu

F32 = jnp.float32
BF16 = jnp.bfloat16

EPS = 1e-6
CHUNK = 64
WINDOW = 128
PAST_LEN = 4096
ROPE_THETA = 10000.0
HA, DK, DV = 16, 128, 128
N_Q, N_KV, HD = 32, 4, 64
CONV_W = 4
LANES = 128
SUBLANES = 8
VMEM_LIMIT = 56 * 1024 * 1024
MLP_VMEM_LIMIT = 60 * 1024 * 1024
GDN_HEADS_PER_STEP = 16
ATTN_SAMPLE_BATCHES_PER_STEP = 4
ROW_TILES = (1088, 512, 256, 128, 64, 32, 16)
COL_TILES = (1024, 512, 256, 128)


def _pick_tile(n, candidates):
    for c in candidates:
        if n % c == 0:
            return c
    return n


def _dot(a, b, dims=(((1,), (0,)), ((), ()))):
    return lax.dot_general(a.astype(BF16), b.astype(BF16), dims, preferred_element_type=F32)


_NT = (((1,), (1,)), ((), ()))
_TN = (((0,), (0,)), ((), ()))


def _rms_scale(x):
    return lax.rsqrt(jnp.mean(x * x, axis=-1, keepdims=True) + EPS)


def _silu(x):
    return x / (1.0 + jnp.exp(-x))


def _rope_slab(y, cos, sin_signed):
    lane = lax.broadcasted_iota(jnp.int32, y.shape, 1)
    up = pltpu.roll(y, 32, 1)
    down = pltpu.roll(y, LANES - 32, 1)
    swapped = jnp.where((lane % HD) < HD // 2, down, up)
    return y * cos + swapped * sin_signed


def _matmul_kernel(*refs, n_x, split_tile, row_axis, has_norm, has_res, rope_slabs, n_slabs, n_chunk):
    it = iter(refs)
    x_refs = [next(it) for _ in range(n_x)]
    g_ref = next(it) if has_norm else None
    w_ref = next(it)
    res_ref = next(it) if has_res else None
    cos_ref = next(it) if rope_slabs else None
    sin_ref = next(it) if rope_slabs else None
    o_ref = next(it)
    xn_ref = next(it) if has_norm else None

    if has_norm:
        @pl.when(pl.program_id(1) == 0)
        def _():
            x = x_refs[0][...]
            xn_ref[...] = (x * _rms_scale(x) * g_ref[...]).astype(BF16)

    def emit(lhs_ref):
        lhs = lhs_ref[...]
        if not rope_slabs:
            y = jnp.dot(lhs, w_ref[...].astype(BF16), preferred_element_type=F32)
            if has_res:
                y = y + res_ref[...]
            o_ref[...] = y.astype(o_ref.dtype)
            return
        cos = cos_ref[...]
        sin = sin_ref[...]
        for c0 in range(0, n_slabs, n_chunk):
            cols = slice(c0 * LANES, (c0 + n_chunk) * LANES)
            y = jnp.dot(lhs, w_ref[:, cols].astype(BF16), preferred_element_type=F32)
            if has_res:
                y = y + res_ref[:, cols]
            for s in range(n_chunk):
                ys = y[:, s * LANES:(s + 1) * LANES]
                if c0 + s < rope_slabs:
                    ys = _rope_slab(ys, cos, sin)
                o_ref[:, (c0 + s) * LANES:(c0 + s + 1) * LANES] = ys.astype(o_ref.dtype)

    if has_norm:
        emit(xn_ref)
    elif n_x == 1:
        emit(x_refs[0])
    else:
        i = pl.program_id(row_axis)
        pl.when(i < split_tile)(functools.partial(emit, x_refs[0]))
        pl.when(i >= split_tile)(functools.partial(emit, x_refs[1]))


def _matmul(xs, w, *, gain=None, res=None, rope=None, rope_cols=0, out_dtype=F32, tm, tn, name,
            weights_outer=False):
    xs = xs if isinstance(xs, (tuple, list)) else (xs,)
    k = xs[0].shape[1]
    m = sum(x.shape[0] for x in xs)
    n = (w.shape[1] // tn) * tn
    assert all(x.shape[0] % tm == 0 for x in xs) and w.shape[0] == k
    has_norm = gain is not None
    has_res = res is not None
    assert not (has_norm and len(xs) > 1)
    split_tile = xs[0].shape[0] // tm
    rope_slabs = 0
    if rope is not None:
        assert rope_cols == n or tn == n
        rope_slabs = min(rope_cols, tn) // LANES
    assert not (has_norm and weights_outer)

    def spec(shape, f):
        return pl.BlockSpec(shape, (lambda a, b: f(b, a)) if weights_outer else f)

    if len(xs) == 1:
        in_specs = [spec((tm, k), lambda i, j: (i, 0))]
    else:
        in_specs = [spec((tm, k), lambda i, j: (jnp.minimum(i, split_tile - 1), 0)),
                    spec((tm, k), lambda i, j: (jnp.maximum(i - split_tile, 0), 0))]
    args = list(xs)
    if has_norm:
        in_specs.append(spec((1, k), lambda i, j: (0, 0)))
        args.append(gain)
    in_specs.append(spec((k, tn), lambda i, j: (0, j)))
    args.append(w)
    if has_res:
        in_specs.append(spec((tm, tn), lambda i, j: (i, j)))
        args.append(res)
    if rope_slabs:
        in_specs += [spec((tm, LANES), lambda i, j: (i, 0))] * 2
        args += list(rope)
    n_slabs = tn // LANES
    kern = functools.partial(_matmul_kernel, n_x=len(xs), split_tile=split_tile,
                             row_axis=1 if weights_outer else 0, has_norm=has_norm, has_res=has_res,
                             rope_slabs=rope_slabs, n_slabs=n_slabs, n_chunk=2 if n_slabs % 2 == 0 else 1)
    grid = (n // tn, m // tm) if weights_outer else (m // tm, n // tn)
    return pl.pallas_call(
        kern,
        grid=grid,
        in_specs=in_specs,
        out_specs=spec((tm, tn), lambda i, j: (i, j)),
        out_shape=jax.ShapeDtypeStruct((m, n), out_dtype),
        scratch_shapes=[pltpu.VMEM((tm, k), BF16)] if has_norm else [],
        compiler_params=pltpu.CompilerParams(
            dimension_semantics=("parallel", "arbitrary"), vmem_limit_bytes=VMEM_LIMIT),
        name=name,
    )(*args)


def _mlp_kernel(*refs, final_norm):
    if final_norm:
        x_ref, g_ref, wu_ref, wd_ref, fg_ref, o_ref, hn_ref = refs
    else:
        x_ref, g_ref, wu_ref, wd_ref, o_ref, hn_ref = refs
    f = pl.program_id(1)

    @pl.when(f == 0)
    def _():
        x = x_ref[...]
        hn_ref[...] = (x * _rms_scale(x) * g_ref[...]).astype(BF16)
        o_ref[...] = x

    u = jnp.dot(hn_ref[...], wu_ref[...].astype(BF16), preferred_element_type=F32)
    a = jnp.square(jnp.maximum(u, 0.0)).astype(BF16)
    o_ref[...] += jnp.dot(a, wd_ref[...].astype(BF16), preferred_element_type=F32)

    if final_norm:
        @pl.when(f == pl.num_programs(1) - 1)
        def _():
            y = o_ref[...]
            o_ref[...] = y * _rms_scale(y) * fg_ref[...]


def _mlp(x, gain, w_up, w_down, layer, *, final_gain=None, tm, tf, name):
    m, d = x.shape
    dff = w_up.shape[2]
    assert m % tm == 0 and dff % tf == 0
    final_norm = final_gain is not None
    in_specs = [
        pl.BlockSpec((tm, d), lambda i, f: (i, 0)),
        pl.BlockSpec((1, d), lambda i, f: (0, 0)),
        pl.BlockSpec((None, d, tf), lambda i, f: (layer, 0, f)),
        pl.BlockSpec((None, tf, d), lambda i, f: (layer, f, 0)),
    ]
    args = [x, gain, w_up, w_down]
    if final_norm:
        in_specs.append(pl.BlockSpec((1, d), lambda i, f: (0, 0)))
        args.append(final_gain)
    return pl.pallas_call(
        functools.partial(_mlp_kernel, final_norm=final_norm),
        grid=(m // tm, dff // tf),
        in_specs=in_specs,
        out_specs=pl.BlockSpec((tm, d), lambda i, f: (i, 0)),
        out_shape=jax.ShapeDtypeStruct((m, d), F32),
        scratch_shapes=[pltpu.VMEM((tm, d), BF16)],
        compiler_params=pltpu.CompilerParams(
            dimension_semantics=("parallel", "arbitrary"), vmem_limit_bytes=MLP_VMEM_LIMIT),
        name=name,
    )(*args)


def _gdn_kernel(xq_ref, xk_ref, xv_ref, z_ref, gates_ref, alog_ref, dt_ref,
                pq_ref, pk_ref, pv_ref, wq_ref, wk_ref, wv_ref, onorm_ref, s0_ref,
                o_ref, s_ref, hq_ref, hk_ref, hv_ref, padq_ref, padk_ref, padv_ref, *, c, hb):
    n = pl.program_id(2)
    halo = SUBLANES

    @pl.when(n == 0)
    def _():
        padq_ref[0:halo, :] = pq_ref[0]
        padk_ref[0:halo, :] = pk_ref[0]
        padv_ref[0:halo, :] = pv_ref[0]
        s_ref[...] = s0_ref[...]

    def conv(x_ref, pad_ref, w_ref, hist_ref):
        pad_ref[halo:halo + c, :] = x_ref[...]
        xp = pad_ref[...]
        x1 = pltpu.roll(xp, 1, 0)
        a = xp * w_ref[3:4, :] + x1 * w_ref[2:3, :]
        b = xp * w_ref[1:2, :] + x1 * w_ref[0:1, :]
        acc = (a + pltpu.roll(b, 2, 0))[halo:, :]
        tail = xp[c:c + halo, :]
        pad_ref[0:halo, :] = tail
        hist_ref[0] = tail
        return _silu(acc)

    cq = conv(xq_ref, padq_ref, wq_ref, hq_ref)
    ck = conv(xk_ref, padk_ref, wk_ref, hk_ref)
    cv = conv(xv_ref, padv_ref, wv_ref, hv_ref)

    gates = gates_ref[...]
    beta_all = 1.0 / (1.0 + jnp.exp(-gates))
    ga = gates + dt_ref[...]
    softplus = jnp.maximum(ga, 0.0) + jnp.log(1.0 + jnp.exp(-jnp.abs(ga)))
    g_all = -jnp.exp(alog_ref[...]) * softplus
    row = lax.broadcasted_iota(jnp.int32, (c, c), 0)
    col = lax.broadcasted_iota(jnp.int32, (c, c), 1)
    incl = row >= col
    strict = row > col
    tril = jnp.where(incl, 1.0, 0.0).astype(F32)
    gc_all = jnp.dot(tril, g_all, preferred_element_type=F32,
                     precision=lax.Precision.HIGHEST)
    r128 = lax.broadcasted_iota(jnp.int32, (LANES, LANES), 0)
    c128 = lax.broadcasted_iota(jnp.int32, (LANES, LANES), 1)
    eye128 = jnp.where(r128 == c128, 1.0, 0.0).astype(F32)
    gc_t = lax.dot_general(eye128, gc_all, _NT, preferred_element_type=F32,
                           precision=lax.Precision.HIGHEST)

    heads = range(hb)
    sls = [slice(h * LANES, (h + 1) * LANES) for h in heads]
    q = [cq[:, sl] for sl in sls]
    k = [ck[:, sl] for sl in sls]
    v = [cv[:, sl] for sl in sls]
    q = [x * (lax.rsqrt(jnp.sum(x * x, axis=-1, keepdims=True) + EPS) * (DK ** -0.5)) for x in q]
    k = [x * lax.rsqrt(jnp.sum(x * x, axis=-1, keepdims=True) + EPS) for x in k]
    bcol = [beta_all[:, h:h + 1] for h in heads]
    gcol = [gc_all[:, 64 + h:64 + h + 1] for h in heads]
    grow = [gc_t[64 + h:64 + h + 1, :] for h in heads]
    glast = [gc_all[c - 1:c, 64 + h:64 + h + 1] for h in heads]
    decay = [jnp.exp(jnp.where(incl, gcol[h] - grow[h], -jnp.inf)) for h in heads]
    kb = [k[h] * bcol[h] for h in heads]
    eye = jnp.where(row == col, 1.0, 0.0).astype(F32)
    p = [-jnp.where(strict, _dot(kb[h], k[h], _NT) * decay[h], 0.0) for h in heads]
    t = [eye + x for x in p]
    for _ in range(max(int(math.ceil(math.log2(c))) - 1, 0)):
        p = [_dot(x, x) for x in p]
        t = [t[h] + _dot(t[h], p[h]) for h in heads]
    egc = [jnp.exp(x) for x in gcol]
    rhs = [jnp.concatenate([v[h] * bcol[h], kb[h] * egc[h]], axis=-1) for h in heads]
    sol = [_dot(t[h], rhs[h]) for h in heads]
    qk = [_dot(q[h], k[h], _NT) * decay[h] for h in heads]
    s = [s_ref[0, h] for h in heads]
    v_new = [sol[h][:, :DV] - _dot(sol[h][:, DV:], s[h]) for h in heads]
    o = [_dot(q[h] * egc[h], s[h]) + _dot(qk[h], v_new[h]) for h in heads]
    s_new = [s[h] * jnp.exp(glast[h]) + _dot(k[h] * jnp.exp(glast[h] - gcol[h]), v_new[h], _TN)
             for h in heads]
    onorm = onorm_ref[...]
    for h in heads:
        s_ref[0, h] = s_new[h]
        oh = o[h] * _rms_scale(o[h]) * onorm
        o_ref[:, sls[h]] = (oh * _silu(z_ref[:, sls[h]])).astype(o_ref.dtype)


def _gdn(qkvz, gates, alog_row, dt_row, conv_prev_pad, conv_w, onorm_row, s0,
         *, row0, batch, length, c, hb, name):
    nblk = length // c
    ng = HA // hb
    hw = hb * LANES
    rb0 = row0 // c
    assert row0 % c == 0 and length % c == 0 and c >= SUBLANES and CONV_W == 4

    def rows(b, g, n):
        return rb0 + b * nblk + n

    x_spec = lambda part: pl.BlockSpec((c, hw), lambda b, g, n: (rows(b, g, n), part * ng + g))
    prev_spec = lambda part: pl.BlockSpec((1, SUBLANES, hw), lambda b, g, n: (b, 0, part * ng + g))
    w_spec = lambda part: pl.BlockSpec((CONV_W, hw), lambda b, g, n: (0, part * ng + g))
    gate_row_spec = pl.BlockSpec((1, LANES), lambda b, g, n: (0, g))
    hist_spec = pl.BlockSpec((1, SUBLANES, hw), lambda b, g, n: (b, 0, g))
    hist_shape = jax.ShapeDtypeStruct((batch, SUBLANES, HA * LANES), F32)
    in_specs = [
        x_spec(0), x_spec(1), x_spec(2), x_spec(3),
        pl.BlockSpec((c, LANES), lambda b, g, n: (rows(b, g, n), g)),
        gate_row_spec, gate_row_spec,
        prev_spec(0), prev_spec(1), prev_spec(2),
        w_spec(0), w_spec(1), w_spec(2),
        pl.BlockSpec((1, LANES), lambda b, g, n: (0, 0)),
        pl.BlockSpec((1, hb, DK, DV), lambda b, g, n: (b, g, 0, 0)),
    ]
    out_specs = [
        pl.BlockSpec((c, hw), lambda b, g, n: (b * nblk + n, g)),
        pl.BlockSpec((1, hb, DK, DV), lambda b, g, n: (b, g, 0, 0)),
        hist_spec, hist_spec, hist_spec,
    ]
    return pl.pallas_call(
        functools.partial(_gdn_kernel, c=c, hb=hb),
        grid=(batch, ng, nblk),
        in_specs=in_specs,
        out_specs=out_specs,
        out_shape=[jax.ShapeDtypeStruct((batch * length, HA * DV), BF16),
                   jax.ShapeDtypeStruct((batch, HA, DK, DV), F32),
                   hist_shape, hist_shape, hist_shape],
        scratch_shapes=[pltpu.VMEM((c + SUBLANES, hw), F32)] * 3,
        compiler_params=pltpu.CompilerParams(
            dimension_semantics=("parallel", "parallel", "arbitrary"),
            vmem_limit_bytes=VMEM_LIMIT),
        name=name,
    )(qkvz, qkvz, qkvz, qkvz, gates, alog_row, dt_row,
      conv_prev_pad, conv_prev_pad, conv_prev_pad, conv_w, conv_w, conv_w, onorm_row, s0)


def _attn_kernel(sinks_ref, q_ref, kp_ref, vp_ref, kc_ref, vc_ref, o_ref, *, tq, nb, chunk_mask):
    i = pl.program_id(1)
    nk = WINDOW + tq
    lane_k = lax.broadcasted_iota(jnp.int32, (nk, LANES), 1)
    lane_q = lax.broadcasted_iota(jnp.int32, (tq, LANES), 1)
    if chunk_mask:
        qc = lax.broadcasted_iota(jnp.int32, (tq, nk), 0) // CHUNK
        kc = lax.broadcasted_iota(jnp.int32, (tq, nk), 1) // CHUNK
        w_ch = WINDOW // CHUNK
        first_kc = jnp.where(i > 0, 0, w_ch)
        valid = (kc >= jnp.maximum(qc, first_kc)) & (kc <= qc + w_ch)
    scale = HD ** -0.5
    assert math.log2(HD) % 2 == 0
    per_slab = LANES // HD
    group = N_Q // N_KV
    slabs_per_kv = group // per_slab
    half_masks = [(lane_q >= r * HD) & (lane_q < (r + 1) * HD) for r in range(per_slab)]

    units = [(b, h) for b in range(nb) for h in range(N_KV)]
    kdup, vdup, qs = [], [], []
    for b, h in units:
        ksl = slice((h // per_slab) * LANES, (h // per_slab + 1) * LANES)
        kslab = jnp.concatenate([kp_ref[b * WINDOW:(b + 1) * WINDOW, ksl],
                                 kc_ref[b * tq:(b + 1) * tq, ksl]], axis=0)
        vslab = jnp.concatenate([vp_ref[b * WINDOW:(b + 1) * WINDOW, ksl],
                                 vc_ref[b * tq:(b + 1) * tq, ksl]], axis=0)
        first = (lane_k < HD) == (h % per_slab == 0)
        kdup.append(jnp.where(first, kslab, pltpu.roll(kslab, HD, 1)).astype(BF16))
        vdup.append(jnp.where(first, vslab, pltpu.roll(vslab, HD, 1)).astype(BF16))
        pieces = []
        for s in range(slabs_per_kv):
            slab = h * slabs_per_kv + s
            q2 = q_ref[b * tq:(b + 1) * tq, slab * LANES:(slab + 1) * LANES]
            pieces += [jnp.where(mk, q2, jnp.zeros_like(q2)) for mk in half_masks]
        qs.append(jnp.concatenate(pieces, axis=0) * scale)
    sc = [lax.dot_general(qs[u], kdup[u], _NT, preferred_element_type=F32) for u in range(len(units))]
    ps = []
    for u, (b, h) in enumerate(units):
        blocks = []
        for r in range(group):
            sr = sc[u][r * tq:(r + 1) * tq]
            if chunk_mask:
                sr = jnp.where(valid, sr, -jnp.inf)
            sk = sinks_ref[h * group + r]
            mx = jnp.maximum(jnp.max(sr, axis=-1, keepdims=True), sk)
            e = jnp.exp(sr - mx)
            den = jnp.sum(e, axis=-1, keepdims=True) + jnp.exp(sk - mx)
            blocks.append((e / den).astype(BF16))
        ps.append(jnp.concatenate(blocks, axis=0))
    pv = [jnp.dot(ps[u], vdup[u], preferred_element_type=F32) for u in range(len(units))]
    for u, (b, h) in enumerate(units):
        for s in range(slabs_per_kv):
            slab = h * slabs_per_kv + s
            halves = [pv[u][(s * per_slab + r) * tq:(s * per_slab + r + 1) * tq] for r in range(per_slab)]
            out = halves[-1]
            for r in range(per_slab - 2, -1, -1):
                out = jnp.where(half_masks[r], halves[r], out)
            o_ref[b * tq:(b + 1) * tq, slab * LANES:(slab + 1) * LANES] = out.astype(o_ref.dtype)


def _attention(sinks, q, kprev, vprev, kprev_col, vprev_col, kv, *, row0, batch, length,
               tq, nb, chunk_mask, name):
    nq = length // tq
    kvw = N_KV * HD
    assert row0 % (nb * tq) == 0 and length % tq == 0 and batch % nb == 0 and (nb == 1 or nq == 1)
    rb0 = row0 // (nb * tq)

    def cur(b, i):
        return rb0 + b * nq + i

    def prev(b, i):
        return b * nq + jnp.maximum(i - 1, 0)

    in_specs = [
        pl.BlockSpec(memory_space=pltpu.SMEM),
        pl.BlockSpec((nb * tq, N_Q * HD), lambda b, i: (cur(b, i), 0)),
        pl.BlockSpec((nb * WINDOW, kvw), lambda b, i: (prev(b, i), kprev_col)),
        pl.BlockSpec((nb * WINDOW, kvw), lambda b, i: (prev(b, i), vprev_col)),
        pl.BlockSpec((nb * tq, kvw), lambda b, i: (cur(b, i), 0)),
        pl.BlockSpec((nb * tq, kvw), lambda b, i: (cur(b, i), 1)),
    ]
    return pl.pallas_call(
        functools.partial(_attn_kernel, tq=tq, nb=nb, chunk_mask=chunk_mask),
        grid=(batch // nb, nq),
        in_specs=in_specs,
        out_specs=pl.BlockSpec((nb * tq, N_Q * HD), lambda b, i: (b * nq + i, 0)),
        out_shape=jax.ShapeDtypeStruct((batch * length, N_Q * HD), BF16),
        compiler_params=pltpu.CompilerParams(
            dimension_semantics=("parallel", "arbitrary"), vmem_limit_bytes=VMEM_LIMIT),
        name=name,
    )(sinks, q, kprev, vprev, kv, kv)


def _rope_tables(pos):
    half = HD // 2
    inv = 1.0 / (ROPE_THETA ** (jnp.arange(half, dtype=F32) / half))
    ang = pos.astype(F32)[:, None] * inv[None, :]
    cos = jnp.cos(ang)
    sin = jnp.sin(ang)
    return (jnp.concatenate([cos, cos, cos, cos], axis=-1),
            jnp.concatenate([-sin, sin, -sin, sin], axis=-1))


def _gate_layout(t, hb):
    lead = t.shape[:-1]
    ng = HA // hb
    beta = t[..., :HA].reshape(lead + (ng, hb))
    dec = t[..., HA:].reshape(lead + (ng, hb))
    pad = jnp.zeros(lead + (ng, LANES // 2 - hb), t.dtype)
    return jnp.concatenate([beta, pad, dec, pad], axis=-1).reshape(lead + (ng * LANES,))


def kernel(x_prompt, x_sample, cache_conv, state_gdn, cache_k, cache_v, attn_norm, mlp_norm, final_norm, a_w_in, a_conv_w, a_log, a_dt_bias, a_o_norm, a_w_out, kv_norm, w_kv, b_w_q, b_sinks, b_w_o, w_up, w_down):
    bp, lp, d = x_prompt.shape
    bs, ls, _ = x_sample.shape
    mp, ms = bp * lp, bs * ls
    m = mp + ms
    hb = GDN_HEADS_PER_STEP
    dqk, dvw = HA * DK, HA * DV
    conv_ch = 2 * dqk + dvw
    keep = CONV_W - 1
    assert a_w_in.shape[0] == 1 and b_w_q.shape[0] == 1 and lp % WINDOW == 0 and ls >= keep

    x = jnp.concatenate([x_prompt.reshape(mp, d), x_sample.reshape(ms, d)], axis=0)
    tm = _pick_tile(m, ROW_TILES)
    tm2 = _pick_tile(math.gcd(mp, ms), ROW_TILES[1:])
    tn_d = _pick_tile(d, COL_TILES[1:])

    w_in = a_w_in[0].astype(BF16)
    w_gate = _gate_layout(w_in[:, conv_ch + dvw:], hb)
    g0 = attn_norm[0].reshape(1, d)
    qkvz = _matmul(x, w_in, gain=g0, tm=tm, tn=_pick_tile(conv_ch + dvw, COL_TILES), name="gdn_in_proj")
    gates = _matmul(x, w_gate, gain=g0, tm=tm, tn=w_gate.shape[1], name="gdn_gate_proj")
    zeros_gate = jnp.zeros((HA,), F32)
    alog_row = _gate_layout(jnp.concatenate([zeros_gate, a_log[0]]), hb).reshape(1, -1)
    dt_row = _gate_layout(jnp.concatenate([zeros_gate, a_dt_bias[0]]), hb).reshape(1, -1)
    onorm_row = a_o_norm[0].reshape(1, DV)
    conv_w = a_conv_w[0]
    prev_p = jnp.zeros((bp, SUBLANES, conv_ch), F32)
    prev_s = jnp.pad(cache_conv[0], ((0, 0), (SUBLANES - keep, 0), (0, 0)))
    o_p, gdn_p, *hist_p = _gdn(qkvz, gates, alog_row, dt_row, prev_p, conv_w, onorm_row,
                               jnp.zeros((bp, HA, DK, DV), F32), row0=0, batch=bp, length=lp,
                               c=min(CHUNK, lp), hb=hb, name="gdn_prompt")
    o_s, gdn_s, *hist_s = _gdn(qkvz, gates, alog_row, dt_row, prev_s, conv_w, onorm_row,
                               state_gdn[0], row0=mp, batch=bs, length=ls,
                               c=min(CHUNK, ls), hb=hb, name="gdn_sample")
    tn_o = _pick_tile(d, COL_TILES)
    x = _matmul((o_p, o_s), a_w_out[0], res=x, tm=tm2, tn=tn_o, weights_outer=True, name="gdn_out_proj")
    tf = _pick_tile(w_up.shape[2], COL_TILES[1:])
    x = _mlp(x, mlp_norm[0].reshape(1, d), w_up, w_down, 0, tm=tm, tf=tf, name="mlp0")

    pos = jnp.concatenate([jnp.tile(jnp.arange(lp), bp), jnp.tile(PAST_LEN + jnp.arange(ls), bs)])
    rope = _rope_tables(pos)
    kvw = N_KV * HD
    kv = _matmul(x, w_kv, gain=kv_norm.reshape(1, d), rope=rope, rope_cols=kvw,
                 tm=tm, tn=2 * kvw, name="kv_proj")
    q = _matmul(x, b_w_q[0], gain=attn_norm[1].reshape(1, d), rope=rope,
                rope_cols=N_Q * HD, out_dtype=BF16, tm=tm, tn=tn_d, name="q_proj")
    sinks = b_sinks[0].astype(F32)
    a_p = _attention(sinks, q, kv, kv, 0, 1, kv, row0=0, batch=bp, length=lp, tq=WINDOW, nb=1,
                     chunk_mask=True, name="attn_prompt")
    nb = _pick_tile(bs, (ATTN_SAMPLE_BATCHES_PER_STEP, 2, 1))
    a_s = _attention(sinks, q, cache_k.reshape(bs * WINDOW, kvw), cache_v.reshape(bs * WINDOW, kvw),
                     0, 0, kv, row0=mp, batch=bs, length=ls, tq=ls, nb=nb,
                     chunk_mask=False, name="attn_sample")
    x = _matmul((a_p, a_s), b_w_o[0], res=x, tm=tm2, tn=tn_o, weights_outer=True, name="attn_out_proj")
    y = _mlp(x, mlp_norm[1].reshape(1, d), w_up, w_down, 1, final_gain=final_norm.reshape(1, d),
             tm=tm, tf=tf, name="mlp1")

    y_prompt = y[:mp].reshape(bp, lp, d)
    y_sample = y[mp:].reshape(bs, ls, d)
    conv_p = jnp.concatenate([t[:, SUBLANES - keep:] for t in hist_p], axis=-1)[None]
    conv_s = jnp.concatenate([t[:, SUBLANES - keep:] for t in hist_s], axis=-1)[None]
    kv_p = kv[:mp].reshape(bp, lp, 2 * kvw)[:, lp - WINDOW:]
    kv_s = kv[mp:].reshape(bs, ls, 2 * kvw)
    k_p = kv_p[..., :kvw].reshape(bp, WINDOW, N_KV, HD)
    v_p = kv_p[..., kvw:].reshape(bp, WINDOW, N_KV, HD)
    k_s = kv_s[..., :kvw].reshape(bs, ls, N_KV, HD)
    v_s = kv_s[..., kvw:].reshape(bs, ls, N_KV, HD)
    return (y_prompt, y_sample, conv_p, gdn_p[None], k_p, v_p, conv_s, gdn_s[None], k_s, v_s)
```

```python
import functools
import math

import jax
import jax.numpy as jnp
from jax import lax
from jax.experimental import pallas as pl
from jax.experimental.pallas import tpu as pltpu

F32 = jnp.float32
BF16 = jnp.bfloat16

EPS = 1e-6
CHUNK = 64
WINDOW = 128
PAST_LEN = 4096
ROPE_THETA = 10000.0
HA, DK, DV = 16, 128, 128
N_Q, N_KV, HD = 32, 4, 64
CONV_W = 4
LANES = 128
SUBLANES = 8
VMEM_LIMIT = 56 * 1024 * 1024
MLP_VMEM_LIMIT = 60 * 1024 * 1024
GDN_HEADS_PER_STEP = 16
ATTN_SAMPLE_BATCHES_PER_STEP = 4
ROW_TILES = (1088, 512, 256, 128, 64, 32, 16)
COL_TILES = (1024, 512, 256, 128)


def _pick_tile(n, candidates):
    for c in candidates:
        if n % c == 0:
            return c
    return n


def _dot(a, b, dims=(((1,), (0,)), ((), ()))):
    return lax.dot_general(a.astype(BF16), b.astype(BF16), dims, preferred_element_type=F32)


_NT = (((1,), (1,)), ((), ()))
_TN = (((0,), (0,)), ((), ()))


def _rms_scale(x):
    return lax.rsqrt(jnp.mean(x * x, axis=-1, keepdims=True) + EPS)


def _silu(x):
    return x / (1.0 + jnp.exp(-x))


def _rope_slab(y, cos, sin_signed):
    lane = lax.broadcasted_iota(jnp.int32, y.shape, 1)
    up = pltpu.roll(y, 32, 1)
    down = pltpu.roll(y, LANES - 32, 1)
    swapped = jnp.where((lane % HD) < HD // 2, down, up)
    return y * cos + swapped * sin_signed


def _matmul_kernel(*refs, n_x, split_tile, row_axis, has_norm, has_res, rope_slabs, n_slabs, n_chunk):
    it = iter(refs)
    x_refs = [next(it) for _ in range(n_x)]
    g_ref = next(it) if has_norm else None
    w_ref = next(it)
    res_ref = next(it) if has_res else None
    cos_ref = next(it) if rope_slabs else None
    sin_ref = next(it) if rope_slabs else None
    o_ref = next(it)
    xn_ref = next(it) if has_norm else None

    if has_norm:
        @pl.when(pl.program_id(1) == 0)
        def _():
            x = x_refs[0][...]
            xn_ref[...] = (x * _rms_scale(x) * g_ref[...]).astype(BF16)

    def emit(lhs_ref):
        lhs = lhs_ref[...]
        if not rope_slabs:
            y = jnp.dot(lhs, w_ref[...].astype(BF16), preferred_element_type=F32)
            if has_res:
                y = y + res_ref[...]
            o_ref[...] = y.astype(o_ref.dtype)
            return
        cos = cos_ref[...]
        sin = sin_ref[...]
        for c0 in range(0, n_slabs, n_chunk):
            cols = slice(c0 * LANES, (c0 + n_chunk) * LANES)
            y = jnp.dot(lhs, w_ref[:, cols].astype(BF16), preferred_element_type=F32)
            if has_res:
                y = y + res_ref[:, cols]
            for s in range(n_chunk):
                ys = y[:, s * LANES:(s + 1) * LANES]
                if c0 + s < rope_slabs:
                    ys = _rope_slab(ys, cos, sin)
                o_ref[:, (c0 + s) * LANES:(c0 + s + 1) * LANES] = ys.astype(o_ref.dtype)

    if has_norm:
        emit(xn_ref)
    elif n_x == 1:
        emit(x_refs[0])
    else:
        i = pl.program_id(row_axis)
        pl.when(i < split_tile)(functools.partial(emit, x_refs[0]))
        pl.when(i >= split_tile)(functools.partial(emit, x_refs[1]))


def _matmul(xs, w, *, gain=None, res=None, rope=None, rope_cols=0, out_dtype=F32, tm, tn, name,
            weights_outer=False):
    xs = xs if isinstance(xs, (tuple, list)) else (xs,)
    k = xs[0].shape[1]
    m = sum(x.shape[0] for x in xs)
    n = (w.shape[1] // tn) * tn
    assert all(x.shape[0] % tm == 0 for x in xs) and w.shape[0] == k
    has_norm = gain is not None
    has_res = res is not None
    assert not (has_norm and len(xs) > 1)
    split_tile = xs[0].shape[0] // tm
    rope_slabs = 0
    if rope is not None:
        assert rope_cols == n or tn == n
        rope_slabs = min(rope_cols, tn) // LANES
    assert not (has_norm and weights_outer)

    def spec(shape, f):
        return pl.BlockSpec(shape, (lambda a, b: f(b, a)) if weights_outer else f)

    if len(xs) == 1:
        in_specs = [spec((tm, k), lambda i, j: (i, 0))]
    else:
        in_specs = [spec((tm, k), lambda i, j: (jnp.minimum(i, split_tile - 1), 0)),
                    spec((tm, k), lambda i, j: (jnp.maximum(i - split_tile, 0), 0))]
    args = list(xs)
    if has_norm:
        in_specs.append(spec((1, k), lambda i, j: (0, 0)))
        args.append(gain)
    in_specs.append(spec((k, tn), lambda i, j: (0, j)))
    args.append(w)
    if has_res:
        in_specs.append(spec((tm, tn), lambda i, j: (i, j)))
        args.append(res)
    if rope_slabs:
        in_specs += [spec((tm, LANES), lambda i, j: (i, 0))] * 2
        args += list(rope)
    n_slabs = tn // LANES
    kern = functools.partial(_matmul_kernel, n_x=len(xs), split_tile=split_tile,
                             row_axis=1 if weights_outer else 0, has_norm=has_norm, has_res=has_res,
                             rope_slabs=rope_slabs, n_slabs=n_slabs, n_chunk=2 if n_slabs % 2 == 0 else 1)
    grid = (n // tn, m // tm) if weights_outer else (m // tm, n // tn)
    return pl.pallas_call(
        kern,
        grid=grid,
        in_specs=in_specs,
        out_specs=spec((tm, tn), lambda i, j: (i, j)),
        out_shape=jax.ShapeDtypeStruct((m, n), out_dtype),
        scratch_shapes=[pltpu.VMEM((tm, k), BF16)] if has_norm else [],
        compiler_params=pltpu.CompilerParams(
            dimension_semantics=("parallel", "arbitrary"), vmem_limit_bytes=VMEM_LIMIT),
        name=name,
    )(*args)


def _mlp_kernel(*refs, final_norm):
    if final_norm:
        x_ref, g_ref, wu_ref, wd_ref, fg_ref, o_ref, hn_ref = refs
    else:
        x_ref, g_ref, wu_ref, wd_ref, o_ref, hn_ref = refs
    f = pl.program_id(1)

    @pl.when(f == 0)
    def _():
        x = x_ref[...]
        hn_ref[...] = (x * _rms_scale(x) * g_ref[...]).astype(BF16)
        o_ref[...] = x

    u = jnp.dot(hn_ref[...], wu_ref[...].astype(BF16), preferred_element_type=F32)
    a = jnp.square(jnp.maximum(u, 0.0)).astype(BF16)
    o_ref[...] += jnp.dot(a, wd_ref[...].astype(BF16), preferred_element_type=F32)

    if final_norm:
        @pl.when(f == pl.num_programs(1) - 1)
        def _():
            y = o_ref[...]
            o_ref[...] = y * _rms_scale(y) * fg_ref[...]


def _mlp(x, gain, w_up, w_down, layer, *, final_gain=None, tm, tf, name):
    m, d = x.shape
    dff = w_up.shape[2]
    assert m % tm == 0 and dff % tf == 0
    final_norm = final_gain is not None
    in_specs = [
        pl.BlockSpec((tm, d), lambda i, f: (i, 0)),
        pl.BlockSpec((1, d), lambda i, f: (0, 0)),
        pl.BlockSpec((None, d, tf), lambda i, f: (layer, 0, f)),
        pl.BlockSpec((None, tf, d), lambda i, f: (layer, f, 0)),
    ]
    args = [x, gain, w_up, w_down]
    if final_norm:
        in_specs.append(pl.BlockSpec((1, d), lambda i, f: (0, 0)))
        args.append(final_gain)
    return pl.pallas_call(
        functools.partial(_mlp_kernel, final_norm=final_norm),
        grid=(m // tm, dff // tf),
        in_specs=in_specs,
        out_specs=pl.BlockSpec((tm, d), lambda i, f: (i, 0)),
        out_shape=jax.ShapeDtypeStruct((m, d), F32),
        scratch_shapes=[pltpu.VMEM((tm, d), BF16)],
        compiler_params=pltpu.CompilerParams(
            dimension_semantics=("parallel", "arbitrary"), vmem_limit_bytes=MLP_VMEM_LIMIT),
        name=name,
    )(*args)


def _gdn_kernel(xq_ref, xk_ref, xv_ref, z_ref, gates_ref, alog_ref, dt_ref,
                pq_ref, pk_ref, pv_ref, wq_ref, wk_ref, wv_ref, onorm_ref, s0_ref,
                o_ref, s_ref, hq_ref, hk_ref, hv_ref, padq_ref, padk_ref, padv_ref, *, c, hb):
    n = pl.program_id(2)
    halo = SUBLANES

    @pl.when(n == 0)
    def _():
        for pad_ref, prev_ref in ((padq_ref, pq_ref), (padk_ref, pk_ref), (padv_ref, pv_ref)):
            pad_ref[0:halo, :] = jnp.zeros((halo, pad_ref.shape[1]), F32)
            pad_ref[halo - (CONV_W - 1):halo, :] = prev_ref[0]
        s_ref[...] = s0_ref[...]

    def conv(x_ref, pad_ref, w_ref, hist_ref):
        pad_ref[halo:halo + c, :] = x_ref[...]
        xp = pad_ref[...]
        x1 = pltpu.roll(xp, 1, 0)
        a = xp * w_ref[3:4, :] + x1 * w_ref[2:3, :]
        b = xp * w_ref[1:2, :] + x1 * w_ref[0:1, :]
        acc = (a + pltpu.roll(b, 2, 0))[halo:, :]
        tail = xp[c:c + halo, :]
        pad_ref[0:halo, :] = tail
        hist_ref[0] = tail
        return _silu(acc)

    cq = conv(xq_ref, padq_ref, wq_ref, hq_ref)
    ck = conv(xk_ref, padk_ref, wk_ref, hk_ref)
    cv = conv(xv_ref, padv_ref, wv_ref, hv_ref)

    gates = gates_ref[...]
    beta_all = 1.0 / (1.0 + jnp.exp(-gates))
    ga = gates + dt_ref[...]
    softplus = jnp.maximum(ga, 0.0) + jnp.log(1.0 + jnp.exp(-jnp.abs(ga)))
    g_all = -jnp.exp(alog_ref[...]) * softplus
    row = lax.broadcasted_iota(jnp.int32, (c, c), 0)
    col = lax.broadcasted_iota(jnp.int32, (c, c), 1)
    incl = row >= col
    strict = row > col
    tril = jnp.where(incl, 1.0, 0.0).astype(F32)
    gc_all = jnp.dot(tril, g_all, preferred_element_type=F32,
                     precision=lax.Precision.HIGHEST)
    r128 = lax.broadcasted_iota(jnp.int32, (LANES, LANES), 0)
    c128 = lax.broadcasted_iota(jnp.int32, (LANES, LANES), 1)
    eye128 = jnp.where(r128 == c128, 1.0, 0.0).astype(F32)
    gc_t = lax.dot_general(eye128, gc_all, _NT, preferred_element_type=F32,
                           precision=lax.Precision.HIGHEST)

    heads = range(hb)
    sls = [slice(h * LANES, (h + 1) * LANES) for h in heads]
    q = [cq[:, sl] for sl in sls]
    k = [ck[:, sl] for sl in sls]
    v = [cv[:, sl] for sl in sls]
    q = [x * (lax.rsqrt(jnp.sum(x * x, axis=-1, keepdims=True) + EPS) * (DK ** -0.5)) for x in q]
    k = [x * lax.rsqrt(jnp.sum(x * x, axis=-1, keepdims=True) + EPS) for x in k]
    bcol = [beta_all[:, h:h + 1] for h in heads]
    gcol = [gc_all[:, hb + h:hb + h + 1] for h in heads]
    grow = [gc_t[hb + h:hb + h + 1, :] for h in heads]
    glast = [gc_all[c - 1:c, hb + h:hb + h + 1] for h in heads]
    decay = [jnp.exp(jnp.where(incl, gcol[h] - grow[h], -jnp.inf)) for h in heads]
    kb = [k[h] * bcol[h] for h in heads]
    eye = jnp.where(row == col, 1.0, 0.0).astype(F32)
    qkk = [_dot(jnp.concatenate([q[h], kb[h]], axis=0), k[h], _NT) for h in heads]
    qk = [qkk[h][:c] * decay[h] for h in heads]
    p = [-jnp.where(strict, qkk[h][c:] * decay[h], 0.0) for h in heads]
    t = [eye + x for x in p]
    levels = max(int(math.ceil(math.log2(c))) - 1, 0)
    if levels:
        p = [_dot(x, x) for x in p]
    for _ in range(1, levels):
        r = [_dot(jnp.concatenate([p[h], t[h]], axis=0), p[h]) for h in heads]
        t = [t[h] + r[h][c:] for h in heads]
        p = [r[h][:c] for h in heads]
    if levels:
        t = [t[h] + _dot(t[h], p[h]) for h in heads]
    egc = [jnp.exp(x) for x in gcol]
    rhs = [jnp.concatenate([v[h] * bcol[h], kb[h] * egc[h]], axis=-1) for h in heads]
    sol = [_dot(t[h], rhs[h]) for h in heads]
    s = [s_ref[0, h] for h in heads]
    wq = [_dot(jnp.concatenate([sol[h][:, DV:], q[h] * egc[h]], axis=0), s[h]) for h in heads]
    v_new = [sol[h][:, :DV] - wq[h][:c] for h in heads]
    o = [wq[h][c:] + _dot(qk[h], v_new[h]) for h in heads]
    s_new = [s[h] * jnp.exp(glast[h]) + _dot(k[h] * jnp.exp(glast[h] - gcol[h]), v_new[h], _TN)
             for h in heads]
    onorm = onorm_ref[...]
    for h in heads:
        s_ref[0, h] = s_new[h]
        oh = o[h] * _rms_scale(o[h]) * onorm
        o_ref[:, sls[h]] = (oh * _silu(z_ref[:, sls[h]])).astype(o_ref.dtype)


def _gdn(qkvz, gates, alog_row, dt_row, conv_prev, conv_w, onorm_row, s0,
         *, row0, batch, length, c, hb, name):
    nblk = length // c
    ng = HA // hb
    hw = hb * LANES
    rb0 = row0 // c
    assert row0 % c == 0 and length % c == 0 and c >= SUBLANES and CONV_W == 4

    def rows(b, g, n):
        return rb0 + b * nblk + n

    x_spec = lambda part: pl.BlockSpec((c, hw), lambda b, g, n: (rows(b, g, n), part * ng + g))
    prev_spec = lambda part: pl.BlockSpec((1, CONV_W - 1, hw), lambda b, g, n: (b, 0, part * ng + g))
    w_spec = lambda part: pl.BlockSpec((CONV_W, hw), lambda b, g, n: (0, part * ng + g))
    gate_row_spec = pl.BlockSpec((1, LANES), lambda b, g, n: (0, g))
    hist_spec = pl.BlockSpec((1, SUBLANES, hw), lambda b, g, n: (b, 0, g))
    hist_shape = jax.ShapeDtypeStruct((batch, SUBLANES, HA * LANES), F32)
    in_specs = [
        x_spec(0), x_spec(1), x_spec(2), x_spec(3),
        pl.BlockSpec((c, LANES), lambda b, g, n: (rows(b, g, n), g)),
        gate_row_spec, gate_row_spec,
        prev_spec(0), prev_spec(1), prev_spec(2),
        w_spec(0), w_spec(1), w_spec(2),
        pl.BlockSpec((1, LANES), lambda b, g, n: (0, 0)),
        pl.BlockSpec((1, hb, DK, DV), lambda b, g, n: (b, g, 0, 0)),
    ]
    out_specs = [
        pl.BlockSpec((c, hw), lambda b, g, n: (b * nblk + n, g)),
        pl.BlockSpec((1, hb, DK, DV), lambda b, g, n: (b, g, 0, 0)),
        hist_spec, hist_spec, hist_spec,
    ]
    return pl.pallas_call(
        functools.partial(_gdn_kernel, c=c, hb=hb),
        grid=(batch, ng, nblk),
        in_specs=in_specs,
        out_specs=out_specs,
        out_shape=[jax.ShapeDtypeStruct((batch * length, HA * DV), BF16),
                   jax.ShapeDtypeStruct((batch, HA, DK, DV), F32),
                   hist_shape, hist_shape, hist_shape],
        scratch_shapes=[pltpu.VMEM((c + SUBLANES, hw), F32)] * 3,
        compiler_params=pltpu.CompilerParams(
            dimension_semantics=("parallel", "parallel", "arbitrary"),
            vmem_limit_bytes=VMEM_LIMIT),
        name=name,
    )(qkvz, qkvz, qkvz, qkvz, gates, alog_row, dt_row,
      conv_prev, conv_prev, conv_prev, conv_w, conv_w, conv_w, onorm_row, s0)


def _attn_kernel(sinks_ref, q_ref, kp_ref, vp_ref, kc_ref, vc_ref, o_ref, *, tq, nb, chunk_mask):
    i = pl.program_id(1)
    nk = WINDOW + tq
    lane_k = lax.broadcasted_iota(jnp.int32, (nk, LANES), 1)
    lane_q = lax.broadcasted_iota(jnp.int32, (tq, LANES), 1)
    if chunk_mask:
        qc = lax.broadcasted_iota(jnp.int32, (tq, nk), 0) // CHUNK
        kc = lax.broadcasted_iota(jnp.int32, (tq, nk), 1) // CHUNK
        w_ch = WINDOW // CHUNK
        first_kc = jnp.where(i > 0, 0, w_ch)
        valid = (kc >= jnp.maximum(qc, first_kc)) & (kc <= qc + w_ch)
    scale = HD ** -0.5
    assert math.log2(HD) % 2 == 0
    per_slab = LANES // HD
    group = N_Q // N_KV
    slabs_per_kv = group // per_slab
    half_masks = [(lane_q >= r * HD) & (lane_q < (r + 1) * HD) for r in range(per_slab)]

    units = [(b, h) for b in range(nb) for h in range(N_KV)]
    kdup, vdup, qs = [], [], []
    for b, h in units:
        ksl = slice((h // per_slab) * LANES, (h // per_slab + 1) * LANES)
        kslab = jnp.concatenate([kp_ref[b * WINDOW:(b + 1) * WINDOW, ksl],
                                 kc_ref[b * tq:(b + 1) * tq, ksl]], axis=0)
        vslab = jnp.concatenate([vp_ref[b * WINDOW:(b + 1) * WINDOW, ksl],
                                 vc_ref[b * tq:(b + 1) * tq, ksl]], axis=0)
        first = (lane_k < HD) == (h % per_slab == 0)
        kdup.append(jnp.where(first, kslab, pltpu.roll(kslab, HD, 1)).astype(BF16))
        vdup.append(jnp.where(first, vslab, pltpu.roll(vslab, HD, 1)).astype(BF16))
        pieces = []
        for s in range(slabs_per_kv):
            slab = h * slabs_per_kv + s
            q2 = q_ref[b * tq:(b + 1) * tq, slab * LANES:(slab + 1) * LANES]
            pieces += [jnp.where(mk, q2, jnp.zeros_like(q2)) for mk in half_masks]
        qs.append(jnp.concatenate(pieces, axis=0) * scale)
    sc = [lax.dot_general(qs[u], kdup[u], _NT, preferred_element_type=F32) for u in range(len(units))]
    ps = []
    for u, (b, h) in enumerate(units):
        blocks = []
        for r in range(group):
            sr = sc[u][r * tq:(r + 1) * tq]
            if chunk_mask:
                sr = jnp.where(valid, sr, -jnp.inf)
            sk = sinks_ref[h * group + r]
            mx = jnp.maximum(jnp.max(sr, axis=-1, keepdims=True), sk)
            e = jnp.exp(sr - mx)
            den = jnp.sum(e, axis=-1, keepdims=True) + jnp.exp(sk - mx)
            blocks.append((e / den).astype(BF16))
        ps.append(jnp.concatenate(blocks, axis=0))
    pv = [jnp.dot(ps[u], vdup[u], preferred_element_type=F32) for u in range(len(units))]
    for u, (b, h) in enumerate(units):
        for s in range(slabs_per_kv):
            slab = h * slabs_per_kv + s
            halves = [pv[u][(s * per_slab + r) * tq:(s * per_slab + r + 1) * tq] for r in range(per_slab)]
            out = halves[-1]
            for r in range(per_slab - 2, -1, -1):
                out = jnp.where(half_masks[r], halves[r], out)
            o_ref[b * tq:(b + 1) * tq, slab * LANES:(slab + 1) * LANES] = out.astype(o_ref.dtype)


def _attention(sinks, q, kprev, vprev, kprev_col, vprev_col, kv, *, row0, batch, length,
               tq, nb, chunk_mask, name):
    nq = length // tq
    kvw = N_KV * HD
    assert row0 % (nb * tq) == 0 and length % tq == 0 and batch % nb == 0 and (nb == 1 or nq == 1)
    rb0 = row0 // (nb * tq)

    def cur(b, i):
        return rb0 + b * nq + i

    def prev(b, i):
        return b * nq + jnp.maximum(i - 1, 0)

    in_specs = [
        pl.BlockSpec(memory_space=pltpu.SMEM),
        pl.BlockSpec((nb * tq, N_Q * HD), lambda b, i: (cur(b, i), 0)),
        pl.BlockSpec((nb * WINDOW, kvw), lambda b, i: (prev(b, i), kprev_col)),
        pl.BlockSpec((nb * WINDOW, kvw), lambda b, i: (prev(b, i), vprev_col)),
        pl.BlockSpec((nb * tq, kvw), lambda b, i: (cur(b, i), 0)),
        pl.BlockSpec((nb * tq, kvw), lambda b, i: (cur(b, i), 1)),
    ]
    return pl.pallas_call(
        functools.partial(_attn_kernel, tq=tq, nb=nb, chunk_mask=chunk_mask),
        grid=(batch // nb, nq),
        in_specs=in_specs,
        out_specs=pl.BlockSpec((nb * tq, N_Q * HD), lambda b, i: (b * nq + i, 0)),
        out_shape=jax.ShapeDtypeStruct((batch * length, N_Q * HD), BF16),
        compiler_params=pltpu.CompilerParams(
            dimension_semantics=("parallel", "arbitrary"), vmem_limit_bytes=VMEM_LIMIT),
        name=name,
    )(sinks, q, kprev, vprev, kv, kv)


def _rope_tables(pos):
    half = HD // 2
    inv = 1.0 / (ROPE_THETA ** (jnp.arange(half, dtype=F32) / half))
    ang = pos.astype(F32)[:, None] * inv[None, :]
    cos = jnp.cos(ang)
    sin = jnp.sin(ang)
    return (jnp.concatenate([cos, cos, cos, cos], axis=-1),
            jnp.concatenate([-sin, sin, -sin, sin], axis=-1))


def _gate_layout(t, hb):
    lead = t.shape[:-1]
    ng = HA // hb
    if ng == 1:
        return jnp.pad(t, [(0, 0)] * len(lead) + [(0, LANES - 2 * HA)])
    beta = t[..., :HA].reshape(lead + (ng, hb))
    dec = t[..., HA:].reshape(lead + (ng, hb))
    pad = jnp.zeros(lead + (ng, LANES - 2 * hb), t.dtype)
    return jnp.concatenate([beta, dec, pad], axis=-1).reshape(lead + (ng * LANES,))


def kernel(x_prompt, x_sample, cache_conv, state_gdn, cache_k, cache_v, attn_norm, mlp_norm, final_norm, a_w_in, a_conv_w, a_log, a_dt_bias, a_o_norm, a_w_out, kv_norm, w_kv, b_w_q, b_sinks, b_w_o, w_up, w_down):
    bp, lp, d = x_prompt.shape
    bs, ls, _ = x_sample.shape
    mp, ms = bp * lp, bs * ls
    m = mp + ms
    hb = GDN_HEADS_PER_STEP
    dqk, dvw = HA * DK, HA * DV
    conv_ch = 2 * dqk + dvw
    keep = CONV_W - 1
    assert a_w_in.shape[0] == 1 and b_w_q.shape[0] == 1 and lp % WINDOW == 0 and ls >= keep

    x = jnp.concatenate([x_prompt.reshape(mp, d), x_sample.reshape(ms, d)], axis=0)
    tm = _pick_tile(m, ROW_TILES)
    tm2 = _pick_tile(math.gcd(mp, ms), ROW_TILES[1:])
    tn_d = _pick_tile(d, COL_TILES[1:])

    w_in = a_w_in[0].astype(BF16)
    w_gate = _gate_layout(w_in[:, conv_ch + dvw:], hb)
    g0 = attn_norm[0].reshape(1, d)
    qkvz = _matmul(x, w_in, gain=g0, tm=tm, tn=_pick_tile(conv_ch + dvw, COL_TILES), name="gdn_in_proj")
    gates = _matmul(x, w_gate, gain=g0, tm=tm, tn=w_gate.shape[1], name="gdn_gate_proj")
    zeros_gate = jnp.zeros((HA,), F32)
    alog_row = _gate_layout(jnp.concatenate([zeros_gate, a_log[0]]), hb).reshape(1, -1)
    dt_row = _gate_layout(jnp.concatenate([zeros_gate, a_dt_bias[0]]), hb).reshape(1, -1)
    onorm_row = a_o_norm[0].reshape(1, DV)
    conv_w = a_conv_w[0]
    prev_p = jnp.zeros((bp, keep, conv_ch), F32)
    prev_s = cache_conv[0]
    o_p, gdn_p, *hist_p = _gdn(qkvz, gates, alog_row, dt_row, prev_p, conv_w, onorm_row,
                               jnp.zeros((bp, HA, DK, DV), F32), row0=0, batch=bp, length=lp,
                               c=min(CHUNK, lp), hb=hb, name="gdn_prompt")
    o_s, gdn_s, *hist_s = _gdn(qkvz, gates, alog_row, dt_row, prev_s, conv_w, onorm_row,
                               state_gdn[0], row0=mp, batch=bs, length=ls,
                               c=min(CHUNK, ls), hb=hb, name="gdn_sample")
    tn_o = _pick_tile(d, COL_TILES)
    x = _matmul((o_p, o_s), a_w_out[0], res=x, tm=tm2, tn=tn_o, weights_outer=True, name="gdn_out_proj")
    tf = _pick_tile(w_up.shape[2], COL_TILES[1:])
    x = _mlp(x, mlp_norm[0].reshape(1, d), w_up, w_down, 0, tm=tm, tf=tf, name="mlp0")

    pos = jnp.concatenate([jnp.tile(jnp.arange(lp), bp), jnp.tile(PAST_LEN + jnp.arange(ls), bs)])
    rope = _rope_tables(pos)
    kvw = N_KV * HD
    kv = _matmul(x, w_kv, gain=kv_norm.reshape(1, d), rope=rope, rope_cols=kvw,
                 tm=tm, tn=2 * kvw, name="kv_proj")
    q = _matmul(x, b_w_q[0], gain=attn_norm[1].reshape(1, d), rope=rope,
                rope_cols=N_Q * HD, out_dtype=BF16, tm=tm, tn=tn_d, name="q_proj")
    sinks = b_sinks[0].astype(F32)
    a_p = _attention(sinks, q, kv, kv, 0, 1, kv, row0=0, batch=bp, length=lp, tq=WINDOW, nb=1,
                     chunk_mask=True, name="attn_prompt")
    nb = _pick_tile(bs, (ATTN_SAMPLE_BATCHES_PER_STEP, 2, 1))
    a_s = _attention(sinks, q, cache_k.reshape(bs * WINDOW, kvw), cache_v.reshape(bs * WINDOW, kvw),
                     0, 0, kv, row0=mp, batch=bs, length=ls, tq=ls, nb=nb,
                     chunk_mask=False, name="attn_sample")
    x = _matmul((a_p, a_s), b_w_o[0], res=x, tm=tm2, tn=tn_o, weights_outer=True, name="attn_out_proj")
    y = _mlp(x, mlp_norm[1].reshape(1, d), w_up, w_down, 1, final_gain=final_norm.reshape(1, d),
             tm=tm, tf=tf, name="mlp1")

    y_prompt = y[:mp].reshape(bp, lp, d)
    y_sample = y[mp:].reshape(bs, ls, d)
    conv_p = jnp.concatenate([t[:, SUBLANES - keep:] for t in hist_p], axis=-1)[None]
    conv_s = jnp.concatenate([t[:, SUBLANES - keep:] for t in hist_s], axis=-1)[None]
    kv_p = kv[:mp].reshape(bp, lp, 2 * kvw)[:, lp - WINDOW:]
    kv_s = kv[mp:].reshape(bs, ls, 2 * kvw)
    k_p = kv_p[..., :kvw].reshape(bp, WINDOW, N_KV, HD)
    v_p = kv_p[..., kvw:].reshape(bp, WINDOW, N_KV, HD)
    k_s = kv_s[..., :kvw].reshape(bs, ls, N_KV, HD)
    v_s = kv_s[..., kvw:].reshape(bs, ls, N_KV, HD)
    return (y_prompt, y_sample, conv_p, gdn_p[None], k_p, v_p, conv_s, gdn_s[None], k_s, v_s)
```

```python
import functools
import math

import jax
import jax.numpy as jnp
from jax import lax
from jax.experimental import pallas as pl
from jax.experimental.pallas import tpu as pltpu

F32 = jnp.float32
BF16 = jnp.bfloat16

EPS = 1e-6
CHUNK = 64
WINDOW = 128
PAST_LEN = 4096
ROPE_THETA = 10000.0
HA, DK, DV = 16, 128, 128
N_Q, N_KV, HD = 32, 4, 64
CONV_W = 4
LANES = 128
SUBLANES = 8
VMEM_LIMIT = 56 * 1024 * 1024
MLP_VMEM_LIMIT = 60 * 1024 * 1024
GDN_HEADS_PER_STEP = 16
ATTN_SAMPLE_BATCHES_PER_STEP = 4
ROW_TILES = (1088, 512, 256, 128, 64, 32, 16)
COL_TILES = (1024, 512, 256, 128)


def _pick_tile(n, candidates):
    for c in candidates:
        if n % c == 0:
            return c
    return n


def _dot(a, b, dims=(((1,), (0,)), ((), ()))):
    return lax.dot_general(a.astype(BF16), b.astype(BF16), dims, preferred_element_type=F32)


_NT = (((1,), (1,)), ((), ()))
_TN = (((0,), (0,)), ((), ()))


def _rms_scale(x):
    return lax.rsqrt(jnp.mean(x * x, axis=-1, keepdims=True) + EPS)


def _silu(x):
    return x / (1.0 + jnp.exp(-x))


def _rope_slab(y, cos, sin_signed):
    lane = lax.broadcasted_iota(jnp.int32, y.shape, 1)
    up = pltpu.roll(y, 32, 1)
    down = pltpu.roll(y, LANES - 32, 1)
    swapped = jnp.where((lane % HD) < HD // 2, down, up)
    return y * cos + swapped * sin_signed


def _project(lhs, w_ref, res_ref, cos_ref, sin_ref, o_ref, *, rope_slabs, n_slabs):
    if not rope_slabs:
        y = jnp.dot(lhs, w_ref[...].astype(BF16), preferred_element_type=F32)
        if res_ref is not None:
            y = y + res_ref[...]
        o_ref[...] = y.astype(o_ref.dtype)
        return
    cos = cos_ref[...]
    sin = sin_ref[...]
    n_chunk = 2 if n_slabs % 2 == 0 else 1
    for c0 in range(0, n_slabs, n_chunk):
        cols = slice(c0 * LANES, (c0 + n_chunk) * LANES)
        y = jnp.dot(lhs, w_ref[:, cols].astype(BF16), preferred_element_type=F32)
        if res_ref is not None:
            y = y + res_ref[:, cols]
        for s in range(n_chunk):
            ys = y[:, s * LANES:(s + 1) * LANES]
            if c0 + s < rope_slabs:
                ys = _rope_slab(ys, cos, sin)
            o_ref[:, (c0 + s) * LANES:(c0 + s + 1) * LANES] = ys.astype(o_ref.dtype)


def _matmul_kernel(*refs, n_x, n_res, split_tile, row_axis, has_norm, rope_slabs, n_slabs):
    it = iter(refs)
    x_refs = [next(it) for _ in range(n_x)]
    g_ref = next(it) if has_norm else None
    w_ref = next(it)
    res_refs = [next(it) for _ in range(n_res)]
    cos_ref = next(it) if rope_slabs else None
    sin_ref = next(it) if rope_slabs else None
    o_ref = next(it)
    xn_ref = next(it) if has_norm else None

    if has_norm:
        @pl.when(pl.program_id(1) == 0)
        def _():
            x = x_refs[0][...]
            xn_ref[...] = (x * _rms_scale(x) * g_ref[...]).astype(BF16)

    def emit(lhs_ref, res_ref):
        _project(lhs_ref[...], w_ref, res_ref, cos_ref, sin_ref, o_ref, rope_slabs=rope_slabs, n_slabs=n_slabs)

    if n_x == 1:
        emit(xn_ref if has_norm else x_refs[0], res_refs[0] if n_res else None)
    else:
        i = pl.program_id(row_axis)
        pl.when(i < split_tile)(functools.partial(emit, x_refs[0], res_refs[0] if n_res else None))
        pl.when(i >= split_tile)(functools.partial(emit, x_refs[1], res_refs[-1] if n_res else None))


def _matmul(xs, w, *, gain=None, res=None, rope=None, rope_cols=0, out_dtype=F32, tm, tn, name,
            weights_outer=False):
    xs = xs if isinstance(xs, (tuple, list)) else (xs,)
    ress = () if res is None else (res if isinstance(res, (tuple, list)) else (res,))
    k = xs[0].shape[1]
    m = sum(x.shape[0] for x in xs)
    n = (w.shape[1] // tn) * tn
    assert all(x.shape[0] % tm == 0 for x in xs) and w.shape[0] == k
    has_norm = gain is not None
    assert not (has_norm and len(xs) > 1)
    assert len(ress) <= len(xs) and all(r.shape[0] == x.shape[0] for r, x in zip(ress, xs) if len(ress) > 1)
    split_tile = xs[0].shape[0] // tm
    rope_slabs = 0
    if rope is not None:
        assert rope_cols == n or tn == n
        rope_slabs = min(rope_cols, tn) // LANES
    assert not (has_norm and weights_outer)

    def spec(shape, f):
        return pl.BlockSpec(shape, (lambda a, b: f(b, a)) if weights_outer else f)

    def row_specs(arrays, width, col):
        if len(arrays) == 1:
            return [spec((tm, width), lambda i, j: (i, col(j)))]
        return [spec((tm, width), lambda i, j: (jnp.minimum(i, split_tile - 1), col(j))),
                spec((tm, width), lambda i, j: (jnp.maximum(i - split_tile, 0), col(j)))]

    in_specs = row_specs(xs, k, lambda j: 0)
    args = list(xs)
    if has_norm:
        in_specs.append(spec((1, k), lambda i, j: (0, 0)))
        args.append(gain)
    in_specs.append(spec((k, tn), lambda i, j: (0, j)))
    args.append(w)
    in_specs += row_specs(ress, tn, lambda j: j) if ress else []
    args += list(ress)
    if rope_slabs:
        in_specs += [spec((tm, LANES), lambda i, j: (i, 0))] * 2
        args += list(rope)
    kern = functools.partial(_matmul_kernel, n_x=len(xs), n_res=len(ress), split_tile=split_tile,
                             row_axis=1 if weights_outer else 0, has_norm=has_norm,
                             rope_slabs=rope_slabs, n_slabs=tn // LANES)
    grid = (n // tn, m // tm) if weights_outer else (m // tm, n // tn)
    return pl.pallas_call(
        kern,
        grid=grid,
        in_specs=in_specs,
        out_specs=spec((tm, tn), lambda i, j: (i, j)),
        out_shape=jax.ShapeDtypeStruct((m, n), out_dtype),
        scratch_shapes=[pltpu.VMEM((tm, k), BF16)] if has_norm else [],
        compiler_params=pltpu.CompilerParams(
            dimension_semantics=("parallel", "arbitrary"), vmem_limit_bytes=VMEM_LIMIT),
        name=name,
    )(*args)


def _mlp_kernel(*refs, final_norm):
    if final_norm:
        x_ref, g_ref, wu_ref, wd_ref, fg_ref, o_ref, hn_ref = refs
    else:
        x_ref, g_ref, wu_ref, wd_ref, o_ref, hn_ref = refs
    f = pl.program_id(1)

    @pl.when(f == 0)
    def _():
        x = x_ref[...]
        hn_ref[...] = (x * _rms_scale(x) * g_ref[...]).astype(BF16)
        o_ref[...] = x

    u = jnp.dot(hn_ref[...], wu_ref[...].astype(BF16), preferred_element_type=F32)
    a = jnp.square(jnp.maximum(u, 0.0)).astype(BF16)
    o_ref[...] += jnp.dot(a, wd_ref[...].astype(BF16), preferred_element_type=F32)

    if final_norm:
        @pl.when(f == pl.num_programs(1) - 1)
        def _():
            y = o_ref[...]
            o_ref[...] = y * _rms_scale(y) * fg_ref[...]


def _mlp(x, gain, w_up, w_down, layer, *, final_gain=None, tm, tf, name):
    m, d = x.shape
    dff = w_up.shape[2]
    assert m % tm == 0 and dff % tf == 0
    final_norm = final_gain is not None
    in_specs = [
        pl.BlockSpec((tm, d), lambda i, f: (i, 0)),
        pl.BlockSpec((1, d), lambda i, f: (0, 0)),
        pl.BlockSpec((None, d, tf), lambda i, f: (layer, 0, f)),
        pl.BlockSpec((None, tf, d), lambda i, f: (layer, f, 0)),
    ]
    args = [x, gain, w_up, w_down]
    if final_norm:
        in_specs.append(pl.BlockSpec((1, d), lambda i, f: (0, 0)))
        args.append(final_gain)
    return pl.pallas_call(
        functools.partial(_mlp_kernel, final_norm=final_norm),
        grid=(m // tm, dff // tf),
        in_specs=in_specs,
        out_specs=pl.BlockSpec((tm, d), lambda i, f: (i, 0)),
        out_shape=jax.ShapeDtypeStruct((m, d), F32),
        scratch_shapes=[pltpu.VMEM((tm, d), BF16)],
        compiler_params=pltpu.CompilerParams(
            dimension_semantics=("parallel", "arbitrary"), vmem_limit_bytes=MLP_VMEM_LIMIT),
        name=name,
    )(*args)


def _gdn_kernel(xq_ref, xk_ref, xv_ref, z_ref, gates_ref, alog_ref, dt_ref,
                pq_ref, pk_ref, pv_ref, wq_ref, wk_ref, wv_ref, onorm_ref, s0_ref,
                o_ref, s_ref, hq_ref, hk_ref, hv_ref, padq_ref, padk_ref, padv_ref, *, c, hb):
    n = pl.program_id(2)
    halo = SUBLANES

    @pl.when(n == 0)
    def _():
        for pad_ref, prev_ref in ((padq_ref, pq_ref), (padk_ref, pk_ref), (padv_ref, pv_ref)):
            pad_ref[0:halo, :] = jnp.zeros((halo, pad_ref.shape[1]), F32)
            pad_ref[halo - (CONV_W - 1):halo, :] = prev_ref[0]
        s_ref[...] = s0_ref[...]

    def conv(x_ref, pad_ref, w_ref, hist_ref):
        pad_ref[halo:halo + c, :] = x_ref[...]
        xp = pad_ref[...]
        x1 = pltpu.roll(xp, 1, 0)
        a = xp * w_ref[3:4, :] + x1 * w_ref[2:3, :]
        b = xp * w_ref[1:2, :] + x1 * w_ref[0:1, :]
        acc = (a + pltpu.roll(b, 2, 0))[halo:, :]
        tail = xp[c:c + halo, :]
        pad_ref[0:halo, :] = tail
        hist_ref[0] = tail
        return _silu(acc)

    cq = conv(xq_ref, padq_ref, wq_ref, hq_ref)
    ck = conv(xk_ref, padk_ref, wk_ref, hk_ref)
    cv = conv(xv_ref, padv_ref, wv_ref, hv_ref)

    gates = gates_ref[...]
    beta_all = 1.0 / (1.0 + jnp.exp(-gates))
    ga = gates + dt_ref[...]
    softplus = jnp.maximum(ga, 0.0) + jnp.log(1.0 + jnp.exp(-jnp.abs(ga)))
    g_all = -jnp.exp(alog_ref[...]) * softplus
    row = lax.broadcasted_iota(jnp.int32, (c, c), 0)
    col = lax.broadcasted_iota(jnp.int32, (c, c), 1)
    incl = row >= col
    strict = row > col
    tril = jnp.where(incl, 1.0, 0.0).astype(F32)
    gc_all = jnp.dot(tril, g_all, preferred_element_type=F32,
                     precision=lax.Precision.HIGHEST)
    r128 = lax.broadcasted_iota(jnp.int32, (LANES, LANES), 0)
    c128 = lax.broadcasted_iota(jnp.int32, (LANES, LANES), 1)
    eye128 = jnp.where(r128 == c128, 1.0, 0.0).astype(F32)
    gc_t = lax.dot_general(eye128, gc_all, _NT, preferred_element_type=F32,
                           precision=lax.Precision.HIGHEST)

    heads = range(hb)
    sls = [slice(h * LANES, (h + 1) * LANES) for h in heads]
    q = [cq[:, sl] for sl in sls]
    k = [ck[:, sl] for sl in sls]
    v = [cv[:, sl] for sl in sls]
    q = [x * (lax.rsqrt(jnp.sum(x * x, axis=-1, keepdims=True) + EPS) * (DK ** -0.5)) for x in q]
    k = [x * lax.rsqrt(jnp.sum(x * x, axis=-1, keepdims=True) + EPS) for x in k]
    bcol = [beta_all[:, h:h + 1] for h in heads]
    gcol = [gc_all[:, hb + h:hb + h + 1] for h in heads]
    grow = [gc_t[hb + h:hb + h + 1, :] for h in heads]
    glast = [gc_all[c - 1:c, hb + h:hb + h + 1] for h in heads]
    decay = [jnp.exp(jnp.where(incl, gcol[h] - grow[h], -jnp.inf)) for h in heads]
    kb = [k[h] * bcol[h] for h in heads]
    eye = jnp.where(row == col, 1.0, 0.0).astype(F32)
    qkk = [_dot(jnp.concatenate([q[h], kb[h]], axis=0), k[h], _NT) for h in heads]
    qk = [qkk[h][:c] * decay[h] for h in heads]
    p = [-jnp.where(strict, qkk[h][c:] * decay[h], 0.0) for h in heads]
    t = [eye + x for x in p]
    levels = max(int(math.ceil(math.log2(c))) - 1, 0)
    if levels:
        p = [_dot(x, x) for x in p]
    for _ in range(1, levels):
        r = [_dot(jnp.concatenate([p[h], t[h]], axis=0), p[h]) for h in heads]
        t = [t[h] + r[h][c:] for h in heads]
        p = [r[h][:c] for h in heads]
    if levels:
        t = [t[h] + _dot(t[h], p[h]) for h in heads]
    egc = [jnp.exp(x) for x in gcol]
    rhs = [jnp.concatenate([v[h] * bcol[h], kb[h] * egc[h]], axis=-1) for h in heads]
    sol = [_dot(t[h], rhs[h]) for h in heads]
    s = [s_ref[0, h] for h in heads]
    wq = [_dot(jnp.concatenate([sol[h][:, DV:], q[h] * egc[h]], axis=0), s[h]) for h in heads]
    v_new = [sol[h][:, :DV] - wq[h][:c] for h in heads]
    o = [wq[h][c:] + _dot(qk[h], v_new[h]) for h in heads]
    s_new = [s[h] * jnp.exp(glast[h]) + _dot(k[h] * jnp.exp(glast[h] - gcol[h]), v_new[h], _TN)
             for h in heads]
    onorm = onorm_ref[...]
    for h in heads:
        s_ref[0, h] = s_new[h]
        oh = o[h] * _rms_scale(o[h]) * onorm
        o_ref[:, sls[h]] = (oh * _silu(z_ref[:, sls[h]])).astype(o_ref.dtype)


def _gdn(qkvz, gates, alog_row, dt_row, conv_prev, conv_w, onorm_row, s0,
         *, row0, batch, length, c, hb, name):
    nblk = length // c
    ng = HA // hb
    hw = hb * LANES
    rb0 = row0 // c
    assert row0 % c == 0 and length % c == 0 and c >= SUBLANES and CONV_W == 4

    def rows(b, g, n):
        return rb0 + b * nblk + n

    x_spec = lambda part: pl.BlockSpec((c, hw), lambda b, g, n: (rows(b, g, n), part * ng + g))
    prev_spec = lambda part: pl.BlockSpec((1, CONV_W - 1, hw), lambda b, g, n: (b, 0, part * ng + g))
    w_spec = lambda part: pl.BlockSpec((CONV_W, hw), lambda b, g, n: (0, part * ng + g))
    gate_row_spec = pl.BlockSpec((1, LANES), lambda b, g, n: (0, g))
    hist_spec = pl.BlockSpec((1, SUBLANES, hw), lambda b, g, n: (b, 0, g))
    hist_shape = jax.ShapeDtypeStruct((batch, SUBLANES, HA * LANES), F32)
    in_specs = [
        x_spec(0), x_spec(1), x_spec(2), x_spec(3),
        pl.BlockSpec((c, LANES), lambda b, g, n: (rows(b, g, n), g)),
        gate_row_spec, gate_row_spec,
        prev_spec(0), prev_spec(1), prev_spec(2),
        w_spec(0), w_spec(1), w_spec(2),
        pl.BlockSpec((1, LANES), lambda b, g, n: (0, 0)),
        pl.BlockSpec((1, hb, DK, DV), lambda b, g, n: (b, g, 0, 0)),
    ]
    out_specs = [
        pl.BlockSpec((c, hw), lambda b, g, n: (b * nblk + n, g)),
        pl.BlockSpec((1, hb, DK, DV), lambda b, g, n: (b, g, 0, 0)),
        hist_spec, hist_spec, hist_spec,
    ]
    return pl.pallas_call(
        functools.partial(_gdn_kernel, c=c, hb=hb),
        grid=(batch, ng, nblk),
        in_specs=in_specs,
        out_specs=out_specs,
        out_shape=[jax.ShapeDtypeStruct((batch * length, HA * DV), BF16),
                   jax.ShapeDtypeStruct((batch, HA, DK, DV), F32),
                   hist_shape, hist_shape, hist_shape],
        scratch_shapes=[pltpu.VMEM((c + SUBLANES, hw), F32)] * 3,
        compiler_params=pltpu.CompilerParams(
            dimension_semantics=("parallel", "parallel", "arbitrary"),
            vmem_limit_bytes=VMEM_LIMIT),
        name=name,
    )(qkvz, qkvz, qkvz, qkvz, gates, alog_row, dt_row,
      conv_prev, conv_prev, conv_prev, conv_w, conv_w, conv_w, onorm_row, s0)


def _attn_kernel(sinks_ref, q_ref, kp_ref, vp_ref, kc_ref, vc_ref, o_ref, *, tq, nb, chunk_mask):
    i = pl.program_id(1)
    nk = WINDOW + tq
    lane_k = lax.broadcasted_iota(jnp.int32, (nk, LANES), 1)
    lane_q = lax.broadcasted_iota(jnp.int32, (tq, LANES), 1)
    if chunk_mask:
        qc = lax.broadcasted_iota(jnp.int32, (tq, nk), 0) // CHUNK
        kc = lax.broadcasted_iota(jnp.int32, (tq, nk), 1) // CHUNK
        w_ch = WINDOW // CHUNK
        first_kc = jnp.where(i > 0, 0, w_ch)
        valid = (kc >= jnp.maximum(qc, first_kc)) & (kc <= qc + w_ch)
    scale = HD ** -0.5
    assert math.log2(HD) % 2 == 0
    per_slab = LANES // HD
    group = N_Q // N_KV
    slabs_per_kv = group // per_slab
    half_masks = [(lane_q >= r * HD) & (lane_q < (r + 1) * HD) for r in range(per_slab)]

    units = [(b, h) for b in range(nb) for h in range(N_KV)]
    kdup, vdup, qs = [], [], []
    for b, h in units:
        ksl = slice((h // per_slab) * LANES, (h // per_slab + 1) * LANES)
        kslab = jnp.concatenate([kp_ref[b * WINDOW:(b + 1) * WINDOW, ksl],
                                 kc_ref[b * tq:(b + 1) * tq, ksl]], axis=0)
        vslab = jnp.concatenate([vp_ref[b * WINDOW:(b + 1) * WINDOW, ksl],
                                 vc_ref[b * tq:(b + 1) * tq, ksl]], axis=0)
        first = (lane_k < HD) == (h % per_slab == 0)
        kdup.append(jnp.where(first, kslab, pltpu.roll(kslab, HD, 1)).astype(BF16))
        vdup.append(jnp.where(first, vslab, pltpu.roll(vslab, HD, 1)).astype(BF16))
        pieces = []
        for s in range(slabs_per_kv):
            slab = h * slabs_per_kv + s
            q2 = q_ref[b * tq:(b + 1) * tq, slab * LANES:(slab + 1) * LANES]
            pieces += [jnp.where(mk, q2, jnp.zeros_like(q2)) for mk in half_masks]
        qs.append(jnp.concatenate(pieces, axis=0) * scale)
    sc = [lax.dot_general(qs[u], kdup[u], _NT, preferred_element_type=F32) for u in range(len(units))]
    ps = []
    for u, (b, h) in enumerate(units):
        blocks = []
        for r in range(group):
            sr = sc[u][r * tq:(r + 1) * tq]
            if chunk_mask:
                sr = jnp.where(valid, sr, -jnp.inf)
            sk = sinks_ref[h * group + r]
            mx = jnp.maximum(jnp.max(sr, axis=-1, keepdims=True), sk)
            e = jnp.exp(sr - mx)
            den = jnp.sum(e, axis=-1, keepdims=True) + jnp.exp(sk - mx)
            blocks.append((e / den).astype(BF16))
        ps.append(jnp.concatenate(blocks, axis=0))
    pv = [jnp.dot(ps[u], vdup[u], preferred_element_type=F32) for u in range(len(units))]
    for u, (b, h) in enumerate(units):
        for s in range(slabs_per_kv):
            slab = h * slabs_per_kv + s
            halves = [pv[u][(s * per_slab + r) * tq:(s * per_slab + r + 1) * tq] for r in range(per_slab)]
            out = halves[-1]
            for r in range(per_slab - 2, -1, -1):
                out = jnp.where(half_masks[r], halves[r], out)
            o_ref[b * tq:(b + 1) * tq, slab * LANES:(slab + 1) * LANES] = out.astype(o_ref.dtype)


def _attention(sinks, q, kprev, vprev, kprev_col, vprev_col, kv, *, row0, batch, length,
               tq, nb, chunk_mask, name):
    nq = length // tq
    kvw = N_KV * HD
    assert row0 % (nb * tq) == 0 and length % tq == 0 and batch % nb == 0 and (nb == 1 or nq == 1)
    rb0 = row0 // (nb * tq)

    def cur(b, i):
        return rb0 + b * nq + i

    def prev(b, i):
        return b * nq + jnp.maximum(i - 1, 0)

    in_specs = [
        pl.BlockSpec(memory_space=pltpu.SMEM),
        pl.BlockSpec((nb * tq, N_Q * HD), lambda b, i: (cur(b, i), 0)),
        pl.BlockSpec((nb * WINDOW, kvw), lambda b, i: (prev(b, i), kprev_col)),
        pl.BlockSpec((nb * WINDOW, kvw), lambda b, i: (prev(b, i), vprev_col)),
        pl.BlockSpec((nb * tq, kvw), lambda b, i: (cur(b, i), 0)),
        pl.BlockSpec((nb * tq, kvw), lambda b, i: (cur(b, i), 1)),
    ]
    return pl.pallas_call(
        functools.partial(_attn_kernel, tq=tq, nb=nb, chunk_mask=chunk_mask),
        grid=(batch // nb, nq),
        in_specs=in_specs,
        out_specs=pl.BlockSpec((nb * tq, N_Q * HD), lambda b, i: (b * nq + i, 0)),
        out_shape=jax.ShapeDtypeStruct((batch * length, N_Q * HD), BF16),
        compiler_params=pltpu.CompilerParams(
            dimension_semantics=("parallel", "arbitrary"), vmem_limit_bytes=VMEM_LIMIT),
        name=name,
    )(sinks, q, kprev, vprev, kv, kv)


def _rope_tables(pos):
    half = HD // 2
    inv = 1.0 / (ROPE_THETA ** (jnp.arange(half, dtype=F32) / half))
    ang = pos.astype(F32)[:, None] * inv[None, :]
    cos = jnp.cos(ang)
    sin = jnp.sin(ang)
    return (jnp.concatenate([cos, cos, cos, cos], axis=-1),
            jnp.concatenate([-sin, sin, -sin, sin], axis=-1))


def _gate_layout(t, hb):
    lead = t.shape[:-1]
    ng = HA // hb
    if ng == 1:
        return jnp.pad(t, [(0, 0)] * len(lead) + [(0, LANES - 2 * HA)])
    beta = t[..., :HA].reshape(lead + (ng, hb))
    dec = t[..., HA:].reshape(lead + (ng, hb))
    pad = jnp.zeros(lead + (ng, LANES - 2 * hb), t.dtype)
    return jnp.concatenate([beta, dec, pad], axis=-1).reshape(lead + (ng * LANES,))


def kernel(x_prompt, x_sample, cache_conv, state_gdn, cache_k, cache_v, attn_norm, mlp_norm, final_norm, a_w_in, a_conv_w, a_log, a_dt_bias, a_o_norm, a_w_out, kv_norm, w_kv, b_w_q, b_sinks, b_w_o, w_up, w_down):
    bp, lp, d = x_prompt.shape
    bs, ls, _ = x_sample.shape
    mp, ms = bp * lp, bs * ls
    m = mp + ms
    hb = GDN_HEADS_PER_STEP
    dqk, dvw = HA * DK, HA * DV
    conv_ch = 2 * dqk + dvw
    keep = CONV_W - 1
    assert a_w_in.shape[0] == 1 and b_w_q.shape[0] == 1 and lp % WINDOW == 0 and ls >= keep

    xp = x_prompt.reshape(mp, d)
    xs = x_sample.reshape(ms, d)
    tm = _pick_tile(m, ROW_TILES)
    tm2 = _pick_tile(math.gcd(mp, ms), ROW_TILES[1:])
    tm_p = _pick_tile(mp, (1024,) + ROW_TILES[1:])
    tm_s = _pick_tile(ms, ROW_TILES[1:])
    tn_d = _pick_tile(d, COL_TILES[1:])

    w_in = a_w_in[0].astype(BF16)
    w_gate = _gate_layout(w_in[:, conv_ch + dvw:], hb)
    g0 = attn_norm[0].reshape(1, d)
    tn_in = _pick_tile(conv_ch + dvw, COL_TILES)
    zeros_gate = jnp.zeros((HA,), F32)
    alog_row = _gate_layout(jnp.concatenate([zeros_gate, a_log[0]]), hb).reshape(1, -1)
    dt_row = _gate_layout(jnp.concatenate([zeros_gate, a_dt_bias[0]]), hb).reshape(1, -1)
    onorm_row = a_o_norm[0].reshape(1, DV)
    streams = (("prompt", xp, tm_p, bp, lp, jnp.zeros((bp, keep, conv_ch), F32),
                jnp.zeros((bp, HA, DK, DV), F32)),
               ("sample", xs, tm_s, bs, ls, cache_conv[0], state_gdn[0]))
    mixed = []
    for tag, xr, tmr, nbatch, length, conv_prev, s0 in streams:
        qkvz = _matmul(xr, w_in, gain=g0, tm=tmr, tn=tn_in, name="gdn_in_proj_" + tag)
        gates = _matmul(xr, w_gate, gain=g0, tm=tmr, tn=w_gate.shape[1], name="gdn_gate_proj_" + tag)
        mixed.append(_gdn(qkvz, gates, alog_row, dt_row, conv_prev, a_conv_w[0], onorm_row, s0,
                          row0=0, batch=nbatch, length=length, c=min(CHUNK, length), hb=hb,
                          name="gdn_" + tag))
    (o_p, gdn_p, *hist_p), (o_s, gdn_s, *hist_s) = mixed
    tn_o = _pick_tile(d, COL_TILES)
    x = _matmul((o_p, o_s), a_w_out[0], res=(xp, xs), tm=tm2, tn=tn_o, weights_outer=True,
                name="gdn_out_proj")
    tf = _pick_tile(w_up.shape[2], COL_TILES[1:])
    x = _mlp(x, mlp_norm[0].reshape(1, d), w_up, w_down, 0, tm=tm, tf=tf, name="mlp0")

    pos = jnp.concatenate([jnp.tile(jnp.arange(lp), bp), jnp.tile(PAST_LEN + jnp.arange(ls), bs)])
    rope = _rope_tables(pos)
    kvw = N_KV * HD
    kv = _matmul(x, w_kv, gain=kv_norm.reshape(1, d), rope=rope, rope_cols=kvw,
                 tm=tm, tn=2 * kvw, name="kv_proj")
    q = _matmul(x, b_w_q[0], gain=attn_norm[1].reshape(1, d), rope=rope,
                rope_cols=N_Q * HD, out_dtype=BF16, tm=tm, tn=tn_d, name="q_proj")
    sinks = b_sinks[0].astype(F32)
    a_p = _attention(sinks, q, kv, kv, 0, 1, kv, row0=0, batch=bp, length=lp, tq=WINDOW, nb=1,
                     chunk_mask=True, name="attn_prompt")
    nb = _pick_tile(bs, (ATTN_SAMPLE_BATCHES_PER_STEP, 2, 1))
    a_s = _attention(sinks, q, cache_k.reshape(bs * WINDOW, kvw), cache_v.reshape(bs * WINDOW, kvw),
                     0, 0, kv, row0=mp, batch=bs, length=ls, tq=ls, nb=nb,
                     chunk_mask=False, name="attn_sample")
    x = _matmul((a_p, a_s), b_w_o[0], res=x, tm=tm2, tn=tn_o, weights_outer=True, name="attn_out_proj")
    y = _mlp(x, mlp_norm[1].reshape(1, d), w_up, w_down, 1, final_gain=final_norm.reshape(1, d),
             tm=tm, tf=tf, name="mlp1")

    y_prompt = y[:mp].reshape(bp, lp, d)
    y_sample = y[mp:].reshape(bs, ls, d)
    conv_p = jnp.concatenate([t[:, SUBLANES - keep:] for t in hist_p], axis=-1)[None]
    conv_s = jnp.concatenate([t[:, SUBLANES - keep:] for t in hist_s], axis=-1)[None]
    kv_p = kv[:mp].reshape(bp, lp, 2 * kvw)[:, lp - WINDOW:]
    kv_s = kv[mp:].reshape(bs, ls, 2 * kvw)
    k_p = kv_p[..., :kvw].reshape(bp, WINDOW, N_KV, HD)
    v_p = kv_p[..., kvw:].reshape(bp, WINDOW, N_KV, HD)
    k_s = kv_s[..., :kvw].reshape(bs, ls, N_KV, HD)
    v_s = kv_s[..., kvw:].reshape(bs, ls, N_KV, HD)
    return (y_prompt, y_sample, conv_p, gdn_p[None], k_p, v_p, conv_s, gdn_s[None], k_s, v_s)
```

```python
import functools
import math

import jax
import jax.numpy as jnp
from jax import lax
from jax.experimental import pallas as pl
from jax.experimental.pallas import tpu as pltpu

F32 = jnp.float32
BF16 = jnp.bfloat16

EPS = 1e-6
CHUNK = 64
WINDOW = 128
PAST_LEN = 4096
ROPE_THETA = 10000.0
HA, DK, DV = 16, 128, 128
N_Q, N_KV, HD = 32, 4, 64
CONV_W = 4
LANES = 128
SUBLANES = 8
VMEM_LIMIT = 56 * 1024 * 1024
MLP_VMEM_LIMIT = 60 * 1024 * 1024
GDN_HEADS_PER_STEP = 16
ATTN_SAMPLE_BATCHES_PER_STEP = 4
ROW_TILES = (1088, 512, 256, 128, 64, 32, 16)
COL_TILES = (1024, 512, 256, 128)


def _pick_tile(n, candidates):
    for c in candidates:
        if n % c == 0:
            return c
    return n


def _dot(a, b, dims=(((1,), (0,)), ((), ()))):
    return lax.dot_general(a.astype(BF16), b.astype(BF16), dims, preferred_element_type=F32)


_NT = (((1,), (1,)), ((), ()))
_TN = (((0,), (0,)), ((), ()))


def _rms_scale(x):
    return lax.rsqrt(jnp.mean(x * x, axis=-1, keepdims=True) + EPS)


def _silu(x):
    return x / (1.0 + jnp.exp(-x))


def _rope_slab(y, cos, sin_signed):
    lane = lax.broadcasted_iota(jnp.int32, y.shape, 1)
    up = pltpu.roll(y, 32, 1)
    down = pltpu.roll(y, LANES - 32, 1)
    swapped = jnp.where((lane % HD) < HD // 2, down, up)
    return y * cos + swapped * sin_signed


def _project(lhs, w_ref, res_ref, cos_ref, sin_ref, o_ref, *, rope_slabs, n_slabs):
    if not rope_slabs:
        y = jnp.dot(lhs, w_ref[...].astype(BF16), preferred_element_type=F32)
        if res_ref is not None:
            y = y + res_ref[...]
        o_ref[...] = y.astype(o_ref.dtype)
        return
    cos = cos_ref[...]
    sin = sin_ref[...]
    n_chunk = 2 if n_slabs % 2 == 0 else 1
    for c0 in range(0, n_slabs, n_chunk):
        cols = slice(c0 * LANES, (c0 + n_chunk) * LANES)
        y = jnp.dot(lhs, w_ref[:, cols].astype(BF16), preferred_element_type=F32)
        if res_ref is not None:
            y = y + res_ref[:, cols]
        for s in range(n_chunk):
            ys = y[:, s * LANES:(s + 1) * LANES]
            if c0 + s < rope_slabs:
                ys = _rope_slab(ys, cos, sin)
            o_ref[:, (c0 + s) * LANES:(c0 + s + 1) * LANES] = ys.astype(o_ref.dtype)


def _matmul_kernel(*refs, n_x, n_res, split_tile, row_axis, has_norm, rope_slabs, n_slabs):
    it = iter(refs)
    x_refs = [next(it) for _ in range(n_x)]
    g_ref = next(it) if has_norm else None
    w_ref = next(it)
    res_refs = [next(it) for _ in range(n_res)]
    cos_ref = next(it) if rope_slabs else None
    sin_ref = next(it) if rope_slabs else None
    o_ref = next(it)
    xn_ref = next(it) if has_norm else None

    if has_norm:
        @pl.when(pl.program_id(1) == 0)
        def _():
            x = x_refs[0][...]
            xn_ref[...] = (x * _rms_scale(x) * g_ref[...]).astype(BF16)

    def emit(lhs_ref, res_ref):
        _project(lhs_ref[...], w_ref, res_ref, cos_ref, sin_ref, o_ref, rope_slabs=rope_slabs, n_slabs=n_slabs)

    if n_x == 1:
        emit(xn_ref if has_norm else x_refs[0], res_refs[0] if n_res else None)
    else:
        i = pl.program_id(row_axis)
        pl.when(i < split_tile)(functools.partial(emit, x_refs[0], res_refs[0] if n_res else None))
        pl.when(i >= split_tile)(functools.partial(emit, x_refs[1], res_refs[-1] if n_res else None))


def _matmul(xs, w, *, gain=None, res=None, rope=None, rope_cols=0, out_dtype=F32, tm, tn, name,
            weights_outer=False):
    xs = xs if isinstance(xs, (tuple, list)) else (xs,)
    ress = () if res is None else (res if isinstance(res, (tuple, list)) else (res,))
    k = xs[0].shape[1]
    m = sum(x.shape[0] for x in xs)
    n = (w.shape[1] // tn) * tn
    assert all(x.shape[0] % tm == 0 for x in xs) and w.shape[0] == k
    has_norm = gain is not None
    assert not (has_norm and len(xs) > 1)
    assert len(ress) <= len(xs) and all(r.shape[0] == x.shape[0] for r, x in zip(ress, xs) if len(ress) > 1)
    split_tile = xs[0].shape[0] // tm
    rope_slabs = 0
    if rope is not None:
        assert rope_cols == n or tn == n
        rope_slabs = min(rope_cols, tn) // LANES
    assert not (has_norm and weights_outer)

    def spec(shape, f):
        return pl.BlockSpec(shape, (lambda a, b: f(b, a)) if weights_outer else f)

    def row_specs(arrays, width, col):
        if len(arrays) == 1:
            return [spec((tm, width), lambda i, j: (i, col(j)))]
        return [spec((tm, width), lambda i, j: (jnp.minimum(i, split_tile - 1), col(j))),
                spec((tm, width), lambda i, j: (jnp.maximum(i - split_tile, 0), col(j)))]

    in_specs = row_specs(xs, k, lambda j: 0)
    args = list(xs)
    if has_norm:
        in_specs.append(spec((1, k), lambda i, j: (0, 0)))
        args.append(gain)
    in_specs.append(spec((k, tn), lambda i, j: (0, j)))
    args.append(w)
    in_specs += row_specs(ress, tn, lambda j: j) if ress else []
    args += list(ress)
    if rope_slabs:
        in_specs += [spec((tm, LANES), lambda i, j: (i, 0))] * 2
        args += list(rope)
    kern = functools.partial(_matmul_kernel, n_x=len(xs), n_res=len(ress), split_tile=split_tile,
                             row_axis=1 if weights_outer else 0, has_norm=has_norm,
                             rope_slabs=rope_slabs, n_slabs=tn // LANES)
    grid = (n // tn, m // tm) if weights_outer else (m // tm, n // tn)
    return pl.pallas_call(
        kern,
        grid=grid,
        in_specs=in_specs,
        out_specs=spec((tm, tn), lambda i, j: (i, j)),
        out_shape=jax.ShapeDtypeStruct((m, n), out_dtype),
        scratch_shapes=[pltpu.VMEM((tm, k), BF16)] if has_norm else [],
        compiler_params=pltpu.CompilerParams(
            dimension_semantics=("parallel", "arbitrary"), vmem_limit_bytes=VMEM_LIMIT),
        name=name,
    )(*args)


def _mlp_kernel(x_ref, g_ref, wu_ref, wd_ref, o_ref, hn_ref):
    f = pl.program_id(1)

    @pl.when(f == 0)
    def _():
        x = x_ref[...]
        hn_ref[...] = (x * _rms_scale(x) * g_ref[...]).astype(BF16)
        o_ref[...] = x

    u = jnp.dot(hn_ref[...], wu_ref[...].astype(BF16), preferred_element_type=F32)
    a = jnp.square(jnp.maximum(u, 0.0)).astype(BF16)
    o_ref[...] += jnp.dot(a, wd_ref[...].astype(BF16), preferred_element_type=F32)


def _mlp_in_specs(tm, d, tf, layer):
    return [
        pl.BlockSpec((tm, d), lambda i, f: (i, 0)),
        pl.BlockSpec((1, d), lambda i, f: (0, 0)),
        pl.BlockSpec((None, d, tf), lambda i, f: (layer, 0, f)),
        pl.BlockSpec((None, tf, d), lambda i, f: (layer, f, 0)),
    ]


def _mlp(x, gain, w_up, w_down, layer, *, tm, tf, name):
    m, d = x.shape
    dff = w_up.shape[2]
    assert m % tm == 0 and dff % tf == 0
    return pl.pallas_call(
        _mlp_kernel,
        grid=(m // tm, dff // tf),
        in_specs=_mlp_in_specs(tm, d, tf, layer),
        out_specs=pl.BlockSpec((tm, d), lambda i, f: (i, 0)),
        out_shape=jax.ShapeDtypeStruct((m, d), F32),
        scratch_shapes=[pltpu.VMEM((tm, d), BF16)],
        compiler_params=pltpu.CompilerParams(
            dimension_semantics=("parallel", "arbitrary"), vmem_limit_bytes=MLP_VMEM_LIMIT),
        name=name,
    )(x, gain, w_up, w_down)


def _mlp_split_kernel(x_ref, g_ref, wu_ref, wd_ref, fg_ref, yp_ref, ys_ref, acc_ref, hn_ref, sem_ref,
                      *, tm, n_tiles, split_row):
    i = pl.program_id(0)
    f = pl.program_id(1)
    slot = i % 2

    def writebacks(t):
        r0, r1 = t * tm, (t + 1) * tm
        s = t % 2
        cps = []
        if r0 < split_row:
            n = min(r1, split_row) - r0
            cps.append(pltpu.make_async_copy(acc_ref.at[s, 0:n], yp_ref.at[r0:r0 + n], sem_ref.at[s, 0]))
        if r1 > split_row:
            a = max(r0, split_row)
            cps.append(pltpu.make_async_copy(acc_ref.at[s, a - r0:tm], ys_ref.at[a - split_row:r1 - split_row],
                                             sem_ref.at[s, 1]))
        return cps

    def start_tile(t):
        for cp in writebacks(t):
            cp.start()

    def wait_tile(t):
        for cp in writebacks(t):
            cp.wait()

    @pl.when(f == 0)
    def _():
        for t in range(n_tiles - 2):
            pl.when(i == t + 2)(functools.partial(wait_tile, t))
        x = x_ref[...]
        hn_ref[...] = (x * _rms_scale(x) * g_ref[...]).astype(BF16)
        acc_ref[slot] = x

    u = jnp.dot(hn_ref[...], wu_ref[...].astype(BF16), preferred_element_type=F32)
    a = jnp.square(jnp.maximum(u, 0.0)).astype(BF16)
    acc_ref[slot] += jnp.dot(a, wd_ref[...].astype(BF16), preferred_element_type=F32)

    @pl.when(f == pl.num_programs(1) - 1)
    def _():
        y = acc_ref[slot]
        acc_ref[slot] = y * _rms_scale(y) * fg_ref[...]
        for t in range(n_tiles):
            pl.when(i == t)(functools.partial(start_tile, t))

        @pl.when(i == n_tiles - 1)
        def _():
            for t in range(max(n_tiles - 2, 0), n_tiles):
                wait_tile(t)


def _mlp_split(x, gain, w_up, w_down, layer, final_gain, *, split_row, tm, tf, name):
    m, d = x.shape
    dff = w_up.shape[2]
    assert m % tm == 0 and dff % tf == 0 and split_row % SUBLANES == 0 and 0 < split_row < m
    n_tiles = m // tm
    return pl.pallas_call(
        functools.partial(_mlp_split_kernel, tm=tm, n_tiles=n_tiles, split_row=split_row),
        grid=(n_tiles, dff // tf),
        in_specs=_mlp_in_specs(tm, d, tf, layer) + [pl.BlockSpec((1, d), lambda i, f: (0, 0))],
        out_specs=[pl.BlockSpec(memory_space=pl.ANY), pl.BlockSpec(memory_space=pl.ANY)],
        out_shape=[jax.ShapeDtypeStruct((split_row, d), F32), jax.ShapeDtypeStruct((m - split_row, d), F32)],
        scratch_shapes=[pltpu.VMEM((2, tm, d), F32), pltpu.VMEM((tm, d), BF16),
                        pltpu.SemaphoreType.DMA((2, 2))],
        compiler_params=pltpu.CompilerParams(
            dimension_semantics=("arbitrary", "arbitrary"), vmem_limit_bytes=MLP_VMEM_LIMIT),
        name=name,
    )(x, gain, w_up, w_down, final_gain)


def _gdn_kernel(xq_ref, xk_ref, xv_ref, z_ref, gates_ref, alog_ref, dt_ref,
                pq_ref, pk_ref, pv_ref, wq_ref, wk_ref, wv_ref, onorm_ref, s0_ref,
                o_ref, s_ref, hq_ref, hk_ref, hv_ref, padq_ref, padk_ref, padv_ref, *, c, hb):
    n = pl.program_id(2)
    halo = SUBLANES

    @pl.when(n == 0)
    def _():
        for pad_ref, prev_ref in ((padq_ref, pq_ref), (padk_ref, pk_ref), (padv_ref, pv_ref)):
            pad_ref[0:halo, :] = jnp.zeros((halo, pad_ref.shape[1]), F32)
            pad_ref[halo - (CONV_W - 1):halo, :] = prev_ref[0]
        s_ref[...] = s0_ref[...]

    def conv(x_ref, pad_ref, w_ref, hist_ref):
        pad_ref[halo:halo + c, :] = x_ref[...]
        xp = pad_ref[...]
        x1 = pltpu.roll(xp, 1, 0)
        a = xp * w_ref[3:4, :] + x1 * w_ref[2:3, :]
        b = xp * w_ref[1:2, :] + x1 * w_ref[0:1, :]
        acc = (a + pltpu.roll(b, 2, 0))[halo:, :]
        tail = xp[c:c + halo, :]
        pad_ref[0:halo, :] = tail
        hist_ref[0] = tail
        return _silu(acc)

    cq = conv(xq_ref, padq_ref, wq_ref, hq_ref)
    ck = conv(xk_ref, padk_ref, wk_ref, hk_ref)
    cv = conv(xv_ref, padv_ref, wv_ref, hv_ref)

    gates = gates_ref[...]
    beta_all = 1.0 / (1.0 + jnp.exp(-gates))
    ga = gates + dt_ref[...]
    softplus = jnp.maximum(ga, 0.0) + jnp.log(1.0 + jnp.exp(-jnp.abs(ga)))
    g_all = -jnp.exp(alog_ref[...]) * softplus
    row = lax.broadcasted_iota(jnp.int32, (c, c), 0)
    col = lax.broadcasted_iota(jnp.int32, (c, c), 1)
    incl = row >= col
    strict = row > col
    tril = jnp.where(incl, 1.0, 0.0).astype(F32)
    gc_all = jnp.dot(tril, g_all, preferred_element_type=F32,
                     precision=lax.Precision.HIGHEST)
    r128 = lax.broadcasted_iota(jnp.int32, (LANES, LANES), 0)
    c128 = lax.broadcasted_iota(jnp.int32, (LANES, LANES), 1)
    eye128 = jnp.where(r128 == c128, 1.0, 0.0).astype(F32)
    gc_t = lax.dot_general(eye128, gc_all, _NT, preferred_element_type=F32,
                           precision=lax.Precision.HIGHEST)

    heads = range(hb)
    sls = [slice(h * LANES, (h + 1) * LANES) for h in heads]
    q = [cq[:, sl] for sl in sls]
    k = [ck[:, sl] for sl in sls]
    v = [cv[:, sl] for sl in sls]
    q = [x * (lax.rsqrt(jnp.sum(x * x, axis=-1, keepdims=True) + EPS) * (DK ** -0.5)) for x in q]
    k = [x * lax.rsqrt(jnp.sum(x * x, axis=-1, keepdims=True) + EPS) for x in k]
    bcol = [beta_all[:, h:h + 1] for h in heads]
    gcol = [gc_all[:, hb + h:hb + h + 1] for h in heads]
    grow = [gc_t[hb + h:hb + h + 1, :] for h in heads]
    glast = [gc_all[c - 1:c, hb + h:hb + h + 1] for h in heads]
    decay = [jnp.exp(jnp.where(incl, gcol[h] - grow[h], -jnp.inf)) for h in heads]
    kb = [k[h] * bcol[h] for h in heads]
    eye = jnp.where(row == col, 1.0, 0.0).astype(F32)
    qkk = [_dot(jnp.concatenate([q[h], kb[h]], axis=0), k[h], _NT) for h in heads]
    qk = [qkk[h][:c] * decay[h] for h in heads]
    p = [-jnp.where(strict, qkk[h][c:] * decay[h], 0.0) for h in heads]
    t = [eye + x for x in p]
    levels = max(int(math.ceil(math.log2(c))) - 1, 0)
    if levels:
        p = [_dot(x, x) for x in p]
    for _ in range(1, levels):
        r = [_dot(jnp.concatenate([p[h], t[h]], axis=0), p[h]) for h in heads]
        t = [t[h] + r[h][c:] for h in heads]
        p = [r[h][:c] for h in heads]
    if levels:
        t = [t[h] + _dot(t[h], p[h]) for h in heads]
    egc = [jnp.exp(x) for x in gcol]
    rhs = [jnp.concatenate([v[h] * bcol[h], kb[h] * egc[h]], axis=-1) for h in heads]
    sol = [_dot(t[h], rhs[h]) for h in heads]
    s = [s_ref[0, h] for h in heads]
    wq = [_dot(jnp.concatenate([sol[h][:, DV:], q[h] * egc[h]], axis=0), s[h]) for h in heads]
    v_new = [sol[h][:, :DV] - wq[h][:c] for h in heads]
    o = [wq[h][c:] + _dot(qk[h], v_new[h]) for h in heads]
    s_new = [s[h] * jnp.exp(glast[h]) + _dot(k[h] * jnp.exp(glast[h] - gcol[h]), v_new[h], _TN)
             for h in heads]
    onorm = onorm_ref[...]
    for h in heads:
        s_ref[0, h] = s_new[h]
        oh = o[h] * _rms_scale(o[h]) * onorm
        o_ref[:, sls[h]] = (oh * _silu(z_ref[:, sls[h]])).astype(o_ref.dtype)


def _gdn(qkvz, gates, alog_row, dt_row, conv_prev, conv_w, onorm_row, s0,
         *, row0, batch, length, c, hb, name):
    nblk = length // c
    ng = HA // hb
    hw = hb * LANES
    rb0 = row0 // c
    assert row0 % c == 0 and length % c == 0 and c >= SUBLANES and CONV_W == 4

    def rows(b, g, n):
        return rb0 + b * nblk + n

    x_spec = lambda part: pl.BlockSpec((c, hw), lambda b, g, n: (rows(b, g, n), part * ng + g))
    prev_spec = lambda part: pl.BlockSpec((1, CONV_W - 1, hw), lambda b, g, n: (b, 0, part * ng + g))
    w_spec = lambda part: pl.BlockSpec((CONV_W, hw), lambda b, g, n: (0, part * ng + g))
    gate_row_spec = pl.BlockSpec((1, LANES), lambda b, g, n: (0, g))
    hist_spec = pl.BlockSpec((1, SUBLANES, hw), lambda b, g, n: (b, 0, g))
    hist_shape = jax.ShapeDtypeStruct((batch, SUBLANES, HA * LANES), F32)
    in_specs = [
        x_spec(0), x_spec(1), x_spec(2), x_spec(3),
        pl.BlockSpec((c, LANES), lambda b, g, n: (rows(b, g, n), g)),
        gate_row_spec, gate_row_spec,
        prev_spec(0), prev_spec(1), prev_spec(2),
        w_spec(0), w_spec(1), w_spec(2),
        pl.BlockSpec((1, LANES), lambda b, g, n: (0, 0)),
        pl.BlockSpec((1, hb, DK, DV), lambda b, g, n: (b, g, 0, 0)),
    ]
    out_specs = [
        pl.BlockSpec((c, hw), lambda b, g, n: (b * nblk + n, g)),
        pl.BlockSpec((1, hb, DK, DV), lambda b, g, n: (b, g, 0, 0)),
        hist_spec, hist_spec, hist_spec,
    ]
    return pl.pallas_call(
        functools.partial(_gdn_kernel, c=c, hb=hb),
        grid=(batch, ng, nblk),
        in_specs=in_specs,
        out_specs=out_specs,
        out_shape=[jax.ShapeDtypeStruct((batch * length, HA * DV), BF16),
                   jax.ShapeDtypeStruct((batch, HA, DK, DV), F32),
                   hist_shape, hist_shape, hist_shape],
        scratch_shapes=[pltpu.VMEM((c + SUBLANES, hw), F32)] * 3,
        compiler_params=pltpu.CompilerParams(
            dimension_semantics=("parallel", "parallel", "arbitrary"),
            vmem_limit_bytes=VMEM_LIMIT),
        name=name,
    )(qkvz, qkvz, qkvz, qkvz, gates, alog_row, dt_row,
      conv_prev, conv_prev, conv_prev, conv_w, conv_w, conv_w, onorm_row, s0)


def _attn_kernel(sinks_ref, q_ref, kp_ref, vp_ref, kc_ref, vc_ref, o_ref, *, tq, nb, chunk_mask):
    i = pl.program_id(1)
    nk = WINDOW + tq
    lane_k = lax.broadcasted_iota(jnp.int32, (nk, LANES), 1)
    lane_q = lax.broadcasted_iota(jnp.int32, (tq, LANES), 1)
    if chunk_mask:
        qc = lax.broadcasted_iota(jnp.int32, (tq, nk), 0) // CHUNK
        kc = lax.broadcasted_iota(jnp.int32, (tq, nk), 1) // CHUNK
        w_ch = WINDOW // CHUNK
        first_kc = jnp.where(i > 0, 0, w_ch)
        valid = (kc >= jnp.maximum(qc, first_kc)) & (kc <= qc + w_ch)
    scale = HD ** -0.5
    assert math.log2(HD) % 2 == 0
    per_slab = LANES // HD
    group = N_Q // N_KV
    slabs_per_kv = group // per_slab
    half_masks = [(lane_q >= r * HD) & (lane_q < (r + 1) * HD) for r in range(per_slab)]

    units = [(b, h) for b in range(nb) for h in range(N_KV)]
    kdup, vdup, qs = [], [], []
    for b, h in units:
        ksl = slice((h // per_slab) * LANES, (h // per_slab + 1) * LANES)
        kslab = jnp.concatenate([kp_ref[b * WINDOW:(b + 1) * WINDOW, ksl],
                                 kc_ref[b * tq:(b + 1) * tq, ksl]], axis=0)
        vslab = jnp.concatenate([vp_ref[b * WINDOW:(b + 1) * WINDOW, ksl],
                                 vc_ref[b * tq:(b + 1) * tq, ksl]], axis=0)
        first = (lane_k < HD) == (h % per_slab == 0)
        kdup.append(jnp.where(first, kslab, pltpu.roll(kslab, HD, 1)).astype(BF16))
        vdup.append(jnp.where(first, vslab, pltpu.roll(vslab, HD, 1)).astype(BF16))
        pieces = []
        for s in range(slabs_per_kv):
            slab = h * slabs_per_kv + s
            q2 = q_ref[b * tq:(b + 1) * tq, slab * LANES:(slab + 1) * LANES]
            pieces += [jnp.where(mk, q2, jnp.zeros_like(q2)) for mk in half_masks]
        qs.append(jnp.concatenate(pieces, axis=0) * scale)
    sc = [lax.dot_general(qs[u], kdup[u], _NT, preferred_element_type=F32) for u in range(len(units))]
    ps = []
    for u, (b, h) in enumerate(units):
        blocks = []
        for r in range(group):
            sr = sc[u][r * tq:(r + 1) * tq]
            if chunk_mask:
                sr = jnp.where(valid, sr, -jnp.inf)
            sk = sinks_ref[h * group + r]
            mx = jnp.maximum(jnp.max(sr, axis=-1, keepdims=True), sk)
            e = jnp.exp(sr - mx)
            den = jnp.sum(e, axis=-1, keepdims=True) + jnp.exp(sk - mx)
            blocks.append((e / den).astype(BF16))
        ps.append(jnp.concatenate(blocks, axis=0))
    pv = [jnp.dot(ps[u], vdup[u], preferred_element_type=F32) for u in range(len(units))]
    for u, (b, h) in enumerate(units):
        for s in range(slabs_per_kv):
            slab = h * slabs_per_kv + s
            halves = [pv[u][(s * per_slab + r) * tq:(s * per_slab + r + 1) * tq] for r in range(per_slab)]
            out = halves[-1]
            for r in range(per_slab - 2, -1, -1):
                out = jnp.where(half_masks[r], halves[r], out)
            o_ref[b * tq:(b + 1) * tq, slab * LANES:(slab + 1) * LANES] = out.astype(o_ref.dtype)


def _attention(sinks, q, kprev, vprev, kprev_col, vprev_col, kv, *, row0, batch, length,
               tq, nb, chunk_mask, name):
    nq = length // tq
    kvw = N_KV * HD
    assert row0 % (nb * tq) == 0 and length % tq == 0 and batch % nb == 0 and (nb == 1 or nq == 1)
    rb0 = row0 // (nb * tq)

    def cur(b, i):
        return rb0 + b * nq + i

    def prev(b, i):
        return b * nq + jnp.maximum(i - 1, 0)

    in_specs = [
        pl.BlockSpec(memory_space=pltpu.SMEM),
        pl.BlockSpec((nb * tq, N_Q * HD), lambda b, i: (cur(b, i), 0)),
        pl.BlockSpec((nb * WINDOW, kvw), lambda b, i: (prev(b, i), kprev_col)),
        pl.BlockSpec((nb * WINDOW, kvw), lambda b, i: (prev(b, i), vprev_col)),
        pl.BlockSpec((nb * tq, kvw), lambda b, i: (cur(b, i), 0)),
        pl.BlockSpec((nb * tq, kvw), lambda b, i: (cur(b, i), 1)),
    ]
    return pl.pallas_call(
        functools.partial(_attn_kernel, tq=tq, nb=nb, chunk_mask=chunk_mask),
        grid=(batch // nb, nq),
        in_specs=in_specs,
        out_specs=pl.BlockSpec((nb * tq, N_Q * HD), lambda b, i: (b * nq + i, 0)),
        out_shape=jax.ShapeDtypeStruct((batch * length, N_Q * HD), BF16),
        compiler_params=pltpu.CompilerParams(
            dimension_semantics=("parallel", "arbitrary"), vmem_limit_bytes=VMEM_LIMIT),
        name=name,
    )(sinks, q, kprev, vprev, kv, kv)


def _rope_tables(pos):
    half = HD // 2
    inv = 1.0 / (ROPE_THETA ** (jnp.arange(half, dtype=F32) / half))
    ang = pos.astype(F32)[:, None] * jnp.tile(inv, LANES // half)[None, :]
    sign = jnp.tile(jnp.concatenate([-jnp.ones((half,), F32), jnp.ones((half,), F32)]), LANES // HD)
    return jnp.cos(ang), jnp.sin(ang) * sign[None, :]


def _gate_layout(t, hb):
    lead = t.shape[:-1]
    ng = HA // hb
    if ng == 1:
        return jnp.pad(t, [(0, 0)] * len(lead) + [(0, LANES - 2 * HA)])
    beta = t[..., :HA].reshape(lead + (ng, hb))
    dec = t[..., HA:].reshape(lead + (ng, hb))
    pad = jnp.zeros(lead + (ng, LANES - 2 * hb), t.dtype)
    return jnp.concatenate([beta, dec, pad], axis=-1).reshape(lead + (ng * LANES,))


def kernel(x_prompt, x_sample, cache_conv, state_gdn, cache_k, cache_v, attn_norm, mlp_norm, final_norm, a_w_in, a_conv_w, a_log, a_dt_bias, a_o_norm, a_w_out, kv_norm, w_kv, b_w_q, b_sinks, b_w_o, w_up, w_down):
    bp, lp, d = x_prompt.shape
    bs, ls, _ = x_sample.shape
    mp, ms = bp * lp, bs * ls
    m = mp + ms
    hb = GDN_HEADS_PER_STEP
    dqk, dvw = HA * DK, HA * DV
    conv_ch = 2 * dqk + dvw
    keep = CONV_W - 1
    assert a_w_in.shape[0] == 1 and b_w_q.shape[0] == 1 and lp % WINDOW == 0 and ls >= keep

    xp = x_prompt.reshape(mp, d)
    xs = x_sample.reshape(ms, d)
    tm = _pick_tile(m, ROW_TILES)
    tm2 = _pick_tile(math.gcd(mp, ms), ROW_TILES[1:])
    tm_p = _pick_tile(mp, (1024,) + ROW_TILES[1:])
    tm_s = _pick_tile(ms, ROW_TILES[1:])
    tn_d = _pick_tile(d, COL_TILES[1:])

    w_in = a_w_in[0].astype(BF16)
    w_gate = _gate_layout(w_in[:, conv_ch + dvw:], hb)
    g0 = attn_norm[0].reshape(1, d)
    tn_in = _pick_tile(conv_ch + dvw, COL_TILES)
    zeros_gate = jnp.zeros((HA,), F32)
    alog_row = _gate_layout(jnp.concatenate([zeros_gate, a_log[0]]), hb).reshape(1, -1)
    dt_row = _gate_layout(jnp.concatenate([zeros_gate, a_dt_bias[0]]), hb).reshape(1, -1)
    onorm_row = a_o_norm[0].reshape(1, DV)
    streams = (("prompt", xp, tm_p, bp, lp, jnp.zeros((bp, keep, conv_ch), F32),
                jnp.zeros((bp, HA, DK, DV), F32)),
               ("sample", xs, tm_s, bs, ls, cache_conv[0], state_gdn[0]))
    mixed = []
    for tag, xr, tmr, nbatch, length, conv_prev, s0 in streams:
        qkvz = _matmul(xr, w_in, gain=g0, tm=tmr, tn=tn_in, name="gdn_in_proj_" + tag)
        gates = _matmul(xr, w_gate, gain=g0, tm=tmr, tn=w_gate.shape[1], name="gdn_gate_proj_" + tag)
        mixed.append(_gdn(qkvz, gates, alog_row, dt_row, conv_prev, a_conv_w[0], onorm_row, s0,
                          row0=0, batch=nbatch, length=length, c=min(CHUNK, length), hb=hb,
                          name="gdn_" + tag))
    (o_p, gdn_p, *hist_p), (o_s, gdn_s, *hist_s) = mixed
    tn_o = _pick_tile(d, COL_TILES)
    x = _matmul((o_p, o_s), a_w_out[0], res=(xp, xs), tm=tm2, tn=tn_o, weights_outer=True,
                name="gdn_out_proj")
    tf = _pick_tile(w_up.shape[2], COL_TILES[1:])
    x = _mlp(x, mlp_norm[0].reshape(1, d), w_up, w_down, 0, tm=tm, tf=tf, name="mlp0")

    cos_p, sin_p = _rope_tables(jnp.arange(lp))
    cos_s, sin_s = _rope_tables(PAST_LEN + jnp.arange(ls))
    rope = (jnp.concatenate([cos_p] * bp + [cos_s] * bs, axis=0),
            jnp.concatenate([sin_p] * bp + [sin_s] * bs, axis=0))
    kvw = N_KV * HD
    kv = _matmul(x, w_kv, gain=kv_norm.reshape(1, d), rope=rope, rope_cols=kvw,
                 tm=tm, tn=2 * kvw, name="kv_proj")
    q = _matmul(x, b_w_q[0], gain=attn_norm[1].reshape(1, d), rope=rope,
                rope_cols=N_Q * HD, out_dtype=BF16, tm=tm, tn=tn_d, name="q_proj")
    sinks = b_sinks[0].astype(F32)
    a_p = _attention(sinks, q, kv, kv, 0, 1, kv, row0=0, batch=bp, length=lp, tq=WINDOW, nb=1,
                     chunk_mask=True, name="attn_prompt")
    nb = _pick_tile(bs, (ATTN_SAMPLE_BATCHES_PER_STEP, 2, 1))
    a_s = _attention(sinks, q, cache_k.reshape(bs * WINDOW, kvw), cache_v.reshape(bs * WINDOW, kvw),
                     0, 0, kv, row0=mp, batch=bs, length=ls, tq=ls, nb=nb,
                     chunk_mask=False, name="attn_sample")
    x = _matmul((a_p, a_s), b_w_o[0], res=x, tm=tm2, tn=tn_o, weights_outer=True, name="attn_out_proj")
    y_p, y_s = _mlp_split(x, mlp_norm[1].reshape(1, d), w_up, w_down, 1, final_norm.reshape(1, d),
                          split_row=mp, tm=tm, tf=tf, name="mlp1")

    y_prompt = y_p.reshape(bp, lp, d)
    y_sample = y_s.reshape(bs, ls, d)
    conv_p = jnp.concatenate([t[:, SUBLANES - keep:] for t in hist_p], axis=-1)[None]
    conv_s = jnp.concatenate([t[:, SUBLANES - keep:] for t in hist_s], axis=-1)[None]
    kv_p = kv[:mp].reshape(bp, lp, 2 * kvw)[:, lp - WINDOW:]
    kv_s = kv[mp:].reshape(bs, ls, 2 * kvw)
    k_p = kv_p[..., :kvw].reshape(bp, WINDOW, N_KV, HD)
    v_p = kv_p[..., kvw:].reshape(bp, WINDOW, N_KV, HD)
    k_s = kv_s[..., :kvw].reshape(bs, ls, N_KV, HD)
    v_s = kv_s[..., kvw:].reshape(bs, ls, N_KV, HD)
    return (y_prompt, y_sample, conv_p, gdn_p[None], k_p, v_p, conv_s, gdn_s[None], k_s, v_s)
```

```python
import functools
import math

import jax
import jax.numpy as jnp
from jax import lax
from jax.experimental import pallas as pl
from jax.experimental.pallas import tpu as pltpu

F32 = jnp.float32
BF16 = jnp.bfloat16

EPS = 1e-6
CHUNK = 64
WINDOW = 128
PAST_LEN = 4096
ROPE_THETA = 10000.0
HA, DK, DV = 16, 128, 128
N_Q, N_KV, HD = 32, 4, 64
CONV_W = 4
LANES = 128
SUBLANES = 8
VMEM_LIMIT = 56 * 1024 * 1024
MLP_VMEM_LIMIT = 60 * 1024 * 1024
GDN_HEADS_PER_STEP = 16
ATTN_SAMPLE_BATCHES_PER_STEP = 4
ROW_TILES = (1088, 512, 256, 128, 64, 32, 16)
COL_TILES = (1024, 512, 256, 128)


def _pick_tile(n, candidates):
    for c in candidates:
        if n % c == 0:
            return c
    return n


def _dot(a, b, dims=(((1,), (0,)), ((), ()))):
    return lax.dot_general(a.astype(BF16), b.astype(BF16), dims, preferred_element_type=F32)


_NT = (((1,), (1,)), ((), ()))
_TN = (((0,), (0,)), ((), ()))


def _rms_scale(x):
    return lax.rsqrt(jnp.mean(x * x, axis=-1, keepdims=True) + EPS)


def _silu(x):
    h = 0.5 * x
    return h + h * jnp.tanh(h)


def _rope_slab(y, cos, sin_signed):
    lane = lax.broadcasted_iota(jnp.int32, y.shape, 1)
    up = pltpu.roll(y, 32, 1)
    down = pltpu.roll(y, LANES - 32, 1)
    swapped = jnp.where((lane % HD) < HD // 2, down, up)
    return y * cos + swapped * sin_signed


def _project(lhs, w_ref, res_ref, cos_ref, sin_ref, o_ref, *, rope_slabs, n_slabs):
    if not rope_slabs:
        y = jnp.dot(lhs, w_ref[...].astype(BF16), preferred_element_type=F32)
        if res_ref is not None:
            y = y + res_ref[...]
        o_ref[...] = y.astype(o_ref.dtype)
        return
    cos = cos_ref[...]
    sin = sin_ref[...]
    n_chunk = 2 if n_slabs % 2 == 0 else 1
    for c0 in range(0, n_slabs, n_chunk):
        cols = slice(c0 * LANES, (c0 + n_chunk) * LANES)
        y = jnp.dot(lhs, w_ref[:, cols].astype(BF16), preferred_element_type=F32)
        if res_ref is not None:
            y = y + res_ref[:, cols]
        for s in range(n_chunk):
            ys = y[:, s * LANES:(s + 1) * LANES]
            if c0 + s < rope_slabs:
                ys = _rope_slab(ys, cos, sin)
            o_ref[:, (c0 + s) * LANES:(c0 + s + 1) * LANES] = ys.astype(o_ref.dtype)


def _matmul_kernel(*refs, n_x, n_res, split_tile, row_axis, has_norm, rope_slabs, n_slabs):
    it = iter(refs)
    x_refs = [next(it) for _ in range(n_x)]
    g_ref = next(it) if has_norm else None
    w_ref = next(it)
    res_refs = [next(it) for _ in range(n_res)]
    cos_ref = next(it) if rope_slabs else None
    sin_ref = next(it) if rope_slabs else None
    o_ref = next(it)
    xn_ref = next(it) if has_norm else None

    if has_norm:
        @pl.when(pl.program_id(1) == 0)
        def _():
            x = x_refs[0][...]
            xn_ref[...] = (x * _rms_scale(x) * g_ref[...]).astype(BF16)

    def emit(lhs_ref, res_ref):
        _project(lhs_ref[...], w_ref, res_ref, cos_ref, sin_ref, o_ref, rope_slabs=rope_slabs, n_slabs=n_slabs)

    if n_x == 1:
        emit(xn_ref if has_norm else x_refs[0], res_refs[0] if n_res else None)
    else:
        i = pl.program_id(row_axis)
        pl.when(i < split_tile)(functools.partial(emit, x_refs[0], res_refs[0] if n_res else None))
        pl.when(i >= split_tile)(functools.partial(emit, x_refs[1], res_refs[-1] if n_res else None))


def _matmul(xs, w, *, gain=None, res=None, rope=None, rope_cols=0, out_dtype=F32, tm, tn, name,
            weights_outer=False):
    xs = xs if isinstance(xs, (tuple, list)) else (xs,)
    ress = () if res is None else (res if isinstance(res, (tuple, list)) else (res,))
    k = xs[0].shape[1]
    m = sum(x.shape[0] for x in xs)
    n = (w.shape[1] // tn) * tn
    assert all(x.shape[0] % tm == 0 for x in xs) and w.shape[0] == k
    has_norm = gain is not None
    assert not (has_norm and len(xs) > 1)
    assert len(ress) <= len(xs) and all(r.shape[0] == x.shape[0] for r, x in zip(ress, xs) if len(ress) > 1)
    split_tile = xs[0].shape[0] // tm
    rope_slabs = 0
    if rope is not None:
        assert rope_cols == n or tn == n
        rope_slabs = min(rope_cols, tn) // LANES
    assert not (has_norm and weights_outer)

    def spec(shape, f):
        return pl.BlockSpec(shape, (lambda a, b: f(b, a)) if weights_outer else f)

    def row_specs(arrays, width, col):
        if len(arrays) == 1:
            return [spec((tm, width), lambda i, j: (i, col(j)))]
        return [spec((tm, width), lambda i, j: (jnp.minimum(i, split_tile - 1), col(j))),
                spec((tm, width), lambda i, j: (jnp.maximum(i - split_tile, 0), col(j)))]

    in_specs = row_specs(xs, k, lambda j: 0)
    args = list(xs)
    if has_norm:
        in_specs.append(spec((1, k), lambda i, j: (0, 0)))
        args.append(gain)
    in_specs.append(spec((k, tn), lambda i, j: (0, j)))
    args.append(w)
    in_specs += row_specs(ress, tn, lambda j: j) if ress else []
    args += list(ress)
    if rope_slabs:
        in_specs += [spec((tm, LANES), lambda i, j: (i, 0))] * 2
        args += list(rope)
    kern = functools.partial(_matmul_kernel, n_x=len(xs), n_res=len(ress), split_tile=split_tile,
                             row_axis=1 if weights_outer else 0, has_norm=has_norm,
                             rope_slabs=rope_slabs, n_slabs=tn // LANES)
    grid = (n // tn, m // tm) if weights_outer else (m // tm, n // tn)
    return pl.pallas_call(
        kern,
        grid=grid,
        in_specs=in_specs,
        out_specs=spec((tm, tn), lambda i, j: (i, j)),
        out_shape=jax.ShapeDtypeStruct((m, n), out_dtype),
        scratch_shapes=[pltpu.VMEM((tm, k), BF16)] if has_norm else [],
        compiler_params=pltpu.CompilerParams(
            dimension_semantics=("parallel", "arbitrary"), vmem_limit_bytes=VMEM_LIMIT),
        name=name,
    )(*args)


def _in_proj_kernel(x_ref, g_ref, w_ref, wt_ref, o_ref, ot_ref, xn_ref, *, n_main):
    j = pl.program_id(1)

    @pl.when(j == 0)
    def _():
        x = x_ref[...]
        xn_ref[...] = (x * _rms_scale(x) * g_ref[...]).astype(BF16)
        ot_ref[...] = jnp.zeros_like(ot_ref)

    @pl.when(j < n_main)
    def _():
        o_ref[...] = jnp.dot(xn_ref[...], w_ref[...].astype(BF16), preferred_element_type=F32)

    @pl.when(j == n_main)
    def _():
        ot_ref[...] = jnp.dot(xn_ref[...], wt_ref[...].astype(BF16), preferred_element_type=F32)


def _in_proj(x, gain, w, w_tail, *, tm, tn, name):
    m, k = x.shape
    n_main = w.shape[1] // tn
    nt = w_tail.shape[1]
    assert m % tm == 0 and n_main > 0 and nt % LANES == 0
    main_col = lambda j: jnp.minimum(j, n_main - 1)
    return pl.pallas_call(
        functools.partial(_in_proj_kernel, n_main=n_main),
        grid=(m // tm, n_main + 1),
        in_specs=[pl.BlockSpec((tm, k), lambda i, j: (i, 0)),
                  pl.BlockSpec((1, k), lambda i, j: (0, 0)),
                  pl.BlockSpec((k, tn), lambda i, j: (0, main_col(j))),
                  pl.BlockSpec((k, nt), lambda i, j: (0, 0))],
        out_specs=[pl.BlockSpec((tm, tn), lambda i, j: (i, main_col(j))),
                   pl.BlockSpec((tm, nt), lambda i, j: (i, 0))],
        out_shape=[jax.ShapeDtypeStruct((m, n_main * tn), F32), jax.ShapeDtypeStruct((m, nt), F32)],
        scratch_shapes=[pltpu.VMEM((tm, k), BF16)],
        compiler_params=pltpu.CompilerParams(
            dimension_semantics=("parallel", "arbitrary"), vmem_limit_bytes=VMEM_LIMIT),
        name=name,
    )(x, gain, w, w_tail)


def _mlp_kernel(x_ref, g_ref, wu_ref, wd_ref, o_ref, hn_ref):
    f = pl.program_id(1)

    @pl.when(f == 0)
    def _():
        x = x_ref[...]
        hn_ref[...] = (x * _rms_scale(x) * g_ref[...]).astype(BF16)
        o_ref[...] = x

    u = jnp.dot(hn_ref[...], wu_ref[...].astype(BF16), preferred_element_type=F32)
    a = jnp.square(jnp.maximum(u, 0.0)).astype(BF16)
    o_ref[...] += jnp.dot(a, wd_ref[...].astype(BF16), preferred_element_type=F32)


def _mlp_in_specs(tm, d, tf, layer):
    return [
        pl.BlockSpec((tm, d), lambda i, f: (i, 0)),
        pl.BlockSpec((1, d), lambda i, f: (0, 0)),
        pl.BlockSpec((None, d, tf), lambda i, f: (layer, 0, f)),
        pl.BlockSpec((None, tf, d), lambda i, f: (layer, f, 0)),
    ]


def _mlp(x, gain, w_up, w_down, layer, *, tm, tf, name):
    m, d = x.shape
    dff = w_up.shape[2]
    assert m % tm == 0 and dff % tf == 0
    return pl.pallas_call(
        _mlp_kernel,
        grid=(m // tm, dff // tf),
        in_specs=_mlp_in_specs(tm, d, tf, layer),
        out_specs=pl.BlockSpec((tm, d), lambda i, f: (i, 0)),
        out_shape=jax.ShapeDtypeStruct((m, d), F32),
        scratch_shapes=[pltpu.VMEM((tm, d), BF16)],
        compiler_params=pltpu.CompilerParams(
            dimension_semantics=("parallel", "arbitrary"), vmem_limit_bytes=MLP_VMEM_LIMIT),
        name=name,
    )(x, gain, w_up, w_down)


def _mlp_split_kernel(x_ref, g_ref, wu_ref, wd_ref, fg_ref, yp_ref, ys_ref, acc_ref, hn_ref, sem_ref,
                      *, tm, n_tiles, split_row):
    i = pl.program_id(0)
    f = pl.program_id(1)
    slot = i % 2

    def writebacks(t):
        r0, r1 = t * tm, (t + 1) * tm
        s = t % 2
        cps = []
        if r0 < split_row:
            n = min(r1, split_row) - r0
            cps.append(pltpu.make_async_copy(acc_ref.at[s, 0:n], yp_ref.at[r0:r0 + n], sem_ref.at[s, 0]))
        if r1 > split_row:
            a = max(r0, split_row)
            cps.append(pltpu.make_async_copy(acc_ref.at[s, a - r0:tm], ys_ref.at[a - split_row:r1 - split_row],
                                             sem_ref.at[s, 1]))
        return cps

    def start_tile(t):
        for cp in writebacks(t):
            cp.start()

    def wait_tile(t):
        for cp in writebacks(t):
            cp.wait()

    @pl.when(f == 0)
    def _():
        for t in range(n_tiles - 2):
            pl.when(i == t + 2)(functools.partial(wait_tile, t))
        x = x_ref[...]
        hn_ref[...] = (x * _rms_scale(x) * g_ref[...]).astype(BF16)
        acc_ref[slot] = x

    u = jnp.dot(hn_ref[...], wu_ref[...].astype(BF16), preferred_element_type=F32)
    a = jnp.square(jnp.maximum(u, 0.0)).astype(BF16)
    acc_ref[slot] += jnp.dot(a, wd_ref[...].astype(BF16), preferred_element_type=F32)

    @pl.when(f == pl.num_programs(1) - 1)
    def _():
        y = acc_ref[slot]
        acc_ref[slot] = y * _rms_scale(y) * fg_ref[...]
        for t in range(n_tiles):
            pl.when(i == t)(functools.partial(start_tile, t))

        @pl.when(i == n_tiles - 1)
        def _():
            for t in range(max(n_tiles - 2, 0), n_tiles):
                wait_tile(t)


def _mlp_split(x, gain, w_up, w_down, layer, final_gain, *, split_row, tm, tf, name):
    m, d = x.shape
    dff = w_up.shape[2]
    assert m % tm == 0 and dff % tf == 0 and split_row % SUBLANES == 0 and 0 < split_row < m
    n_tiles = m // tm
    return pl.pallas_call(
        functools.partial(_mlp_split_kernel, tm=tm, n_tiles=n_tiles, split_row=split_row),
        grid=(n_tiles, dff // tf),
        in_specs=_mlp_in_specs(tm, d, tf, layer) + [pl.BlockSpec((1, d), lambda i, f: (0, 0))],
        out_specs=[pl.BlockSpec(memory_space=pl.ANY), pl.BlockSpec(memory_space=pl.ANY)],
        out_shape=[jax.ShapeDtypeStruct((split_row, d), F32), jax.ShapeDtypeStruct((m - split_row, d), F32)],
        scratch_shapes=[pltpu.VMEM((2, tm, d), F32), pltpu.VMEM((tm, d), BF16),
                        pltpu.SemaphoreType.DMA((2, 2))],
        compiler_params=pltpu.CompilerParams(
            dimension_semantics=("arbitrary", "arbitrary"), vmem_limit_bytes=MLP_VMEM_LIMIT),
        name=name,
    )(x, gain, w_up, w_down, final_gain)


def _gdn_kernel(xq_ref, xk_ref, xv_ref, z_ref, gates_ref, alog_ref, dt_ref,
                pq_ref, pk_ref, pv_ref, wq_ref, wk_ref, wv_ref, onorm_ref, s0_ref,
                o_ref, s_ref, hq_ref, hk_ref, hv_ref, padq_ref, padk_ref, padv_ref, *, c, hb):
    n = pl.program_id(2)
    halo = SUBLANES

    @pl.when(n == 0)
    def _():
        for pad_ref, prev_ref in ((padq_ref, pq_ref), (padk_ref, pk_ref), (padv_ref, pv_ref)):
            pad_ref[0:halo, :] = jnp.zeros((halo, pad_ref.shape[1]), F32)
            pad_ref[halo - (CONV_W - 1):halo, :] = prev_ref[0]
        s_ref[...] = s0_ref[...]

    def conv(x_ref, pad_ref, w_ref, hist_ref):
        pad_ref[halo:halo + c, :] = x_ref[...]
        xp = pad_ref[...]
        x1 = pltpu.roll(xp, 1, 0)
        a = xp * w_ref[3:4, :] + x1 * w_ref[2:3, :]
        b = xp * w_ref[1:2, :] + x1 * w_ref[0:1, :]
        acc = (a + pltpu.roll(b, 2, 0))[halo:, :]
        tail = xp[c:c + halo, :]
        pad_ref[0:halo, :] = tail
        hist_ref[0] = tail
        return _silu(acc)

    cq = conv(xq_ref, padq_ref, wq_ref, hq_ref)
    ck = conv(xk_ref, padk_ref, wk_ref, hk_ref)
    cv = conv(xv_ref, padv_ref, wv_ref, hv_ref)

    gates = gates_ref[...]
    beta_all = 0.5 + 0.5 * jnp.tanh(0.5 * gates)
    ga = gates + dt_ref[...]
    softplus = jnp.maximum(ga, 0.0) + jnp.log(1.0 + jnp.exp(-jnp.abs(ga)))
    g_all = -jnp.exp(alog_ref[...]) * softplus
    row = lax.broadcasted_iota(jnp.int32, (c, c), 0)
    col = lax.broadcasted_iota(jnp.int32, (c, c), 1)
    incl = row >= col
    strict = row > col
    tril = jnp.where(incl, 1.0, 0.0).astype(F32)
    gc_all = jnp.dot(tril, g_all, preferred_element_type=F32,
                     precision=lax.Precision.HIGHEST)
    r128 = lax.broadcasted_iota(jnp.int32, (LANES, LANES), 0)
    c128 = lax.broadcasted_iota(jnp.int32, (LANES, LANES), 1)
    eye128 = jnp.where(r128 == c128, 1.0, 0.0).astype(F32)
    gc_t = lax.dot_general(eye128, gc_all, _NT, preferred_element_type=F32,
                           precision=lax.Precision.HIGHEST)

    heads = range(hb)
    sls = [slice(h * LANES, (h + 1) * LANES) for h in heads]
    q = [cq[:, sl] for sl in sls]
    k = [ck[:, sl] for sl in sls]
    v = [cv[:, sl] for sl in sls]
    q = [x * (lax.rsqrt(jnp.sum(x * x, axis=-1, keepdims=True) + EPS) * (DK ** -0.5)) for x in q]
    k = [x * lax.rsqrt(jnp.sum(x * x, axis=-1, keepdims=True) + EPS) for x in k]
    bcol = [beta_all[:, h:h + 1] for h in heads]
    gcol = [gc_all[:, hb + h:hb + h + 1] for h in heads]
    grow = [gc_t[hb + h:hb + h + 1, :] for h in heads]
    glast = [gc_all[c - 1:c, hb + h:hb + h + 1] for h in heads]
    decay = [jnp.exp(jnp.where(incl, gcol[h] - grow[h], -jnp.inf)) for h in heads]
    kb = [k[h] * bcol[h] for h in heads]
    eye = jnp.where(row == col, 1.0, 0.0).astype(F32)
    qkk = [_dot(jnp.concatenate([q[h], kb[h]], axis=0), k[h], _NT) for h in heads]
    qk = [qkk[h][:c] * decay[h] for h in heads]
    p = [-jnp.where(strict, qkk[h][c:] * decay[h], 0.0) for h in heads]
    t = [eye + x for x in p]
    levels = max(int(math.ceil(math.log2(c))) - 1, 0)
    if levels:
        p = [_dot(x, x) for x in p]
    for _ in range(1, levels):
        r = [_dot(jnp.concatenate([p[h], t[h]], axis=0), p[h]) for h in heads]
        t = [t[h] + r[h][c:] for h in heads]
        p = [r[h][:c] for h in heads]
    if levels:
        t = [t[h] + _dot(t[h], p[h]) for h in heads]
    egc = [jnp.exp(x) for x in gcol]
    rhs = [jnp.concatenate([v[h] * bcol[h], kb[h] * egc[h]], axis=-1) for h in heads]
    sol = [_dot(t[h], rhs[h]) for h in heads]
    s = [s_ref[0, h] for h in heads]
    wq = [_dot(jnp.concatenate([sol[h][:, DV:], q[h] * egc[h]], axis=0), s[h]) for h in heads]
    v_new = [sol[h][:, :DV] - wq[h][:c] for h in heads]
    o = [wq[h][c:] + _dot(qk[h], v_new[h]) for h in heads]
    s_new = [s[h] * jnp.exp(glast[h]) + _dot(k[h] * jnp.exp(glast[h] - gcol[h]), v_new[h], _TN)
             for h in heads]
    onorm = onorm_ref[...]
    for h in heads:
        s_ref[0, h] = s_new[h]
        oh = o[h] * _rms_scale(o[h]) * onorm
        o_ref[:, sls[h]] = (oh * _silu(z_ref[:, sls[h]])).astype(o_ref.dtype)


def _gdn(qkvz, gates, alog_row, dt_row, conv_prev, conv_w, onorm_row, s0,
         *, row0, batch, length, c, hb, name):
    nblk = length // c
    ng = HA // hb
    hw = hb * LANES
    rb0 = row0 // c
    assert row0 % c == 0 and length % c == 0 and c >= SUBLANES and CONV_W == 4

    def rows(b, g, n):
        return rb0 + b * nblk + n

    x_spec = lambda part: pl.BlockSpec((c, hw), lambda b, g, n: (rows(b, g, n), part * ng + g))
    prev_spec = lambda part: pl.BlockSpec((1, CONV_W - 1, hw), lambda b, g, n: (b, 0, part * ng + g))
    w_spec = lambda part: pl.BlockSpec((CONV_W, hw), lambda b, g, n: (0, part * ng + g))
    gate_row_spec = pl.BlockSpec((1, LANES), lambda b, g, n: (0, g))
    hist_spec = pl.BlockSpec((1, SUBLANES, hw), lambda b, g, n: (b, 0, g))
    hist_shape = jax.ShapeDtypeStruct((batch, SUBLANES, HA * LANES), F32)
    in_specs = [
        x_spec(0), x_spec(1), x_spec(2), x_spec(3),
        pl.BlockSpec((c, LANES), lambda b, g, n: (rows(b, g, n), g)),
        gate_row_spec, gate_row_spec,
        prev_spec(0), prev_spec(1), prev_spec(2),
        w_spec(0), w_spec(1), w_spec(2),
        pl.BlockSpec((1, LANES), lambda b, g, n: (0, 0)),
        pl.BlockSpec((1, hb, DK, DV), lambda b, g, n: (b, g, 0, 0)),
    ]
    out_specs = [
        pl.BlockSpec((c, hw), lambda b, g, n: (b * nblk + n, g)),
        pl.BlockSpec((1, hb, DK, DV), lambda b, g, n: (b, g, 0, 0)),
        hist_spec, hist_spec, hist_spec,
    ]
    return pl.pallas_call(
        functools.partial(_gdn_kernel, c=c, hb=hb),
        grid=(batch, ng, nblk),
        in_specs=in_specs,
        out_specs=out_specs,
        out_shape=[jax.ShapeDtypeStruct((batch * length, HA * DV), BF16),
                   jax.ShapeDtypeStruct((batch, HA, DK, DV), F32),
                   hist_shape, hist_shape, hist_shape],
        scratch_shapes=[pltpu.VMEM((c + SUBLANES, hw), F32)] * 3,
        compiler_params=pltpu.CompilerParams(
            dimension_semantics=("parallel", "parallel", "arbitrary"),
            vmem_limit_bytes=VMEM_LIMIT),
        name=name,
    )(qkvz, qkvz, qkvz, qkvz, gates, alog_row, dt_row,
      conv_prev, conv_prev, conv_prev, conv_w, conv_w, conv_w, onorm_row, s0)


def _attn_kernel(sinks_ref, q_ref, kp_ref, vp_ref, kc_ref, vc_ref, o_ref, *, tq, nb, chunk_mask):
    i = pl.program_id(1)
    nk = WINDOW + tq
    lane_k = lax.broadcasted_iota(jnp.int32, (nk, LANES), 1)
    lane_q = lax.broadcasted_iota(jnp.int32, (tq, LANES), 1)
    if chunk_mask:
        qc = lax.broadcasted_iota(jnp.int32, (tq, nk), 0) // CHUNK
        kc = lax.broadcasted_iota(jnp.int32, (tq, nk), 1) // CHUNK
        w_ch = WINDOW // CHUNK
        first_kc = jnp.where(i > 0, 0, w_ch)
        valid = (kc >= jnp.maximum(qc, first_kc)) & (kc <= qc + w_ch)
    scale = HD ** -0.5
    assert math.log2(HD) % 2 == 0
    per_slab = LANES // HD
    group = N_Q // N_KV
    slabs_per_kv = group // per_slab
    half_masks = [(lane_q >= r * HD) & (lane_q < (r + 1) * HD) for r in range(per_slab)]

    units = [(b, h) for b in range(nb) for h in range(N_KV)]
    kdup, vdup, qs = [], [], []
    for b, h in units:
        ksl = slice((h // per_slab) * LANES, (h // per_slab + 1) * LANES)
        kslab = jnp.concatenate([kp_ref[b * WINDOW:(b + 1) * WINDOW, ksl],
                                 kc_ref[b * tq:(b + 1) * tq, ksl]], axis=0)
        vslab = jnp.concatenate([vp_ref[b * WINDOW:(b + 1) * WINDOW, ksl],
                                 vc_ref[b * tq:(b + 1) * tq, ksl]], axis=0)
        first = (lane_k < HD) == (h % per_slab == 0)
        kdup.append(jnp.where(first, kslab, pltpu.roll(kslab, HD, 1)).astype(BF16))
        vdup.append(jnp.where(first, vslab, pltpu.roll(vslab, HD, 1)).astype(BF16))
        pieces = []
        for s in range(slabs_per_kv):
            slab = h * slabs_per_kv + s
            q2 = q_ref[b * tq:(b + 1) * tq, slab * LANES:(slab + 1) * LANES]
            pieces += [jnp.where(mk, q2, jnp.zeros_like(q2)) for mk in half_masks]
        qs.append(jnp.concatenate(pieces, axis=0) * scale)
    sc = [lax.dot_general(qs[u], kdup[u], _NT, preferred_element_type=F32) for u in range(len(units))]
    ps = []
    for u, (b, h) in enumerate(units):
        blocks = []
        for r in range(group):
            sr = sc[u][r * tq:(r + 1) * tq]
            if chunk_mask:
                sr = jnp.where(valid, sr, -jnp.inf)
            sk = sinks_ref[h * group + r]
            mx = jnp.maximum(jnp.max(sr, axis=-1, keepdims=True), sk)
            e = jnp.exp(sr - mx)
            den = jnp.sum(e, axis=-1, keepdims=True) + jnp.exp(sk - mx)
            blocks.append((e / den).astype(BF16))
        ps.append(jnp.concatenate(blocks, axis=0))
    pv = [jnp.dot(ps[u], vdup[u], preferred_element_type=F32) for u in range(len(units))]
    for u, (b, h) in enumerate(units):
        for s in range(slabs_per_kv):
            slab = h * slabs_per_kv + s
            halves = [pv[u][(s * per_slab + r) * tq:(s * per_slab + r + 1) * tq] for r in range(per_slab)]
            out = halves[-1]
            for r in range(per_slab - 2, -1, -1):
                out = jnp.where(half_masks[r], halves[r], out)
            o_ref[b * tq:(b + 1) * tq, slab * LANES:(slab + 1) * LANES] = out.astype(o_ref.dtype)


def _attention(sinks, q, kprev, vprev, kprev_col, vprev_col, kv, *, row0, batch, length,
               tq, nb, chunk_mask, name):
    nq = length // tq
    kvw = N_KV * HD
    assert row0 % (nb * tq) == 0 and length % tq == 0 and batch % nb == 0 and (nb == 1 or nq == 1)
    rb0 = row0 // (nb * tq)

    def cur(b, i):
        return rb0 + b * nq + i

    def prev(b, i):
        return b * nq + jnp.maximum(i - 1, 0)

    in_specs = [
        pl.BlockSpec(memory_space=pltpu.SMEM),
        pl.BlockSpec((nb * tq, N_Q * HD), lambda b, i: (cur(b, i), 0)),
        pl.BlockSpec((nb * WINDOW, kvw), lambda b, i: (prev(b, i), kprev_col)),
        pl.BlockSpec((nb * WINDOW, kvw), lambda b, i: (prev(b, i), vprev_col)),
        pl.BlockSpec((nb * tq, kvw), lambda b, i: (cur(b, i), 0)),
        pl.BlockSpec((nb * tq, kvw), lambda b, i: (cur(b, i), 1)),
    ]
    return pl.pallas_call(
        functools.partial(_attn_kernel, tq=tq, nb=nb, chunk_mask=chunk_mask),
        grid=(batch // nb, nq),
        in_specs=in_specs,
        out_specs=pl.BlockSpec((nb * tq, N_Q * HD), lambda b, i: (b * nq + i, 0)),
        out_shape=jax.ShapeDtypeStruct((batch * length, N_Q * HD), BF16),
        compiler_params=pltpu.CompilerParams(
            dimension_semantics=("parallel", "arbitrary"), vmem_limit_bytes=VMEM_LIMIT),
        name=name,
    )(sinks, q, kprev, vprev, kv, kv)


def _rope_tables(pos):
    half = HD // 2
    inv = 1.0 / (ROPE_THETA ** (jnp.arange(half, dtype=F32) / half))
    ang = pos.astype(F32)[:, None] * jnp.tile(inv, LANES // half)[None, :]
    sign = jnp.tile(jnp.concatenate([-jnp.ones((half,), F32), jnp.ones((half,), F32)]), LANES // HD)
    return jnp.cos(ang), jnp.sin(ang) * sign[None, :]


def _gate_layout(t, hb):
    lead = t.shape[:-1]
    ng = HA // hb
    if ng == 1:
        return jnp.pad(t, [(0, 0)] * len(lead) + [(0, LANES - 2 * HA)])
    beta = t[..., :HA].reshape(lead + (ng, hb))
    dec = t[..., HA:].reshape(lead + (ng, hb))
    pad = jnp.zeros(lead + (ng, LANES - 2 * hb), t.dtype)
    return jnp.concatenate([beta, dec, pad], axis=-1).reshape(lead + (ng * LANES,))


def kernel(x_prompt, x_sample, cache_conv, state_gdn, cache_k, cache_v, attn_norm, mlp_norm, final_norm, a_w_in, a_conv_w, a_log, a_dt_bias, a_o_norm, a_w_out, kv_norm, w_kv, b_w_q, b_sinks, b_w_o, w_up, w_down):
    bp, lp, d = x_prompt.shape
    bs, ls, _ = x_sample.shape
    mp, ms = bp * lp, bs * ls
    m = mp + ms
    hb = GDN_HEADS_PER_STEP
    dqk, dvw = HA * DK, HA * DV
    conv_ch = 2 * dqk + dvw
    keep = CONV_W - 1
    assert a_w_in.shape[0] == 1 and b_w_q.shape[0] == 1 and lp % WINDOW == 0 and ls >= keep

    xp = x_prompt.reshape(mp, d)
    xs = x_sample.reshape(ms, d)
    tm = _pick_tile(m, ROW_TILES)
    tm2 = _pick_tile(math.gcd(mp, ms), ROW_TILES[1:])
    tm_p = _pick_tile(mp, (1024,) + ROW_TILES[1:])
    tm_s = _pick_tile(ms, ROW_TILES[1:])
    tn_d = _pick_tile(d, COL_TILES[1:])

    w_in = a_w_in[0].astype(BF16)
    w_gate = _gate_layout(w_in[:, conv_ch + dvw:], hb)
    g0 = attn_norm[0].reshape(1, d)
    tn_in = _pick_tile(conv_ch + dvw, COL_TILES)
    zeros_gate = jnp.zeros((HA,), F32)
    alog_row = _gate_layout(jnp.concatenate([zeros_gate, a_log[0]]), hb).reshape(1, -1)
    dt_row = _gate_layout(jnp.concatenate([zeros_gate, a_dt_bias[0]]), hb).reshape(1, -1)
    onorm_row = a_o_norm[0].reshape(1, DV)
    streams = (("prompt", xp, tm_p, bp, lp, jnp.zeros((bp, keep, conv_ch), F32),
                jnp.zeros((bp, HA, DK, DV), F32)),
               ("sample", xs, tm_s, bs, ls, cache_conv[0], state_gdn[0]))
    mixed = []
    for tag, xr, tmr, nbatch, length, conv_prev, s0 in streams:
        qkvz, gates = _in_proj(xr, g0, w_in, w_gate, tm=tmr, tn=tn_in, name="gdn_in_proj_" + tag)
        mixed.append(_gdn(qkvz, gates, alog_row, dt_row, conv_prev, a_conv_w[0], onorm_row, s0,
                          row0=0, batch=nbatch, length=length, c=min(CHUNK, length), hb=hb,
                          name="gdn_" + tag))
    (o_p, gdn_p, *hist_p), (o_s, gdn_s, *hist_s) = mixed
    tn_o = _pick_tile(d, COL_TILES)
    x = _matmul((o_p, o_s), a_w_out[0], res=(xp, xs), tm=tm2, tn=tn_o, weights_outer=True,
                name="gdn_out_proj")
    tf = _pick_tile(w_up.shape[2], COL_TILES[1:])
    x = _mlp(x, mlp_norm[0].reshape(1, d), w_up, w_down, 0, tm=tm, tf=tf, name="mlp0")

    cos_p, sin_p = _rope_tables(jnp.arange(lp))
    cos_s, sin_s = _rope_tables(PAST_LEN + jnp.arange(ls))
    rope = (jnp.concatenate([cos_p] * bp + [cos_s] * bs, axis=0),
            jnp.concatenate([sin_p] * bp + [sin_s] * bs, axis=0))
    kvw = N_KV * HD
    kv = _matmul(x, w_kv, gain=kv_norm.reshape(1, d), rope=rope, rope_cols=kvw,
                 tm=tm, tn=2 * kvw, name="kv_proj")
    q = _matmul(x, b_w_q[0], gain=attn_norm[1].reshape(1, d), rope=rope,
                rope_cols=N_Q * HD, out_dtype=BF16, tm=tm, tn=tn_d, name="q_proj")
    sinks = b_sinks[0].astype(F32)
    a_p = _attention(sinks, q, kv, kv, 0, 1, kv, row0=0, batch=bp, length=lp, tq=WINDOW, nb=1,
                     chunk_mask=True, name="attn_prompt")
    nb = _pick_tile(bs, (ATTN_SAMPLE_BATCHES_PER_STEP, 2, 1))
    a_s = _attention(sinks, q, cache_k.reshape(bs * WINDOW, kvw), cache_v.reshape(bs * WINDOW, kvw),
                     0, 0, kv, row0=mp, batch=bs, length=ls, tq=ls, nb=nb,
                     chunk_mask=False, name="attn_sample")
    x = _matmul((a_p, a_s), b_w_o[0], res=x, tm=tm2, tn=tn_o, weights_outer=True, name="attn_out_proj")
    y_p, y_s = _mlp_split(x, mlp_norm[1].reshape(1, d), w_up, w_down, 1, final_norm.reshape(1, d),
                          split_row=mp, tm=tm, tf=tf, name="mlp1")

    y_prompt = y_p.reshape(bp, lp, d)
    y_sample = y_s.reshape(bs, ls, d)
    conv_p = jnp.concatenate([t[:, SUBLANES - keep:] for t in hist_p], axis=-1)[None]
    conv_s = jnp.concatenate([t[:, SUBLANES - keep:] for t in hist_s], axis=-1)[None]
    kv_p = kv[:mp].reshape(bp, lp, 2 * kvw)[:, lp - WINDOW:]
    kv_s = kv[mp:].reshape(bs, ls, 2 * kvw)
    k_p = kv_p[..., :kvw].reshape(bp, WINDOW, N_KV, HD)
    v_p = kv_p[..., kvw:].reshape(bp, WINDOW, N_KV, HD)
    k_s = kv_s[..., :kvw].reshape(bs, ls, N_KV, HD)
    v_s = kv_s[..., kvw:].reshape(bs, ls, N_KV, HD)
    return (y_prompt, y_sample, conv_p, gdn_p[None], k_p, v_p, conv_s, gdn_s[None], k_s, v_s)
```

```python
import functools
import math

import jax
import jax.numpy as jnp
from jax import lax
from jax.experimental import pallas as pl
from jax.experimental.pallas import tpu as pltpu

F32 = jnp.float32
BF16 = jnp.bfloat16

EPS = 1e-6
CHUNK = 64
WINDOW = 128
PAST_LEN = 4096
ROPE_THETA = 10000.0
HA, DK, DV = 16, 128, 128
N_Q, N_KV, HD = 32, 4, 64
CONV_W = 4
LANES = 128
SUBLANES = 8
VMEM_LIMIT = 56 * 1024 * 1024
MLP_VMEM_LIMIT = 60 * 1024 * 1024
GDN_HEADS_PER_STEP = 16
GDN_SAMPLE_SEQS_PER_STEP = 4
ATTN_SAMPLE_BATCHES_PER_STEP = 4
ROW_TILES = (1088, 512, 256, 128, 64, 32, 16)
COL_TILES = (1024, 512, 256, 128)


def _pick_tile(n, candidates):
    for c in candidates:
        if n % c == 0:
            return c
    return n


def _dot(a, b, dims=(((1,), (0,)), ((), ()))):
    return lax.dot_general(a.astype(BF16), b.astype(BF16), dims, preferred_element_type=F32)


_NT = (((1,), (1,)), ((), ()))
_TN = (((0,), (0,)), ((), ()))


def _rms_scale(x):
    return lax.rsqrt(jnp.mean(x * x, axis=-1, keepdims=True) + EPS)


def _silu(x):
    h = 0.5 * x
    return h + h * jnp.tanh(h)


def _rope_slab(y, cos, sin_signed):
    lane = lax.broadcasted_iota(jnp.int32, y.shape, 1)
    up = pltpu.roll(y, 32, 1)
    down = pltpu.roll(y, LANES - 32, 1)
    swapped = jnp.where((lane % HD) < HD // 2, down, up)
    return y * cos + swapped * sin_signed


def _project(lhs, w_ref, res_ref, cos_ref, sin_ref, o_ref, *, rope_slabs, n_slabs):
    if not rope_slabs:
        y = jnp.dot(lhs, w_ref[...].astype(BF16), preferred_element_type=F32)
        if res_ref is not None:
            y = y + res_ref[...]
        o_ref[...] = y.astype(o_ref.dtype)
        return
    cos = cos_ref[...]
    sin = sin_ref[...]
    n_chunk = 2 if n_slabs % 2 == 0 else 1
    for c0 in range(0, n_slabs, n_chunk):
        cols = slice(c0 * LANES, (c0 + n_chunk) * LANES)
        y = jnp.dot(lhs, w_ref[:, cols].astype(BF16), preferred_element_type=F32)
        if res_ref is not None:
            y = y + res_ref[:, cols]
        for s in range(n_chunk):
            ys = y[:, s * LANES:(s + 1) * LANES]
            if c0 + s < rope_slabs:
                ys = _rope_slab(ys, cos, sin)
            o_ref[:, (c0 + s) * LANES:(c0 + s + 1) * LANES] = ys.astype(o_ref.dtype)


def _matmul_kernel(*refs, n_x, n_res, split_tile, row_axis, has_norm, rope_slabs, n_slabs):
    it = iter(refs)
    x_refs = [next(it) for _ in range(n_x)]
    g_ref = next(it) if has_norm else None
    w_ref = next(it)
    res_refs = [next(it) for _ in range(n_res)]
    cos_ref = next(it) if rope_slabs else None
    sin_ref = next(it) if rope_slabs else None
    o_ref = next(it)
    xn_ref = next(it) if has_norm else None

    if has_norm:
        @pl.when(pl.program_id(1) == 0)
        def _():
            x = x_refs[0][...]
            xn_ref[...] = (x * _rms_scale(x) * g_ref[...]).astype(BF16)

    def emit(lhs_ref, res_ref):
        _project(lhs_ref[...], w_ref, res_ref, cos_ref, sin_ref, o_ref, rope_slabs=rope_slabs, n_slabs=n_slabs)

    if n_x == 1:
        emit(xn_ref if has_norm else x_refs[0], res_refs[0] if n_res else None)
    else:
        i = pl.program_id(row_axis)
        pl.when(i < split_tile)(functools.partial(emit, x_refs[0], res_refs[0] if n_res else None))
        pl.when(i >= split_tile)(functools.partial(emit, x_refs[1], res_refs[-1] if n_res else None))


def _matmul(xs, w, *, gain=None, res=None, rope=None, rope_cols=0, out_dtype=F32, tm, tn, name,
            weights_outer=False):
    xs = xs if isinstance(xs, (tuple, list)) else (xs,)
    ress = () if res is None else (res if isinstance(res, (tuple, list)) else (res,))
    k = xs[0].shape[1]
    m = sum(x.shape[0] for x in xs)
    n = (w.shape[1] // tn) * tn
    assert all(x.shape[0] % tm == 0 for x in xs) and w.shape[0] == k
    has_norm = gain is not None
    assert not (has_norm and len(xs) > 1)
    assert len(ress) <= len(xs) and all(r.shape[0] == x.shape[0] for r, x in zip(ress, xs) if len(ress) > 1)
    split_tile = xs[0].shape[0] // tm
    rope_slabs = 0
    if rope is not None:
        assert rope_cols == n or tn == n
        rope_slabs = min(rope_cols, tn) // LANES
    assert not (has_norm and weights_outer)

    def spec(shape, f):
        return pl.BlockSpec(shape, (lambda a, b: f(b, a)) if weights_outer else f)

    def row_specs(arrays, width, col):
        if len(arrays) == 1:
            return [spec((tm, width), lambda i, j: (i, col(j)))]
        return [spec((tm, width), lambda i, j: (jnp.minimum(i, split_tile - 1), col(j))),
                spec((tm, width), lambda i, j: (jnp.maximum(i - split_tile, 0), col(j)))]

    in_specs = row_specs(xs, k, lambda j: 0)
    args = list(xs)
    if has_norm:
        in_specs.append(spec((1, k), lambda i, j: (0, 0)))
        args.append(gain)
    in_specs.append(spec((k, tn), lambda i, j: (0, j)))
    args.append(w)
    in_specs += row_specs(ress, tn, lambda j: j) if ress else []
    args += list(ress)
    if rope_slabs:
        in_specs += [spec((tm, LANES), lambda i, j: (i, 0))] * 2
        args += list(rope)
    kern = functools.partial(_matmul_kernel, n_x=len(xs), n_res=len(ress), split_tile=split_tile,
                             row_axis=1 if weights_outer else 0, has_norm=has_norm,
                             rope_slabs=rope_slabs, n_slabs=tn // LANES)
    grid = (n // tn, m // tm) if weights_outer else (m // tm, n // tn)
    return pl.pallas_call(
        kern,
        grid=grid,
        in_specs=in_specs,
        out_specs=spec((tm, tn), lambda i, j: (i, j)),
        out_shape=jax.ShapeDtypeStruct((m, n), out_dtype),
        scratch_shapes=[pltpu.VMEM((tm, k), BF16)] if has_norm else [],
        compiler_params=pltpu.CompilerParams(
            dimension_semantics=("parallel", "arbitrary"), vmem_limit_bytes=VMEM_LIMIT),
        name=name,
    )(*args)


def _mlp_kernel(x_ref, g_ref, wu_ref, wd_ref, o_ref, hn_ref):
    f = pl.program_id(1)

    @pl.when(f == 0)
    def _():
        x = x_ref[...]
        hn_ref[...] = (x * _rms_scale(x) * g_ref[...]).astype(BF16)
        o_ref[...] = x

    u = jnp.dot(hn_ref[...], wu_ref[...].astype(BF16), preferred_element_type=F32)
    a = jnp.square(jnp.maximum(u, 0.0)).astype(BF16)
    o_ref[...] += jnp.dot(a, wd_ref[...].astype(BF16), preferred_element_type=F32)


def _mlp_in_specs(tm, d, tf, layer):
    return [
        pl.BlockSpec((tm, d), lambda i, f: (i, 0)),
        pl.BlockSpec((1, d), lambda i, f: (0, 0)),
        pl.BlockSpec((None, d, tf), lambda i, f: (layer, 0, f)),
        pl.BlockSpec((None, tf, d), lambda i, f: (layer, f, 0)),
    ]


def _mlp(x, gain, w_up, w_down, layer, *, tm, tf, name):
    m, d = x.shape
    dff = w_up.shape[2]
    assert m % tm == 0 and dff % tf == 0
    return pl.pallas_call(
        _mlp_kernel,
        grid=(m // tm, dff // tf),
        in_specs=_mlp_in_specs(tm, d, tf, layer),
        out_specs=pl.BlockSpec((tm, d), lambda i, f: (i, 0)),
        out_shape=jax.ShapeDtypeStruct((m, d), F32),
        scratch_shapes=[pltpu.VMEM((tm, d), BF16)],
        compiler_params=pltpu.CompilerParams(
            dimension_semantics=("parallel", "arbitrary"), vmem_limit_bytes=MLP_VMEM_LIMIT),
        name=name,
    )(x, gain, w_up, w_down)


def _mlp_split_kernel(x_ref, g_ref, wu_ref, wd_ref, fg_ref, yp_ref, ys_ref, acc_ref, hn_ref, sem_ref,
                      *, tm, n_tiles, split_row):
    i = pl.program_id(0)
    f = pl.program_id(1)
    slot = i % 2

    def writebacks(t):
        r0, r1 = t * tm, (t + 1) * tm
        s = t % 2
        cps = []
        if r0 < split_row:
            n = min(r1, split_row) - r0
            cps.append(pltpu.make_async_copy(acc_ref.at[s, 0:n], yp_ref.at[r0:r0 + n], sem_ref.at[s, 0]))
        if r1 > split_row:
            a = max(r0, split_row)
            cps.append(pltpu.make_async_copy(acc_ref.at[s, a - r0:tm], ys_ref.at[a - split_row:r1 - split_row],
                                             sem_ref.at[s, 1]))
        return cps

    def start_tile(t):
        for cp in writebacks(t):
            cp.start()

    def wait_tile(t):
        for cp in writebacks(t):
            cp.wait()

    @pl.when(f == 0)
    def _():
        for t in range(n_tiles - 2):
            pl.when(i == t + 2)(functools.partial(wait_tile, t))
        x = x_ref[...]
        hn_ref[...] = (x * _rms_scale(x) * g_ref[...]).astype(BF16)
        acc_ref[slot] = x

    u = jnp.dot(hn_ref[...], wu_ref[...].astype(BF16), preferred_element_type=F32)
    a = jnp.square(jnp.maximum(u, 0.0)).astype(BF16)
    acc_ref[slot] += jnp.dot(a, wd_ref[...].astype(BF16), preferred_element_type=F32)

    @pl.when(f == pl.num_programs(1) - 1)
    def _():
        y = acc_ref[slot]
        acc_ref[slot] = y * _rms_scale(y) * fg_ref[...]
        for t in range(n_tiles):
            pl.when(i == t)(functools.partial(start_tile, t))

        @pl.when(i == n_tiles - 1)
        def _():
            for t in range(max(n_tiles - 2, 0), n_tiles):
                wait_tile(t)


def _mlp_split(x, gain, w_up, w_down, layer, final_gain, *, split_row, tm, tf, name):
    m, d = x.shape
    dff = w_up.shape[2]
    assert m % tm == 0 and dff % tf == 0 and split_row % SUBLANES == 0 and 0 < split_row < m
    n_tiles = m // tm
    return pl.pallas_call(
        functools.partial(_mlp_split_kernel, tm=tm, n_tiles=n_tiles, split_row=split_row),
        grid=(n_tiles, dff // tf),
        in_specs=_mlp_in_specs(tm, d, tf, layer) + [pl.BlockSpec((1, d), lambda i, f: (0, 0))],
        out_specs=[pl.BlockSpec(memory_space=pl.ANY), pl.BlockSpec(memory_space=pl.ANY)],
        out_shape=[jax.ShapeDtypeStruct((split_row, d), F32), jax.ShapeDtypeStruct((m - split_row, d), F32)],
        scratch_shapes=[pltpu.VMEM((2, tm, d), F32), pltpu.VMEM((tm, d), BF16),
                        pltpu.SemaphoreType.DMA((2, 2))],
        compiler_params=pltpu.CompilerParams(
            dimension_semantics=("arbitrary", "arbitrary"), vmem_limit_bytes=MLP_VMEM_LIMIT),
        name=name,
    )(x, gain, w_up, w_down, final_gain)


def _gdn_kernel(xq_ref, xk_ref, xv_ref, z_ref, gates_ref, alog_ref, dt_ref,
                pq_ref, pk_ref, pv_ref, wq_ref, wk_ref, wv_ref, onorm_ref, s0_ref,
                o_ref, s_ref, hq_ref, hk_ref, hv_ref, padq_ref, padk_ref, padv_ref, *, c, hb, nseq):
    n = pl.program_id(2)
    halo = SUBLANES
    seqs = range(nseq)

    @pl.when(n == 0)
    def _():
        for pad_ref, prev_ref in ((padq_ref, pq_ref), (padk_ref, pk_ref), (padv_ref, pv_ref)):
            for sq in seqs:
                pad_ref[sq, 0:halo, :] = jnp.zeros((halo, pad_ref.shape[2]), F32)
                pad_ref[sq, halo - (CONV_W - 1):halo, :] = prev_ref[sq]
        s_ref[...] = s0_ref[...]

    def conv(x_ref, pad_ref, w_ref, hist_ref, sq):
        pad_ref[sq, halo:halo + c, :] = x_ref[sq * c:(sq + 1) * c, :]
        xp = pad_ref[sq]
        x1 = pltpu.roll(xp, 1, 0)
        a = xp * w_ref[3:4, :] + x1 * w_ref[2:3, :]
        b = xp * w_ref[1:2, :] + x1 * w_ref[0:1, :]
        acc = (a + pltpu.roll(b, 2, 0))[halo:, :]
        tail = xp[c:c + halo, :]
        pad_ref[sq, 0:halo, :] = tail
        hist_ref[sq] = tail
        return _silu(acc)

    cq = [conv(xq_ref, padq_ref, wq_ref, hq_ref, sq) for sq in seqs]
    ck = [conv(xk_ref, padk_ref, wk_ref, hk_ref, sq) for sq in seqs]
    cv = [conv(xv_ref, padv_ref, wv_ref, hv_ref, sq) for sq in seqs]

    row = lax.broadcasted_iota(jnp.int32, (c, c), 0)
    col = lax.broadcasted_iota(jnp.int32, (c, c), 1)
    incl = row >= col
    strict = row > col
    tril = jnp.where(incl, 1.0, 0.0).astype(F32)
    r128 = lax.broadcasted_iota(jnp.int32, (LANES, LANES), 0)
    c128 = lax.broadcasted_iota(jnp.int32, (LANES, LANES), 1)
    eye128 = jnp.where(r128 == c128, 1.0, 0.0).astype(F32)
    beta_all, gc_all, gc_t = [], [], []
    for sq in seqs:
        gates = gates_ref[sq * c:(sq + 1) * c, :]
        beta_all.append(0.5 + 0.5 * jnp.tanh(0.5 * gates))
        ga = gates + dt_ref[...]
        softplus = jnp.maximum(ga, 0.0) + jnp.log(1.0 + jnp.exp(-jnp.abs(ga)))
        g_all = -jnp.exp(alog_ref[...]) * softplus
        gc_all.append(jnp.dot(tril, g_all, preferred_element_type=F32,
                              precision=lax.Precision.HIGHEST))
        gc_t.append(lax.dot_general(eye128, gc_all[sq], _NT, preferred_element_type=F32,
                                    precision=lax.Precision.HIGHEST))

    units = [(sq, h) for sq in seqs for h in range(hb)]
    un = range(len(units))
    sls = [slice(h * LANES, (h + 1) * LANES) for _, h in units]
    q = [cq[sq][:, sls[u]] for u, (sq, h) in enumerate(units)]
    k = [ck[sq][:, sls[u]] for u, (sq, h) in enumerate(units)]
    v = [cv[sq][:, sls[u]] for u, (sq, h) in enumerate(units)]
    q = [x * (lax.rsqrt(jnp.sum(x * x, axis=-1, keepdims=True) + EPS) * (DK ** -0.5)) for x in q]
    k = [x * lax.rsqrt(jnp.sum(x * x, axis=-1, keepdims=True) + EPS) for x in k]
    bcol = [beta_all[sq][:, h:h + 1] for sq, h in units]
    gcol = [gc_all[sq][:, hb + h:hb + h + 1] for sq, h in units]
    grow = [gc_t[sq][hb + h:hb + h + 1, :] for sq, h in units]
    glast = [gc_all[sq][c - 1:c, hb + h:hb + h + 1] for sq, h in units]
    decay = [jnp.exp(jnp.where(incl, gcol[u] - grow[u], -jnp.inf)) for u in un]
    kb = [k[u] * bcol[u] for u in un]
    eye = jnp.where(row == col, 1.0, 0.0).astype(F32)
    qkk = [_dot(jnp.concatenate([q[u], kb[u]], axis=0), k[u], _NT) for u in un]
    qk = [qkk[u][:c] * decay[u] for u in un]
    p = [-jnp.where(strict, qkk[u][c:] * decay[u], 0.0) for u in un]
    t = [eye + x for x in p]
    levels = max(int(math.ceil(math.log2(c))) - 1, 0)
    if levels:
        p = [_dot(x, x) for x in p]
    for _ in range(1, levels):
        r = [_dot(jnp.concatenate([p[u], t[u]], axis=0), p[u]) for u in un]
        t = [t[u] + r[u][c:] for u in un]
        p = [r[u][:c] for u in un]
    if levels:
        t = [t[u] + _dot(t[u], p[u]) for u in un]
    egc = [jnp.exp(x) for x in gcol]
    rhs = [jnp.concatenate([v[u] * bcol[u], kb[u] * egc[u]], axis=-1) for u in un]
    sol = [_dot(t[u], rhs[u]) for u in un]
    s = [s_ref[sq, h] for sq, h in units]
    wq = [_dot(jnp.concatenate([sol[u][:, DV:], q[u] * egc[u]], axis=0), s[u]) for u in un]
    v_new = [sol[u][:, :DV] - wq[u][:c] for u in un]
    o = [wq[u][c:] + _dot(qk[u], v_new[u]) for u in un]
    s_new = [s[u] * jnp.exp(glast[u]) + _dot(k[u] * jnp.exp(glast[u] - gcol[u]), v_new[u], _TN)
             for u in un]
    onorm = onorm_ref[...]
    for u, (sq, h) in enumerate(units):
        s_ref[sq, h] = s_new[u]
        oh = o[u] * _rms_scale(o[u]) * onorm
        rows = slice(sq * c, (sq + 1) * c)
        o_ref[rows, sls[u]] = (oh * _silu(z_ref[rows, sls[u]])).astype(o_ref.dtype)


def _gdn(qkvz, gates, alog_row, dt_row, conv_prev, conv_w, onorm_row, s0,
         *, row0, batch, length, c, hb, nseq, name):
    nblk = length // c
    ng = HA // hb
    hw = hb * LANES
    rows_step = nseq * c
    rb0 = row0 // rows_step
    assert row0 % rows_step == 0 and length % c == 0 and c >= SUBLANES and CONV_W == 4
    assert batch % nseq == 0 and (nseq == 1 or nblk == 1)

    def rows(b, g, n):
        return rb0 + b * nblk + n

    x_spec = lambda part: pl.BlockSpec((rows_step, hw), lambda b, g, n: (rows(b, g, n), part * ng + g))
    prev_spec = lambda part: pl.BlockSpec((nseq, CONV_W - 1, hw), lambda b, g, n: (b, 0, part * ng + g))
    w_spec = lambda part: pl.BlockSpec((CONV_W, hw), lambda b, g, n: (0, part * ng + g))
    gate_row_spec = pl.BlockSpec((1, LANES), lambda b, g, n: (0, g))
    hist_spec = pl.BlockSpec((nseq, SUBLANES, hw), lambda b, g, n: (b, 0, g))
    hist_shape = jax.ShapeDtypeStruct((batch, SUBLANES, HA * LANES), F32)
    state_spec = pl.BlockSpec((nseq, hb, DK, DV), lambda b, g, n: (b, g, 0, 0))
    in_specs = [
        x_spec(0), x_spec(1), x_spec(2), x_spec(3),
        pl.BlockSpec((rows_step, LANES), lambda b, g, n: (rows(b, g, n), g)),
        gate_row_spec, gate_row_spec,
        prev_spec(0), prev_spec(1), prev_spec(2),
        w_spec(0), w_spec(1), w_spec(2),
        pl.BlockSpec((1, LANES), lambda b, g, n: (0, 0)),
        state_spec,
    ]
    out_specs = [
        pl.BlockSpec((rows_step, hw), lambda b, g, n: (b * nblk + n, g)),
        state_spec,
        hist_spec, hist_spec, hist_spec,
    ]
    return pl.pallas_call(
        functools.partial(_gdn_kernel, c=c, hb=hb, nseq=nseq),
        grid=(batch // nseq, ng, nblk),
        in_specs=in_specs,
        out_specs=out_specs,
        out_shape=[jax.ShapeDtypeStruct((batch * length, HA * DV), BF16),
                   jax.ShapeDtypeStruct((batch, HA, DK, DV), F32),
                   hist_shape, hist_shape, hist_shape],
        scratch_shapes=[pltpu.VMEM((nseq, c + SUBLANES, hw), F32)] * 3,
        compiler_params=pltpu.CompilerParams(
            dimension_semantics=("parallel", "parallel", "arbitrary"),
            vmem_limit_bytes=VMEM_LIMIT),
        name=name,
    )(qkvz, qkvz, qkvz, qkvz, gates, alog_row, dt_row,
      conv_prev, conv_prev, conv_prev, conv_w, conv_w, conv_w, onorm_row, s0)


def _attn_kernel(sinks_ref, q_ref, kp_ref, vp_ref, kc_ref, vc_ref, o_ref, *, tq, nb, chunk_mask):
    i = pl.program_id(1)
    nk = WINDOW + tq
    lane_k = lax.broadcasted_iota(jnp.int32, (nk, LANES), 1)
    lane_q = lax.broadcasted_iota(jnp.int32, (tq, LANES), 1)
    if chunk_mask:
        qc = lax.broadcasted_iota(jnp.int32, (tq, nk), 0) // CHUNK
        kc = lax.broadcasted_iota(jnp.int32, (tq, nk), 1) // CHUNK
        w_ch = WINDOW // CHUNK
        first_kc = jnp.where(i > 0, 0, w_ch)
        valid = (kc >= jnp.maximum(qc, first_kc)) & (kc <= qc + w_ch)
    scale = HD ** -0.5
    assert math.log2(HD) % 2 == 0
    per_slab = LANES // HD
    group = N_Q // N_KV
    slabs_per_kv = group // per_slab
    half_masks = [(lane_q >= r * HD) & (lane_q < (r + 1) * HD) for r in range(per_slab)]

    units = [(b, h) for b in range(nb) for h in range(N_KV)]
    kdup, vdup, qs = [], [], []
    for b, h in units:
        ksl = slice((h // per_slab) * LANES, (h // per_slab + 1) * LANES)
        kslab = jnp.concatenate([kp_ref[b * WINDOW:(b + 1) * WINDOW, ksl],
                                 kc_ref[b * tq:(b + 1) * tq, ksl]], axis=0)
        vslab = jnp.concatenate([vp_ref[b * WINDOW:(b + 1) * WINDOW, ksl],
                                 vc_ref[b * tq:(b + 1) * tq, ksl]], axis=0)
        first = (lane_k < HD) == (h % per_slab == 0)
        kdup.append(jnp.where(first, kslab, pltpu.roll(kslab, HD, 1)).astype(BF16))
        vdup.append(jnp.where(first, vslab, pltpu.roll(vslab, HD, 1)).astype(BF16))
        pieces = []
        for s in range(slabs_per_kv):
            slab = h * slabs_per_kv + s
            q2 = q_ref[b * tq:(b + 1) * tq, slab * LANES:(slab + 1) * LANES]
            pieces += [jnp.where(mk, q2, jnp.zeros_like(q2)) for mk in half_masks]
        qs.append(jnp.concatenate(pieces, axis=0) * scale)
    sc = [lax.dot_general(qs[u], kdup[u], _NT, preferred_element_type=F32) for u in range(len(units))]
    ps = []
    for u, (b, h) in enumerate(units):
        blocks = []
        for r in range(group):
            sr = sc[u][r * tq:(r + 1) * tq]
            if chunk_mask:
                sr = jnp.where(valid, sr, -jnp.inf)
            sk = sinks_ref[h * group + r]
            mx = jnp.maximum(jnp.max(sr, axis=-1, keepdims=True), sk)
            e = jnp.exp(sr - mx)
            den = jnp.sum(e, axis=-1, keepdims=True) + jnp.exp(sk - mx)
            blocks.append((e / den).astype(BF16))
        ps.append(jnp.concatenate(blocks, axis=0))
    pv = [jnp.dot(ps[u], vdup[u], preferred_element_type=F32) for u in range(len(units))]
    for u, (b, h) in enumerate(units):
        for s in range(slabs_per_kv):
            slab = h * slabs_per_kv + s
            halves = [pv[u][(s * per_slab + r) * tq:(s * per_slab + r + 1) * tq] for r in range(per_slab)]
            out = halves[-1]
            for r in range(per_slab - 2, -1, -1):
                out = jnp.where(half_masks[r], halves[r], out)
            o_ref[b * tq:(b + 1) * tq, slab * LANES:(slab + 1) * LANES] = out.astype(o_ref.dtype)


def _attention(sinks, q, kprev, vprev, kprev_col, vprev_col, kv, *, row0, batch, length,
               tq, nb, chunk_mask, name):
    nq = length // tq
    kvw = N_KV * HD
    assert row0 % (nb * tq) == 0 and length % tq == 0 and batch % nb == 0 and (nb == 1 or nq == 1)
    rb0 = row0 // (nb * tq)

    def cur(b, i):
        return rb0 + b * nq + i

    def prev(b, i):
        return b * nq + jnp.maximum(i - 1, 0)

    in_specs = [
        pl.BlockSpec(memory_space=pltpu.SMEM),
        pl.BlockSpec((nb * tq, N_Q * HD), lambda b, i: (cur(b, i), 0)),
        pl.BlockSpec((nb * WINDOW, kvw), lambda b, i: (prev(b, i), kprev_col)),
        pl.BlockSpec((nb * WINDOW, kvw), lambda b, i: (prev(b, i), vprev_col)),
        pl.BlockSpec((nb * tq, kvw), lambda b, i: (cur(b, i), 0)),
        pl.BlockSpec((nb * tq, kvw), lambda b, i: (cur(b, i), 1)),
    ]
    return pl.pallas_call(
        functools.partial(_attn_kernel, tq=tq, nb=nb, chunk_mask=chunk_mask),
        grid=(batch // nb, nq),
        in_specs=in_specs,
        out_specs=pl.BlockSpec((nb * tq, N_Q * HD), lambda b, i: (b * nq + i, 0)),
        out_shape=jax.ShapeDtypeStruct((batch * length, N_Q * HD), BF16),
        compiler_params=pltpu.CompilerParams(
            dimension_semantics=("parallel", "arbitrary"), vmem_limit_bytes=VMEM_LIMIT),
        name=name,
    )(sinks, q, kprev, vprev, kv, kv)


def _rope_tables(pos):
    half = HD // 2
    inv = 1.0 / (ROPE_THETA ** (jnp.arange(half, dtype=F32) / half))
    ang = pos.astype(F32)[:, None] * jnp.tile(inv, LANES // half)[None, :]
    sign = jnp.tile(jnp.concatenate([-jnp.ones((half,), F32), jnp.ones((half,), F32)]), LANES // HD)
    return jnp.cos(ang), jnp.sin(ang) * sign[None, :]


def _gate_layout(t, hb):
    lead = t.shape[:-1]
    ng = HA // hb
    if ng == 1:
        return jnp.pad(t, [(0, 0)] * len(lead) + [(0, LANES - 2 * HA)])
    beta = t[..., :HA].reshape(lead + (ng, hb))
    dec = t[..., HA:].reshape(lead + (ng, hb))
    pad = jnp.zeros(lead + (ng, LANES - 2 * hb), t.dtype)
    return jnp.concatenate([beta, dec, pad], axis=-1).reshape(lead + (ng * LANES,))


def kernel(x_prompt, x_sample, cache_conv, state_gdn, cache_k, cache_v, attn_norm, mlp_norm, final_norm, a_w_in, a_conv_w, a_log, a_dt_bias, a_o_norm, a_w_out, kv_norm, w_kv, b_w_q, b_sinks, b_w_o, w_up, w_down):
    bp, lp, d = x_prompt.shape
    bs, ls, _ = x_sample.shape
    mp, ms = bp * lp, bs * ls
    m = mp + ms
    hb = GDN_HEADS_PER_STEP
    dqk, dvw = HA * DK, HA * DV
    conv_ch = 2 * dqk + dvw
    keep = CONV_W - 1
    assert a_w_in.shape[0] == 1 and b_w_q.shape[0] == 1 and lp % WINDOW == 0 and ls >= keep

    xp = x_prompt.reshape(mp, d)
    xs = x_sample.reshape(ms, d)
    tm = _pick_tile(m, ROW_TILES)
    tm2 = _pick_tile(math.gcd(mp, ms), ROW_TILES[1:])
    tm_p = _pick_tile(mp, (1024,) + ROW_TILES[1:])
    tm_s = _pick_tile(ms, ROW_TILES[1:])
    tn_d = _pick_tile(d, COL_TILES[1:])

    w_in = a_w_in[0].astype(BF16)
    w_gate = _gate_layout(w_in[:, conv_ch + dvw:], hb)
    g0 = attn_norm[0].reshape(1, d)
    tn_in = _pick_tile(conv_ch + dvw, COL_TILES)
    zeros_gate = jnp.zeros((HA,), F32)
    alog_row = _gate_layout(jnp.concatenate([zeros_gate, a_log[0]]), hb).reshape(1, -1)
    dt_row = _gate_layout(jnp.concatenate([zeros_gate, a_dt_bias[0]]), hb).reshape(1, -1)
    onorm_row = a_o_norm[0].reshape(1, DV)
    streams = (("prompt", xp, tm_p, bp, lp, 1, jnp.zeros((bp, keep, conv_ch), F32),
                jnp.zeros((bp, HA, DK, DV), F32)),
               ("sample", xs, tm_s, bs, ls, _pick_tile(bs, (GDN_SAMPLE_SEQS_PER_STEP, 1)) if ls <= CHUNK else 1,
                cache_conv[0], state_gdn[0]))
    mixed = []
    for tag, xr, tmr, nbatch, length, nseq, conv_prev, s0 in streams:
        qkvz = _matmul(xr, w_in, gain=g0, tm=tmr, tn=tn_in, name="gdn_in_proj_" + tag)
        gates = _matmul(xr, w_gate, gain=g0, tm=tmr, tn=w_gate.shape[1], name="gdn_gate_proj_" + tag)
        mixed.append(_gdn(qkvz, gates, alog_row, dt_row, conv_prev, a_conv_w[0], onorm_row, s0,
                          row0=0, batch=nbatch, length=length, c=min(CHUNK, length), hb=hb, nseq=nseq,
                          name="gdn_" + tag))
    (o_p, gdn_p, *hist_p), (o_s, gdn_s, *hist_s) = mixed
    tn_o = _pick_tile(d, COL_TILES)
    x = _matmul((o_p, o_s), a_w_out[0], res=(xp, xs), tm=tm2, tn=tn_o, weights_outer=True,
                name="gdn_out_proj")
    tf = _pick_tile(w_up.shape[2], COL_TILES[1:])
    x = _mlp(x, mlp_norm[0].reshape(1, d), w_up, w_down, 0, tm=tm, tf=tf, name="mlp0")

    cos_p, sin_p = _rope_tables(jnp.arange(lp))
    cos_s, sin_s = _rope_tables(PAST_LEN + jnp.arange(ls))
    rope = (jnp.concatenate([cos_p] * bp + [cos_s] * bs, axis=0),
            jnp.concatenate([sin_p] * bp + [sin_s] * bs, axis=0))
    kvw = N_KV * HD
    kv = _matmul(x, w_kv, gain=kv_norm.reshape(1, d), rope=rope, rope_cols=kvw,
                 tm=tm, tn=2 * kvw, name="kv_proj")
    q = _matmul(x, b_w_q[0], gain=attn_norm[1].reshape(1, d), rope=rope,
                rope_cols=N_Q * HD, out_dtype=BF16, tm=tm, tn=tn_d, name="q_proj")
    sinks = b_sinks[0].astype(F32)
    a_p = _attention(sinks, q, kv, kv, 0, 1, kv, row0=0, batch=bp, length=lp, tq=WINDOW, nb=1,
                     chunk_mask=True, name="attn_prompt")
    nb = _pick_tile(bs, (ATTN_SAMPLE_BATCHES_PER_STEP, 2, 1))
    a_s = _attention(sinks, q, cache_k.reshape(bs * WINDOW, kvw), cache_v.reshape(bs * WINDOW, kvw),
                     0, 0, kv, row0=mp, batch=bs, length=ls, tq=ls, nb=nb,
                     chunk_mask=False, name="attn_sample")
    x = _matmul((a_p, a_s), b_w_o[0], res=x, tm=tm2, tn=tn_o, weights_outer=True, name="attn_out_proj")
    y_p, y_s = _mlp_split(x, mlp_norm[1].reshape(1, d), w_up, w_down, 1, final_norm.reshape(1, d),
                          split_row=mp, tm=tm, tf=tf, name="mlp1")

    y_prompt = y_p.reshape(bp, lp, d)
    y_sample = y_s.reshape(bs, ls, d)
    conv_p = jnp.concatenate([t[:, SUBLANES - keep:] for t in hist_p], axis=-1)[None]
    conv_s = jnp.concatenate([t[:, SUBLANES - keep:] for t in hist_s], axis=-1)[None]
    kv_p = kv[:mp].reshape(bp, lp, 2 * kvw)[:, lp - WINDOW:]
    kv_s = kv[mp:].reshape(bs, ls, 2 * kvw)
    k_p = kv_p[..., :kvw].reshape(bp, WINDOW, N_KV, HD)
    v_p = kv_p[..., kvw:].reshape(bp, WINDOW, N_KV, HD)
    k_s = kv_s[..., :kvw].reshape(bs, ls, N_KV, HD)
    v_s = kv_s[..., kvw:].reshape(bs, ls, N_KV, HD)
    return (y_prompt, y_sample, conv_p, gdn_p[None], k_p, v_p, conv_s, gdn_s[None], k_s, v_s)
```

```python
import functools
import math

import jax
import jax.numpy as jnp
from jax import lax
from jax.experimental import pallas as pl
from jax.experimental.pallas import tpu as pltpu

F32 = jnp.float32
BF16 = jnp.bfloat16

EPS = 1e-6
CHUNK = 64
WINDOW = 128
PAST_LEN = 4096
ROPE_THETA = 10000.0
HA, DK, DV = 16, 128, 128
N_Q, N_KV, HD = 32, 4, 64
CONV_W = 4
LANES = 128
SUBLANES = 8
VMEM_LIMIT = 56 * 1024 * 1024
MLP_VMEM_LIMIT = 60 * 1024 * 1024
GDN_HEADS_PER_STEP = 16
GDN_SAMPLE_SEQS_PER_STEP = 4
ATTN_SAMPLE_BATCHES_PER_STEP = 4
ROW_TILES = (1088, 512, 256, 128, 64, 32, 16)
PROMPT_ROW_TILES = (1024,) + ROW_TILES[1:]
COL_TILES = (1024, 512, 256, 128)


def _pick_tile(n, candidates):
    for c in candidates:
        if n % c == 0:
            return c
    return n


def _dot(a, b, dims=(((1,), (0,)), ((), ()))):
    return lax.dot_general(a.astype(BF16), b.astype(BF16), dims, preferred_element_type=F32)


_NT = (((1,), (1,)), ((), ()))
_TN = (((0,), (0,)), ((), ()))


def _rms_scale(x):
    return lax.rsqrt(jnp.mean(x * x, axis=-1, keepdims=True) + EPS)


def _silu(x):
    h = 0.5 * x
    return h + h * jnp.tanh(h)


def _rope_slab(y, cos, sin_signed):
    lane = lax.broadcasted_iota(jnp.int32, y.shape, 1)
    up = pltpu.roll(y, 32, 1)
    down = pltpu.roll(y, LANES - 32, 1)
    swapped = jnp.where((lane % HD) < HD // 2, down, up)
    return y * cos + swapped * sin_signed


def _project(lhs, w_ref, res_ref, cos_ref, sin_ref, o_ref, *, rope_slabs, n_slabs):
    if not rope_slabs:
        y = jnp.dot(lhs, w_ref[...].astype(BF16), preferred_element_type=F32)
        if res_ref is not None:
            y = y + res_ref[...]
        o_ref[...] = y.astype(o_ref.dtype)
        return
    cos = cos_ref[...]
    sin = sin_ref[...]
    n_chunk = 2 if n_slabs % 2 == 0 else 1
    for c0 in range(0, n_slabs, n_chunk):
        cols = slice(c0 * LANES, (c0 + n_chunk) * LANES)
        y = jnp.dot(lhs, w_ref[:, cols].astype(BF16), preferred_element_type=F32)
        if res_ref is not None:
            y = y + res_ref[:, cols]
        for s in range(n_chunk):
            ys = y[:, s * LANES:(s + 1) * LANES]
            if c0 + s < rope_slabs:
                ys = _rope_slab(ys, cos, sin)
            o_ref[:, (c0 + s) * LANES:(c0 + s + 1) * LANES] = ys.astype(o_ref.dtype)


def _matmul_kernel(*refs, n_x, n_res, split_tile, row_axis, has_norm, rope_slabs, n_slabs):
    it = iter(refs)
    x_refs = [next(it) for _ in range(n_x)]
    g_ref = next(it) if has_norm else None
    w_ref = next(it)
    res_refs = [next(it) for _ in range(n_res)]
    cos_ref = next(it) if rope_slabs else None
    sin_ref = next(it) if rope_slabs else None
    o_ref = next(it)
    xn_ref = next(it) if has_norm else None

    if has_norm:
        @pl.when(pl.program_id(1) == 0)
        def _():
            x = x_refs[0][...]
            xn_ref[...] = (x * _rms_scale(x) * g_ref[...]).astype(BF16)

    def emit(lhs_ref, res_ref):
        _project(lhs_ref[...], w_ref, res_ref, cos_ref, sin_ref, o_ref, rope_slabs=rope_slabs, n_slabs=n_slabs)

    if n_x == 1:
        emit(xn_ref if has_norm else x_refs[0], res_refs[0] if n_res else None)
    else:
        i = pl.program_id(row_axis)
        pl.when(i < split_tile)(functools.partial(emit, x_refs[0], res_refs[0] if n_res else None))
        pl.when(i >= split_tile)(functools.partial(emit, x_refs[1], res_refs[-1] if n_res else None))


def _matmul(xs, w, *, gain=None, res=None, rope=None, rope_cols=0, out_dtype=F32, tm, tn, name,
            weights_outer=False):
    xs = xs if isinstance(xs, (tuple, list)) else (xs,)
    ress = () if res is None else (res if isinstance(res, (tuple, list)) else (res,))
    k = xs[0].shape[1]
    m = sum(x.shape[0] for x in xs)
    n = (w.shape[1] // tn) * tn
    assert all(x.shape[0] % tm == 0 for x in xs) and w.shape[0] == k
    has_norm = gain is not None
    assert not (has_norm and len(xs) > 1)
    assert len(ress) <= len(xs) and all(r.shape[0] == x.shape[0] for r, x in zip(ress, xs) if len(ress) > 1)
    split_tile = xs[0].shape[0] // tm
    rope_slabs = 0
    if rope is not None:
        assert rope_cols == n or tn == n
        rope_slabs = min(rope_cols, tn) // LANES
    assert not (has_norm and weights_outer)

    def spec(shape, f):
        return pl.BlockSpec(shape, (lambda a, b: f(b, a)) if weights_outer else f)

    def row_specs(arrays, width, col):
        if len(arrays) == 1:
            return [spec((tm, width), lambda i, j: (i, col(j)))]
        return [spec((tm, width), lambda i, j: (jnp.minimum(i, split_tile - 1), col(j))),
                spec((tm, width), lambda i, j: (jnp.maximum(i - split_tile, 0), col(j)))]

    in_specs = row_specs(xs, k, lambda j: 0)
    args = list(xs)
    if has_norm:
        in_specs.append(spec((1, k), lambda i, j: (0, 0)))
        args.append(gain)
    in_specs.append(spec((k, tn), lambda i, j: (0, j)))
    args.append(w)
    in_specs += row_specs(ress, tn, lambda j: j) if ress else []
    args += list(ress)
    if rope_slabs:
        in_specs += [spec((tm, LANES), lambda i, j: (i, 0))] * 2
        args += list(rope)
    kern = functools.partial(_matmul_kernel, n_x=len(xs), n_res=len(ress), split_tile=split_tile,
                             row_axis=1 if weights_outer else 0, has_norm=has_norm,
                             rope_slabs=rope_slabs, n_slabs=tn // LANES)
    grid = (n // tn, m // tm) if weights_outer else (m // tm, n // tn)
    return pl.pallas_call(
        kern,
        grid=grid,
        in_specs=in_specs,
        out_specs=spec((tm, tn), lambda i, j: (i, j)),
        out_shape=jax.ShapeDtypeStruct((m, n), out_dtype),
        scratch_shapes=[pltpu.VMEM((tm, k), BF16)] if has_norm else [],
        compiler_params=pltpu.CompilerParams(
            dimension_semantics=("parallel", "arbitrary"), vmem_limit_bytes=VMEM_LIMIT),
        name=name,
    )(*args)


def _mlp_kernel(x_ref, g_ref, wu_ref, wd_ref, o_ref, hn_ref):
    f = pl.program_id(1)

    @pl.when(f == 0)
    def _():
        x = x_ref[...]
        hn_ref[...] = (x * _rms_scale(x) * g_ref[...]).astype(BF16)
        o_ref[...] = x

    u = jnp.dot(hn_ref[...], wu_ref[...].astype(BF16), preferred_element_type=F32)
    a = jnp.square(jnp.maximum(u, 0.0)).astype(BF16)
    o_ref[...] += jnp.dot(a, wd_ref[...].astype(BF16), preferred_element_type=F32)


def _mlp_in_specs(tm, d, tf, layer):
    return [
        pl.BlockSpec((tm, d), lambda i, f: (i, 0)),
        pl.BlockSpec((1, d), lambda i, f: (0, 0)),
        pl.BlockSpec((None, d, tf), lambda i, f: (layer, 0, f)),
        pl.BlockSpec((None, tf, d), lambda i, f: (layer, f, 0)),
    ]


def _mlp(x, gain, w_up, w_down, layer, *, tm, tf, name):
    m, d = x.shape
    dff = w_up.shape[2]
    assert m % tm == 0 and dff % tf == 0
    return pl.pallas_call(
        _mlp_kernel,
        grid=(m // tm, dff // tf),
        in_specs=_mlp_in_specs(tm, d, tf, layer),
        out_specs=pl.BlockSpec((tm, d), lambda i, f: (i, 0)),
        out_shape=jax.ShapeDtypeStruct((m, d), F32),
        scratch_shapes=[pltpu.VMEM((tm, d), BF16)],
        compiler_params=pltpu.CompilerParams(
            dimension_semantics=("parallel", "arbitrary"), vmem_limit_bytes=MLP_VMEM_LIMIT),
        name=name,
    )(x, gain, w_up, w_down)


def _mlp_split_kernel(x_ref, g_ref, wu_ref, wd_ref, fg_ref, yp_ref, ys_ref, acc_ref, hn_ref, sem_ref,
                      *, tm, n_tiles, split_row):
    i = pl.program_id(0)
    f = pl.program_id(1)
    slot = i % 2

    def writebacks(t):
        r0, r1 = t * tm, (t + 1) * tm
        s = t % 2
        cps = []
        if r0 < split_row:
            n = min(r1, split_row) - r0
            cps.append(pltpu.make_async_copy(acc_ref.at[s, 0:n], yp_ref.at[r0:r0 + n], sem_ref.at[s, 0]))
        if r1 > split_row:
            a = max(r0, split_row)
            cps.append(pltpu.make_async_copy(acc_ref.at[s, a - r0:tm], ys_ref.at[a - split_row:r1 - split_row],
                                             sem_ref.at[s, 1]))
        return cps

    def start_tile(t):
        for cp in writebacks(t):
            cp.start()

    def wait_tile(t):
        for cp in writebacks(t):
            cp.wait()

    @pl.when(f == 0)
    def _():
        for t in range(n_tiles - 2):
            pl.when(i == t + 2)(functools.partial(wait_tile, t))
        x = x_ref[...]
        hn_ref[...] = (x * _rms_scale(x) * g_ref[...]).astype(BF16)
        acc_ref[slot] = x

    u = jnp.dot(hn_ref[...], wu_ref[...].astype(BF16), preferred_element_type=F32)
    a = jnp.square(jnp.maximum(u, 0.0)).astype(BF16)
    acc_ref[slot] += jnp.dot(a, wd_ref[...].astype(BF16), preferred_element_type=F32)

    @pl.when(f == pl.num_programs(1) - 1)
    def _():
        y = acc_ref[slot]
        acc_ref[slot] = y * _rms_scale(y) * fg_ref[...]
        for t in range(n_tiles):
            pl.when(i == t)(functools.partial(start_tile, t))

        @pl.when(i == n_tiles - 1)
        def _():
            for t in range(max(n_tiles - 2, 0), n_tiles):
                wait_tile(t)


def _mlp_split(x, gain, w_up, w_down, layer, final_gain, *, split_row, tm, tf, name):
    m, d = x.shape
    dff = w_up.shape[2]
    assert m % tm == 0 and dff % tf == 0 and split_row % SUBLANES == 0 and 0 < split_row < m
    n_tiles = m // tm
    return pl.pallas_call(
        functools.partial(_mlp_split_kernel, tm=tm, n_tiles=n_tiles, split_row=split_row),
        grid=(n_tiles, dff // tf),
        in_specs=_mlp_in_specs(tm, d, tf, layer) + [pl.BlockSpec((1, d), lambda i, f: (0, 0))],
        out_specs=[pl.BlockSpec(memory_space=pl.ANY), pl.BlockSpec(memory_space=pl.ANY)],
        out_shape=[jax.ShapeDtypeStruct((split_row, d), F32), jax.ShapeDtypeStruct((m - split_row, d), F32)],
        scratch_shapes=[pltpu.VMEM((2, tm, d), F32), pltpu.VMEM((tm, d), BF16),
                        pltpu.SemaphoreType.DMA((2, 2))],
        compiler_params=pltpu.CompilerParams(
            dimension_semantics=("arbitrary", "arbitrary"), vmem_limit_bytes=MLP_VMEM_LIMIT),
        name=name,
    )(x, gain, w_up, w_down, final_gain)


def _gdn_kernel(xq_ref, xk_ref, xv_ref, z_ref, gates_ref, alog_ref, dt_ref,
                pq_ref, pk_ref, pv_ref, wq_ref, wk_ref, wv_ref, onorm_ref, s0_ref,
                o_ref, s_ref, hq_ref, hk_ref, hv_ref, padq_ref, padk_ref, padv_ref, *, c, hb, nseq):
    n = pl.program_id(2)
    halo = SUBLANES
    seqs = range(nseq)

    @pl.when(n == 0)
    def _():
        for pad_ref, prev_ref in ((padq_ref, pq_ref), (padk_ref, pk_ref), (padv_ref, pv_ref)):
            for sq in seqs:
                pad_ref[sq, 0:halo, :] = jnp.zeros((halo, pad_ref.shape[2]), F32)
                pad_ref[sq, halo - (CONV_W - 1):halo, :] = prev_ref[sq]
        s_ref[...] = s0_ref[...]

    def conv(x_ref, pad_ref, w_ref, hist_ref, sq):
        pad_ref[sq, halo:halo + c, :] = x_ref[sq * c:(sq + 1) * c, :]
        xp = pad_ref[sq]
        x1 = pltpu.roll(xp, 1, 0)
        a = xp * w_ref[3:4, :] + x1 * w_ref[2:3, :]
        b = xp * w_ref[1:2, :] + x1 * w_ref[0:1, :]
        acc = (a + pltpu.roll(b, 2, 0))[halo:, :]
        tail = xp[c:c + halo, :]
        pad_ref[sq, 0:halo, :] = tail
        hist_ref[sq] = tail
        return _silu(acc)

    cq = [conv(xq_ref, padq_ref, wq_ref, hq_ref, sq) for sq in seqs]
    ck = [conv(xk_ref, padk_ref, wk_ref, hk_ref, sq) for sq in seqs]
    cv = [conv(xv_ref, padv_ref, wv_ref, hv_ref, sq) for sq in seqs]

    row = lax.broadcasted_iota(jnp.int32, (c, c), 0)
    col = lax.broadcasted_iota(jnp.int32, (c, c), 1)
    incl = row >= col
    strict = row > col
    tril = jnp.where(incl, 1.0, 0.0).astype(F32)
    r128 = lax.broadcasted_iota(jnp.int32, (LANES, LANES), 0)
    c128 = lax.broadcasted_iota(jnp.int32, (LANES, LANES), 1)
    eye128 = jnp.where(r128 == c128, 1.0, 0.0).astype(F32)
    beta_all, gc_all, gc_t = [], [], []
    for sq in seqs:
        gates = gates_ref[sq * c:(sq + 1) * c, :]
        beta_all.append(0.5 + 0.5 * jnp.tanh(0.5 * gates))
        ga = gates + dt_ref[...]
        softplus = jnp.maximum(ga, 0.0) + jnp.log(1.0 + jnp.exp(-jnp.abs(ga)))
        g_all = -jnp.exp(alog_ref[...]) * softplus
        gc_all.append(jnp.dot(tril, g_all, preferred_element_type=F32,
                              precision=lax.Precision.HIGHEST))
        gc_t.append(lax.dot_general(eye128, gc_all[sq], _NT, preferred_element_type=F32,
                                    precision=lax.Precision.HIGHEST))

    units = [(sq, h) for sq in seqs for h in range(hb)]
    un = range(len(units))
    sls = [slice(h * LANES, (h + 1) * LANES) for _, h in units]
    q = [cq[sq][:, sls[u]] for u, (sq, h) in enumerate(units)]
    k = [ck[sq][:, sls[u]] for u, (sq, h) in enumerate(units)]
    v = [cv[sq][:, sls[u]] for u, (sq, h) in enumerate(units)]
    q = [x * (lax.rsqrt(jnp.sum(x * x, axis=-1, keepdims=True) + EPS) * (DK ** -0.5)) for x in q]
    k = [x * lax.rsqrt(jnp.sum(x * x, axis=-1, keepdims=True) + EPS) for x in k]
    bcol = [beta_all[sq][:, h:h + 1] for sq, h in units]
    gcol = [gc_all[sq][:, hb + h:hb + h + 1] for sq, h in units]
    grow = [gc_t[sq][hb + h:hb + h + 1, :] for sq, h in units]
    glast = [gc_all[sq][c - 1:c, hb + h:hb + h + 1] for sq, h in units]
    decay = [jnp.exp(jnp.where(incl, gcol[u] - grow[u], -jnp.inf)) for u in un]
    kb = [k[u] * bcol[u] for u in un]
    eye = jnp.where(row == col, 1.0, 0.0).astype(F32)
    qkk = [_dot(jnp.concatenate([q[u], kb[u]], axis=0), k[u], _NT) for u in un]
    qk = [qkk[u][:c] * decay[u] for u in un]
    p = [-jnp.where(strict, qkk[u][c:] * decay[u], 0.0) for u in un]
    t = [eye + x for x in p]
    levels = max(int(math.ceil(math.log2(c))) - 1, 0)
    if levels:
        p = [_dot(x, x) for x in p]
    for _ in range(1, levels):
        r = [_dot(jnp.concatenate([p[u], t[u]], axis=0), p[u]) for u in un]
        t = [t[u] + r[u][c:] for u in un]
        p = [r[u][:c] for u in un]
    if levels:
        t = [t[u] + _dot(t[u], p[u]) for u in un]
    egc = [jnp.exp(x) for x in gcol]
    rhs = [jnp.concatenate([v[u] * bcol[u], kb[u] * egc[u]], axis=-1) for u in un]
    sol = [_dot(t[u], rhs[u]) for u in un]
    s = [s_ref[sq, h] for sq, h in units]
    wq = [_dot(jnp.concatenate([sol[u][:, DV:], q[u] * egc[u]], axis=0), s[u]) for u in un]
    v_new = [sol[u][:, :DV] - wq[u][:c] for u in un]
    o = [wq[u][c:] + _dot(qk[u], v_new[u]) for u in un]
    s_new = [s[u] * jnp.exp(glast[u]) + _dot(k[u] * jnp.exp(glast[u] - gcol[u]), v_new[u], _TN)
             for u in un]
    onorm = onorm_ref[...]
    for u, (sq, h) in enumerate(units):
        s_ref[sq, h] = s_new[u]
        oh = o[u] * _rms_scale(o[u]) * onorm
        rows = slice(sq * c, (sq + 1) * c)
        o_ref[rows, sls[u]] = (oh * _silu(z_ref[rows, sls[u]])).astype(o_ref.dtype)


def _gdn(qkvz, gates, alog_row, dt_row, conv_prev, conv_w, onorm_row, s0,
         *, batch, length, c, hb, nseq, name):
    nblk = length // c
    ng = HA // hb
    hw = hb * LANES
    rows_step = nseq * c
    assert length % c == 0 and c >= SUBLANES and CONV_W == 4
    assert batch % nseq == 0 and (nseq == 1 or nblk == 1)

    def rows(b, g, n):
        return b * nblk + n

    x_spec = lambda part: pl.BlockSpec((rows_step, hw), lambda b, g, n: (rows(b, g, n), part * ng + g))
    prev_spec = lambda part: pl.BlockSpec((nseq, CONV_W - 1, hw), lambda b, g, n: (b, 0, part * ng + g))
    w_spec = lambda part: pl.BlockSpec((CONV_W, hw), lambda b, g, n: (0, part * ng + g))
    gate_row_spec = pl.BlockSpec((1, LANES), lambda b, g, n: (0, g))
    hist_spec = pl.BlockSpec((nseq, SUBLANES, hw), lambda b, g, n: (b, 0, g))
    hist_shape = jax.ShapeDtypeStruct((batch, SUBLANES, HA * LANES), F32)
    state_spec = pl.BlockSpec((nseq, hb, DK, DV), lambda b, g, n: (b, g, 0, 0))
    in_specs = [
        x_spec(0), x_spec(1), x_spec(2), x_spec(3),
        pl.BlockSpec((rows_step, LANES), lambda b, g, n: (rows(b, g, n), g)),
        gate_row_spec, gate_row_spec,
        prev_spec(0), prev_spec(1), prev_spec(2),
        w_spec(0), w_spec(1), w_spec(2),
        pl.BlockSpec((1, LANES), lambda b, g, n: (0, 0)),
        state_spec,
    ]
    out_specs = [
        pl.BlockSpec((rows_step, hw), lambda b, g, n: (b * nblk + n, g)),
        state_spec,
        hist_spec, hist_spec, hist_spec,
    ]
    return pl.pallas_call(
        functools.partial(_gdn_kernel, c=c, hb=hb, nseq=nseq),
        grid=(batch // nseq, ng, nblk),
        in_specs=in_specs,
        out_specs=out_specs,
        out_shape=[jax.ShapeDtypeStruct((batch * length, HA * DV), BF16),
                   jax.ShapeDtypeStruct((batch, HA, DK, DV), F32),
                   hist_shape, hist_shape, hist_shape],
        scratch_shapes=[pltpu.VMEM((nseq, c + SUBLANES, hw), F32)] * 3,
        compiler_params=pltpu.CompilerParams(
            dimension_semantics=("parallel", "parallel", "arbitrary"),
            vmem_limit_bytes=VMEM_LIMIT),
        name=name,
    )(qkvz, qkvz, qkvz, qkvz, gates, alog_row, dt_row,
      conv_prev, conv_prev, conv_prev, conv_w, conv_w, conv_w, onorm_row, s0)


def _attn_kernel(sinks_ref, q_ref, kp_ref, vp_ref, kc_ref, vc_ref, o_ref, *, tq, nb, chunk_mask):
    i = pl.program_id(1)
    nk = WINDOW + tq
    lane_k = lax.broadcasted_iota(jnp.int32, (nk, LANES), 1)
    lane_q = lax.broadcasted_iota(jnp.int32, (tq, LANES), 1)
    if chunk_mask:
        qc = lax.broadcasted_iota(jnp.int32, (tq, nk), 0) // CHUNK
        kc = lax.broadcasted_iota(jnp.int32, (tq, nk), 1) // CHUNK
        w_ch = WINDOW // CHUNK
        first_kc = jnp.where(i > 0, 0, w_ch)
        valid = (kc >= jnp.maximum(qc, first_kc)) & (kc <= qc + w_ch)
    scale = HD ** -0.5
    assert math.log2(HD) % 2 == 0
    per_slab = LANES // HD
    group = N_Q // N_KV
    slabs_per_kv = group // per_slab
    half_masks = [(lane_q >= r * HD) & (lane_q < (r + 1) * HD) for r in range(per_slab)]

    units = [(b, h) for b in range(nb) for h in range(N_KV)]
    kdup, vdup, qs = [], [], []
    for b, h in units:
        ksl = slice((h // per_slab) * LANES, (h // per_slab + 1) * LANES)
        kslab = jnp.concatenate([kp_ref[b * WINDOW:(b + 1) * WINDOW, ksl],
                                 kc_ref[b * tq:(b + 1) * tq, ksl]], axis=0)
        vslab = jnp.concatenate([vp_ref[b * WINDOW:(b + 1) * WINDOW, ksl],
                                 vc_ref[b * tq:(b + 1) * tq, ksl]], axis=0)
        first = (lane_k < HD) == (h % per_slab == 0)
        kdup.append(jnp.where(first, kslab, pltpu.roll(kslab, HD, 1)).astype(BF16))
        vdup.append(jnp.where(first, vslab, pltpu.roll(vslab, HD, 1)).astype(BF16))
        pieces = []
        for s in range(slabs_per_kv):
            slab = h * slabs_per_kv + s
            q2 = q_ref[b * tq:(b + 1) * tq, slab * LANES:(slab + 1) * LANES]
            pieces += [jnp.where(mk, q2, jnp.zeros_like(q2)) for mk in half_masks]
        qs.append(jnp.concatenate(pieces, axis=0) * scale)
    sc = [lax.dot_general(qs[u], kdup[u], _NT, preferred_element_type=F32) for u in range(len(units))]
    ps = []
    for u, (b, h) in enumerate(units):
        blocks = []
        for r in range(group):
            sr = sc[u][r * tq:(r + 1) * tq]
            if chunk_mask:
                sr = jnp.where(valid, sr, -jnp.inf)
            sk = sinks_ref[h * group + r]
            mx = jnp.maximum(jnp.max(sr, axis=-1, keepdims=True), sk)
            e = jnp.exp(sr - mx)
            den = jnp.sum(e, axis=-1, keepdims=True) + jnp.exp(sk - mx)
            blocks.append((e / den).astype(BF16))
        ps.append(jnp.concatenate(blocks, axis=0))
    pv = [jnp.dot(ps[u], vdup[u], preferred_element_type=F32) for u in range(len(units))]
    for u, (b, h) in enumerate(units):
        for s in range(slabs_per_kv):
            slab = h * slabs_per_kv + s
            halves = [pv[u][(s * per_slab + r) * tq:(s * per_slab + r + 1) * tq] for r in range(per_slab)]
            out = halves[-1]
            for r in range(per_slab - 2, -1, -1):
                out = jnp.where(half_masks[r], halves[r], out)
            o_ref[b * tq:(b + 1) * tq, slab * LANES:(slab + 1) * LANES] = out.astype(o_ref.dtype)


def _attention(sinks, q, kprev, vprev, kprev_col, vprev_col, kv, *, row0, batch, length,
               tq, nb, chunk_mask, name):
    nq = length // tq
    kvw = N_KV * HD
    assert row0 % (nb * tq) == 0 and length % tq == 0 and batch % nb == 0 and (nb == 1 or nq == 1)
    rb0 = row0 // (nb * tq)

    def cur(b, i):
        return rb0 + b * nq + i

    def prev(b, i):
        return b * nq + jnp.maximum(i - 1, 0)

    in_specs = [
        pl.BlockSpec(memory_space=pltpu.SMEM),
        pl.BlockSpec((nb * tq, N_Q * HD), lambda b, i: (cur(b, i), 0)),
        pl.BlockSpec((nb * WINDOW, kvw), lambda b, i: (prev(b, i), kprev_col)),
        pl.BlockSpec((nb * WINDOW, kvw), lambda b, i: (prev(b, i), vprev_col)),
        pl.BlockSpec((nb * tq, kvw), lambda b, i: (cur(b, i), 0)),
        pl.BlockSpec((nb * tq, kvw), lambda b, i: (cur(b, i), 1)),
    ]
    return pl.pallas_call(
        functools.partial(_attn_kernel, tq=tq, nb=nb, chunk_mask=chunk_mask),
        grid=(batch // nb, nq),
        in_specs=in_specs,
        out_specs=pl.BlockSpec((nb * tq, N_Q * HD), lambda b, i: (b * nq + i, 0)),
        out_shape=jax.ShapeDtypeStruct((batch * length, N_Q * HD), BF16),
        compiler_params=pltpu.CompilerParams(
            dimension_semantics=("parallel", "arbitrary"), vmem_limit_bytes=VMEM_LIMIT),
        name=name,
    )(sinks, q, kprev, vprev, kv, kv)


def _rope_tables(pos):
    half = HD // 2
    inv = 1.0 / (ROPE_THETA ** (jnp.arange(half, dtype=F32) / half))
    ang = pos.astype(F32)[:, None] * jnp.tile(inv, LANES // half)[None, :]
    sign = jnp.tile(jnp.concatenate([-jnp.ones((half,), F32), jnp.ones((half,), F32)]), LANES // HD)
    return jnp.cos(ang), jnp.sin(ang) * sign[None, :]


def _gate_layout(t, hb):
    lead = t.shape[:-1]
    ng = HA // hb
    if ng == 1:
        return jnp.pad(t, [(0, 0)] * len(lead) + [(0, LANES - 2 * HA)])
    beta = t[..., :HA].reshape(lead + (ng, hb))
    dec = t[..., HA:].reshape(lead + (ng, hb))
    pad = jnp.zeros(lead + (ng, LANES - 2 * hb), t.dtype)
    return jnp.concatenate([beta, dec, pad], axis=-1).reshape(lead + (ng * LANES,))


def kernel(x_prompt, x_sample, cache_conv, state_gdn, cache_k, cache_v, attn_norm, mlp_norm, final_norm, a_w_in, a_conv_w, a_log, a_dt_bias, a_o_norm, a_w_out, kv_norm, w_kv, b_w_q, b_sinks, b_w_o, w_up, w_down):
    bp, lp, d = x_prompt.shape
    bs, ls, _ = x_sample.shape
    mp, ms = bp * lp, bs * ls
    m = mp + ms
    hb = GDN_HEADS_PER_STEP
    dqk, dvw = HA * DK, HA * DV
    conv_ch = 2 * dqk + dvw
    keep = CONV_W - 1
    assert a_w_in.shape[0] == 1 and b_w_q.shape[0] == 1 and lp % WINDOW == 0 and ls >= keep

    xp = x_prompt.reshape(mp, d)
    xs = x_sample.reshape(ms, d)
    tm = _pick_tile(m, ROW_TILES)
    tm2 = _pick_tile(math.gcd(mp, ms), ROW_TILES[1:])
    tm_p = _pick_tile(mp, PROMPT_ROW_TILES)
    tm_s = _pick_tile(ms, ROW_TILES[1:])
    tn_d = _pick_tile(d, COL_TILES[1:])

    w_in = a_w_in[0].astype(BF16)
    w_gate = _gate_layout(w_in[:, conv_ch + dvw:], hb)
    g0 = attn_norm[0].reshape(1, d)
    tn_in = _pick_tile(conv_ch + dvw, COL_TILES)
    zeros_gate = jnp.zeros((HA,), F32)
    alog_row = _gate_layout(jnp.concatenate([zeros_gate, a_log[0]]), hb).reshape(1, -1)
    dt_row = _gate_layout(jnp.concatenate([zeros_gate, a_dt_bias[0]]), hb).reshape(1, -1)
    onorm_row = a_o_norm[0].reshape(1, DV)
    streams = (("prompt", xp, tm_p, bp, lp, 1, jnp.zeros((bp, keep, conv_ch), F32),
                jnp.zeros((bp, HA, DK, DV), F32)),
               ("sample", xs, tm_s, bs, ls, _pick_tile(bs, (GDN_SAMPLE_SEQS_PER_STEP, 1)) if ls <= CHUNK else 1,
                cache_conv[0], state_gdn[0]))
    mixed = []
    for tag, xr, tmr, nbatch, length, nseq, conv_prev, s0 in streams:
        qkvz = _matmul(xr, w_in, gain=g0, tm=tmr, tn=tn_in, name="gdn_in_proj_" + tag)
        gates = _matmul(xr, w_gate, gain=g0, tm=tmr, tn=w_gate.shape[1], name="gdn_gate_proj_" + tag)
        mixed.append(_gdn(qkvz, gates, alog_row, dt_row, conv_prev, a_conv_w[0], onorm_row, s0,
                          batch=nbatch, length=length, c=min(CHUNK, length), hb=hb, nseq=nseq,
                          name="gdn_" + tag))
    (o_p, gdn_p, *hist_p), (o_s, gdn_s, *hist_s) = mixed
    tn_o = _pick_tile(d, COL_TILES)
    x = _matmul((o_p, o_s), a_w_out[0], res=(xp, xs), tm=tm2, tn=tn_o, weights_outer=True,
                name="gdn_out_proj")
    tf = _pick_tile(w_up.shape[2], COL_TILES[1:])
    x = _mlp(x, mlp_norm[0].reshape(1, d), w_up, w_down, 0, tm=tm, tf=tf, name="mlp0")

    cos_p, sin_p = _rope_tables(jnp.arange(lp))
    cos_s, sin_s = _rope_tables(PAST_LEN + jnp.arange(ls))
    rope = (jnp.concatenate([cos_p] * bp + [cos_s] * bs, axis=0),
            jnp.concatenate([sin_p] * bp + [sin_s] * bs, axis=0))
    kvw = N_KV * HD
    kv = _matmul(x, w_kv, gain=kv_norm.reshape(1, d), rope=rope, rope_cols=kvw,
                 tm=tm, tn=2 * kvw, name="kv_proj")
    q = _matmul(x, b_w_q[0], gain=attn_norm[1].reshape(1, d), rope=rope,
                rope_cols=N_Q * HD, out_dtype=BF16, tm=tm, tn=tn_d, name="q_proj")
    sinks = b_sinks[0].astype(F32)
    a_p = _attention(sinks, q, kv, kv, 0, 1, kv, row0=0, batch=bp, length=lp, tq=WINDOW, nb=1,
                     chunk_mask=True, name="attn_prompt")
    nb = _pick_tile(bs, (ATTN_SAMPLE_BATCHES_PER_STEP, 2, 1))
    a_s = _attention(sinks, q, cache_k.reshape(bs * WINDOW, kvw), cache_v.reshape(bs * WINDOW, kvw),
                     0, 0, kv, row0=mp, batch=bs, length=ls, tq=ls, nb=nb,
                     chunk_mask=False, name="attn_sample")
    x = _matmul((a_p, a_s), b_w_o[0], res=x, tm=tm2, tn=tn_o, weights_outer=True, name="attn_out_proj")
    y_p, y_s = _mlp_split(x, mlp_norm[1].reshape(1, d), w_up, w_down, 1, final_norm.reshape(1, d),
                          split_row=mp, tm=tm, tf=tf, name="mlp1")

    y_prompt = y_p.reshape(bp, lp, d)
    y_sample = y_s.reshape(bs, ls, d)
    conv_p = jnp.concatenate([t[:, SUBLANES - keep:] for t in hist_p], axis=-1)[None]
    conv_s = jnp.concatenate([t[:, SUBLANES - keep:] for t in hist_s], axis=-1)[None]
    kv_p = kv[:mp].reshape(bp, lp, 2 * kvw)[:, lp - WINDOW:]
    kv_s = kv[mp:].reshape(bs, ls, 2 * kvw)
    k_p = kv_p[..., :kvw].reshape(bp, WINDOW, N_KV, HD)
    v_p = kv_p[..., kvw:].reshape(bp, WINDOW, N_KV, HD)
    k_s = kv_s[..., :kvw].reshape(bs, ls, N_KV, HD)
    v_s = kv_s[..., kvw:].reshape(bs, ls, N_KV, HD)
    return (y_prompt, y_sample, conv_p, gdn_p[None], k_p, v_p, conv_s, gdn_s[None], k_s, v_s)
```

```python
import functools
import math

import jax
import jax.numpy as jnp
from jax import lax
from jax.experimental import pallas as pl
from jax.experimental.pallas import tpu as pltpu

F32 = jnp.float32
BF16 = jnp.bfloat16

EPS = 1e-6
CHUNK = 64
WINDOW = 128
PAST_LEN = 4096
ROPE_THETA = 10000.0
HA, DK, DV = 16, 128, 128
N_Q, N_KV, HD = 32, 4, 64
CONV_W = 4
LANES = 128
SUBLANES = 8
VMEM_LIMIT = 56 * 1024 * 1024
MLP_VMEM_LIMIT = 60 * 1024 * 1024
GDN_HEADS_PER_STEP = 16
GDN_SAMPLE_SEQS_PER_STEP = 4
GDN_PROMPT_BLOCKS_PER_STEP = 4
ATTN_SAMPLE_BATCHES_PER_STEP = 4
ROW_TILES = (1088, 512, 256, 128, 64, 32, 16)
PROMPT_ROW_TILES = (1024,) + ROW_TILES[1:]
COL_TILES = (1024, 512, 256, 128)


def _pick_tile(n, candidates):
    for c in candidates:
        if n % c == 0:
            return c
    return n


def _dot(a, b, dims=(((1,), (0,)), ((), ()))):
    return lax.dot_general(a.astype(BF16), b.astype(BF16), dims, preferred_element_type=F32)


_NT = (((1,), (1,)), ((), ()))
_TN = (((0,), (0,)), ((), ()))


def _rms_scale(x):
    return lax.rsqrt(jnp.mean(x * x, axis=-1, keepdims=True) + EPS)


def _silu(x):
    h = 0.5 * x
    return h + h * jnp.tanh(h)


def _rope_slab(y, cos, sin_signed):
    lane = lax.broadcasted_iota(jnp.int32, y.shape, 1)
    up = pltpu.roll(y, 32, 1)
    down = pltpu.roll(y, LANES - 32, 1)
    swapped = jnp.where((lane % HD) < HD // 2, down, up)
    return y * cos + swapped * sin_signed


def _project(lhs, w_ref, res_ref, cos_ref, sin_ref, o_ref, *, rope_slabs, n_slabs):
    if not rope_slabs:
        y = jnp.dot(lhs, w_ref[...].astype(BF16), preferred_element_type=F32)
        if res_ref is not None:
            y = y + res_ref[...]
        o_ref[...] = y.astype(o_ref.dtype)
        return
    cos = cos_ref[...]
    sin = sin_ref[...]
    n_chunk = 2 if n_slabs % 2 == 0 else 1
    for c0 in range(0, n_slabs, n_chunk):
        cols = slice(c0 * LANES, (c0 + n_chunk) * LANES)
        y = jnp.dot(lhs, w_ref[:, cols].astype(BF16), preferred_element_type=F32)
        if res_ref is not None:
            y = y + res_ref[:, cols]
        for s in range(n_chunk):
            ys = y[:, s * LANES:(s + 1) * LANES]
            if c0 + s < rope_slabs:
                ys = _rope_slab(ys, cos, sin)
            o_ref[:, (c0 + s) * LANES:(c0 + s + 1) * LANES] = ys.astype(o_ref.dtype)


def _matmul_kernel(*refs, n_x, n_res, split_tile, row_axis, has_norm, rope_slabs, n_slabs):
    it = iter(refs)
    x_refs = [next(it) for _ in range(n_x)]
    g_ref = next(it) if has_norm else None
    w_ref = next(it)
    res_refs = [next(it) for _ in range(n_res)]
    cos_ref = next(it) if rope_slabs else None
    sin_ref = next(it) if rope_slabs else None
    o_ref = next(it)
    xn_ref = next(it) if has_norm else None

    if has_norm:
        @pl.when(pl.program_id(1) == 0)
        def _():
            x = x_refs[0][...]
            xn_ref[...] = (x * _rms_scale(x) * g_ref[...]).astype(BF16)

    def emit(lhs_ref, res_ref):
        _project(lhs_ref[...], w_ref, res_ref, cos_ref, sin_ref, o_ref, rope_slabs=rope_slabs, n_slabs=n_slabs)

    if n_x == 1:
        emit(xn_ref if has_norm else x_refs[0], res_refs[0] if n_res else None)
    else:
        i = pl.program_id(row_axis)
        pl.when(i < split_tile)(functools.partial(emit, x_refs[0], res_refs[0] if n_res else None))
        pl.when(i >= split_tile)(functools.partial(emit, x_refs[1], res_refs[-1] if n_res else None))


def _matmul(xs, w, *, gain=None, res=None, rope=None, rope_cols=0, out_dtype=F32, tm, tn, name,
            weights_outer=False):
    xs = xs if isinstance(xs, (tuple, list)) else (xs,)
    ress = () if res is None else (res if isinstance(res, (tuple, list)) else (res,))
    k = xs[0].shape[1]
    m = sum(x.shape[0] for x in xs)
    n = (w.shape[1] // tn) * tn
    assert all(x.shape[0] % tm == 0 for x in xs) and w.shape[0] == k
    has_norm = gain is not None
    assert not (has_norm and len(xs) > 1)
    assert len(ress) <= len(xs) and all(r.shape[0] == x.shape[0] for r, x in zip(ress, xs) if len(ress) > 1)
    split_tile = xs[0].shape[0] // tm
    rope_slabs = 0
    if rope is not None:
        assert rope_cols == n or tn == n
        rope_slabs = min(rope_cols, tn) // LANES
    assert not (has_norm and weights_outer)

    def spec(shape, f):
        return pl.BlockSpec(shape, (lambda a, b: f(b, a)) if weights_outer else f)

    def row_specs(arrays, width, col):
        if len(arrays) == 1:
            return [spec((tm, width), lambda i, j: (i, col(j)))]
        return [spec((tm, width), lambda i, j: (jnp.minimum(i, split_tile - 1), col(j))),
                spec((tm, width), lambda i, j: (jnp.maximum(i - split_tile, 0), col(j)))]

    in_specs = row_specs(xs, k, lambda j: 0)
    args = list(xs)
    if has_norm:
        in_specs.append(spec((1, k), lambda i, j: (0, 0)))
        args.append(gain)
    in_specs.append(spec((k, tn), lambda i, j: (0, j)))
    args.append(w)
    in_specs += row_specs(ress, tn, lambda j: j) if ress else []
    args += list(ress)
    if rope_slabs:
        in_specs += [spec((tm, LANES), lambda i, j: (i, 0))] * 2
        args += list(rope)
    kern = functools.partial(_matmul_kernel, n_x=len(xs), n_res=len(ress), split_tile=split_tile,
                             row_axis=1 if weights_outer else 0, has_norm=has_norm,
                             rope_slabs=rope_slabs, n_slabs=tn // LANES)
    grid = (n // tn, m // tm) if weights_outer else (m // tm, n // tn)
    return pl.pallas_call(
        kern,
        grid=grid,
        in_specs=in_specs,
        out_specs=spec((tm, tn), lambda i, j: (i, j)),
        out_shape=jax.ShapeDtypeStruct((m, n), out_dtype),
        scratch_shapes=[pltpu.VMEM((tm, k), BF16)] if has_norm else [],
        compiler_params=pltpu.CompilerParams(
            dimension_semantics=("parallel", "arbitrary"), vmem_limit_bytes=VMEM_LIMIT),
        name=name,
    )(*args)


def _mlp_kernel(x_ref, g_ref, wu_ref, wd_ref, o_ref, hn_ref):
    f = pl.program_id(1)

    @pl.when(f == 0)
    def _():
        x = x_ref[...]
        hn_ref[...] = (x * _rms_scale(x) * g_ref[...]).astype(BF16)
        o_ref[...] = x

    u = jnp.dot(hn_ref[...], wu_ref[...].astype(BF16), preferred_element_type=F32)
    a = jnp.square(jnp.maximum(u, 0.0)).astype(BF16)
    o_ref[...] += jnp.dot(a, wd_ref[...].astype(BF16), preferred_element_type=F32)


def _mlp_in_specs(tm, d, tf, layer):
    return [
        pl.BlockSpec((tm, d), lambda i, f: (i, 0)),
        pl.BlockSpec((1, d), lambda i, f: (0, 0)),
        pl.BlockSpec((None, d, tf), lambda i, f: (layer, 0, f)),
        pl.BlockSpec((None, tf, d), lambda i, f: (layer, f, 0)),
    ]


def _mlp(x, gain, w_up, w_down, layer, *, tm, tf, name):
    m, d = x.shape
    dff = w_up.shape[2]
    assert m % tm == 0 and dff % tf == 0
    return pl.pallas_call(
        _mlp_kernel,
        grid=(m // tm, dff // tf),
        in_specs=_mlp_in_specs(tm, d, tf, layer),
        out_specs=pl.BlockSpec((tm, d), lambda i, f: (i, 0)),
        out_shape=jax.ShapeDtypeStruct((m, d), F32),
        scratch_shapes=[pltpu.VMEM((tm, d), BF16)],
        compiler_params=pltpu.CompilerParams(
            dimension_semantics=("parallel", "arbitrary"), vmem_limit_bytes=MLP_VMEM_LIMIT),
        name=name,
    )(x, gain, w_up, w_down)


def _mlp_split_kernel(x_ref, g_ref, wu_ref, wd_ref, fg_ref, yp_ref, ys_ref, acc_ref, hn_ref, sem_ref,
                      *, tm, n_tiles, split_row):
    i = pl.program_id(0)
    f = pl.program_id(1)
    slot = i % 2

    def writebacks(t):
        r0, r1 = t * tm, (t + 1) * tm
        s = t % 2
        cps = []
        if r0 < split_row:
            n = min(r1, split_row) - r0
            cps.append(pltpu.make_async_copy(acc_ref.at[s, 0:n], yp_ref.at[r0:r0 + n], sem_ref.at[s, 0]))
        if r1 > split_row:
            a = max(r0, split_row)
            cps.append(pltpu.make_async_copy(acc_ref.at[s, a - r0:tm], ys_ref.at[a - split_row:r1 - split_row],
                                             sem_ref.at[s, 1]))
        return cps

    def start_tile(t):
        for cp in writebacks(t):
            cp.start()

    def wait_tile(t):
        for cp in writebacks(t):
            cp.wait()

    @pl.when(f == 0)
    def _():
        for t in range(n_tiles - 2):
            pl.when(i == t + 2)(functools.partial(wait_tile, t))
        x = x_ref[...]
        hn_ref[...] = (x * _rms_scale(x) * g_ref[...]).astype(BF16)
        acc_ref[slot] = x

    u = jnp.dot(hn_ref[...], wu_ref[...].astype(BF16), preferred_element_type=F32)
    a = jnp.square(jnp.maximum(u, 0.0)).astype(BF16)
    acc_ref[slot] += jnp.dot(a, wd_ref[...].astype(BF16), preferred_element_type=F32)

    @pl.when(f == pl.num_programs(1) - 1)
    def _():
        y = acc_ref[slot]
        acc_ref[slot] = y * _rms_scale(y) * fg_ref[...]
        for t in range(n_tiles):
            pl.when(i == t)(functools.partial(start_tile, t))

        @pl.when(i == n_tiles - 1)
        def _():
            for t in range(max(n_tiles - 2, 0), n_tiles):
                wait_tile(t)


def _mlp_split(x, gain, w_up, w_down, layer, final_gain, *, split_row, tm, tf, name):
    m, d = x.shape
    dff = w_up.shape[2]
    assert m % tm == 0 and dff % tf == 0 and split_row % SUBLANES == 0 and 0 < split_row < m
    n_tiles = m // tm
    return pl.pallas_call(
        functools.partial(_mlp_split_kernel, tm=tm, n_tiles=n_tiles, split_row=split_row),
        grid=(n_tiles, dff // tf),
        in_specs=_mlp_in_specs(tm, d, tf, layer) + [pl.BlockSpec((1, d), lambda i, f: (0, 0))],
        out_specs=[pl.BlockSpec(memory_space=pl.ANY), pl.BlockSpec(memory_space=pl.ANY)],
        out_shape=[jax.ShapeDtypeStruct((split_row, d), F32), jax.ShapeDtypeStruct((m - split_row, d), F32)],
        scratch_shapes=[pltpu.VMEM((2, tm, d), F32), pltpu.VMEM((tm, d), BF16),
                        pltpu.SemaphoreType.DMA((2, 2))],
        compiler_params=pltpu.CompilerParams(
            dimension_semantics=("arbitrary", "arbitrary"), vmem_limit_bytes=MLP_VMEM_LIMIT),
        name=name,
    )(x, gain, w_up, w_down, final_gain)


def _gdn_kernel(xq_ref, xk_ref, xv_ref, z_ref, gates_ref, alog_ref, dt_ref,
                pq_ref, pk_ref, pv_ref, wq_ref, wk_ref, wv_ref, onorm_ref, s0_ref,
                o_ref, s_ref, hq_ref, hk_ref, hv_ref, padq_ref, padk_ref, padv_ref, *, c, hb, nseq, nsub):
    n = pl.program_id(2)
    halo = SUBLANES
    seqs = range(nseq)
    rows_seq = nsub * c

    @pl.when(n == 0)
    def _():
        for pad_ref, prev_ref in ((padq_ref, pq_ref), (padk_ref, pk_ref), (padv_ref, pv_ref)):
            for sq in seqs:
                pad_ref[sq, 0:halo, :] = jnp.zeros((halo, pad_ref.shape[2]), F32)
                pad_ref[sq, halo - (CONV_W - 1):halo, :] = prev_ref[sq]
        s_ref[...] = s0_ref[...]

    def conv(x_ref, pad_ref, w_ref, hist_ref, sq):
        pad_ref[sq, halo:halo + rows_seq, :] = x_ref[sq * rows_seq:(sq + 1) * rows_seq, :]
        xp = pad_ref[sq]
        x1 = pltpu.roll(xp, 1, 0)
        a = xp * w_ref[3:4, :] + x1 * w_ref[2:3, :]
        b = xp * w_ref[1:2, :] + x1 * w_ref[0:1, :]
        acc = (a + pltpu.roll(b, 2, 0))[halo:, :]
        tail = xp[rows_seq:rows_seq + halo, :]
        pad_ref[sq, 0:halo, :] = tail
        hist_ref[sq] = tail
        return _silu(acc)

    cq = [conv(xq_ref, padq_ref, wq_ref, hq_ref, sq) for sq in seqs]
    ck = [conv(xk_ref, padk_ref, wk_ref, hk_ref, sq) for sq in seqs]
    cv = [conv(xv_ref, padv_ref, wv_ref, hv_ref, sq) for sq in seqs]

    row = lax.broadcasted_iota(jnp.int32, (c, c), 0)
    col = lax.broadcasted_iota(jnp.int32, (c, c), 1)
    incl = row >= col
    strict = row > col
    tril = jnp.where(incl, 1.0, 0.0).astype(F32)
    r128 = lax.broadcasted_iota(jnp.int32, (LANES, LANES), 0)
    c128 = lax.broadcasted_iota(jnp.int32, (LANES, LANES), 1)
    eye128 = jnp.where(r128 == c128, 1.0, 0.0).astype(F32)
    blocks = [(sq, sb) for sq in seqs for sb in range(nsub)]
    brow = {blk: slice(blk[0] * rows_seq + blk[1] * c, blk[0] * rows_seq + (blk[1] + 1) * c) for blk in blocks}
    beta_all, gc_all, gc_t = {}, {}, {}
    for blk in blocks:
        gates = gates_ref[brow[blk], :]
        beta_all[blk] = 0.5 + 0.5 * jnp.tanh(0.5 * gates)
        ga = gates + dt_ref[...]
        softplus = jnp.maximum(ga, 0.0) + jnp.log(1.0 + jnp.exp(-jnp.abs(ga)))
        g_all = -jnp.exp(alog_ref[...]) * softplus
        gc_all[blk] = jnp.dot(tril, g_all, preferred_element_type=F32,
                              precision=lax.Precision.HIGHEST)
        gc_t[blk] = lax.dot_general(eye128, gc_all[blk], _NT, preferred_element_type=F32,
                                    precision=lax.Precision.HIGHEST)

    units = [(blk, h) for blk in blocks for h in range(hb)]
    un = range(len(units))
    sls = [slice(h * LANES, (h + 1) * LANES) for _, h in units]
    srow = [slice(blk[1] * c, (blk[1] + 1) * c) for blk, _ in units]
    q = [cq[blk[0]][srow[u], sls[u]] for u, (blk, h) in enumerate(units)]
    k = [ck[blk[0]][srow[u], sls[u]] for u, (blk, h) in enumerate(units)]
    v = [cv[blk[0]][srow[u], sls[u]] for u, (blk, h) in enumerate(units)]
    q = [x * (lax.rsqrt(jnp.sum(x * x, axis=-1, keepdims=True) + EPS) * (DK ** -0.5)) for x in q]
    k = [x * lax.rsqrt(jnp.sum(x * x, axis=-1, keepdims=True) + EPS) for x in k]
    bcol = [beta_all[blk][:, h:h + 1] for blk, h in units]
    gcol = [gc_all[blk][:, hb + h:hb + h + 1] for blk, h in units]
    grow = [gc_t[blk][hb + h:hb + h + 1, :] for blk, h in units]
    glast = [gc_all[blk][c - 1:c, hb + h:hb + h + 1] for blk, h in units]
    decay = [jnp.exp(jnp.where(incl, gcol[u] - grow[u], -jnp.inf)) for u in un]
    kb = [k[u] * bcol[u] for u in un]
    eye = jnp.where(row == col, 1.0, 0.0).astype(F32)
    qkk = [_dot(jnp.concatenate([q[u], kb[u]], axis=0), k[u], _NT) for u in un]
    qk = [qkk[u][:c] * decay[u] for u in un]
    p = [-jnp.where(strict, qkk[u][c:] * decay[u], 0.0) for u in un]
    t = [eye + x for x in p]
    levels = max(int(math.ceil(math.log2(c))) - 1, 0)
    if levels:
        p = [_dot(x, x) for x in p]
    for _ in range(1, levels):
        r = [_dot(jnp.concatenate([p[u], t[u]], axis=0), p[u]) for u in un]
        t = [t[u] + r[u][c:] for u in un]
        p = [r[u][:c] for u in un]
    if levels:
        t = [t[u] + _dot(t[u], p[u]) for u in un]
    egc = [jnp.exp(x) for x in gcol]
    rhs = [jnp.concatenate([v[u] * bcol[u], kb[u] * egc[u]], axis=-1) for u in un]
    sol = [_dot(t[u], rhs[u]) for u in un]
    wqe = [jnp.concatenate([sol[u][:, DV:], q[u] * egc[u]], axis=0) for u in un]
    kd = [k[u] * jnp.exp(glast[u] - gcol[u]) for u in un]
    elast = [jnp.exp(x) for x in glast]

    onorm = onorm_ref[...]
    s = {(sq, h): s_ref[sq, h] for sq in seqs for h in range(hb)}
    for sb in range(nsub):
        cur = [u for u in un if units[u][0][1] == sb]
        key = {u: (units[u][0][0], units[u][1]) for u in cur}
        wq = {u: _dot(wqe[u], s[key[u]]) for u in cur}
        v_new = {u: sol[u][:, :DV] - wq[u][:c] for u in cur}
        o = {u: wq[u][c:] + _dot(qk[u], v_new[u]) for u in cur}
        for u in cur:
            s[key[u]] = s[key[u]] * elast[u] + _dot(kd[u], v_new[u], _TN)
        for u in cur:
            oh = o[u] * _rms_scale(o[u]) * onorm
            rows = brow[units[u][0]]
            o_ref[rows, sls[u]] = (oh * _silu(z_ref[rows, sls[u]])).astype(o_ref.dtype)
    for (sq, h), val in s.items():
        s_ref[sq, h] = val


def _gdn(qkvz, gates, alog_row, dt_row, conv_prev, conv_w, onorm_row, s0,
         *, batch, length, c, hb, nseq, nsub, name):
    nblk = length // (nsub * c)
    ng = HA // hb
    hw = hb * LANES
    rows_step = nseq * nsub * c
    assert length % (nsub * c) == 0 and c >= SUBLANES and CONV_W == 4
    assert batch % nseq == 0 and (nseq == 1 or nblk == 1)

    def rows(b, g, n):
        return b * nblk + n

    x_spec = lambda part: pl.BlockSpec((rows_step, hw), lambda b, g, n: (rows(b, g, n), part * ng + g))
    prev_spec = lambda part: pl.BlockSpec((nseq, CONV_W - 1, hw), lambda b, g, n: (b, 0, part * ng + g))
    w_spec = lambda part: pl.BlockSpec((CONV_W, hw), lambda b, g, n: (0, part * ng + g))
    gate_row_spec = pl.BlockSpec((1, LANES), lambda b, g, n: (0, g))
    hist_spec = pl.BlockSpec((nseq, SUBLANES, hw), lambda b, g, n: (b, 0, g))
    hist_shape = jax.ShapeDtypeStruct((batch, SUBLANES, HA * LANES), F32)
    state_spec = pl.BlockSpec((nseq, hb, DK, DV), lambda b, g, n: (b, g, 0, 0))
    in_specs = [
        x_spec(0), x_spec(1), x_spec(2), x_spec(3),
        pl.BlockSpec((rows_step, LANES), lambda b, g, n: (rows(b, g, n), g)),
        gate_row_spec, gate_row_spec,
        prev_spec(0), prev_spec(1), prev_spec(2),
        w_spec(0), w_spec(1), w_spec(2),
        pl.BlockSpec((1, LANES), lambda b, g, n: (0, 0)),
        state_spec,
    ]
    out_specs = [
        pl.BlockSpec((rows_step, hw), lambda b, g, n: (b * nblk + n, g)),
        state_spec,
        hist_spec, hist_spec, hist_spec,
    ]
    return pl.pallas_call(
        functools.partial(_gdn_kernel, c=c, hb=hb, nseq=nseq, nsub=nsub),
        grid=(batch // nseq, ng, nblk),
        in_specs=in_specs,
        out_specs=out_specs,
        out_shape=[jax.ShapeDtypeStruct((batch * length, HA * DV), BF16),
                   jax.ShapeDtypeStruct((batch, HA, DK, DV), F32),
                   hist_shape, hist_shape, hist_shape],
        scratch_shapes=[pltpu.VMEM((nseq, nsub * c + SUBLANES, hw), F32)] * 3,
        compiler_params=pltpu.CompilerParams(
            dimension_semantics=("parallel", "parallel", "arbitrary"),
            vmem_limit_bytes=VMEM_LIMIT),
        name=name,
    )(qkvz, qkvz, qkvz, qkvz, gates, alog_row, dt_row,
      conv_prev, conv_prev, conv_prev, conv_w, conv_w, conv_w, onorm_row, s0)


def _attn_kernel(sinks_ref, q_ref, kp_ref, vp_ref, kc_ref, vc_ref, o_ref, *, tq, nb, chunk_mask):
    i = pl.program_id(1)
    nk = WINDOW + tq
    lane_k = lax.broadcasted_iota(jnp.int32, (nk, LANES), 1)
    lane_q = lax.broadcasted_iota(jnp.int32, (tq, LANES), 1)
    if chunk_mask:
        qc = lax.broadcasted_iota(jnp.int32, (tq, nk), 0) // CHUNK
        kc = lax.broadcasted_iota(jnp.int32, (tq, nk), 1) // CHUNK
        w_ch = WINDOW // CHUNK
        first_kc = jnp.where(i > 0, 0, w_ch)
        valid = (kc >= jnp.maximum(qc, first_kc)) & (kc <= qc + w_ch)
    scale = HD ** -0.5
    assert math.log2(HD) % 2 == 0
    per_slab = LANES // HD
    group = N_Q // N_KV
    slabs_per_kv = group // per_slab
    half_masks = [(lane_q >= r * HD) & (lane_q < (r + 1) * HD) for r in range(per_slab)]

    units = [(b, h) for b in range(nb) for h in range(N_KV)]
    kdup, vdup, qs = [], [], []
    for b, h in units:
        ksl = slice((h // per_slab) * LANES, (h // per_slab + 1) * LANES)
        kslab = jnp.concatenate([kp_ref[b * WINDOW:(b + 1) * WINDOW, ksl],
                                 kc_ref[b * tq:(b + 1) * tq, ksl]], axis=0)
        vslab = jnp.concatenate([vp_ref[b * WINDOW:(b + 1) * WINDOW, ksl],
                                 vc_ref[b * tq:(b + 1) * tq, ksl]], axis=0)
        first = (lane_k < HD) == (h % per_slab == 0)
        kdup.append(jnp.where(first, kslab, pltpu.roll(kslab, HD, 1)).astype(BF16))
        vdup.append(jnp.where(first, vslab, pltpu.roll(vslab, HD, 1)).astype(BF16))
        pieces = []
        for s in range(slabs_per_kv):
            slab = h * slabs_per_kv + s
            q2 = q_ref[b * tq:(b + 1) * tq, slab * LANES:(slab + 1) * LANES]
            pieces += [jnp.where(mk, q2, jnp.zeros_like(q2)) for mk in half_masks]
        qs.append(jnp.concatenate(pieces, axis=0) * scale)
    sc = [lax.dot_general(qs[u], kdup[u], _NT, preferred_element_type=F32) for u in range(len(units))]
    ps = []
    for u, (b, h) in enumerate(units):
        blocks = []
        for r in range(group):
            sr = sc[u][r * tq:(r + 1) * tq]
            if chunk_mask:
                sr = jnp.where(valid, sr, -jnp.inf)
            sk = sinks_ref[h * group + r]
            mx = jnp.maximum(jnp.max(sr, axis=-1, keepdims=True), sk)
            e = jnp.exp(sr - mx)
            den = jnp.sum(e, axis=-1, keepdims=True) + jnp.exp(sk - mx)
            blocks.append((e / den).astype(BF16))
        ps.append(jnp.concatenate(blocks, axis=0))
    pv = [jnp.dot(ps[u], vdup[u], preferred_element_type=F32) for u in range(len(units))]
    for u, (b, h) in enumerate(units):
        for s in range(slabs_per_kv):
            slab = h * slabs_per_kv + s
            halves = [pv[u][(s * per_slab + r) * tq:(s * per_slab + r + 1) * tq] for r in range(per_slab)]
            out = halves[-1]
            for r in range(per_slab - 2, -1, -1):
                out = jnp.where(half_masks[r], halves[r], out)
            o_ref[b * tq:(b + 1) * tq, slab * LANES:(slab + 1) * LANES] = out.astype(o_ref.dtype)


def _attention(sinks, q, kprev, vprev, kprev_col, vprev_col, kv, *, row0, batch, length,
               tq, nb, chunk_mask, name):
    nq = length // tq
    kvw = N_KV * HD
    assert row0 % (nb * tq) == 0 and length % tq == 0 and batch % nb == 0 and (nb == 1 or nq == 1)
    rb0 = row0 // (nb * tq)

    def cur(b, i):
        return rb0 + b * nq + i

    def prev(b, i):
        return b * nq + jnp.maximum(i - 1, 0)

    in_specs = [
        pl.BlockSpec(memory_space=pltpu.SMEM),
        pl.BlockSpec((nb * tq, N_Q * HD), lambda b, i: (cur(b, i), 0)),
        pl.BlockSpec((nb * WINDOW, kvw), lambda b, i: (prev(b, i), kprev_col)),
        pl.BlockSpec((nb * WINDOW, kvw), lambda b, i: (prev(b, i), vprev_col)),
        pl.BlockSpec((nb * tq, kvw), lambda b, i: (cur(b, i), 0)),
        pl.BlockSpec((nb * tq, kvw), lambda b, i: (cur(b, i), 1)),
    ]
    return pl.pallas_call(
        functools.partial(_attn_kernel, tq=tq, nb=nb, chunk_mask=chunk_mask),
        grid=(batch // nb, nq),
        in_specs=in_specs,
        out_specs=pl.BlockSpec((nb * tq, N_Q * HD), lambda b, i: (b * nq + i, 0)),
        out_shape=jax.ShapeDtypeStruct((batch * length, N_Q * HD), BF16),
        compiler_params=pltpu.CompilerParams(
            dimension_semantics=("parallel", "arbitrary"), vmem_limit_bytes=VMEM_LIMIT),
        name=name,
    )(sinks, q, kprev, vprev, kv, kv)


def _rope_tables(pos):
    half = HD // 2
    inv = 1.0 / (ROPE_THETA ** (jnp.arange(half, dtype=F32) / half))
    ang = pos.astype(F32)[:, None] * jnp.tile(inv, LANES // half)[None, :]
    sign = jnp.tile(jnp.concatenate([-jnp.ones((half,), F32), jnp.ones((half,), F32)]), LANES // HD)
    return jnp.cos(ang), jnp.sin(ang) * sign[None, :]


def _gate_layout(t, hb):
    lead = t.shape[:-1]
    ng = HA // hb
    if ng == 1:
        return jnp.pad(t, [(0, 0)] * len(lead) + [(0, LANES - 2 * HA)])
    beta = t[..., :HA].reshape(lead + (ng, hb))
    dec = t[..., HA:].reshape(lead + (ng, hb))
    pad = jnp.zeros(lead + (ng, LANES - 2 * hb), t.dtype)
    return jnp.concatenate([beta, dec, pad], axis=-1).reshape(lead + (ng * LANES,))


def kernel(x_prompt, x_sample, cache_conv, state_gdn, cache_k, cache_v, attn_norm, mlp_norm, final_norm, a_w_in, a_conv_w, a_log, a_dt_bias, a_o_norm, a_w_out, kv_norm, w_kv, b_w_q, b_sinks, b_w_o, w_up, w_down):
    bp, lp, d = x_prompt.shape
    bs, ls, _ = x_sample.shape
    mp, ms = bp * lp, bs * ls
    m = mp + ms
    hb = GDN_HEADS_PER_STEP
    dqk, dvw = HA * DK, HA * DV
    conv_ch = 2 * dqk + dvw
    keep = CONV_W - 1
    assert a_w_in.shape[0] == 1 and b_w_q.shape[0] == 1 and lp % WINDOW == 0 and ls >= keep

    xp = x_prompt.reshape(mp, d)
    xs = x_sample.reshape(ms, d)
    tm = _pick_tile(m, ROW_TILES)
    tm2 = _pick_tile(math.gcd(mp, ms), ROW_TILES[1:])
    tm_p = _pick_tile(mp, PROMPT_ROW_TILES)
    tm_s = _pick_tile(ms, ROW_TILES[1:])
    tn_d = _pick_tile(d, COL_TILES[1:])

    w_in = a_w_in[0].astype(BF16)
    w_gate = _gate_layout(w_in[:, conv_ch + dvw:], hb)
    g0 = attn_norm[0].reshape(1, d)
    tn_in = _pick_tile(conv_ch + dvw, COL_TILES)
    zeros_gate = jnp.zeros((HA,), F32)
    alog_row = _gate_layout(jnp.concatenate([zeros_gate, a_log[0]]), hb).reshape(1, -1)
    dt_row = _gate_layout(jnp.concatenate([zeros_gate, a_dt_bias[0]]), hb).reshape(1, -1)
    onorm_row = a_o_norm[0].reshape(1, DV)
    c_p, c_s = min(CHUNK, lp), min(CHUNK, ls)
    streams = (("prompt", xp, tm_p, bp, lp, c_p, 1, _pick_tile(lp // c_p, (GDN_PROMPT_BLOCKS_PER_STEP, 1)),
                jnp.zeros((bp, keep, conv_ch), F32), jnp.zeros((bp, HA, DK, DV), F32)),
               ("sample", xs, tm_s, bs, ls, c_s, _pick_tile(bs, (GDN_SAMPLE_SEQS_PER_STEP, 1)) if ls == c_s else 1,
                1, cache_conv[0], state_gdn[0]))
    mixed = []
    for tag, xr, tmr, nbatch, length, c_blk, nseq, nsub, conv_prev, s0 in streams:
        qkvz = _matmul(xr, w_in, gain=g0, tm=tmr, tn=tn_in, name="gdn_in_proj_" + tag)
        gates = _matmul(xr, w_gate, gain=g0, tm=tmr, tn=w_gate.shape[1], name="gdn_gate_proj_" + tag)
        mixed.append(_gdn(qkvz, gates, alog_row, dt_row, conv_prev, a_conv_w[0], onorm_row, s0,
                          batch=nbatch, length=length, c=c_blk, hb=hb, nseq=nseq, nsub=nsub,
                          name="gdn_" + tag))
    (o_p, gdn_p, *hist_p), (o_s, gdn_s, *hist_s) = mixed
    tn_o = _pick_tile(d, COL_TILES)
    x = _matmul((o_p, o_s), a_w_out[0], res=(xp, xs), tm=tm2, tn=tn_o, weights_outer=True,
                name="gdn_out_proj")
    tf = _pick_tile(w_up.shape[2], COL_TILES[1:])
    x = _mlp(x, mlp_norm[0].reshape(1, d), w_up, w_down, 0, tm=tm, tf=tf, name="mlp0")

    cos_p, sin_p = _rope_tables(jnp.arange(lp))
    cos_s, sin_s = _rope_tables(PAST_LEN + jnp.arange(ls))
    rope = (jnp.concatenate([cos_p] * bp + [cos_s] * bs, axis=0),
            jnp.concatenate([sin_p] * bp + [sin_s] * bs, axis=0))
    kvw = N_KV * HD
    kv = _matmul(x, w_kv, gain=kv_norm.reshape(1, d), rope=rope, rope_cols=kvw,
                 tm=tm, tn=2 * kvw, name="kv_proj")
    q = _matmul(x, b_w_q[0], gain=attn_norm[1].reshape(1, d), rope=rope,
                rope_cols=N_Q * HD, out_dtype=BF16, tm=tm, tn=tn_d, name="q_proj")
    sinks = b_sinks[0].astype(F32)
    a_p = _attention(sinks, q, kv, kv, 0, 1, kv, row0=0, batch=bp, length=lp, tq=WINDOW, nb=1,
                     chunk_mask=True, name="attn_prompt")
    nb = _pick_tile(bs, (ATTN_SAMPLE_BATCHES_PER_STEP, 2, 1))
    a_s = _attention(sinks, q, cache_k.reshape(bs * WINDOW, kvw), cache_v.reshape(bs * WINDOW, kvw),
                     0, 0, kv, row0=mp, batch=bs, length=ls, tq=ls, nb=nb,
                     chunk_mask=False, name="attn_sample")
    x = _matmul((a_p, a_s), b_w_o[0], res=x, tm=tm2, tn=tn_o, weights_outer=True, name="attn_out_proj")
    y_p, y_s = _mlp_split(x, mlp_norm[1].reshape(1, d), w_up, w_down, 1, final_norm.reshape(1, d),
                          split_row=mp, tm=tm, tf=tf, name="mlp1")

    y_prompt = y_p.reshape(bp, lp, d)
    y_sample = y_s.reshape(bs, ls, d)
    conv_p = jnp.concatenate([t[:, SUBLANES - keep:] for t in hist_p], axis=-1)[None]
    conv_s = jnp.concatenate([t[:, SUBLANES - keep:] for t in hist_s], axis=-1)[None]
    kv_p = kv[:mp].reshape(bp, lp, 2 * kvw)[:, lp - WINDOW:]
    kv_s = kv[mp:].reshape(bs, ls, 2 * kvw)
    k_p = kv_p[..., :kvw].reshape(bp, WINDOW, N_KV, HD)
    v_p = kv_p[..., kvw:].reshape(bp, WINDOW, N_KV, HD)
    k_s = kv_s[..., :kvw].reshape(bs, ls, N_KV, HD)
    v_s = kv_s[..., kvw:].reshape(bs, ls, N_KV, HD)
    return (y_prompt, y_sample, conv_p, gdn_p[None], k_p, v_p, conv_s, gdn_s[None], k_s, v_s)
```

```python
import functools
import math

import jax
import jax.numpy as jnp
from jax import lax
from jax.experimental import pallas as pl
from jax.experimental.pallas import tpu as pltpu

F32 = jnp.float32
BF16 = jnp.bfloat16

EPS = 1e-6
CHUNK = 64
WINDOW = 128
PAST_LEN = 4096
ROPE_THETA = 10000.0
HA, DK, DV = 16, 128, 128
N_Q, N_KV, HD = 32, 4, 64
CONV_W = 4
LANES = 128
SUBLANES = 8
VMEM_LIMIT = 56 * 1024 * 1024
MLP_VMEM_LIMIT = 60 * 1024 * 1024
GDN_HEADS_PER_STEP = 16
GDN_SAMPLE_SEQS_PER_STEP = 4
GDN_PROMPT_BLOCKS_PER_STEP = 4
ATTN_SAMPLE_BATCHES_PER_STEP = 4
ROW_TILES = (1088, 512, 256, 128, 64, 32, 16)
PROMPT_ROW_TILES = (1024,) + ROW_TILES[1:]
COL_TILES = (1024, 512, 256, 128)


def _pick_tile(n, candidates):
    for c in candidates:
        if n % c == 0:
            return c
    return n


def _dot(a, b, dims=(((1,), (0,)), ((), ()))):
    return lax.dot_general(a.astype(BF16), b.astype(BF16), dims, preferred_element_type=F32)


_NT = (((1,), (1,)), ((), ()))
_TN = (((0,), (0,)), ((), ()))


def _rms_scale(x):
    return lax.rsqrt(jnp.mean(x * x, axis=-1, keepdims=True) + EPS)


def _silu(x):
    h = 0.5 * x
    return h + h * jnp.tanh(h)


def _rope_slab(y, cos, sin_signed):
    lane = lax.broadcasted_iota(jnp.int32, y.shape, 1)
    up = pltpu.roll(y, 32, 1)
    down = pltpu.roll(y, LANES - 32, 1)
    swapped = jnp.where((lane % HD) < HD // 2, down, up)
    return y * cos + swapped * sin_signed


def _project(lhs, w_ref, res_ref, cos_ref, sin_ref, o_ref, *, rope_slabs, n_slabs):
    if not rope_slabs:
        y = jnp.dot(lhs, w_ref[...].astype(BF16), preferred_element_type=F32)
        if res_ref is not None:
            y = y + res_ref[...]
        o_ref[...] = y.astype(o_ref.dtype)
        return
    cos = cos_ref[...]
    sin = sin_ref[...]
    n_chunk = 2 if n_slabs % 2 == 0 else 1
    for c0 in range(0, n_slabs, n_chunk):
        cols = slice(c0 * LANES, (c0 + n_chunk) * LANES)
        y = jnp.dot(lhs, w_ref[:, cols].astype(BF16), preferred_element_type=F32)
        if res_ref is not None:
            y = y + res_ref[:, cols]
        for s in range(n_chunk):
            ys = y[:, s * LANES:(s + 1) * LANES]
            if c0 + s < rope_slabs:
                ys = _rope_slab(ys, cos, sin)
            o_ref[:, (c0 + s) * LANES:(c0 + s + 1) * LANES] = ys.astype(o_ref.dtype)


def _matmul_kernel(*refs, n_x, n_res, split_tile, row_axis, has_norm, rope_slabs, n_slabs):
    it = iter(refs)
    x_refs = [next(it) for _ in range(n_x)]
    g_ref = next(it) if has_norm else None
    w_ref = next(it)
    res_refs = [next(it) for _ in range(n_res)]
    cos_ref = next(it) if rope_slabs else None
    sin_ref = next(it) if rope_slabs else None
    o_ref = next(it)
    xn_ref = next(it) if has_norm else None

    if has_norm:
        @pl.when(pl.program_id(1) == 0)
        def _():
            x = x_refs[0][...]
            xn_ref[...] = (x * _rms_scale(x) * g_ref[...]).astype(BF16)

    def emit(lhs_ref, res_ref):
        _project(lhs_ref[...], w_ref, res_ref, cos_ref, sin_ref, o_ref, rope_slabs=rope_slabs, n_slabs=n_slabs)

    if n_x == 1:
        emit(xn_ref if has_norm else x_refs[0], res_refs[0] if n_res else None)
    else:
        i = pl.program_id(row_axis)
        pl.when(i < split_tile)(functools.partial(emit, x_refs[0], res_refs[0] if n_res else None))
        pl.when(i >= split_tile)(functools.partial(emit, x_refs[1], res_refs[-1] if n_res else None))


def _matmul(xs, w, *, gain=None, res=None, rope=None, rope_cols=0, out_dtype=F32, tm, tn, name,
            weights_outer=False):
    xs = xs if isinstance(xs, (tuple, list)) else (xs,)
    ress = () if res is None else (res if isinstance(res, (tuple, list)) else (res,))
    k = xs[0].shape[1]
    m = sum(x.shape[0] for x in xs)
    n = (w.shape[1] // tn) * tn
    assert all(x.shape[0] % tm == 0 for x in xs) and w.shape[0] == k
    has_norm = gain is not None
    assert not (has_norm and len(xs) > 1)
    assert len(ress) <= len(xs) and all(r.shape[0] == x.shape[0] for r, x in zip(ress, xs) if len(ress) > 1)
    split_tile = xs[0].shape[0] // tm
    rope_slabs = 0
    if rope is not None:
        assert rope_cols == n or tn == n
        rope_slabs = min(rope_cols, tn) // LANES
    assert not (has_norm and weights_outer)

    def spec(shape, f):
        return pl.BlockSpec(shape, (lambda a, b: f(b, a)) if weights_outer else f)

    def row_specs(arrays, width, col):
        if len(arrays) == 1:
            return [spec((tm, width), lambda i, j: (i, col(j)))]
        return [spec((tm, width), lambda i, j: (jnp.minimum(i, split_tile - 1), col(j))),
                spec((tm, width), lambda i, j: (jnp.maximum(i - split_tile, 0), col(j)))]

    in_specs = row_specs(xs, k, lambda j: 0)
    args = list(xs)
    if has_norm:
        in_specs.append(spec((1, k), lambda i, j: (0, 0)))
        args.append(gain)
    in_specs.append(spec((k, tn), lambda i, j: (0, j)))
    args.append(w)
    in_specs += row_specs(ress, tn, lambda j: j) if ress else []
    args += list(ress)
    if rope_slabs:
        in_specs += [spec((tm, LANES), lambda i, j: (i, 0))] * 2
        args += list(rope)
    kern = functools.partial(_matmul_kernel, n_x=len(xs), n_res=len(ress), split_tile=split_tile,
                             row_axis=1 if weights_outer else 0, has_norm=has_norm,
                             rope_slabs=rope_slabs, n_slabs=tn // LANES)
    grid = (n // tn, m // tm) if weights_outer else (m // tm, n // tn)
    return pl.pallas_call(
        kern,
        grid=grid,
        in_specs=in_specs,
        out_specs=spec((tm, tn), lambda i, j: (i, j)),
        out_shape=jax.ShapeDtypeStruct((m, n), out_dtype),
        scratch_shapes=[pltpu.VMEM((tm, k), BF16)] if has_norm else [],
        compiler_params=pltpu.CompilerParams(
            dimension_semantics=("parallel", "arbitrary"), vmem_limit_bytes=VMEM_LIMIT),
        name=name,
    )(*args)


def _mlp_kernel(x_ref, g_ref, wu_ref, wd_ref, o_ref, hn_ref):
    f = pl.program_id(1)

    @pl.when(f == 0)
    def _():
        x = x_ref[...]
        hn_ref[...] = (x * _rms_scale(x) * g_ref[...]).astype(BF16)
        o_ref[...] = x

    u = jnp.dot(hn_ref[...], wu_ref[...].astype(BF16), preferred_element_type=F32)
    a = jnp.square(jnp.maximum(u, 0.0)).astype(BF16)
    o_ref[...] += jnp.dot(a, wd_ref[...].astype(BF16), preferred_element_type=F32)


def _mlp_in_specs(tm, d, tf, layer):
    return [
        pl.BlockSpec((tm, d), lambda i, f: (i, 0)),
        pl.BlockSpec((1, d), lambda i, f: (0, 0)),
        pl.BlockSpec((None, d, tf), lambda i, f: (layer, 0, f)),
        pl.BlockSpec((None, tf, d), lambda i, f: (layer, f, 0)),
    ]


def _mlp(x, gain, w_up, w_down, layer, *, tm, tf, name):
    m, d = x.shape
    dff = w_up.shape[2]
    assert m % tm == 0 and dff % tf == 0
    return pl.pallas_call(
        _mlp_kernel,
        grid=(m // tm, dff // tf),
        in_specs=_mlp_in_specs(tm, d, tf, layer),
        out_specs=pl.BlockSpec((tm, d), lambda i, f: (i, 0)),
        out_shape=jax.ShapeDtypeStruct((m, d), F32),
        scratch_shapes=[pltpu.VMEM((tm, d), BF16)],
        compiler_params=pltpu.CompilerParams(
            dimension_semantics=("parallel", "arbitrary"), vmem_limit_bytes=MLP_VMEM_LIMIT),
        name=name,
    )(x, gain, w_up, w_down)


def _mlp_split_kernel(x_ref, g_ref, wu_ref, wd_ref, fg_ref, yp_ref, ys_ref, acc_ref, hn_ref, sem_ref,
                      *, tm, n_tiles, split_row):
    i = pl.program_id(0)
    f = pl.program_id(1)
    slot = i % 2

    def writebacks(t):
        r0, r1 = t * tm, (t + 1) * tm
        s = t % 2
        cps = []
        if r0 < split_row:
            n = min(r1, split_row) - r0
            cps.append(pltpu.make_async_copy(acc_ref.at[s, 0:n], yp_ref.at[r0:r0 + n], sem_ref.at[s, 0]))
        if r1 > split_row:
            a = max(r0, split_row)
            cps.append(pltpu.make_async_copy(acc_ref.at[s, a - r0:tm], ys_ref.at[a - split_row:r1 - split_row],
                                             sem_ref.at[s, 1]))
        return cps

    def start_tile(t):
        for cp in writebacks(t):
            cp.start()

    def wait_tile(t):
        for cp in writebacks(t):
            cp.wait()

    @pl.when(f == 0)
    def _():
        for t in range(n_tiles - 2):
            pl.when(i == t + 2)(functools.partial(wait_tile, t))
        x = x_ref[...]
        hn_ref[...] = (x * _rms_scale(x) * g_ref[...]).astype(BF16)
        acc_ref[slot] = x

    u = jnp.dot(hn_ref[...], wu_ref[...].astype(BF16), preferred_element_type=F32)
    a = jnp.square(jnp.maximum(u, 0.0)).astype(BF16)
    acc_ref[slot] += jnp.dot(a, wd_ref[...].astype(BF16), preferred_element_type=F32)

    @pl.when(f == pl.num_programs(1) - 1)
    def _():
        y = acc_ref[slot]
        acc_ref[slot] = y * _rms_scale(y) * fg_ref[...]
        for t in range(n_tiles):
            pl.when(i == t)(functools.partial(start_tile, t))

        @pl.when(i == n_tiles - 1)
        def _():
            for t in range(max(n_tiles - 2, 0), n_tiles):
                wait_tile(t)


def _mlp_split(x, gain, w_up, w_down, layer, final_gain, *, split_row, tm, tf, name):
    m, d = x.shape
    dff = w_up.shape[2]
    assert m % tm == 0 and dff % tf == 0 and split_row % SUBLANES == 0 and 0 < split_row < m
    n_tiles = m // tm
    return pl.pallas_call(
        functools.partial(_mlp_split_kernel, tm=tm, n_tiles=n_tiles, split_row=split_row),
        grid=(n_tiles, dff // tf),
        in_specs=_mlp_in_specs(tm, d, tf, layer) + [pl.BlockSpec((1, d), lambda i, f: (0, 0))],
        out_specs=[pl.BlockSpec(memory_space=pl.ANY), pl.BlockSpec(memory_space=pl.ANY)],
        out_shape=[jax.ShapeDtypeStruct((split_row, d), F32), jax.ShapeDtypeStruct((m - split_row, d), F32)],
        scratch_shapes=[pltpu.VMEM((2, tm, d), F32), pltpu.VMEM((tm, d), BF16),
                        pltpu.SemaphoreType.DMA((2, 2))],
        compiler_params=pltpu.CompilerParams(
            dimension_semantics=("arbitrary", "arbitrary"), vmem_limit_bytes=MLP_VMEM_LIMIT),
        name=name,
    )(x, gain, w_up, w_down, final_gain)


def _gdn_kernel(xq_ref, xk_ref, xv_ref, z_ref, gates_ref, alog_ref, dt_ref,
                pq_ref, pk_ref, pv_ref, wq_ref, wk_ref, wv_ref, onorm_ref, s0_ref,
                o_ref, s_ref, hq_ref, hk_ref, hv_ref, padq_ref, padk_ref, padv_ref, *, c, hb, nseq, nsub):
    n = pl.program_id(2)
    halo = SUBLANES
    seqs = range(nseq)
    rows_seq = nsub * c

    @pl.when(n == 0)
    def _():
        for pad_ref, prev_ref in ((padq_ref, pq_ref), (padk_ref, pk_ref), (padv_ref, pv_ref)):
            for sq in seqs:
                pad_ref[sq, 0:halo, :] = jnp.zeros((halo, pad_ref.shape[2]), F32)
                pad_ref[sq, halo - (CONV_W - 1):halo, :] = prev_ref[sq]
        s_ref[...] = s0_ref[...]

    def conv(x_ref, pad_ref, w_ref, hist_ref, sq):
        pad_ref[sq, halo:halo + rows_seq, :] = x_ref[sq * rows_seq:(sq + 1) * rows_seq, :]
        xp = pad_ref[sq]
        x1 = pltpu.roll(xp, 1, 0)
        a = xp * w_ref[3:4, :] + x1 * w_ref[2:3, :]
        b = xp * w_ref[1:2, :] + x1 * w_ref[0:1, :]
        acc = (a + pltpu.roll(b, 2, 0))[halo:, :]
        tail = xp[rows_seq:rows_seq + halo, :]
        pad_ref[sq, 0:halo, :] = tail
        hist_ref[sq] = tail
        return _silu(acc)

    cq = [conv(xq_ref, padq_ref, wq_ref, hq_ref, sq) for sq in seqs]
    ck = [conv(xk_ref, padk_ref, wk_ref, hk_ref, sq) for sq in seqs]
    cv = [conv(xv_ref, padv_ref, wv_ref, hv_ref, sq) for sq in seqs]

    row = lax.broadcasted_iota(jnp.int32, (c, c), 0)
    col = lax.broadcasted_iota(jnp.int32, (c, c), 1)
    incl = row >= col
    strict = row > col
    tril = jnp.where(incl, 1.0, 0.0).astype(F32)
    r128 = lax.broadcasted_iota(jnp.int32, (LANES, LANES), 0)
    c128 = lax.broadcasted_iota(jnp.int32, (LANES, LANES), 1)
    eye128 = jnp.where(r128 == c128, 1.0, 0.0).astype(F32)
    blocks = [(sq, sb) for sq in seqs for sb in range(nsub)]
    brow = {blk: slice(blk[0] * rows_seq + blk[1] * c, blk[0] * rows_seq + (blk[1] + 1) * c) for blk in blocks}
    beta_all, gc_all, gc_t = {}, {}, {}
    for blk in blocks:
        gates = gates_ref[brow[blk], :]
        beta_all[blk] = 0.5 + 0.5 * jnp.tanh(0.5 * gates)
        ga = gates + dt_ref[...]
        softplus = jnp.maximum(ga, 0.0) + jnp.log(1.0 + jnp.exp(-jnp.abs(ga)))
        g_all = -jnp.exp(alog_ref[...]) * softplus
        gc_all[blk] = jnp.dot(tril, g_all, preferred_element_type=F32,
                              precision=lax.Precision.HIGHEST)
        gc_t[blk] = lax.dot_general(eye128, gc_all[blk], _NT, preferred_element_type=F32,
                                    precision=lax.Precision.HIGHEST)

    units = [(blk, h) for blk in blocks for h in range(hb)]
    un = range(len(units))
    sls = [slice(h * LANES, (h + 1) * LANES) for _, h in units]
    srow = [slice(blk[1] * c, (blk[1] + 1) * c) for blk, _ in units]
    q = [cq[blk[0]][srow[u], sls[u]] for u, (blk, h) in enumerate(units)]
    k = [ck[blk[0]][srow[u], sls[u]] for u, (blk, h) in enumerate(units)]
    v = [cv[blk[0]][srow[u], sls[u]] for u, (blk, h) in enumerate(units)]
    q = [x * (lax.rsqrt(jnp.sum(x * x, axis=-1, keepdims=True) + EPS) * (DK ** -0.5)) for x in q]
    k = [x * lax.rsqrt(jnp.sum(x * x, axis=-1, keepdims=True) + EPS) for x in k]
    bcol = [beta_all[blk][:, h:h + 1] for blk, h in units]
    gcol = [gc_all[blk][:, hb + h:hb + h + 1] for blk, h in units]
    grow = [gc_t[blk][hb + h:hb + h + 1, :] for blk, h in units]
    glast = [gc_all[blk][c - 1:c, hb + h:hb + h + 1] for blk, h in units]
    decay = [jnp.exp(jnp.where(incl, gcol[u] - grow[u], -jnp.inf)) for u in un]
    kb = [k[u] * bcol[u] for u in un]
    eye = jnp.where(row == col, 1.0, 0.0).astype(F32)
    qkk = [_dot(jnp.concatenate([q[u], kb[u]], axis=0), k[u], _NT) for u in un]
    qk = [qkk[u][:c] * decay[u] for u in un]
    p = [-jnp.where(strict, qkk[u][c:] * decay[u], 0.0) for u in un]
    t = [eye + x for x in p]
    levels = max(int(math.ceil(math.log2(c))) - 1, 0)
    if levels:
        p = [_dot(x, x) for x in p]
    for _ in range(1, levels):
        r = [_dot(jnp.concatenate([p[u], t[u]], axis=0), p[u]) for u in un]
        t = [t[u] + r[u][c:] for u in un]
        p = [r[u][:c] for u in un]
    if levels:
        t = [t[u] + _dot(t[u], p[u]) for u in un]
    egc = [jnp.exp(x) for x in gcol]
    rhs = [jnp.concatenate([v[u] * bcol[u], kb[u] * egc[u]], axis=-1) for u in un]
    sol = [_dot(t[u], rhs[u]) for u in un]
    wqe = [jnp.concatenate([sol[u][:, DV:], q[u] * egc[u]], axis=0) for u in un]
    kd = [k[u] * jnp.exp(glast[u] - gcol[u]) for u in un]
    elast = [jnp.exp(x) for x in glast]

    onorm = onorm_ref[...]
    s = {(sq, h): s_ref[sq, h] for sq in seqs for h in range(hb)}
    for sb in range(nsub):
        cur = [u for u in un if units[u][0][1] == sb]
        key = {u: (units[u][0][0], units[u][1]) for u in cur}
        wq = {u: _dot(wqe[u], s[key[u]]) for u in cur}
        v_new = {u: sol[u][:, :DV] - wq[u][:c] for u in cur}
        o = {u: wq[u][c:] + _dot(qk[u], v_new[u]) for u in cur}
        for u in cur:
            s[key[u]] = s[key[u]] * elast[u] + _dot(kd[u], v_new[u], _TN)
        for u in cur:
            oh = o[u] * _rms_scale(o[u]) * onorm
            rows = brow[units[u][0]]
            o_ref[rows, sls[u]] = (oh * _silu(z_ref[rows, sls[u]])).astype(o_ref.dtype)
    for (sq, h), val in s.items():
        s_ref[sq, h] = val


def _gdn(qkvz, gates, alog_row, dt_row, conv_prev, conv_w, onorm_row, s0,
         *, batch, length, c, hb, nseq, nsub, name):
    nblk = length // (nsub * c)
    ng = HA // hb
    hw = hb * LANES
    rows_step = nseq * nsub * c
    assert length % (nsub * c) == 0 and c >= SUBLANES and CONV_W == 4
    assert batch % nseq == 0 and (nseq == 1 or nblk == 1)

    def rows(b, g, n):
        return b * nblk + n

    x_spec = lambda part: pl.BlockSpec((rows_step, hw), lambda b, g, n: (rows(b, g, n), part * ng + g))
    prev_spec = lambda part: pl.BlockSpec((nseq, CONV_W - 1, hw), lambda b, g, n: (b, 0, part * ng + g))
    w_spec = lambda part: pl.BlockSpec((CONV_W, hw), lambda b, g, n: (0, part * ng + g))
    gate_row_spec = pl.BlockSpec((1, LANES), lambda b, g, n: (0, g))
    hist_spec = pl.BlockSpec((nseq, SUBLANES, hw), lambda b, g, n: (b, 0, g))
    hist_shape = jax.ShapeDtypeStruct((batch, SUBLANES, HA * LANES), F32)
    state_spec = pl.BlockSpec((nseq, hb, DK, DV), lambda b, g, n: (b, g, 0, 0))
    in_specs = [
        x_spec(0), x_spec(1), x_spec(2), x_spec(3),
        pl.BlockSpec((rows_step, LANES), lambda b, g, n: (rows(b, g, n), g)),
        gate_row_spec, gate_row_spec,
        prev_spec(0), prev_spec(1), prev_spec(2),
        w_spec(0), w_spec(1), w_spec(2),
        pl.BlockSpec((1, LANES), lambda b, g, n: (0, 0)),
        state_spec,
    ]
    out_specs = [
        pl.BlockSpec((rows_step, hw), lambda b, g, n: (b * nblk + n, g)),
        state_spec,
        hist_spec, hist_spec, hist_spec,
    ]
    return pl.pallas_call(
        functools.partial(_gdn_kernel, c=c, hb=hb, nseq=nseq, nsub=nsub),
        grid=(batch // nseq, ng, nblk),
        in_specs=in_specs,
        out_specs=out_specs,
        out_shape=[jax.ShapeDtypeStruct((batch * length, HA * DV), BF16),
                   jax.ShapeDtypeStruct((batch, HA, DK, DV), F32),
                   hist_shape, hist_shape, hist_shape],
        scratch_shapes=[pltpu.VMEM((nseq, nsub * c + SUBLANES, hw), F32)] * 3,
        compiler_params=pltpu.CompilerParams(
            dimension_semantics=("parallel", "parallel", "arbitrary"),
            vmem_limit_bytes=VMEM_LIMIT),
        name=name,
    )(qkvz, qkvz, qkvz, qkvz, gates, alog_row, dt_row,
      conv_prev, conv_prev, conv_prev, conv_w, conv_w, conv_w, onorm_row, s0)


def _attn_kernel(sinks_ref, q_ref, kp_ref, vp_ref, kc_ref, vc_ref, o_ref, *, tq, nb, chunk_mask):
    i = pl.program_id(1)
    nk = WINDOW + tq
    lane_k = lax.broadcasted_iota(jnp.int32, (nk, LANES), 1)
    lane_q = lax.broadcasted_iota(jnp.int32, (tq, LANES), 1)
    if chunk_mask:
        qc = lax.broadcasted_iota(jnp.int32, (tq, nk), 0) // CHUNK
        kc = lax.broadcasted_iota(jnp.int32, (tq, nk), 1) // CHUNK
        w_ch = WINDOW // CHUNK
        first_kc = jnp.where(i > 0, 0, w_ch)
        valid = (kc >= jnp.maximum(qc, first_kc)) & (kc <= qc + w_ch)
    scale = HD ** -0.5
    assert math.log2(HD) % 2 == 0
    per_slab = LANES // HD
    group = N_Q // N_KV
    slabs_per_kv = group // per_slab
    half_masks = [(lane_q >= r * HD) & (lane_q < (r + 1) * HD) for r in range(per_slab)]

    units = [(b, h) for b in range(nb) for h in range(N_KV)]
    kdup, vdup, qs = [], [], []
    for b, h in units:
        ksl = slice((h // per_slab) * LANES, (h // per_slab + 1) * LANES)
        kslab = jnp.concatenate([kp_ref[b * WINDOW:(b + 1) * WINDOW, ksl],
                                 kc_ref[b * tq:(b + 1) * tq, ksl]], axis=0)
        vslab = jnp.concatenate([vp_ref[b * WINDOW:(b + 1) * WINDOW, ksl],
                                 vc_ref[b * tq:(b + 1) * tq, ksl]], axis=0)
        first = (lane_k < HD) == (h % per_slab == 0)
        kdup.append(jnp.where(first, kslab, pltpu.roll(kslab, HD, 1)).astype(BF16))
        vdup.append(jnp.where(first, vslab, pltpu.roll(vslab, HD, 1)).astype(BF16))
        pieces = []
        for s in range(slabs_per_kv):
            slab = h * slabs_per_kv + s
            q2 = q_ref[b * tq:(b + 1) * tq, slab * LANES:(slab + 1) * LANES]
            pieces += [jnp.where(mk, q2, jnp.zeros_like(q2)) for mk in half_masks]
        qs.append(jnp.concatenate(pieces, axis=0) * scale)
    sc = [lax.dot_general(qs[u], kdup[u], _NT, preferred_element_type=F32) for u in range(len(units))]
    ps = []
    for u, (b, h) in enumerate(units):
        blocks = []
        for r in range(group):
            sr = sc[u][r * tq:(r + 1) * tq]
            if chunk_mask:
                sr = jnp.where(valid, sr, -jnp.inf)
            sk = sinks_ref[h * group + r]
            mx = jnp.maximum(jnp.max(sr, axis=-1, keepdims=True), sk)
            e = jnp.exp(sr - mx)
            den = jnp.sum(e, axis=-1, keepdims=True) + jnp.exp(sk - mx)
            blocks.append((e / den).astype(BF16))
        ps.append(jnp.concatenate(blocks, axis=0))
    pv = [jnp.dot(ps[u], vdup[u], preferred_element_type=F32) for u in range(len(units))]
    for u, (b, h) in enumerate(units):
        for s in range(slabs_per_kv):
            slab = h * slabs_per_kv + s
            halves = [pv[u][(s * per_slab + r) * tq:(s * per_slab + r + 1) * tq] for r in range(per_slab)]
            out = halves[-1]
            for r in range(per_slab - 2, -1, -1):
                out = jnp.where(half_masks[r], halves[r], out)
            o_ref[b * tq:(b + 1) * tq, slab * LANES:(slab + 1) * LANES] = out.astype(o_ref.dtype)


def _attention(sinks, q, kprev, vprev, kprev_col, vprev_col, kv, *, row0, batch, length,
               tq, nb, chunk_mask, name):
    nq = length // tq
    kvw = N_KV * HD
    assert row0 % (nb * tq) == 0 and length % tq == 0 and batch % nb == 0 and (nb == 1 or nq == 1)
    rb0 = row0 // (nb * tq)

    def cur(b, i):
        return rb0 + b * nq + i

    def prev(b, i):
        return b * nq + jnp.maximum(i - 1, 0)

    in_specs = [
        pl.BlockSpec(memory_space=pltpu.SMEM),
        pl.BlockSpec((nb * tq, N_Q * HD), lambda b, i: (cur(b, i), 0)),
        pl.BlockSpec((nb * WINDOW, kvw), lambda b, i: (prev(b, i), kprev_col)),
        pl.BlockSpec((nb * WINDOW, kvw), lambda b, i: (prev(b, i), vprev_col)),
        pl.BlockSpec((nb * tq, kvw), lambda b, i: (cur(b, i), 0)),
        pl.BlockSpec((nb * tq, kvw), lambda b, i: (cur(b, i), 1)),
    ]
    return pl.pallas_call(
        functools.partial(_attn_kernel, tq=tq, nb=nb, chunk_mask=chunk_mask),
        grid=(batch // nb, nq),
        in_specs=in_specs,
        out_specs=pl.BlockSpec((nb * tq, N_Q * HD), lambda b, i: (b * nq + i, 0)),
        out_shape=jax.ShapeDtypeStruct((batch * length, N_Q * HD), BF16),
        compiler_params=pltpu.CompilerParams(
            dimension_semantics=("parallel", "arbitrary"), vmem_limit_bytes=VMEM_LIMIT),
        name=name,
    )(sinks, q, kprev, vprev, kv, kv)


def _rope_tables(pos):
    half = HD // 2
    inv = 1.0 / (ROPE_THETA ** (jnp.arange(half, dtype=F32) / half))
    ang = pos.astype(F32)[:, None] * jnp.tile(inv, LANES // half)[None, :]
    sign = jnp.tile(jnp.concatenate([-jnp.ones((half,), F32), jnp.ones((half,), F32)]), LANES // HD)
    return jnp.cos(ang), jnp.sin(ang) * sign[None, :]


def _gate_layout(t, hb):
    lead = t.shape[:-1]
    ng = HA // hb
    if ng == 1:
        return jnp.pad(t, [(0, 0)] * len(lead) + [(0, LANES - 2 * HA)])
    beta = t[..., :HA].reshape(lead + (ng, hb))
    dec = t[..., HA:].reshape(lead + (ng, hb))
    pad = jnp.zeros(lead + (ng, LANES - 2 * hb), t.dtype)
    return jnp.concatenate([beta, dec, pad], axis=-1).reshape(lead + (ng * LANES,))


def kernel(x_prompt, x_sample, cache_conv, state_gdn, cache_k, cache_v, attn_norm, mlp_norm, final_norm, a_w_in, a_conv_w, a_log, a_dt_bias, a_o_norm, a_w_out, kv_norm, w_kv, b_w_q, b_sinks, b_w_o, w_up, w_down):
    bp, lp, d = x_prompt.shape
    bs, ls, _ = x_sample.shape
    mp, ms = bp * lp, bs * ls
    m = mp + ms
    hb = GDN_HEADS_PER_STEP
    dqk, dvw = HA * DK, HA * DV
    conv_ch = 2 * dqk + dvw
    keep = CONV_W - 1
    assert a_w_in.shape[0] == 1 and b_w_q.shape[0] == 1 and lp % WINDOW == 0 and ls >= keep

    xp = x_prompt.reshape(mp, d)
    xs = x_sample.reshape(ms, d)
    tm = _pick_tile(m, ROW_TILES)
    tm2 = _pick_tile(math.gcd(mp, ms), ROW_TILES[1:])
    tm_p = _pick_tile(mp, PROMPT_ROW_TILES)
    tm_s = _pick_tile(ms, ROW_TILES[1:])

    w_in = a_w_in[0].astype(BF16)
    w_gate = _gate_layout(w_in[:, conv_ch + dvw:], hb)
    g0 = attn_norm[0].reshape(1, d)
    tn_in = _pick_tile(conv_ch + dvw, (2048,) + COL_TILES)
    zeros_gate = jnp.zeros((HA,), F32)
    alog_row = _gate_layout(jnp.concatenate([zeros_gate, a_log[0]]), hb).reshape(1, -1)
    dt_row = _gate_layout(jnp.concatenate([zeros_gate, a_dt_bias[0]]), hb).reshape(1, -1)
    onorm_row = a_o_norm[0].reshape(1, DV)
    c_p, c_s = min(CHUNK, lp), min(CHUNK, ls)
    streams = (("prompt", xp, tm_p, bp, lp, c_p, 1, _pick_tile(lp // c_p, (GDN_PROMPT_BLOCKS_PER_STEP, 1)),
                jnp.zeros((bp, keep, conv_ch), F32), jnp.zeros((bp, HA, DK, DV), F32)),
               ("sample", xs, tm_s, bs, ls, c_s, _pick_tile(bs, (GDN_SAMPLE_SEQS_PER_STEP, 1)) if ls == c_s else 1,
                1, cache_conv[0], state_gdn[0]))
    mixed = []
    for tag, xr, tmr, nbatch, length, c_blk, nseq, nsub, conv_prev, s0 in streams:
        qkvz = _matmul(xr, w_in, gain=g0, tm=tmr, tn=tn_in, name="gdn_in_proj_" + tag)
        gates = _matmul(xr, w_gate, gain=g0, tm=tmr, tn=w_gate.shape[1], name="gdn_gate_proj_" + tag)
        mixed.append(_gdn(qkvz, gates, alog_row, dt_row, conv_prev, a_conv_w[0], onorm_row, s0,
                          batch=nbatch, length=length, c=c_blk, hb=hb, nseq=nseq, nsub=nsub,
                          name="gdn_" + tag))
    (o_p, gdn_p, *hist_p), (o_s, gdn_s, *hist_s) = mixed
    tn_o = _pick_tile(d, COL_TILES)
    x = _matmul((o_p, o_s), a_w_out[0], res=(xp, xs), tm=tm2, tn=tn_o, weights_outer=True,
                name="gdn_out_proj")
    tf = _pick_tile(w_up.shape[2], COL_TILES[1:])
    x = _mlp(x, mlp_norm[0].reshape(1, d), w_up, w_down, 0, tm=tm, tf=tf, name="mlp0")

    cos_p, sin_p = _rope_tables(jnp.arange(lp))
    cos_s, sin_s = _rope_tables(PAST_LEN + jnp.arange(ls))
    rope = (jnp.concatenate([cos_p] * bp + [cos_s] * bs, axis=0),
            jnp.concatenate([sin_p] * bp + [sin_s] * bs, axis=0))
    kvw = N_KV * HD
    kv = _matmul(x, w_kv, gain=kv_norm.reshape(1, d), rope=rope, rope_cols=kvw,
                 tm=tm, tn=2 * kvw, name="kv_proj")
    q = _matmul(x, b_w_q[0], gain=attn_norm[1].reshape(1, d), rope=rope,
                rope_cols=N_Q * HD, out_dtype=BF16, tm=tm, tn=tn_o, name="q_proj")
    sinks = b_sinks[0].astype(F32)
    a_p = _attention(sinks, q, kv, kv, 0, 1, kv, row0=0, batch=bp, length=lp, tq=WINDOW, nb=1,
                     chunk_mask=True, name="attn_prompt")
    nb = _pick_tile(bs, (ATTN_SAMPLE_BATCHES_PER_STEP, 2, 1))
    a_s = _attention(sinks, q, cache_k.reshape(bs * WINDOW, kvw), cache_v.reshape(bs * WINDOW, kvw),
                     0, 0, kv, row0=mp, batch=bs, length=ls, tq=ls, nb=nb,
                     chunk_mask=False, name="attn_sample")
    x = _matmul((a_p, a_s), b_w_o[0], res=x, tm=tm2, tn=tn_o, weights_outer=True, name="attn_out_proj")
    y_p, y_s = _mlp_split(x, mlp_norm[1].reshape(1, d), w_up, w_down, 1, final_norm.reshape(1, d),
                          split_row=mp, tm=tm, tf=tf, name="mlp1")

    y_prompt = y_p.reshape(bp, lp, d)
    y_sample = y_s.reshape(bs, ls, d)
    conv_p = jnp.concatenate([t[:, SUBLANES - keep:] for t in hist_p], axis=-1)[None]
    conv_s = jnp.concatenate([t[:, SUBLANES - keep:] for t in hist_s], axis=-1)[None]
    kv_p = kv[:mp].reshape(bp, lp, 2 * kvw)[:, lp - WINDOW:]
    kv_s = kv[mp:].reshape(bs, ls, 2 * kvw)
    k_p = kv_p[..., :kvw].reshape(bp, WINDOW, N_KV, HD)
    v_p = kv_p[..., kvw:].reshape(bp, WINDOW, N_KV, HD)
    k_s = kv_s[..., :kvw].reshape(bs, ls, N_KV, HD)
    v_s = kv_s[..., kvw:].reshape(bs, ls, N_KV, HD)
    return (y_prompt, y_sample, conv_p, gdn_p[None], k_p, v_p, conv_s, gdn_s[None], k_s, v_s)
```

```python
import functools
import math

import jax
import jax.numpy as jnp
from jax import lax
from jax.experimental import pallas as pl
from jax.experimental.pallas import tpu as pltpu

F32 = jnp.float32
BF16 = jnp.bfloat16

EPS = 1e-6
CHUNK = 64
WINDOW = 128
PAST_LEN = 4096
ROPE_THETA = 10000.0
HA, DK, DV = 16, 128, 128
N_Q, N_KV, HD = 32, 4, 64
CONV_W = 4
LANES = 128
SUBLANES = 8
VMEM_LIMIT = 56 * 1024 * 1024
MLP_VMEM_LIMIT = 60 * 1024 * 1024
GDN_HEADS_PER_STEP = 16
GDN_SAMPLE_SEQS_PER_STEP = 4
GDN_PROMPT_BLOCKS_PER_STEP = 4
ATTN_SAMPLE_BATCHES_PER_STEP = 4
ROW_TILES = (1088, 512, 256, 128, 64, 32, 16)
PROMPT_ROW_TILES = (1024,) + ROW_TILES[1:]
COL_TILES = (1024, 512, 256, 128)
IN_PROJ_COL_TILES = (2048,) + COL_TILES


def _pick_tile(n, candidates):
    for c in candidates:
        if n % c == 0:
            return c
    return n


def _dot(a, b, dims=(((1,), (0,)), ((), ()))):
    return lax.dot_general(a.astype(BF16), b.astype(BF16), dims, preferred_element_type=F32)


_NT = (((1,), (1,)), ((), ()))
_TN = (((0,), (0,)), ((), ()))


def _rms_scale(x):
    return lax.rsqrt(jnp.mean(x * x, axis=-1, keepdims=True) + EPS)


def _silu(x):
    h = 0.5 * x
    return h + h * jnp.tanh(h)


def _rope_slab(y, cos, sin_signed):
    lane = lax.broadcasted_iota(jnp.int32, y.shape, 1)
    up = pltpu.roll(y, 32, 1)
    down = pltpu.roll(y, LANES - 32, 1)
    swapped = jnp.where((lane % HD) < HD // 2, down, up)
    return y * cos + swapped * sin_signed


def _project(lhs, w_ref, res_ref, cos_ref, sin_ref, o_ref, *, rope_slabs, n_slabs):
    if not rope_slabs:
        y = jnp.dot(lhs, w_ref[...].astype(BF16), preferred_element_type=F32)
        if res_ref is not None:
            y = y + res_ref[...]
        o_ref[...] = y.astype(o_ref.dtype)
        return
    cos = cos_ref[...]
    sin = sin_ref[...]
    n_chunk = 2 if n_slabs % 2 == 0 else 1
    for c0 in range(0, n_slabs, n_chunk):
        cols = slice(c0 * LANES, (c0 + n_chunk) * LANES)
        y = jnp.dot(lhs, w_ref[:, cols].astype(BF16), preferred_element_type=F32)
        if res_ref is not None:
            y = y + res_ref[:, cols]
        for s in range(n_chunk):
            ys = y[:, s * LANES:(s + 1) * LANES]
            if c0 + s < rope_slabs:
                ys = _rope_slab(ys, cos, sin)
            o_ref[:, (c0 + s) * LANES:(c0 + s + 1) * LANES] = ys.astype(o_ref.dtype)


def _matmul_kernel(*refs, n_x, n_res, split_tile, row_axis, has_norm, rope_slabs, n_slabs):
    it = iter(refs)
    x_refs = [next(it) for _ in range(n_x)]
    g_ref = next(it) if has_norm else None
    w_ref = next(it)
    res_refs = [next(it) for _ in range(n_res)]
    cos_ref = next(it) if rope_slabs else None
    sin_ref = next(it) if rope_slabs else None
    o_ref = next(it)
    xn_ref = next(it) if has_norm else None

    if has_norm:
        @pl.when(pl.program_id(1) == 0)
        def _():
            x = x_refs[0][...]
            xn_ref[...] = (x * _rms_scale(x) * g_ref[...]).astype(BF16)

    def emit(lhs_ref, res_ref):
        _project(lhs_ref[...], w_ref, res_ref, cos_ref, sin_ref, o_ref, rope_slabs=rope_slabs, n_slabs=n_slabs)

    if n_x == 1:
        emit(xn_ref if has_norm else x_refs[0], res_refs[0] if n_res else None)
    else:
        i = pl.program_id(row_axis)
        pl.when(i < split_tile)(functools.partial(emit, x_refs[0], res_refs[0] if n_res else None))
        pl.when(i >= split_tile)(functools.partial(emit, x_refs[1], res_refs[-1] if n_res else None))


def _matmul(xs, w, *, gain=None, res=None, rope=None, rope_cols=0, out_dtype=F32, tm, tn, name,
            weights_outer=False):
    xs = xs if isinstance(xs, (tuple, list)) else (xs,)
    ress = () if res is None else (res if isinstance(res, (tuple, list)) else (res,))
    k = xs[0].shape[1]
    m = sum(x.shape[0] for x in xs)
    n = (w.shape[1] // tn) * tn
    assert all(x.shape[0] % tm == 0 for x in xs) and w.shape[0] == k
    has_norm = gain is not None
    assert not (has_norm and len(xs) > 1)
    assert len(ress) <= len(xs) and all(r.shape[0] == x.shape[0] for r, x in zip(ress, xs) if len(ress) > 1)
    split_tile = xs[0].shape[0] // tm
    rope_slabs = 0
    if rope is not None:
        assert rope_cols == n or tn == n
        rope_slabs = min(rope_cols, tn) // LANES
    assert not (has_norm and weights_outer)

    def spec(shape, f):
        return pl.BlockSpec(shape, (lambda a, b: f(b, a)) if weights_outer else f)

    def row_specs(arrays, width, col):
        if len(arrays) == 1:
            return [spec((tm, width), lambda i, j: (i, col(j)))]
        return [spec((tm, width), lambda i, j: (jnp.minimum(i, split_tile - 1), col(j))),
                spec((tm, width), lambda i, j: (jnp.maximum(i - split_tile, 0), col(j)))]

    in_specs = row_specs(xs, k, lambda j: 0)
    args = list(xs)
    if has_norm:
        in_specs.append(spec((1, k), lambda i, j: (0, 0)))
        args.append(gain)
    in_specs.append(spec((k, tn), lambda i, j: (0, j)))
    args.append(w)
    in_specs += row_specs(ress, tn, lambda j: j) if ress else []
    args += list(ress)
    if rope_slabs:
        in_specs += [spec((tm, LANES), lambda i, j: (i, 0))] * 2
        args += list(rope)
    kern = functools.partial(_matmul_kernel, n_x=len(xs), n_res=len(ress), split_tile=split_tile,
                             row_axis=1 if weights_outer else 0, has_norm=has_norm,
                             rope_slabs=rope_slabs, n_slabs=tn // LANES)
    grid = (n // tn, m // tm) if weights_outer else (m // tm, n // tn)
    return pl.pallas_call(
        kern,
        grid=grid,
        in_specs=in_specs,
        out_specs=spec((tm, tn), lambda i, j: (i, j)),
        out_shape=jax.ShapeDtypeStruct((m, n), out_dtype),
        scratch_shapes=[pltpu.VMEM((tm, k), BF16)] if has_norm else [],
        compiler_params=pltpu.CompilerParams(
            dimension_semantics=("parallel", "arbitrary"), vmem_limit_bytes=VMEM_LIMIT),
        name=name,
    )(*args)


def _mlp_kernel(x_ref, g_ref, wu_ref, wd_ref, o_ref, hn_ref):
    f = pl.program_id(1)

    @pl.when(f == 0)
    def _():
        x = x_ref[...]
        hn_ref[...] = (x * _rms_scale(x) * g_ref[...]).astype(BF16)
        o_ref[...] = x

    u = jnp.dot(hn_ref[...], wu_ref[...].astype(BF16), preferred_element_type=F32)
    a = jnp.square(jnp.maximum(u, 0.0)).astype(BF16)
    o_ref[...] += jnp.dot(a, wd_ref[...].astype(BF16), preferred_element_type=F32)


def _mlp_in_specs(tm, d, tf, layer):
    return [
        pl.BlockSpec((tm, d), lambda i, f: (i, 0)),
        pl.BlockSpec((1, d), lambda i, f: (0, 0)),
        pl.BlockSpec((None, d, tf), lambda i, f: (layer, 0, f)),
        pl.BlockSpec((None, tf, d), lambda i, f: (layer, f, 0)),
    ]


def _mlp(x, gain, w_up, w_down, layer, *, tm, tf, name):
    m, d = x.shape
    dff = w_up.shape[2]
    assert m % tm == 0 and dff % tf == 0
    return pl.pallas_call(
        _mlp_kernel,
        grid=(m // tm, dff // tf),
        in_specs=_mlp_in_specs(tm, d, tf, layer),
        out_specs=pl.BlockSpec((tm, d), lambda i, f: (i, 0)),
        out_shape=jax.ShapeDtypeStruct((m, d), F32),
        scratch_shapes=[pltpu.VMEM((tm, d), BF16)],
        compiler_params=pltpu.CompilerParams(
            dimension_semantics=("parallel", "arbitrary"), vmem_limit_bytes=MLP_VMEM_LIMIT),
        name=name,
    )(x, gain, w_up, w_down)


def _mlp_split_kernel(x_ref, g_ref, wu_ref, wd_ref, fg_ref, yp_ref, ys_ref, acc_ref, hn_ref, sem_ref,
                      *, tm, n_tiles, split_row):
    i = pl.program_id(0)
    f = pl.program_id(1)
    slot = i % 2

    def writebacks(t):
        r0, r1 = t * tm, (t + 1) * tm
        s = t % 2
        cps = []
        if r0 < split_row:
            n = min(r1, split_row) - r0
            cps.append(pltpu.make_async_copy(acc_ref.at[s, 0:n], yp_ref.at[r0:r0 + n], sem_ref.at[s, 0]))
        if r1 > split_row:
            a = max(r0, split_row)
            cps.append(pltpu.make_async_copy(acc_ref.at[s, a - r0:tm], ys_ref.at[a - split_row:r1 - split_row],
                                             sem_ref.at[s, 1]))
        return cps

    def start_tile(t):
        for cp in writebacks(t):
            cp.start()

    def wait_tile(t):
        for cp in writebacks(t):
            cp.wait()

    @pl.when(f == 0)
    def _():
        for t in range(n_tiles - 2):
            pl.when(i == t + 2)(functools.partial(wait_tile, t))
        x = x_ref[...]
        hn_ref[...] = (x * _rms_scale(x) * g_ref[...]).astype(BF16)
        acc_ref[slot] = x

    u = jnp.dot(hn_ref[...], wu_ref[...].astype(BF16), preferred_element_type=F32)
    a = jnp.square(jnp.maximum(u, 0.0)).astype(BF16)
    acc_ref[slot] += jnp.dot(a, wd_ref[...].astype(BF16), preferred_element_type=F32)

    @pl.when(f == pl.num_programs(1) - 1)
    def _():
        y = acc_ref[slot]
        acc_ref[slot] = y * _rms_scale(y) * fg_ref[...]
        for t in range(n_tiles):
            pl.when(i == t)(functools.partial(start_tile, t))

        @pl.when(i == n_tiles - 1)
        def _():
            for t in range(max(n_tiles - 2, 0), n_tiles):
                wait_tile(t)


def _mlp_split(x, gain, w_up, w_down, layer, final_gain, *, split_row, tm, tf, name):
    m, d = x.shape
    dff = w_up.shape[2]
    assert m % tm == 0 and dff % tf == 0 and split_row % SUBLANES == 0 and 0 < split_row < m
    n_tiles = m // tm
    return pl.pallas_call(
        functools.partial(_mlp_split_kernel, tm=tm, n_tiles=n_tiles, split_row=split_row),
        grid=(n_tiles, dff // tf),
        in_specs=_mlp_in_specs(tm, d, tf, layer) + [pl.BlockSpec((1, d), lambda i, f: (0, 0))],
        out_specs=[pl.BlockSpec(memory_space=pl.ANY), pl.BlockSpec(memory_space=pl.ANY)],
        out_shape=[jax.ShapeDtypeStruct((split_row, d), F32), jax.ShapeDtypeStruct((m - split_row, d), F32)],
        scratch_shapes=[pltpu.VMEM((2, tm, d), F32), pltpu.VMEM((tm, d), BF16),
                        pltpu.SemaphoreType.DMA((2, 2))],
        compiler_params=pltpu.CompilerParams(
            dimension_semantics=("arbitrary", "arbitrary"), vmem_limit_bytes=MLP_VMEM_LIMIT),
        name=name,
    )(x, gain, w_up, w_down, final_gain)


def _gdn_kernel(xq_ref, xk_ref, xv_ref, z_ref, gates_ref, alog_ref, dt_ref,
                pq_ref, pk_ref, pv_ref, wq_ref, wk_ref, wv_ref, onorm_ref, s0_ref,
                o_ref, s_ref, hq_ref, hk_ref, hv_ref, padq_ref, padk_ref, padv_ref, *, c, hb, nseq, nsub):
    n = pl.program_id(2)
    halo = SUBLANES
    seqs = range(nseq)
    rows_seq = nsub * c

    @pl.when(n == 0)
    def _():
        for pad_ref, prev_ref in ((padq_ref, pq_ref), (padk_ref, pk_ref), (padv_ref, pv_ref)):
            for sq in seqs:
                pad_ref[sq, 0:halo, :] = jnp.zeros((halo, pad_ref.shape[2]), F32)
                pad_ref[sq, halo - (CONV_W - 1):halo, :] = prev_ref[sq]
        s_ref[...] = s0_ref[...]

    def conv(x_ref, pad_ref, w_ref, hist_ref, sq):
        pad_ref[sq, halo:halo + rows_seq, :] = x_ref[sq * rows_seq:(sq + 1) * rows_seq, :]
        xp = pad_ref[sq]
        x1 = pltpu.roll(xp, 1, 0)
        a = xp * w_ref[3:4, :] + x1 * w_ref[2:3, :]
        b = xp * w_ref[1:2, :] + x1 * w_ref[0:1, :]
        acc = (a + pltpu.roll(b, 2, 0))[halo:, :]
        tail = xp[rows_seq:rows_seq + halo, :]
        pad_ref[sq, 0:halo, :] = tail
        hist_ref[sq] = tail
        return _silu(acc)

    cq = [conv(xq_ref, padq_ref, wq_ref, hq_ref, sq) for sq in seqs]
    ck = [conv(xk_ref, padk_ref, wk_ref, hk_ref, sq) for sq in seqs]
    cv = [conv(xv_ref, padv_ref, wv_ref, hv_ref, sq) for sq in seqs]

    row = lax.broadcasted_iota(jnp.int32, (c, c), 0)
    col = lax.broadcasted_iota(jnp.int32, (c, c), 1)
    incl = row >= col
    strict = row > col
    tril = jnp.where(incl, 1.0, 0.0).astype(F32)
    r128 = lax.broadcasted_iota(jnp.int32, (LANES, LANES), 0)
    c128 = lax.broadcasted_iota(jnp.int32, (LANES, LANES), 1)
    eye128 = jnp.where(r128 == c128, 1.0, 0.0).astype(F32)
    blocks = [(sq, sb) for sq in seqs for sb in range(nsub)]
    brow = {blk: slice(blk[0] * rows_seq + blk[1] * c, blk[0] * rows_seq + (blk[1] + 1) * c) for blk in blocks}
    beta_all, gc_all, gc_t = {}, {}, {}
    for blk in blocks:
        gates = gates_ref[brow[blk], :]
        beta_all[blk] = 0.5 + 0.5 * jnp.tanh(0.5 * gates)
        ga = gates + dt_ref[...]
        softplus = jnp.maximum(ga, 0.0) + jnp.log(1.0 + jnp.exp(-jnp.abs(ga)))
        g_all = -jnp.exp(alog_ref[...]) * softplus
        gc_all[blk] = jnp.dot(tril, g_all, preferred_element_type=F32,
                              precision=lax.Precision.HIGHEST)
        gc_t[blk] = lax.dot_general(eye128, gc_all[blk], _NT, preferred_element_type=F32,
                                    precision=lax.Precision.HIGHEST)

    units = [(blk, h) for blk in blocks for h in range(hb)]
    un = range(len(units))
    sls = [slice(h * LANES, (h + 1) * LANES) for _, h in units]
    srow = [slice(blk[1] * c, (blk[1] + 1) * c) for blk, _ in units]
    q = [cq[blk[0]][srow[u], sls[u]] for u, (blk, h) in enumerate(units)]
    k = [ck[blk[0]][srow[u], sls[u]] for u, (blk, h) in enumerate(units)]
    v = [cv[blk[0]][srow[u], sls[u]] for u, (blk, h) in enumerate(units)]
    q = [x * (lax.rsqrt(jnp.sum(x * x, axis=-1, keepdims=True) + EPS) * (DK ** -0.5)) for x in q]
    k = [x * lax.rsqrt(jnp.sum(x * x, axis=-1, keepdims=True) + EPS) for x in k]
    bcol = [beta_all[blk][:, h:h + 1] for blk, h in units]
    gcol = [gc_all[blk][:, hb + h:hb + h + 1] for blk, h in units]
    grow = [gc_t[blk][hb + h:hb + h + 1, :] for blk, h in units]
    glast = [gc_all[blk][c - 1:c, hb + h:hb + h + 1] for blk, h in units]
    decay = [jnp.exp(jnp.where(incl, gcol[u] - grow[u], -jnp.inf)) for u in un]
    kb = [k[u] * bcol[u] for u in un]
    eye = jnp.where(row == col, 1.0, 0.0).astype(F32)
    qkk = [_dot(jnp.concatenate([q[u], kb[u]], axis=0), k[u], _NT) for u in un]
    qk = [qkk[u][:c] * decay[u] for u in un]
    p = [-jnp.where(strict, qkk[u][c:] * decay[u], 0.0) for u in un]
    t = [eye + x for x in p]
    levels = max(int(math.ceil(math.log2(c))) - 1, 0)
    if levels:
        p = [_dot(x, x) for x in p]
    for _ in range(1, levels):
        r = [_dot(jnp.concatenate([p[u], t[u]], axis=0), p[u]) for u in un]
        t = [t[u] + r[u][c:] for u in un]
        p = [r[u][:c] for u in un]
    if levels:
        t = [t[u] + _dot(t[u], p[u]) for u in un]
    egc = [jnp.exp(x) for x in gcol]
    rhs = [jnp.concatenate([v[u] * bcol[u], kb[u] * egc[u]], axis=-1) for u in un]
    sol = [_dot(t[u], rhs[u]) for u in un]
    wqe = [jnp.concatenate([sol[u][:, DV:], q[u] * egc[u]], axis=0) for u in un]
    kd = [k[u] * jnp.exp(glast[u] - gcol[u]) for u in un]
    elast = [jnp.exp(x) for x in glast]

    onorm = onorm_ref[...]
    s = {(sq, h): s_ref[sq, h] for sq in seqs for h in range(hb)}
    for sb in range(nsub):
        cur = [u for u in un if units[u][0][1] == sb]
        key = {u: (units[u][0][0], units[u][1]) for u in cur}
        wq = {u: _dot(wqe[u], s[key[u]]) for u in cur}
        v_new = {u: sol[u][:, :DV] - wq[u][:c] for u in cur}
        o = {u: wq[u][c:] + _dot(qk[u], v_new[u]) for u in cur}
        for u in cur:
            s[key[u]] = s[key[u]] * elast[u] + _dot(kd[u], v_new[u], _TN)
        for u in cur:
            oh = o[u] * _rms_scale(o[u]) * onorm
            rows = brow[units[u][0]]
            o_ref[rows, sls[u]] = (oh * _silu(z_ref[rows, sls[u]])).astype(o_ref.dtype)
    for (sq, h), val in s.items():
        s_ref[sq, h] = val


def _gdn(qkvz, gates, alog_row, dt_row, conv_prev, conv_w, onorm_row, s0,
         *, batch, length, c, hb, nseq, nsub, name):
    nblk = length // (nsub * c)
    ng = HA // hb
    hw = hb * LANES
    rows_step = nseq * nsub * c
    assert length % (nsub * c) == 0 and c >= SUBLANES and CONV_W == 4
    assert batch % nseq == 0 and (nseq == 1 or nblk == 1)

    def rows(b, g, n):
        return b * nblk + n

    x_spec = lambda part: pl.BlockSpec((rows_step, hw), lambda b, g, n: (rows(b, g, n), part * ng + g))
    prev_spec = lambda part: pl.BlockSpec((nseq, CONV_W - 1, hw), lambda b, g, n: (b, 0, part * ng + g))
    w_spec = lambda part: pl.BlockSpec((CONV_W, hw), lambda b, g, n: (0, part * ng + g))
    gate_row_spec = pl.BlockSpec((1, LANES), lambda b, g, n: (0, g))
    hist_spec = pl.BlockSpec((nseq, SUBLANES, hw), lambda b, g, n: (b, 0, g))
    hist_shape = jax.ShapeDtypeStruct((batch, SUBLANES, HA * LANES), F32)
    state_spec = pl.BlockSpec((nseq, hb, DK, DV), lambda b, g, n: (b, g, 0, 0))
    in_specs = [
        x_spec(0), x_spec(1), x_spec(2), x_spec(3),
        pl.BlockSpec((rows_step, LANES), lambda b, g, n: (rows(b, g, n), g)),
        gate_row_spec, gate_row_spec,
        prev_spec(0), prev_spec(1), prev_spec(2),
        w_spec(0), w_spec(1), w_spec(2),
        pl.BlockSpec((1, LANES), lambda b, g, n: (0, 0)),
        state_spec,
    ]
    out_specs = [
        pl.BlockSpec((rows_step, hw), lambda b, g, n: (b * nblk + n, g)),
        state_spec,
        hist_spec, hist_spec, hist_spec,
    ]
    return pl.pallas_call(
        functools.partial(_gdn_kernel, c=c, hb=hb, nseq=nseq, nsub=nsub),
        grid=(batch // nseq, ng, nblk),
        in_specs=in_specs,
        out_specs=out_specs,
        out_shape=[jax.ShapeDtypeStruct((batch * length, HA * DV), BF16),
                   jax.ShapeDtypeStruct((batch, HA, DK, DV), F32),
                   hist_shape, hist_shape, hist_shape],
        scratch_shapes=[pltpu.VMEM((nseq, nsub * c + SUBLANES, hw), F32)] * 3,
        compiler_params=pltpu.CompilerParams(
            dimension_semantics=("parallel", "parallel", "arbitrary"),
            vmem_limit_bytes=VMEM_LIMIT),
        name=name,
    )(qkvz, qkvz, qkvz, qkvz, gates, alog_row, dt_row,
      conv_prev, conv_prev, conv_prev, conv_w, conv_w, conv_w, onorm_row, s0)


def _attn_kernel(sinks_ref, q_ref, kp_ref, vp_ref, kc_ref, vc_ref, o_ref, *, tq, nb, chunk_mask,
                 prev_transposed):
    i = pl.program_id(1)
    nk = WINDOW + tq
    lane_q = lax.broadcasted_iota(jnp.int32, (tq, LANES), 1)
    if chunk_mask:
        qc = lax.broadcasted_iota(jnp.int32, (tq, nk), 0) // CHUNK
        kc = lax.broadcasted_iota(jnp.int32, (tq, nk), 1) // CHUNK
        w_ch = WINDOW // CHUNK
        first_kc = jnp.where(i > 0, 0, w_ch)
        valid = (kc >= jnp.maximum(qc, first_kc)) & (kc <= qc + w_ch)
    scale = HD ** -0.5
    assert math.log2(HD) % 2 == 0
    per_slab = LANES // HD
    group = N_Q // N_KV
    slabs_per_kv = group // per_slab
    half_masks = [(lane_q >= r * HD) & (lane_q < (r + 1) * HD) for r in range(per_slab)]

    units = [(b, h) for b in range(nb) for h in range(N_KV)]
    kdup, vdup, qs = [], [], []
    for b, h in units:
        ksl = slice((h // per_slab) * LANES, (h // per_slab + 1) * LANES)

        def dup(prev_ref, cur_ref):
            if prev_transposed:
                xt = prev_ref[b, h]
                front = [jnp.concatenate([xt, xt], axis=0).T]
                slab = cur_ref[b * tq:(b + 1) * tq, ksl]
            else:
                front = []
                slab = jnp.concatenate([prev_ref[b * WINDOW:(b + 1) * WINDOW, ksl],
                                        cur_ref[b * tq:(b + 1) * tq, ksl]], axis=0)
            lane = lax.broadcasted_iota(jnp.int32, slab.shape, 1)
            first = (lane < HD) == (h % per_slab == 0)
            own = jnp.where(first, slab, pltpu.roll(slab, HD, 1))
            return jnp.concatenate(front + [own], axis=0).astype(BF16)

        kdup.append(dup(kp_ref, kc_ref))
        vdup.append(dup(vp_ref, vc_ref))
        pieces = []
        for s in range(slabs_per_kv):
            slab = h * slabs_per_kv + s
            q2 = q_ref[b * tq:(b + 1) * tq, slab * LANES:(slab + 1) * LANES]
            pieces += [jnp.where(mk, q2, jnp.zeros_like(q2)) for mk in half_masks]
        qs.append(jnp.concatenate(pieces, axis=0) * scale)
    sc = [lax.dot_general(qs[u], kdup[u], _NT, preferred_element_type=F32) for u in range(len(units))]
    ps = []
    for u, (b, h) in enumerate(units):
        blocks = []
        for r in range(group):
            sr = sc[u][r * tq:(r + 1) * tq]
            if chunk_mask:
                sr = jnp.where(valid, sr, -jnp.inf)
            sk = sinks_ref[h * group + r]
            mx = jnp.maximum(jnp.max(sr, axis=-1, keepdims=True), sk)
            e = jnp.exp(sr - mx)
            den = jnp.sum(e, axis=-1, keepdims=True) + jnp.exp(sk - mx)
            blocks.append((e / den).astype(BF16))
        ps.append(jnp.concatenate(blocks, axis=0))
    pv = [jnp.dot(ps[u], vdup[u], preferred_element_type=F32) for u in range(len(units))]
    for u, (b, h) in enumerate(units):
        for s in range(slabs_per_kv):
            slab = h * slabs_per_kv + s
            halves = [pv[u][(s * per_slab + r) * tq:(s * per_slab + r + 1) * tq] for r in range(per_slab)]
            out = halves[-1]
            for r in range(per_slab - 2, -1, -1):
                out = jnp.where(half_masks[r], halves[r], out)
            o_ref[b * tq:(b + 1) * tq, slab * LANES:(slab + 1) * LANES] = out.astype(o_ref.dtype)


def _attention(sinks, q, kprev, vprev, kprev_col, vprev_col, kv, *, row0, batch, length,
               tq, nb, chunk_mask, prev_transposed, name):
    nq = length // tq
    kvw = N_KV * HD
    assert row0 % (nb * tq) == 0 and length % tq == 0 and batch % nb == 0 and (nb == 1 or nq == 1)
    rb0 = row0 // (nb * tq)

    def cur(b, i):
        return rb0 + b * nq + i

    def prev(b, i):
        return b * nq + jnp.maximum(i - 1, 0)

    in_specs = [
        pl.BlockSpec(memory_space=pltpu.SMEM),
        pl.BlockSpec((nb * tq, N_Q * HD), lambda b, i: (cur(b, i), 0)),
        *([pl.BlockSpec((nb, N_KV, HD, WINDOW), lambda b, i: (b, 0, 0, 0))] * 2 if prev_transposed else
          [pl.BlockSpec((nb * WINDOW, kvw), lambda b, i: (prev(b, i), kprev_col)),
           pl.BlockSpec((nb * WINDOW, kvw), lambda b, i: (prev(b, i), vprev_col))]),
        pl.BlockSpec((nb * tq, kvw), lambda b, i: (cur(b, i), 0)),
        pl.BlockSpec((nb * tq, kvw), lambda b, i: (cur(b, i), 1)),
    ]
    return pl.pallas_call(
        functools.partial(_attn_kernel, tq=tq, nb=nb, chunk_mask=chunk_mask,
                          prev_transposed=prev_transposed),
        grid=(batch // nb, nq),
        in_specs=in_specs,
        out_specs=pl.BlockSpec((nb * tq, N_Q * HD), lambda b, i: (b * nq + i, 0)),
        out_shape=jax.ShapeDtypeStruct((batch * length, N_Q * HD), BF16),
        compiler_params=pltpu.CompilerParams(
            dimension_semantics=("parallel", "arbitrary"), vmem_limit_bytes=VMEM_LIMIT),
        name=name,
    )(sinks, q, kprev, vprev, kv, kv)


def _rope_tables(pos):
    half = HD // 2
    inv = 1.0 / (ROPE_THETA ** (jnp.arange(half, dtype=F32) / half))
    ang = pos.astype(F32)[:, None] * jnp.tile(inv, LANES // half)[None, :]
    sign = jnp.tile(jnp.concatenate([-jnp.ones((half,), F32), jnp.ones((half,), F32)]), LANES // HD)
    return jnp.cos(ang), jnp.sin(ang) * sign[None, :]


def _gate_layout(t, hb):
    lead = t.shape[:-1]
    ng = HA // hb
    if ng == 1:
        return jnp.pad(t, [(0, 0)] * len(lead) + [(0, LANES - 2 * HA)])
    beta = t[..., :HA].reshape(lead + (ng, hb))
    dec = t[..., HA:].reshape(lead + (ng, hb))
    pad = jnp.zeros(lead + (ng, LANES - 2 * hb), t.dtype)
    return jnp.concatenate([beta, dec, pad], axis=-1).reshape(lead + (ng * LANES,))


def kernel(x_prompt, x_sample, cache_conv, state_gdn, cache_k, cache_v, attn_norm, mlp_norm, final_norm, a_w_in, a_conv_w, a_log, a_dt_bias, a_o_norm, a_w_out, kv_norm, w_kv, b_w_q, b_sinks, b_w_o, w_up, w_down):
    bp, lp, d = x_prompt.shape
    bs, ls, _ = x_sample.shape
    mp, ms = bp * lp, bs * ls
    m = mp + ms
    hb = GDN_HEADS_PER_STEP
    dqk, dvw = HA * DK, HA * DV
    conv_ch = 2 * dqk + dvw
    keep = CONV_W - 1
    assert a_w_in.shape[0] == 1 and b_w_q.shape[0] == 1 and lp % WINDOW == 0 and ls >= keep

    xp = x_prompt.reshape(mp, d)
    xs = x_sample.reshape(ms, d)
    tm = _pick_tile(m, ROW_TILES)
    tm2 = _pick_tile(math.gcd(mp, ms), ROW_TILES[1:])
    tm_p = _pick_tile(mp, PROMPT_ROW_TILES)
    tm_s = _pick_tile(ms, ROW_TILES[1:])

    w_in = a_w_in[0].astype(BF16)
    w_gate = _gate_layout(w_in[:, conv_ch + dvw:], hb)
    g0 = attn_norm[0].reshape(1, d)
    tn_in = _pick_tile(conv_ch + dvw, IN_PROJ_COL_TILES)
    zeros_gate = jnp.zeros((HA,), F32)
    alog_row = _gate_layout(jnp.concatenate([zeros_gate, a_log[0]]), hb).reshape(1, -1)
    dt_row = _gate_layout(jnp.concatenate([zeros_gate, a_dt_bias[0]]), hb).reshape(1, -1)
    onorm_row = a_o_norm[0].reshape(1, DV)
    c_p, c_s = min(CHUNK, lp), min(CHUNK, ls)
    streams = (("prompt", xp, tm_p, bp, lp, c_p, 1, _pick_tile(lp // c_p, (GDN_PROMPT_BLOCKS_PER_STEP, 1)),
                jnp.zeros((bp, keep, conv_ch), F32), jnp.zeros((bp, HA, DK, DV), F32)),
               ("sample", xs, tm_s, bs, ls, c_s, _pick_tile(bs, (GDN_SAMPLE_SEQS_PER_STEP, 1)) if ls == c_s else 1,
                1, cache_conv[0], state_gdn[0]))
    mixed = []
    for tag, xr, tmr, nbatch, length, c_blk, nseq, nsub, conv_prev, s0 in streams:
        qkvz = _matmul(xr, w_in, gain=g0, tm=tmr, tn=tn_in, name="gdn_in_proj_" + tag)
        gates = _matmul(xr, w_gate, gain=g0, tm=tmr, tn=w_gate.shape[1], name="gdn_gate_proj_" + tag)
        mixed.append(_gdn(qkvz, gates, alog_row, dt_row, conv_prev, a_conv_w[0], onorm_row, s0,
                          batch=nbatch, length=length, c=c_blk, hb=hb, nseq=nseq, nsub=nsub,
                          name="gdn_" + tag))
    (o_p, gdn_p, *hist_p), (o_s, gdn_s, *hist_s) = mixed
    tn_o = _pick_tile(d, COL_TILES)
    x = _matmul((o_p, o_s), a_w_out[0], res=(xp, xs), tm=tm2, tn=tn_o, weights_outer=True,
                name="gdn_out_proj")
    tf = _pick_tile(w_up.shape[2], COL_TILES[1:])
    x = _mlp(x, mlp_norm[0].reshape(1, d), w_up, w_down, 0, tm=tm, tf=tf, name="mlp0")

    cos_p, sin_p = _rope_tables(jnp.arange(lp))
    cos_s, sin_s = _rope_tables(PAST_LEN + jnp.arange(ls))
    rope = (jnp.concatenate([cos_p] * bp + [cos_s] * bs, axis=0),
            jnp.concatenate([sin_p] * bp + [sin_s] * bs, axis=0))
    kvw = N_KV * HD
    kv = _matmul(x, w_kv, gain=kv_norm.reshape(1, d), rope=rope, rope_cols=kvw,
                 tm=tm, tn=2 * kvw, name="kv_proj")
    q = _matmul(x, b_w_q[0], gain=attn_norm[1].reshape(1, d), rope=rope,
                rope_cols=N_Q * HD, out_dtype=BF16, tm=tm, tn=tn_o, name="q_proj")
    sinks = b_sinks[0].astype(F32)
    a_p = _attention(sinks, q, kv, kv, 0, 1, kv, row0=0, batch=bp, length=lp, tq=WINDOW, nb=1,
                     chunk_mask=True, prev_transposed=False, name="attn_prompt")
    nb = _pick_tile(bs, (ATTN_SAMPLE_BATCHES_PER_STEP, 2, 1))
    a_s = _attention(sinks, q, jnp.transpose(cache_k, (0, 2, 3, 1)), jnp.transpose(cache_v, (0, 2, 3, 1)),
                     0, 0, kv, row0=mp, batch=bs, length=ls, tq=ls, nb=nb,
                     chunk_mask=False, prev_transposed=True, name="attn_sample")
    x = _matmul((a_p, a_s), b_w_o[0], res=x, tm=tm2, tn=tn_o, weights_outer=True, name="attn_out_proj")
    y_p, y_s = _mlp_split(x, mlp_norm[1].reshape(1, d), w_up, w_down, 1, final_norm.reshape(1, d),
                          split_row=mp, tm=tm, tf=tf, name="mlp1")

    y_prompt = y_p.reshape(bp, lp, d)
    y_sample = y_s.reshape(bs, ls, d)
    conv_p = jnp.concatenate([t[:, SUBLANES - keep:] for t in hist_p], axis=-1)[None]
    conv_s = jnp.concatenate([t[:, SUBLANES - keep:] for t in hist_s], axis=-1)[None]
    kv_p = jnp.stack([kv[(b + 1) * lp - WINDOW:(b + 1) * lp] for b in range(bp)])
    kv_s = kv[mp:].reshape(bs, ls, 2 * kvw)
    k_p = kv_p[..., :kvw].reshape(bp, WINDOW, N_KV, HD)
    v_p = kv_p[..., kvw:].reshape(bp, WINDOW, N_KV, HD)
    k_s = kv_s[..., :kvw].reshape(bs, ls, N_KV, HD)
    v_s = kv_s[..., kvw:].reshape(bs, ls, N_KV, HD)
    return (y_prompt, y_sample, conv_p, gdn_p[None], k_p, v_p, conv_s, gdn_s[None], k_s, v_s)
```

```python
import functools
import math

import jax
import jax.numpy as jnp
from jax import lax
from jax.experimental import pallas as pl
from jax.experimental.pallas import tpu as pltpu

F32 = jnp.float32
BF16 = jnp.bfloat16

EPS = 1e-6
CHUNK = 64
WINDOW = 128
PAST_LEN = 4096
ROPE_THETA = 10000.0
HA, DK, DV = 16, 128, 128
N_Q, N_KV, HD = 32, 4, 64
CONV_W = 4
LANES = 128
SUBLANES = 8
VMEM_LIMIT = 56 * 1024 * 1024
MLP_VMEM_LIMIT = 60 * 1024 * 1024
GDN_HEADS_PER_STEP = 16
GDN_SAMPLE_SEQS_PER_STEP = 4
GDN_PROMPT_BLOCKS_PER_STEP = 4
ATTN_SAMPLE_BATCHES_PER_STEP = 4
ROW_TILES = (1088, 512, 256, 128, 64, 32, 16)
PROMPT_ROW_TILES = (1024,) + ROW_TILES[1:]
COL_TILES = (1024, 512, 256, 128)
IN_PROJ_COL_TILES = (2048,) + COL_TILES


def _pick_tile(n, candidates):
    for c in candidates:
        if n % c == 0:
            return c
    return n


def _dot(a, b, dims=(((1,), (0,)), ((), ()))):
    return lax.dot_general(a.astype(BF16), b.astype(BF16), dims, preferred_element_type=F32)


_NT = (((1,), (1,)), ((), ()))
_TN = (((0,), (0,)), ((), ()))


def _rms_scale(x):
    return lax.rsqrt(jnp.mean(x * x, axis=-1, keepdims=True) + EPS)


def _silu(x):
    h = 0.5 * x
    return h + h * jnp.tanh(h)


def _rope_slab(y, cos, sin_signed):
    lane = lax.broadcasted_iota(jnp.int32, y.shape, 1)
    up = pltpu.roll(y, 32, 1)
    down = pltpu.roll(y, LANES - 32, 1)
    swapped = jnp.where((lane % HD) < HD // 2, down, up)
    return y * cos + swapped * sin_signed


def _project(lhs, w_ref, res_ref, cos_ref, sin_ref, o_ref, *, rope_slabs, n_slabs):
    if not rope_slabs:
        y = jnp.dot(lhs, w_ref[...].astype(BF16), preferred_element_type=F32)
        if res_ref is not None:
            y = y + res_ref[...]
        o_ref[...] = y.astype(o_ref.dtype)
        return
    cos = cos_ref[...]
    sin = sin_ref[...]
    n_chunk = 2 if n_slabs % 2 == 0 else 1
    for c0 in range(0, n_slabs, n_chunk):
        cols = slice(c0 * LANES, (c0 + n_chunk) * LANES)
        y = jnp.dot(lhs, w_ref[:, cols].astype(BF16), preferred_element_type=F32)
        if res_ref is not None:
            y = y + res_ref[:, cols]
        for s in range(n_chunk):
            ys = y[:, s * LANES:(s + 1) * LANES]
            if c0 + s < rope_slabs:
                ys = _rope_slab(ys, cos, sin)
            o_ref[:, (c0 + s) * LANES:(c0 + s + 1) * LANES] = ys.astype(o_ref.dtype)


def _matmul_kernel(*refs, n_x, n_res, split_tile, row_axis, has_norm, rope_slabs, n_slabs):
    it = iter(refs)
    x_refs = [next(it) for _ in range(n_x)]
    g_ref = next(it) if has_norm else None
    w_ref = next(it)
    res_refs = [next(it) for _ in range(n_res)]
    cos_ref = next(it) if rope_slabs else None
    sin_ref = next(it) if rope_slabs else None
    o_ref = next(it)
    xn_ref = next(it) if has_norm else None

    if has_norm:
        @pl.when(pl.program_id(1) == 0)
        def _():
            x = x_refs[0][...]
            xn_ref[...] = (x * _rms_scale(x) * g_ref[...]).astype(BF16)

    def emit(lhs_ref, res_ref):
        _project(lhs_ref[...], w_ref, res_ref, cos_ref, sin_ref, o_ref, rope_slabs=rope_slabs, n_slabs=n_slabs)

    if n_x == 1:
        emit(xn_ref if has_norm else x_refs[0], res_refs[0] if n_res else None)
    else:
        i = pl.program_id(row_axis)
        pl.when(i < split_tile)(functools.partial(emit, x_refs[0], res_refs[0] if n_res else None))
        pl.when(i >= split_tile)(functools.partial(emit, x_refs[1], res_refs[-1] if n_res else None))


def _matmul(xs, w, *, gain=None, res=None, rope=None, rope_cols=0, out_dtype=F32, tm, tn, name,
            weights_outer=False):
    xs = xs if isinstance(xs, (tuple, list)) else (xs,)
    ress = () if res is None else (res if isinstance(res, (tuple, list)) else (res,))
    k = xs[0].shape[1]
    m = sum(x.shape[0] for x in xs)
    n = (w.shape[1] // tn) * tn
    assert all(x.shape[0] % tm == 0 for x in xs) and w.shape[0] == k
    has_norm = gain is not None
    assert not (has_norm and len(xs) > 1)
    assert len(ress) <= len(xs) and all(r.shape[0] == x.shape[0] for r, x in zip(ress, xs) if len(ress) > 1)
    split_tile = xs[0].shape[0] // tm
    rope_slabs = 0
    if rope is not None:
        assert rope_cols == n or tn == n
        rope_slabs = min(rope_cols, tn) // LANES
    assert not (has_norm and weights_outer)

    def spec(shape, f):
        return pl.BlockSpec(shape, (lambda a, b: f(b, a)) if weights_outer else f)

    def row_specs(arrays, width, col):
        if len(arrays) == 1:
            return [spec((tm, width), lambda i, j: (i, col(j)))]
        return [spec((tm, width), lambda i, j: (jnp.minimum(i, split_tile - 1), col(j))),
                spec((tm, width), lambda i, j: (jnp.maximum(i - split_tile, 0), col(j)))]

    in_specs = row_specs(xs, k, lambda j: 0)
    args = list(xs)
    if has_norm:
        in_specs.append(spec((1, k), lambda i, j: (0, 0)))
        args.append(gain)
    in_specs.append(spec((k, tn), lambda i, j: (0, j)))
    args.append(w)
    in_specs += row_specs(ress, tn, lambda j: j) if ress else []
    args += list(ress)
    if rope_slabs:
        in_specs += [spec((tm, LANES), lambda i, j: (i, 0))] * 2
        args += list(rope)
    kern = functools.partial(_matmul_kernel, n_x=len(xs), n_res=len(ress), split_tile=split_tile,
                             row_axis=1 if weights_outer else 0, has_norm=has_norm,
                             rope_slabs=rope_slabs, n_slabs=tn // LANES)
    grid = (n // tn, m // tm) if weights_outer else (m // tm, n // tn)
    return pl.pallas_call(
        kern,
        grid=grid,
        in_specs=in_specs,
        out_specs=spec((tm, tn), lambda i, j: (i, j)),
        out_shape=jax.ShapeDtypeStruct((m, n), out_dtype),
        scratch_shapes=[pltpu.VMEM((tm, k), BF16)] if has_norm else [],
        compiler_params=pltpu.CompilerParams(
            dimension_semantics=("parallel", "arbitrary"), vmem_limit_bytes=VMEM_LIMIT),
        name=name,
    )(*args)


def _mlp_kernel(*refs, tm, n_tiles, split_row, final_norm, prefetch_step):
    x_hbm, g_ref, wu_ref, wd_ref = refs[:4]
    fg_ref = refs[4] if final_norm else None
    *out_refs, acc_ref, hn_ref, sem_in, sem_out = refs[5 if final_norm else 4:]
    i = pl.program_id(0)
    f = pl.program_id(1)
    slot = i % 2

    def fetch(t):
        return pltpu.make_async_copy(x_hbm.at[pl.ds(t * tm, tm)], acc_ref.at[t % 2], sem_in.at[t % 2])

    def writebacks(t):
        r0, r1 = t * tm, (t + 1) * tm
        s = t % 2
        cps = []
        if r0 < split_row:
            n = min(r1, split_row) - r0
            cps.append(pltpu.make_async_copy(acc_ref.at[s, 0:n], out_refs[0].at[r0:r0 + n], sem_out.at[s, 0]))
        if r1 > split_row:
            a = max(r0, split_row)
            cps.append(pltpu.make_async_copy(acc_ref.at[s, a - r0:tm],
                                             out_refs[1].at[a - split_row:r1 - split_row], sem_out.at[s, 1]))
        return cps

    def start_tile(t):
        for cp in writebacks(t):
            cp.start()

    def wait_tile(t):
        for cp in writebacks(t):
            cp.wait()

    @pl.when(f == 0)
    def _():
        @pl.when(i == 0)
        def _():
            fetch(i).start()
        fetch(i).wait()
        x = acc_ref[slot]
        hn_ref[...] = (x * _rms_scale(x) * g_ref[...]).astype(BF16)

    u = jnp.dot(hn_ref[...], wu_ref[...].astype(BF16), preferred_element_type=F32)
    a = jnp.square(jnp.maximum(u, 0.0)).astype(BF16)
    acc_ref[slot] += jnp.dot(a, wd_ref[...].astype(BF16), preferred_element_type=F32)

    @pl.when((f == prefetch_step) & (i + 1 < n_tiles))
    def _():
        for t in range(n_tiles - 2):
            pl.when(i == t + 1)(functools.partial(wait_tile, t))
        fetch(i + 1).start()

    @pl.when(f == pl.num_programs(1) - 1)
    def _():
        if final_norm:
            y = acc_ref[slot]
            acc_ref[slot] = y * _rms_scale(y) * fg_ref[...]
        for t in range(n_tiles):
            pl.when(i == t)(functools.partial(start_tile, t))

        @pl.when(i == n_tiles - 1)
        def _():
            for t in range(max(n_tiles - 2, 0), n_tiles):
                wait_tile(t)


def _mlp(x, gain, w_up, w_down, layer, *, final_gain=None, split_row=None, tm, tf, name):
    m, d = x.shape
    dff = w_up.shape[2]
    n_tiles, nf = m // tm, dff // tf
    final_norm = final_gain is not None
    prefetch_step = 1
    assert m % tm == 0 and dff % tf == 0 and nf > prefetch_step + 1
    assert split_row is None or (split_row % SUBLANES == 0 and 0 < split_row < m)
    in_specs = [
        pl.BlockSpec(memory_space=pl.ANY),
        pl.BlockSpec((1, d), lambda i, f: (0, 0)),
        pl.BlockSpec((None, d, tf), lambda i, f: (layer, 0, f)),
        pl.BlockSpec((None, tf, d), lambda i, f: (layer, f, 0)),
    ]
    args = [x, gain, w_up, w_down]
    if final_norm:
        in_specs.append(pl.BlockSpec((1, d), lambda i, f: (0, 0)))
        args.append(final_gain)
    out_rows = [m] if split_row is None else [split_row, m - split_row]
    outs = pl.pallas_call(
        functools.partial(_mlp_kernel, tm=tm, n_tiles=n_tiles, split_row=out_rows[0], final_norm=final_norm,
                          prefetch_step=prefetch_step),
        grid=(n_tiles, nf),
        in_specs=in_specs,
        out_specs=[pl.BlockSpec(memory_space=pl.ANY)] * len(out_rows),
        out_shape=[jax.ShapeDtypeStruct((r, d), F32) for r in out_rows],
        scratch_shapes=[pltpu.VMEM((2, tm, d), F32), pltpu.VMEM((tm, d), BF16),
                        pltpu.SemaphoreType.DMA((2,)), pltpu.SemaphoreType.DMA((2, 2))],
        compiler_params=pltpu.CompilerParams(
            dimension_semantics=("arbitrary", "arbitrary"), vmem_limit_bytes=MLP_VMEM_LIMIT),
        name=name,
    )(*args)
    return outs[0] if split_row is None else outs


def _gdn_kernel(xq_ref, xk_ref, xv_ref, z_ref, gates_ref, alog_ref, dt_ref,
                pq_ref, pk_ref, pv_ref, wq_ref, wk_ref, wv_ref, onorm_ref, s0_ref,
                o_ref, s_ref, hq_ref, hk_ref, hv_ref, padq_ref, padk_ref, padv_ref, *, c, hb, nseq, nsub):
    n = pl.program_id(2)
    halo = SUBLANES
    seqs = range(nseq)
    rows_seq = nsub * c

    @pl.when(n == 0)
    def _():
        for pad_ref, prev_ref in ((padq_ref, pq_ref), (padk_ref, pk_ref), (padv_ref, pv_ref)):
            for sq in seqs:
                pad_ref[sq, 0:halo, :] = jnp.zeros((halo, pad_ref.shape[2]), F32)
                pad_ref[sq, halo - (CONV_W - 1):halo, :] = prev_ref[sq]
        s_ref[...] = s0_ref[...]

    def conv(x_ref, pad_ref, w_ref, hist_ref, sq):
        pad_ref[sq, halo:halo + rows_seq, :] = x_ref[sq * rows_seq:(sq + 1) * rows_seq, :]
        xp = pad_ref[sq]
        x1 = pltpu.roll(xp, 1, 0)
        a = xp * w_ref[3:4, :] + x1 * w_ref[2:3, :]
        b = xp * w_ref[1:2, :] + x1 * w_ref[0:1, :]
        acc = (a + pltpu.roll(b, 2, 0))[halo:, :]
        tail = xp[rows_seq:rows_seq + halo, :]
        pad_ref[sq, 0:halo, :] = tail
        hist_ref[sq] = tail
        return _silu(acc)

    cq = [conv(xq_ref, padq_ref, wq_ref, hq_ref, sq) for sq in seqs]
    ck = [conv(xk_ref, padk_ref, wk_ref, hk_ref, sq) for sq in seqs]
    cv = [conv(xv_ref, padv_ref, wv_ref, hv_ref, sq) for sq in seqs]

    row = lax.broadcasted_iota(jnp.int32, (c, c), 0)
    col = lax.broadcasted_iota(jnp.int32, (c, c), 1)
    incl = row >= col
    strict = row > col
    tril = jnp.where(incl, 1.0, 0.0).astype(F32)
    r128 = lax.broadcasted_iota(jnp.int32, (LANES, LANES), 0)
    c128 = lax.broadcasted_iota(jnp.int32, (LANES, LANES), 1)
    eye128 = jnp.where(r128 == c128, 1.0, 0.0).astype(F32)
    blocks = [(sq, sb) for sq in seqs for sb in range(nsub)]
    brow = {blk: slice(blk[0] * rows_seq + blk[1] * c, blk[0] * rows_seq + (blk[1] + 1) * c) for blk in blocks}
    beta_all, gc_all, gc_t = {}, {}, {}
    for blk in blocks:
        gates = gates_ref[brow[blk], :]
        beta_all[blk] = 0.5 + 0.5 * jnp.tanh(0.5 * gates)
        ga = gates + dt_ref[...]
        softplus = jnp.maximum(ga, 0.0) + jnp.log(1.0 + jnp.exp(-jnp.abs(ga)))
        g_all = -jnp.exp(alog_ref[...]) * softplus
        gc_all[blk] = jnp.dot(tril, g_all, preferred_element_type=F32,
                              precision=lax.Precision.HIGHEST)
        gc_t[blk] = lax.dot_general(eye128, gc_all[blk], _NT, preferred_element_type=F32,
                                    precision=lax.Precision.HIGHEST)

    units = [(blk, h) for blk in blocks for h in range(hb)]
    un = range(len(units))
    sls = [slice(h * LANES, (h + 1) * LANES) for _, h in units]
    srow = [slice(blk[1] * c, (blk[1] + 1) * c) for blk, _ in units]
    q = [cq[blk[0]][srow[u], sls[u]] for u, (blk, h) in enumerate(units)]
    k = [ck[blk[0]][srow[u], sls[u]] for u, (blk, h) in enumerate(units)]
    v = [cv[blk[0]][srow[u], sls[u]] for u, (blk, h) in enumerate(units)]
    q = [x * (lax.rsqrt(jnp.sum(x * x, axis=-1, keepdims=True) + EPS) * (DK ** -0.5)) for x in q]
    k = [x * lax.rsqrt(jnp.sum(x * x, axis=-1, keepdims=True) + EPS) for x in k]
    bcol = [beta_all[blk][:, h:h + 1] for blk, h in units]
    gcol = [gc_all[blk][:, hb + h:hb + h + 1] for blk, h in units]
    grow = [gc_t[blk][hb + h:hb + h + 1, :] for blk, h in units]
    glast = [gc_all[blk][c - 1:c, hb + h:hb + h + 1] for blk, h in units]
    decay = [jnp.exp(jnp.where(incl, gcol[u] - grow[u], -jnp.inf)) for u in un]
    kb = [k[u] * bcol[u] for u in un]
    eye = jnp.where(row == col, 1.0, 0.0).astype(F32)
    qkk = [_dot(jnp.concatenate([q[u], kb[u]], axis=0), k[u], _NT) for u in un]
    qk = [qkk[u][:c] * decay[u] for u in un]
    p = [-jnp.where(strict, qkk[u][c:] * decay[u], 0.0) for u in un]
    t = [eye + x for x in p]
    levels = max(int(math.ceil(math.log2(c))) - 1, 0)
    if levels:
        p = [_dot(x, x) for x in p]
    for _ in range(1, levels):
        r = [_dot(jnp.concatenate([p[u], t[u]], axis=0), p[u]) for u in un]
        t = [t[u] + r[u][c:] for u in un]
        p = [r[u][:c] for u in un]
    if levels:
        t = [t[u] + _dot(t[u], p[u]) for u in un]
    egc = [jnp.exp(x) for x in gcol]
    rhs = [jnp.concatenate([v[u] * bcol[u], kb[u] * egc[u]], axis=-1) for u in un]
    sol = [_dot(t[u], rhs[u]) for u in un]
    wqe = [jnp.concatenate([sol[u][:, DV:], q[u] * egc[u]], axis=0) for u in un]
    kd = [k[u] * jnp.exp(glast[u] - gcol[u]) for u in un]
    elast = [jnp.exp(x) for x in glast]

    onorm = onorm_ref[...]
    s = {(sq, h): s_ref[sq, h] for sq in seqs for h in range(hb)}
    for sb in range(nsub):
        cur = [u for u in un if units[u][0][1] == sb]
        key = {u: (units[u][0][0], units[u][1]) for u in cur}
        wq = {u: _dot(wqe[u], s[key[u]]) for u in cur}
        v_new = {u: sol[u][:, :DV] - wq[u][:c] for u in cur}
        o = {u: wq[u][c:] + _dot(qk[u], v_new[u]) for u in cur}
        for u in cur:
            s[key[u]] = s[key[u]] * elast[u] + _dot(kd[u], v_new[u], _TN)
        for u in cur:
            oh = o[u] * _rms_scale(o[u]) * onorm
            rows = brow[units[u][0]]
            o_ref[rows, sls[u]] = (oh * _silu(z_ref[rows, sls[u]])).astype(o_ref.dtype)
    for (sq, h), val in s.items():
        s_ref[sq, h] = val


def _gdn(qkvz, gates, alog_row, dt_row, conv_prev, conv_w, onorm_row, s0,
         *, batch, length, c, hb, nseq, nsub, name):
    nblk = length // (nsub * c)
    ng = HA // hb
    hw = hb * LANES
    rows_step = nseq * nsub * c
    assert length % (nsub * c) == 0 and c >= SUBLANES and CONV_W == 4
    assert batch % nseq == 0 and (nseq == 1 or nblk == 1)

    def rows(b, g, n):
        return b * nblk + n

    x_spec = lambda part: pl.BlockSpec((rows_step, hw), lambda b, g, n: (rows(b, g, n), part * ng + g))
    prev_spec = lambda part: pl.BlockSpec((nseq, CONV_W - 1, hw), lambda b, g, n: (b, 0, part * ng + g))
    w_spec = lambda part: pl.BlockSpec((CONV_W, hw), lambda b, g, n: (0, part * ng + g))
    gate_row_spec = pl.BlockSpec((1, LANES), lambda b, g, n: (0, g))
    hist_spec = pl.BlockSpec((nseq, SUBLANES, hw), lambda b, g, n: (b, 0, g))
    hist_shape = jax.ShapeDtypeStruct((batch, SUBLANES, HA * LANES), F32)
    state_spec = pl.BlockSpec((nseq, hb, DK, DV), lambda b, g, n: (b, g, 0, 0))
    in_specs = [
        x_spec(0), x_spec(1), x_spec(2), x_spec(3),
        pl.BlockSpec((rows_step, LANES), lambda b, g, n: (rows(b, g, n), g)),
        gate_row_spec, gate_row_spec,
        prev_spec(0), prev_spec(1), prev_spec(2),
        w_spec(0), w_spec(1), w_spec(2),
        pl.BlockSpec((1, LANES), lambda b, g, n: (0, 0)),
        state_spec,
    ]
    out_specs = [
        pl.BlockSpec((rows_step, hw), lambda b, g, n: (b * nblk + n, g)),
        state_spec,
        hist_spec, hist_spec, hist_spec,
    ]
    return pl.pallas_call(
        functools.partial(_gdn_kernel, c=c, hb=hb, nseq=nseq, nsub=nsub),
        grid=(batch // nseq, ng, nblk),
        in_specs=in_specs,
        out_specs=out_specs,
        out_shape=[jax.ShapeDtypeStruct((batch * length, HA * DV), BF16),
                   jax.ShapeDtypeStruct((batch, HA, DK, DV), F32),
                   hist_shape, hist_shape, hist_shape],
        scratch_shapes=[pltpu.VMEM((nseq, nsub * c + SUBLANES, hw), F32)] * 3,
        compiler_params=pltpu.CompilerParams(
            dimension_semantics=("parallel", "parallel", "arbitrary"),
            vmem_limit_bytes=VMEM_LIMIT),
        name=name,
    )(qkvz, qkvz, qkvz, qkvz, gates, alog_row, dt_row,
      conv_prev, conv_prev, conv_prev, conv_w, conv_w, conv_w, onorm_row, s0)


def _attn_kernel(sinks_ref, q_ref, kp_ref, vp_ref, kc_ref, vc_ref, o_ref, *, tq, nb, chunk_mask,
                 prev_transposed):
    i = pl.program_id(1)
    nk = WINDOW + tq
    lane_q = lax.broadcasted_iota(jnp.int32, (tq, LANES), 1)
    if chunk_mask:
        qc = lax.broadcasted_iota(jnp.int32, (tq, nk), 0) // CHUNK
        kc = lax.broadcasted_iota(jnp.int32, (tq, nk), 1) // CHUNK
        w_ch = WINDOW // CHUNK
        first_kc = jnp.where(i > 0, 0, w_ch)
        valid = (kc >= jnp.maximum(qc, first_kc)) & (kc <= qc + w_ch)
    scale = HD ** -0.5
    assert math.log2(HD) % 2 == 0
    per_slab = LANES // HD
    group = N_Q // N_KV
    slabs_per_kv = group // per_slab
    half_masks = [(lane_q >= r * HD) & (lane_q < (r + 1) * HD) for r in range(per_slab)]

    units = [(b, h) for b in range(nb) for h in range(N_KV)]
    kdup, vdup, qs = [], [], []
    for b, h in units:
        ksl = slice((h // per_slab) * LANES, (h // per_slab + 1) * LANES)

        def dup(prev_ref, cur_ref):
            if prev_transposed:
                xt = prev_ref[b, h]
                front = [jnp.concatenate([xt, xt], axis=0).T]
                slab = cur_ref[b * tq:(b + 1) * tq, ksl]
            else:
                front = []
                slab = jnp.concatenate([prev_ref[b * WINDOW:(b + 1) * WINDOW, ksl],
                                        cur_ref[b * tq:(b + 1) * tq, ksl]], axis=0)
            lane = lax.broadcasted_iota(jnp.int32, slab.shape, 1)
            first = (lane < HD) == (h % per_slab == 0)
            own = jnp.where(first, slab, pltpu.roll(slab, HD, 1))
            return jnp.concatenate(front + [own], axis=0).astype(BF16)

        kdup.append(dup(kp_ref, kc_ref))
        vdup.append(dup(vp_ref, vc_ref))
        pieces = []
        for s in range(slabs_per_kv):
            slab = h * slabs_per_kv + s
            q2 = q_ref[b * tq:(b + 1) * tq, slab * LANES:(slab + 1) * LANES]
            pieces += [jnp.where(mk, q2, jnp.zeros_like(q2)) for mk in half_masks]
        qs.append(jnp.concatenate(pieces, axis=0) * scale)
    sc = [lax.dot_general(qs[u], kdup[u], _NT, preferred_element_type=F32) for u in range(len(units))]
    ps = []
    for u, (b, h) in enumerate(units):
        blocks = []
        for r in range(group):
            sr = sc[u][r * tq:(r + 1) * tq]
            if chunk_mask:
                sr = jnp.where(valid, sr, -jnp.inf)
            sk = sinks_ref[h * group + r]
            mx = jnp.maximum(jnp.max(sr, axis=-1, keepdims=True), sk)
            e = jnp.exp(sr - mx)
            den = jnp.sum(e, axis=-1, keepdims=True) + jnp.exp(sk - mx)
            blocks.append((e / den).astype(BF16))
        ps.append(jnp.concatenate(blocks, axis=0))
    pv = [jnp.dot(ps[u], vdup[u], preferred_element_type=F32) for u in range(len(units))]
    for u, (b, h) in enumerate(units):
        for s in range(slabs_per_kv):
            slab = h * slabs_per_kv + s
            halves = [pv[u][(s * per_slab + r) * tq:(s * per_slab + r + 1) * tq] for r in range(per_slab)]
            out = halves[-1]
            for r in range(per_slab - 2, -1, -1):
                out = jnp.where(half_masks[r], halves[r], out)
            o_ref[b * tq:(b + 1) * tq, slab * LANES:(slab + 1) * LANES] = out.astype(o_ref.dtype)


def _attention(sinks, q, kprev, vprev, kprev_col, vprev_col, kv, *, row0, batch, length,
               tq, nb, chunk_mask, prev_transposed, name):
    nq = length // tq
    kvw = N_KV * HD
    assert row0 % (nb * tq) == 0 and length % tq == 0 and batch % nb == 0 and (nb == 1 or nq == 1)
    rb0 = row0 // (nb * tq)

    def cur(b, i):
        return rb0 + b * nq + i

    def prev(b, i):
        return b * nq + jnp.maximum(i - 1, 0)

    in_specs = [
        pl.BlockSpec(memory_space=pltpu.SMEM),
        pl.BlockSpec((nb * tq, N_Q * HD), lambda b, i: (cur(b, i), 0)),
        *([pl.BlockSpec((nb, N_KV, HD, WINDOW), lambda b, i: (b, 0, 0, 0))] * 2 if prev_transposed else
          [pl.BlockSpec((nb * WINDOW, kvw), lambda b, i: (prev(b, i), kprev_col)),
           pl.BlockSpec((nb * WINDOW, kvw), lambda b, i: (prev(b, i), vprev_col))]),
        pl.BlockSpec((nb * tq, kvw), lambda b, i: (cur(b, i), 0)),
        pl.BlockSpec((nb * tq, kvw), lambda b, i: (cur(b, i), 1)),
    ]
    return pl.pallas_call(
        functools.partial(_attn_kernel, tq=tq, nb=nb, chunk_mask=chunk_mask,
                          prev_transposed=prev_transposed),
        grid=(batch // nb, nq),
        in_specs=in_specs,
        out_specs=pl.BlockSpec((nb * tq, N_Q * HD), lambda b, i: (b * nq + i, 0)),
        out_shape=jax.ShapeDtypeStruct((batch * length, N_Q * HD), BF16),
        compiler_params=pltpu.CompilerParams(
            dimension_semantics=("parallel", "arbitrary"), vmem_limit_bytes=VMEM_LIMIT),
        name=name,
    )(sinks, q, kprev, vprev, kv, kv)


def _rope_tables(pos):
    half = HD // 2
    inv = 1.0 / (ROPE_THETA ** (jnp.arange(half, dtype=F32) / half))
    ang = pos.astype(F32)[:, None] * jnp.tile(inv, LANES // half)[None, :]
    sign = jnp.tile(jnp.concatenate([-jnp.ones((half,), F32), jnp.ones((half,), F32)]), LANES // HD)
    return jnp.cos(ang), jnp.sin(ang) * sign[None, :]


def _gate_layout(t, hb):
    lead = t.shape[:-1]
    ng = HA // hb
    if ng == 1:
        return jnp.pad(t, [(0, 0)] * len(lead) + [(0, LANES - 2 * HA)])
    beta = t[..., :HA].reshape(lead + (ng, hb))
    dec = t[..., HA:].reshape(lead + (ng, hb))
    pad = jnp.zeros(lead + (ng, LANES - 2 * hb), t.dtype)
    return jnp.concatenate([beta, dec, pad], axis=-1).reshape(lead + (ng * LANES,))


def kernel(x_prompt, x_sample, cache_conv, state_gdn, cache_k, cache_v, attn_norm, mlp_norm, final_norm, a_w_in, a_conv_w, a_log, a_dt_bias, a_o_norm, a_w_out, kv_norm, w_kv, b_w_q, b_sinks, b_w_o, w_up, w_down):
    bp, lp, d = x_prompt.shape
    bs, ls, _ = x_sample.shape
    mp, ms = bp * lp, bs * ls
    m = mp + ms
    hb = GDN_HEADS_PER_STEP
    dqk, dvw = HA * DK, HA * DV
    conv_ch = 2 * dqk + dvw
    keep = CONV_W - 1
    assert a_w_in.shape[0] == 1 and b_w_q.shape[0] == 1 and lp % WINDOW == 0 and ls >= keep

    xp = x_prompt.reshape(mp, d)
    xs = x_sample.reshape(ms, d)
    tm = _pick_tile(m, ROW_TILES)
    tm2 = _pick_tile(math.gcd(mp, ms), ROW_TILES[1:])
    tm_p = _pick_tile(mp, PROMPT_ROW_TILES)
    tm_s = _pick_tile(ms, ROW_TILES[1:])

    w_in = a_w_in[0].astype(BF16)
    w_gate = _gate_layout(w_in[:, conv_ch + dvw:], hb)
    g0 = attn_norm[0].reshape(1, d)
    tn_in = _pick_tile(conv_ch + dvw, IN_PROJ_COL_TILES)
    zeros_gate = jnp.zeros((HA,), F32)
    alog_row = _gate_layout(jnp.concatenate([zeros_gate, a_log[0]]), hb).reshape(1, -1)
    dt_row = _gate_layout(jnp.concatenate([zeros_gate, a_dt_bias[0]]), hb).reshape(1, -1)
    onorm_row = a_o_norm[0].reshape(1, DV)
    c_p, c_s = min(CHUNK, lp), min(CHUNK, ls)
    streams = (("prompt", xp, tm_p, bp, lp, c_p, 1, _pick_tile(lp // c_p, (GDN_PROMPT_BLOCKS_PER_STEP, 1)),
                jnp.zeros((bp, keep, conv_ch), F32), jnp.zeros((bp, HA, DK, DV), F32)),
               ("sample", xs, tm_s, bs, ls, c_s, _pick_tile(bs, (GDN_SAMPLE_SEQS_PER_STEP, 1)) if ls == c_s else 1,
                1, cache_conv[0], state_gdn[0]))
    mixed = []
    for tag, xr, tmr, nbatch, length, c_blk, nseq, nsub, conv_prev, s0 in streams:
        qkvz = _matmul(xr, w_in, gain=g0, tm=tmr, tn=tn_in, name="gdn_in_proj_" + tag)
        gates = _matmul(xr, w_gate, gain=g0, tm=tmr, tn=w_gate.shape[1], name="gdn_gate_proj_" + tag)
        mixed.append(_gdn(qkvz, gates, alog_row, dt_row, conv_prev, a_conv_w[0], onorm_row, s0,
                          batch=nbatch, length=length, c=c_blk, hb=hb, nseq=nseq, nsub=nsub,
                          name="gdn_" + tag))
    (o_p, gdn_p, *hist_p), (o_s, gdn_s, *hist_s) = mixed
    tn_o = _pick_tile(d, COL_TILES)
    x = _matmul((o_p, o_s), a_w_out[0], res=(xp, xs), tm=tm2, tn=tn_o, weights_outer=True,
                name="gdn_out_proj")
    tf = _pick_tile(w_up.shape[2], COL_TILES)
    x = _mlp(x, mlp_norm[0].reshape(1, d), w_up, w_down, 0, tm=tm, tf=tf, name="mlp0")

    cos_p, sin_p = _rope_tables(jnp.arange(lp))
    cos_s, sin_s = _rope_tables(PAST_LEN + jnp.arange(ls))
    rope = (jnp.concatenate([cos_p] * bp + [cos_s] * bs, axis=0),
            jnp.concatenate([sin_p] * bp + [sin_s] * bs, axis=0))
    kvw = N_KV * HD
    kv = _matmul(x, w_kv, gain=kv_norm.reshape(1, d), rope=rope, rope_cols=kvw,
                 tm=tm, tn=2 * kvw, name="kv_proj")
    q = _matmul(x, b_w_q[0], gain=attn_norm[1].reshape(1, d), rope=rope,
                rope_cols=N_Q * HD, out_dtype=BF16, tm=tm, tn=tn_o, name="q_proj")
    sinks = b_sinks[0].astype(F32)
    a_p = _attention(sinks, q, kv, kv, 0, 1, kv, row0=0, batch=bp, length=lp, tq=WINDOW, nb=1,
                     chunk_mask=True, prev_transposed=False, name="attn_prompt")
    nb = _pick_tile(bs, (ATTN_SAMPLE_BATCHES_PER_STEP, 2, 1))
    a_s = _attention(sinks, q, jnp.transpose(cache_k, (0, 2, 3, 1)), jnp.transpose(cache_v, (0, 2, 3, 1)),
                     0, 0, kv, row0=mp, batch=bs, length=ls, tq=ls, nb=nb,
                     chunk_mask=False, prev_transposed=True, name="attn_sample")
    x = _matmul((a_p, a_s), b_w_o[0], res=x, tm=tm2, tn=tn_o, weights_outer=True, name="attn_out_proj")
    y_p, y_s = _mlp(x, mlp_norm[1].reshape(1, d), w_up, w_down, 1, final_gain=final_norm.reshape(1, d),
                    split_row=mp, tm=tm, tf=tf, name="mlp1")

    y_prompt = y_p.reshape(bp, lp, d)
    y_sample = y_s.reshape(bs, ls, d)
    conv_p = jnp.concatenate([t[:, SUBLANES - keep:] for t in hist_p], axis=-1)[None]
    conv_s = jnp.concatenate([t[:, SUBLANES - keep:] for t in hist_s], axis=-1)[None]
    kv_p = jnp.stack([kv[(b + 1) * lp - WINDOW:(b + 1) * lp] for b in range(bp)])
    kv_s = kv[mp:].reshape(bs, ls, 2 * kvw)
    k_p = kv_p[..., :kvw].reshape(bp, WINDOW, N_KV, HD)
    v_p = kv_p[..., kvw:].reshape(bp, WINDOW, N_KV, HD)
    k_s = kv_s[..., :kvw].reshape(bs, ls, N_KV, HD)
    v_s = kv_s[..., kvw:].reshape(bs, ls, N_KV, HD)
    return (y_prompt, y_sample, conv_p, gdn_p[None], k_p, v_p, conv_s, gdn_s[None], k_s, v_s)
```

```python
import functools
import math

import jax
import jax.numpy as jnp
from jax import lax
from jax.experimental import pallas as pl
from jax.experimental.pallas import tpu as pltpu

F32 = jnp.float32
BF16 = jnp.bfloat16

EPS = 1e-6
CHUNK = 64
WINDOW = 128
PAST_LEN = 4096
ROPE_THETA = 10000.0
HA, DK, DV = 16, 128, 128
N_Q, N_KV, HD = 32, 4, 64
CONV_W = 4
LANES = 128
SUBLANES = 8
VMEM_LIMIT = 60 * 1024 * 1024
GDN_HEADS_PER_STEP = 16
GDN_SAMPLE_SEQS_PER_STEP = 4
GDN_PROMPT_BLOCKS_PER_STEP = 4
ATTN_SAMPLE_BATCHES_PER_STEP = 4
ROW_TILES = (1088, 512, 256, 128, 64, 32, 16)
PROMPT_ROW_TILES = (1024,) + ROW_TILES[1:]
COL_TILES = (1024, 512, 256, 128)
IN_PROJ_COL_TILES = (2048,) + COL_TILES


def _pick_tile(n, candidates):
    for c in candidates:
        if n % c == 0:
            return c
    return n


def _dot(a, b, dims=(((1,), (0,)), ((), ()))):
    return lax.dot_general(a.astype(BF16), b.astype(BF16), dims, preferred_element_type=F32)


_NT = (((1,), (1,)), ((), ()))
_TN = (((0,), (0,)), ((), ()))


def _rms_scale(x):
    return lax.rsqrt(jnp.mean(x * x, axis=-1, keepdims=True) + EPS)


def _silu(x):
    h = 0.5 * x
    return h + h * jnp.tanh(h)


def _rope_slab(y, cos, sin_signed):
    lane = lax.broadcasted_iota(jnp.int32, y.shape, 1)
    half = HD // 2
    up = pltpu.roll(y, half, 1)
    down = pltpu.roll(y, LANES - half, 1)
    swapped = jnp.where((lane % HD) < half, down, up)
    return y * cos + swapped * sin_signed


def _project(lhs, w_ref, res_ref, cos_ref, sin_ref, o_ref, *, rope_slabs, n_slabs):
    if not rope_slabs:
        y = jnp.dot(lhs, w_ref[...].astype(BF16), preferred_element_type=F32)
        if res_ref is not None:
            y = y + res_ref[...]
        o_ref[...] = y.astype(o_ref.dtype)
        return
    cos = cos_ref[...]
    sin = sin_ref[...]
    n_chunk = 2 if n_slabs % 2 == 0 else 1
    for c0 in range(0, n_slabs, n_chunk):
        cols = slice(c0 * LANES, (c0 + n_chunk) * LANES)
        y = jnp.dot(lhs, w_ref[:, cols].astype(BF16), preferred_element_type=F32)
        if res_ref is not None:
            y = y + res_ref[:, cols]
        for s in range(n_chunk):
            ys = y[:, s * LANES:(s + 1) * LANES]
            if c0 + s < rope_slabs:
                ys = _rope_slab(ys, cos, sin)
            o_ref[:, (c0 + s) * LANES:(c0 + s + 1) * LANES] = ys.astype(o_ref.dtype)


def _matmul_kernel(*refs, n_x, n_res, split_tile, row_axis, has_norm, rope_slabs, n_slabs, resident_w):
    it = iter(refs)
    x_refs = [next(it) for _ in range(n_x)]
    g_ref = next(it) if has_norm else None
    w_ref = next(it)
    res_refs = [next(it) for _ in range(n_res)]
    cos_ref = next(it) if rope_slabs else None
    sin_ref = next(it) if rope_slabs else None
    o_ref = next(it)
    xn_ref = next(it) if has_norm else None
    wb_ref = next(it) if resident_w else None

    if resident_w:
        @pl.when(pl.program_id(row_axis) == 0)
        def _():
            wb_ref[...] = w_ref[...].astype(BF16)
        w_ref = wb_ref

    if has_norm:
        @pl.when(pl.program_id(1) == 0)
        def _():
            x = x_refs[0][...]
            xn_ref[...] = (x * _rms_scale(x) * g_ref[...]).astype(BF16)

    def emit(lhs_ref, res_ref):
        _project(lhs_ref[...], w_ref, res_ref, cos_ref, sin_ref, o_ref, rope_slabs=rope_slabs, n_slabs=n_slabs)

    if n_x == 1:
        emit(xn_ref if has_norm else x_refs[0], res_refs[0] if n_res else None)
    else:
        i = pl.program_id(row_axis)
        pl.when(i < split_tile)(functools.partial(emit, x_refs[0], res_refs[0] if n_res else None))
        pl.when(i >= split_tile)(functools.partial(emit, x_refs[1], res_refs[-1] if n_res else None))


def _matmul(xs, w, *, gain=None, res=None, rope=None, rope_cols=0, out_dtype=F32, tm, tn, name,
            weights_outer=False):
    xs = xs if isinstance(xs, (tuple, list)) else (xs,)
    ress = () if res is None else (res if isinstance(res, (tuple, list)) else (res,))
    k = xs[0].shape[1]
    m = sum(x.shape[0] for x in xs)
    n = (w.shape[1] // tn) * tn
    assert all(x.shape[0] % tm == 0 for x in xs) and w.shape[0] == k
    has_norm = gain is not None
    assert not (has_norm and len(xs) > 1)
    assert len(ress) <= len(xs) and all(r.shape[0] == x.shape[0] for r, x in zip(ress, xs) if len(ress) > 1)
    split_tile = xs[0].shape[0] // tm
    rope_slabs = 0
    if rope is not None:
        assert rope_cols == n or tn == n
        rope_slabs = min(rope_cols, tn) // LANES
    assert not (has_norm and weights_outer)
    resident_w = weights_outer and n == tn and w.dtype != BF16

    def spec(shape, f):
        return pl.BlockSpec(shape, (lambda a, b: f(b, a)) if weights_outer else f)

    def row_specs(arrays, width, col):
        if len(arrays) == 1:
            return [spec((tm, width), lambda i, j: (i, col(j)))]
        return [spec((tm, width), lambda i, j: (jnp.minimum(i, split_tile - 1), col(j))),
                spec((tm, width), lambda i, j: (jnp.maximum(i - split_tile, 0), col(j)))]

    in_specs = row_specs(xs, k, lambda j: 0)
    args = list(xs)
    if has_norm:
        in_specs.append(spec((1, k), lambda i, j: (0, 0)))
        args.append(gain)
    in_specs.append(pl.BlockSpec((k, tn), lambda a, b: (0, 0), pipeline_mode=pl.Buffered(1)) if resident_w
                    else spec((k, tn), lambda i, j: (0, j)))
    args.append(w)
    in_specs += row_specs(ress, tn, lambda j: j) if ress else []
    args += list(ress)
    if rope_slabs:
        in_specs += [spec((tm, LANES), lambda i, j: (i, 0))] * 2
        args += list(rope)
    kern = functools.partial(_matmul_kernel, n_x=len(xs), n_res=len(ress), split_tile=split_tile,
                             row_axis=1 if weights_outer else 0, has_norm=has_norm,
                             rope_slabs=rope_slabs, n_slabs=tn // LANES, resident_w=resident_w)
    grid = (n // tn, m // tm) if weights_outer else (m // tm, n // tn)
    return pl.pallas_call(
        kern,
        grid=grid,
        in_specs=in_specs,
        out_specs=spec((tm, tn), lambda i, j: (i, j)),
        out_shape=jax.ShapeDtypeStruct((m, n), out_dtype),
        scratch_shapes=([pltpu.VMEM((tm, k), BF16)] if has_norm else [])
                       + ([pltpu.VMEM((k, tn), BF16)] if resident_w else []),
        compiler_params=pltpu.CompilerParams(
            dimension_semantics=("arbitrary", "arbitrary"), vmem_limit_bytes=VMEM_LIMIT),
        name=name,
    )(*args)


def _mlp_kernel(*refs, tm, n_tiles, split_row, final_norm, prefetch_step):
    x_hbm, g_ref, wu_ref, wd_ref = refs[:4]
    fg_ref = refs[4] if final_norm else None
    *out_refs, acc_ref, hn_ref, sem_in, sem_out = refs[5 if final_norm else 4:]
    i = pl.program_id(0)
    f = pl.program_id(1)
    slot = i % 2

    def fetch(t):
        return pltpu.make_async_copy(x_hbm.at[pl.ds(t * tm, tm)], acc_ref.at[t % 2], sem_in.at[t % 2])

    def writebacks(t):
        r0, r1 = t * tm, (t + 1) * tm
        s = t % 2
        cps = []
        if r0 < split_row:
            n = min(r1, split_row) - r0
            cps.append(pltpu.make_async_copy(acc_ref.at[s, 0:n], out_refs[0].at[r0:r0 + n], sem_out.at[s, 0]))
        if r1 > split_row:
            a = max(r0, split_row)
            cps.append(pltpu.make_async_copy(acc_ref.at[s, a - r0:tm],
                                             out_refs[1].at[a - split_row:r1 - split_row], sem_out.at[s, 1]))
        return cps

    def start_tile(t):
        for cp in writebacks(t):
            cp.start()

    def wait_tile(t):
        for cp in writebacks(t):
            cp.wait()

    @pl.when(f == 0)
    def _():
        @pl.when(i == 0)
        def _():
            fetch(i).start()
        fetch(i).wait()
        x = acc_ref[slot]
        hn_ref[...] = (x * _rms_scale(x) * g_ref[...]).astype(BF16)

    u = jnp.dot(hn_ref[...], wu_ref[...].astype(BF16), preferred_element_type=F32)
    a = jnp.square(jnp.maximum(u, 0.0)).astype(BF16)
    acc_ref[slot] += jnp.dot(a, wd_ref[...].astype(BF16), preferred_element_type=F32)

    @pl.when((f == prefetch_step) & (i + 1 < n_tiles))
    def _():
        for t in range(n_tiles - 2):
            pl.when(i == t + 1)(functools.partial(wait_tile, t))
        fetch(i + 1).start()

    @pl.when(f == pl.num_programs(1) - 1)
    def _():
        if final_norm:
            y = acc_ref[slot]
            acc_ref[slot] = y * _rms_scale(y) * fg_ref[...]
        for t in range(n_tiles):
            pl.when(i == t)(functools.partial(start_tile, t))

        @pl.when(i == n_tiles - 1)
        def _():
            for t in range(max(n_tiles - 2, 0), n_tiles):
                wait_tile(t)


def _mlp(x, gain, w_up, w_down, layer, *, final_gain=None, split_row=None, tm, tf, name):
    m, d = x.shape
    dff = w_up.shape[2]
    n_tiles, nf = m // tm, dff // tf
    final_norm = final_gain is not None
    prefetch_step = 1
    assert m % tm == 0 and dff % tf == 0 and nf > prefetch_step + 1
    assert split_row is None or (split_row % SUBLANES == 0 and 0 < split_row < m)
    in_specs = [
        pl.BlockSpec(memory_space=pl.ANY),
        pl.BlockSpec((1, d), lambda i, f: (0, 0)),
        pl.BlockSpec((None, d, tf), lambda i, f: (layer, 0, f)),
        pl.BlockSpec((None, tf, d), lambda i, f: (layer, f, 0)),
    ]
    args = [x, gain, w_up, w_down]
    if final_norm:
        in_specs.append(pl.BlockSpec((1, d), lambda i, f: (0, 0)))
        args.append(final_gain)
    out_rows = [m] if split_row is None else [split_row, m - split_row]
    outs = pl.pallas_call(
        functools.partial(_mlp_kernel, tm=tm, n_tiles=n_tiles, split_row=out_rows[0], final_norm=final_norm,
                          prefetch_step=prefetch_step),
        grid=(n_tiles, nf),
        in_specs=in_specs,
        out_specs=[pl.BlockSpec(memory_space=pl.ANY)] * len(out_rows),
        out_shape=[jax.ShapeDtypeStruct((r, d), F32) for r in out_rows],
        scratch_shapes=[pltpu.VMEM((2, tm, d), F32), pltpu.VMEM((tm, d), BF16),
                        pltpu.SemaphoreType.DMA((2,)), pltpu.SemaphoreType.DMA((2, 2))],
        compiler_params=pltpu.CompilerParams(
            dimension_semantics=("arbitrary", "arbitrary"), vmem_limit_bytes=VMEM_LIMIT),
        name=name,
    )(*args)
    return outs[0] if split_row is None else outs


def _gdn_kernel(xq_ref, xk_ref, xv_ref, z_ref, gates_ref, alog_ref, dt_ref,
                pq_ref, pk_ref, pv_ref, wq_ref, wk_ref, wv_ref, onorm_ref, s0_ref,
                o_ref, s_ref, hq_ref, hk_ref, hv_ref, padq_ref, padk_ref, padv_ref, *, c, hb, nseq, nsub):
    n = pl.program_id(2)
    halo = SUBLANES
    seqs = range(nseq)
    rows_seq = nsub * c

    @pl.when(n == 0)
    def _():
        for pad_ref, prev_ref in ((padq_ref, pq_ref), (padk_ref, pk_ref), (padv_ref, pv_ref)):
            for sq in seqs:
                pad_ref[sq, 0:halo, :] = jnp.zeros((halo, pad_ref.shape[2]), F32)
                pad_ref[sq, halo - (CONV_W - 1):halo, :] = prev_ref[sq]
        s_ref[...] = s0_ref[...]

    def conv(x_ref, pad_ref, w_ref, hist_ref, sq):
        pad_ref[sq, halo:halo + rows_seq, :] = x_ref[sq * rows_seq:(sq + 1) * rows_seq, :]
        xp = pad_ref[sq]
        x1 = pltpu.roll(xp, 1, 0)
        a = xp * w_ref[3:4, :] + x1 * w_ref[2:3, :]
        b = xp * w_ref[1:2, :] + x1 * w_ref[0:1, :]
        acc = (a + pltpu.roll(b, 2, 0))[halo:, :]
        tail = xp[rows_seq:rows_seq + halo, :]
        pad_ref[sq, 0:halo, :] = tail
        hist_ref[sq] = tail
        return _silu(acc)

    cq = [conv(xq_ref, padq_ref, wq_ref, hq_ref, sq) for sq in seqs]
    ck = [conv(xk_ref, padk_ref, wk_ref, hk_ref, sq) for sq in seqs]
    cv = [conv(xv_ref, padv_ref, wv_ref, hv_ref, sq) for sq in seqs]

    row = lax.broadcasted_iota(jnp.int32, (c, c), 0)
    col = lax.broadcasted_iota(jnp.int32, (c, c), 1)
    incl = row >= col
    strict = row > col
    tril = jnp.where(incl, 1.0, 0.0).astype(F32)
    r128 = lax.broadcasted_iota(jnp.int32, (LANES, LANES), 0)
    c128 = lax.broadcasted_iota(jnp.int32, (LANES, LANES), 1)
    eye128 = jnp.where(r128 == c128, 1.0, 0.0).astype(F32)
    blocks = [(sq, sb) for sq in seqs for sb in range(nsub)]
    brow = {blk: slice(blk[0] * rows_seq + blk[1] * c, blk[0] * rows_seq + (blk[1] + 1) * c) for blk in blocks}
    beta_all, gc_all, gc_t = {}, {}, {}
    for blk in blocks:
        gates = gates_ref[brow[blk], :]
        beta_all[blk] = 0.5 + 0.5 * jnp.tanh(0.5 * gates)
        ga = gates + dt_ref[...]
        softplus = jnp.maximum(ga, 0.0) + jnp.log(1.0 + jnp.exp(-jnp.abs(ga)))
        g_all = -jnp.exp(alog_ref[...]) * softplus
        gc_all[blk] = jnp.dot(tril, g_all, preferred_element_type=F32,
                              precision=lax.Precision.HIGHEST)
        gc_t[blk] = lax.dot_general(eye128, gc_all[blk], _NT, preferred_element_type=F32,
                                    precision=lax.Precision.HIGHEST)

    units = [(blk, h) for blk in blocks for h in range(hb)]
    un = range(len(units))
    sls = [slice(h * LANES, (h + 1) * LANES) for _, h in units]
    srow = [slice(blk[1] * c, (blk[1] + 1) * c) for blk, _ in units]
    q = [cq[blk[0]][srow[u], sls[u]] for u, (blk, h) in enumerate(units)]
    k = [ck[blk[0]][srow[u], sls[u]] for u, (blk, h) in enumerate(units)]
    v = [cv[blk[0]][srow[u], sls[u]] for u, (blk, h) in enumerate(units)]
    q = [x * (lax.rsqrt(jnp.sum(x * x, axis=-1, keepdims=True) + EPS) * (DK ** -0.5)) for x in q]
    k = [x * lax.rsqrt(jnp.sum(x * x, axis=-1, keepdims=True) + EPS) for x in k]
    bcol = [beta_all[blk][:, h:h + 1] for blk, h in units]
    gcol = [gc_all[blk][:, hb + h:hb + h + 1] for blk, h in units]
    grow = [gc_t[blk][hb + h:hb + h + 1, :] for blk, h in units]
    glast = [gc_all[blk][c - 1:c, hb + h:hb + h + 1] for blk, h in units]
    decay = [jnp.exp(jnp.where(incl, gcol[u] - grow[u], -jnp.inf)) for u in un]
    kb = [k[u] * bcol[u] for u in un]
    eye = jnp.where(row == col, 1.0, 0.0).astype(F32)
    qkk = [_dot(jnp.concatenate([q[u], kb[u]], axis=0), k[u], _NT) for u in un]
    qk = [qkk[u][:c] * decay[u] for u in un]
    p = [-jnp.where(strict, qkk[u][c:] * decay[u], 0.0) for u in un]
    t = [eye + x for x in p]
    levels = max(int(math.ceil(math.log2(c))) - 1, 0)
    if levels:
        p = [_dot(x, x) for x in p]
    for _ in range(1, levels):
        r = [_dot(jnp.concatenate([p[u], t[u]], axis=0), p[u]) for u in un]
        t = [t[u] + r[u][c:] for u in un]
        p = [r[u][:c] for u in un]
    if levels:
        t = [t[u] + _dot(t[u], p[u]) for u in un]
    egc = [jnp.exp(x) for x in gcol]
    rhs = [jnp.concatenate([v[u] * bcol[u], kb[u] * egc[u]], axis=-1) for u in un]
    sol = [_dot(t[u], rhs[u]) for u in un]
    wqe = [jnp.concatenate([sol[u][:, DV:], q[u] * egc[u]], axis=0) for u in un]
    kd = [k[u] * jnp.exp(glast[u] - gcol[u]) for u in un]
    elast = [jnp.exp(x) for x in glast]

    onorm = onorm_ref[...]
    s = {(sq, h): s_ref[sq, h] for sq in seqs for h in range(hb)}
    for sb in range(nsub):
        cur = [u for u in un if units[u][0][1] == sb]
        key = {u: (units[u][0][0], units[u][1]) for u in cur}
        wq = {u: _dot(wqe[u], s[key[u]]) for u in cur}
        v_new = {u: sol[u][:, :DV] - wq[u][:c] for u in cur}
        o = {u: wq[u][c:] + _dot(qk[u], v_new[u]) for u in cur}
        for u in cur:
            s[key[u]] = s[key[u]] * elast[u] + _dot(kd[u], v_new[u], _TN)
        for u in cur:
            oh = o[u] * _rms_scale(o[u]) * onorm
            rows = brow[units[u][0]]
            o_ref[rows, sls[u]] = (oh * _silu(z_ref[rows, sls[u]])).astype(o_ref.dtype)
    for (sq, h), val in s.items():
        s_ref[sq, h] = val


def _gdn(qkvz, gates, alog_row, dt_row, conv_prev, conv_w, onorm_row, s0,
         *, batch, length, c, hb, nseq, nsub, name):
    nblk = length // (nsub * c)
    ng = HA // hb
    hw = hb * LANES
    rows_step = nseq * nsub * c
    assert length % (nsub * c) == 0 and c >= SUBLANES and CONV_W == 4
    assert batch % nseq == 0 and (nseq == 1 or nblk == 1)

    def rows(b, g, n):
        return b * nblk + n

    x_spec = lambda part: pl.BlockSpec((rows_step, hw), lambda b, g, n: (rows(b, g, n), part * ng + g))
    prev_spec = lambda part: pl.BlockSpec((nseq, CONV_W - 1, hw), lambda b, g, n: (b, 0, part * ng + g))
    w_spec = lambda part: pl.BlockSpec((CONV_W, hw), lambda b, g, n: (0, part * ng + g))
    gate_row_spec = pl.BlockSpec((1, LANES), lambda b, g, n: (0, g))
    hist_spec = pl.BlockSpec((nseq, SUBLANES, hw), lambda b, g, n: (b, 0, g))
    hist_shape = jax.ShapeDtypeStruct((batch, SUBLANES, HA * LANES), F32)
    state_spec = pl.BlockSpec((nseq, hb, DK, DV), lambda b, g, n: (b, g, 0, 0))
    in_specs = [
        x_spec(0), x_spec(1), x_spec(2), x_spec(3),
        pl.BlockSpec((rows_step, LANES), lambda b, g, n: (rows(b, g, n), g)),
        gate_row_spec, gate_row_spec,
        prev_spec(0), prev_spec(1), prev_spec(2),
        w_spec(0), w_spec(1), w_spec(2),
        pl.BlockSpec((1, LANES), lambda b, g, n: (0, 0)),
        state_spec,
    ]
    out_specs = [
        pl.BlockSpec((rows_step, hw), lambda b, g, n: (b * nblk + n, g)),
        state_spec,
        hist_spec, hist_spec, hist_spec,
    ]
    return pl.pallas_call(
        functools.partial(_gdn_kernel, c=c, hb=hb, nseq=nseq, nsub=nsub),
        grid=(batch // nseq, ng, nblk),
        in_specs=in_specs,
        out_specs=out_specs,
        out_shape=[jax.ShapeDtypeStruct((batch * length, HA * DV), BF16),
                   jax.ShapeDtypeStruct((batch, HA, DK, DV), F32),
                   hist_shape, hist_shape, hist_shape],
        scratch_shapes=[pltpu.VMEM((nseq, nsub * c + SUBLANES, hw), F32)] * 3,
        compiler_params=pltpu.CompilerParams(
            dimension_semantics=("parallel", "parallel", "arbitrary"),
            vmem_limit_bytes=VMEM_LIMIT),
        name=name,
    )(qkvz, qkvz, qkvz, qkvz, gates, alog_row, dt_row,
      conv_prev, conv_prev, conv_prev, conv_w, conv_w, conv_w, onorm_row, s0)


def _attn_kernel(sinks_ref, q_ref, kp_ref, vp_ref, kc_ref, vc_ref, o_ref, *, tq, nb, chunk_mask,
                 prev_transposed):
    i = pl.program_id(1)
    nk = WINDOW + tq
    lane_q = lax.broadcasted_iota(jnp.int32, (tq, LANES), 1)
    if chunk_mask:
        qc = lax.broadcasted_iota(jnp.int32, (tq, nk), 0) // CHUNK
        kc = lax.broadcasted_iota(jnp.int32, (tq, nk), 1) // CHUNK
        w_ch = WINDOW // CHUNK
        first_kc = jnp.where(i > 0, 0, w_ch)
        valid = (kc >= jnp.maximum(qc, first_kc)) & (kc <= qc + w_ch)
    scale = HD ** -0.5
    assert math.log2(HD) % 2 == 0
    per_slab = LANES // HD
    group = N_Q // N_KV
    slabs_per_kv = group // per_slab
    half_masks = [(lane_q >= r * HD) & (lane_q < (r + 1) * HD) for r in range(per_slab)]

    units = [(b, h) for b in range(nb) for h in range(N_KV)]
    kdup, vdup, qs = [], [], []
    for b, h in units:
        ksl = slice((h // per_slab) * LANES, (h // per_slab + 1) * LANES)

        def dup(prev_ref, cur_ref):
            if prev_transposed:
                xt = prev_ref[b, h]
                front = [jnp.concatenate([xt, xt], axis=0).T]
                slab = cur_ref[b * tq:(b + 1) * tq, ksl]
            else:
                front = []
                slab = jnp.concatenate([prev_ref[b * WINDOW:(b + 1) * WINDOW, ksl],
                                        cur_ref[b * tq:(b + 1) * tq, ksl]], axis=0)
            lane = lax.broadcasted_iota(jnp.int32, slab.shape, 1)
            first = (lane < HD) == (h % per_slab == 0)
            own = jnp.where(first, slab, pltpu.roll(slab, HD, 1))
            return jnp.concatenate(front + [own], axis=0).astype(BF16)

        kdup.append(dup(kp_ref, kc_ref))
        vdup.append(dup(vp_ref, vc_ref))
        pieces = []
        for s in range(slabs_per_kv):
            slab = h * slabs_per_kv + s
            q2 = q_ref[b * tq:(b + 1) * tq, slab * LANES:(slab + 1) * LANES]
            pieces += [jnp.where(mk, q2, jnp.zeros_like(q2)) for mk in half_masks]
        qs.append(jnp.concatenate(pieces, axis=0) * scale)
    sc = [lax.dot_general(qs[u], kdup[u], _NT, preferred_element_type=F32) for u in range(len(units))]
    ps = []
    for u, (b, h) in enumerate(units):
        blocks = []
        for r in range(group):
            sr = sc[u][r * tq:(r + 1) * tq]
            if chunk_mask:
                sr = jnp.where(valid, sr, -jnp.inf)
            sk = sinks_ref[h * group + r]
            mx = jnp.maximum(jnp.max(sr, axis=-1, keepdims=True), sk)
            e = jnp.exp(sr - mx)
            den = jnp.sum(e, axis=-1, keepdims=True) + jnp.exp(sk - mx)
            blocks.append((e / den).astype(BF16))
        ps.append(jnp.concatenate(blocks, axis=0))
    pv = [jnp.dot(ps[u], vdup[u], preferred_element_type=F32) for u in range(len(units))]
    for u, (b, h) in enumerate(units):
        for s in range(slabs_per_kv):
            slab = h * slabs_per_kv + s
            halves = [pv[u][(s * per_slab + r) * tq:(s * per_slab + r + 1) * tq] for r in range(per_slab)]
            out = halves[-1]
            for r in range(per_slab - 2, -1, -1):
                out = jnp.where(half_masks[r], halves[r], out)
            o_ref[b * tq:(b + 1) * tq, slab * LANES:(slab + 1) * LANES] = out.astype(o_ref.dtype)


def _attention(sinks, q, kprev, vprev, kprev_col, vprev_col, kv, *, row0, batch, length,
               tq, nb, chunk_mask, prev_transposed, name):
    nq = length // tq
    kvw = N_KV * HD
    assert row0 % (nb * tq) == 0 and length % tq == 0 and batch % nb == 0 and (nb == 1 or nq == 1)
    rb0 = row0 // (nb * tq)

    def cur(b, i):
        return rb0 + b * nq + i

    def prev(b, i):
        return b * nq + jnp.maximum(i - 1, 0)

    in_specs = [
        pl.BlockSpec(memory_space=pltpu.SMEM),
        pl.BlockSpec((nb * tq, N_Q * HD), lambda b, i: (cur(b, i), 0)),
        *([pl.BlockSpec((nb, N_KV, HD, WINDOW), lambda b, i: (b, 0, 0, 0))] * 2 if prev_transposed else
          [pl.BlockSpec((nb * WINDOW, kvw), lambda b, i: (prev(b, i), kprev_col)),
           pl.BlockSpec((nb * WINDOW, kvw), lambda b, i: (prev(b, i), vprev_col))]),
        pl.BlockSpec((nb * tq, kvw), lambda b, i: (cur(b, i), 0)),
        pl.BlockSpec((nb * tq, kvw), lambda b, i: (cur(b, i), 1)),
    ]
    return pl.pallas_call(
        functools.partial(_attn_kernel, tq=tq, nb=nb, chunk_mask=chunk_mask,
                          prev_transposed=prev_transposed),
        grid=(batch // nb, nq),
        in_specs=in_specs,
        out_specs=pl.BlockSpec((nb * tq, N_Q * HD), lambda b, i: (b * nq + i, 0)),
        out_shape=jax.ShapeDtypeStruct((batch * length, N_Q * HD), BF16),
        compiler_params=pltpu.CompilerParams(
            dimension_semantics=("parallel", "arbitrary"), vmem_limit_bytes=VMEM_LIMIT),
        name=name,
    )(sinks, q, kprev, vprev, kv, kv)


def _rope_tables(pos):
    half = HD // 2
    inv = 1.0 / (ROPE_THETA ** (jnp.arange(half, dtype=F32) / half))
    ang = pos.astype(F32)[:, None] * jnp.tile(inv, LANES // half)[None, :]
    sign = jnp.tile(jnp.concatenate([-jnp.ones((half,), F32), jnp.ones((half,), F32)]), LANES // HD)
    return jnp.cos(ang), jnp.sin(ang) * sign[None, :]


def _gate_layout(t, hb):
    lead = t.shape[:-1]
    ng = HA // hb
    if ng == 1:
        return jnp.pad(t, [(0, 0)] * len(lead) + [(0, LANES - 2 * HA)])
    beta = t[..., :HA].reshape(lead + (ng, hb))
    dec = t[..., HA:].reshape(lead + (ng, hb))
    pad = jnp.zeros(lead + (ng, LANES - 2 * hb), t.dtype)
    return jnp.concatenate([beta, dec, pad], axis=-1).reshape(lead + (ng * LANES,))


def kernel(x_prompt, x_sample, cache_conv, state_gdn, cache_k, cache_v, attn_norm, mlp_norm, final_norm, a_w_in, a_conv_w, a_log, a_dt_bias, a_o_norm, a_w_out, kv_norm, w_kv, b_w_q, b_sinks, b_w_o, w_up, w_down):
    bp, lp, d = x_prompt.shape
    bs, ls, _ = x_sample.shape
    mp, ms = bp * lp, bs * ls
    m = mp + ms
    hb = GDN_HEADS_PER_STEP
    dqk, dvw = HA * DK, HA * DV
    conv_ch = 2 * dqk + dvw
    keep = CONV_W - 1
    assert a_w_in.shape[0] == 1 and b_w_q.shape[0] == 1 and lp % WINDOW == 0 and ls >= keep

    xp = x_prompt.reshape(mp, d)
    xs = x_sample.reshape(ms, d)
    tm = _pick_tile(m, ROW_TILES)
    tm2 = _pick_tile(math.gcd(mp, ms), ROW_TILES[1:])
    tm_p = _pick_tile(mp, PROMPT_ROW_TILES)
    tm_s = _pick_tile(ms, ROW_TILES[1:])

    w_in = a_w_in[0].astype(BF16)
    w_gate = _gate_layout(w_in[:, conv_ch + dvw:], hb)
    g0 = attn_norm[0].reshape(1, d)
    tn_in = _pick_tile(conv_ch + dvw, IN_PROJ_COL_TILES)
    zeros_gate = jnp.zeros((HA,), F32)
    alog_row = _gate_layout(jnp.concatenate([zeros_gate, a_log[0]]), hb).reshape(1, -1)
    dt_row = _gate_layout(jnp.concatenate([zeros_gate, a_dt_bias[0]]), hb).reshape(1, -1)
    onorm_row = a_o_norm[0].reshape(1, DV)
    c_p, c_s = min(CHUNK, lp), min(CHUNK, ls)
    streams = (("prompt", xp, tm_p, bp, lp, c_p, 1, _pick_tile(lp // c_p, (GDN_PROMPT_BLOCKS_PER_STEP, 1)),
                jnp.zeros((bp, keep, conv_ch), F32), jnp.zeros((bp, HA, DK, DV), F32)),
               ("sample", xs, tm_s, bs, ls, c_s, _pick_tile(bs, (GDN_SAMPLE_SEQS_PER_STEP, 1)) if ls == c_s else 1,
                1, cache_conv[0], state_gdn[0]))
    mixed = []
    for tag, xr, tmr, nbatch, length, c_blk, nseq, nsub, conv_prev, s0 in streams:
        qkvz = _matmul(xr, w_in, gain=g0, tm=tmr, tn=tn_in, name="gdn_in_proj_" + tag)
        gates = _matmul(xr, w_gate, gain=g0, tm=tmr, tn=w_gate.shape[1], name="gdn_gate_proj_" + tag)
        mixed.append(_gdn(qkvz, gates, alog_row, dt_row, conv_prev, a_conv_w[0], onorm_row, s0,
                          batch=nbatch, length=length, c=c_blk, hb=hb, nseq=nseq, nsub=nsub,
                          name="gdn_" + tag))
    (o_p, gdn_p, *hist_p), (o_s, gdn_s, *hist_s) = mixed
    tn_o = _pick_tile(d, COL_TILES)
    x = _matmul((o_p, o_s), a_w_out[0], res=(xp, xs), tm=tm2, tn=d, weights_outer=True,
                name="gdn_out_proj")
    tf = _pick_tile(w_up.shape[2], COL_TILES)
    x = _mlp(x, mlp_norm[0].reshape(1, d), w_up, w_down, 0, tm=tm, tf=tf, name="mlp0")

    cos_p, sin_p = _rope_tables(jnp.arange(lp))
    cos_s, sin_s = _rope_tables(PAST_LEN + jnp.arange(ls))
    rope = (jnp.concatenate([cos_p] * bp + [cos_s] * bs, axis=0),
            jnp.concatenate([sin_p] * bp + [sin_s] * bs, axis=0))
    kvw = N_KV * HD
    kv = _matmul(x, w_kv, gain=kv_norm.reshape(1, d), rope=rope, rope_cols=kvw,
                 tm=tm, tn=2 * kvw, name="kv_proj")
    q = _matmul(x, b_w_q[0], gain=attn_norm[1].reshape(1, d), rope=rope,
                rope_cols=N_Q * HD, out_dtype=BF16, tm=tm, tn=tn_o, name="q_proj")
    sinks = b_sinks[0].astype(F32)
    a_p = _attention(sinks, q, kv, kv, 0, 1, kv, row0=0, batch=bp, length=lp, tq=WINDOW, nb=1,
                     chunk_mask=True, prev_transposed=False, name="attn_prompt")
    nb = _pick_tile(bs, (ATTN_SAMPLE_BATCHES_PER_STEP, 2, 1))
    a_s = _attention(sinks, q, jnp.transpose(cache_k, (0, 2, 3, 1)), jnp.transpose(cache_v, (0, 2, 3, 1)),
                     0, 0, kv, row0=mp, batch=bs, length=ls, tq=ls, nb=nb,
                     chunk_mask=False, prev_transposed=True, name="attn_sample")
    x = _matmul((a_p, a_s), b_w_o[0], res=x, tm=tm2, tn=d, weights_outer=True, name="attn_out_proj")
    y_p, y_s = _mlp(x, mlp_norm[1].reshape(1, d), w_up, w_down, 1, final_gain=final_norm.reshape(1, d),
                    split_row=mp, tm=tm, tf=tf, name="mlp1")

    y_prompt = y_p.reshape(bp, lp, d)
    y_sample = y_s.reshape(bs, ls, d)
    conv_p = jnp.concatenate([t[:, SUBLANES - keep:] for t in hist_p], axis=-1)[None]
    conv_s = jnp.concatenate([t[:, SUBLANES - keep:] for t in hist_s], axis=-1)[None]
    kv_p = jnp.stack([kv[(b + 1) * lp - WINDOW:(b + 1) * lp] for b in range(bp)])
    kv_s = kv[mp:].reshape(bs, ls, 2 * kvw)
    k_p = kv_p[..., :kvw].reshape(bp, WINDOW, N_KV, HD)
    v_p = kv_p[..., kvw:].reshape(bp, WINDOW, N_KV, HD)
    k_s = kv_s[..., :kvw].reshape(bs, ls, N_KV, HD)
    v_s = kv_s[..., kvw:].reshape(bs, ls, N_KV, HD)
    return (y_prompt, y_sample, conv_p, gdn_p[None], k_p, v_p, conv_s, gdn_s[None], k_s, v_s)
```

```python
import functools
import math

import jax
import jax.numpy as jnp
from jax import lax
from jax.experimental import pallas as pl
from jax.experimental.pallas import tpu as pltpu

F32 = jnp.float32
BF16 = jnp.bfloat16

EPS = 1e-6
CHUNK = 64
WINDOW = 128
PAST_LEN = 4096
ROPE_THETA = 10000.0
HA, DK, DV = 16, 128, 128
N_Q, N_KV, HD = 32, 4, 64
CONV_W = 4
LANES = 128
SUBLANES = 8
VMEM_LIMIT = 56 * 1024 * 1024
BIG_VMEM_LIMIT = 60 * 1024 * 1024
GDN_HEADS_PER_STEP = 16
GDN_SAMPLE_SEQS_PER_STEP = 4
GDN_PROMPT_BLOCKS_PER_STEP = 4
ATTN_SAMPLE_BATCHES_PER_STEP = 4
ROW_TILES = (1088, 512, 256, 128, 64, 32, 16)
PROMPT_ROW_TILES = (1024,) + ROW_TILES[1:]
COL_TILES = (1024, 512, 256, 128)
IN_PROJ_COL_TILES = (2048,) + COL_TILES


def _pick_tile(n, candidates):
    for c in candidates:
        if n % c == 0:
            return c
    return n


def _dot(a, b, dims=(((1,), (0,)), ((), ()))):
    return lax.dot_general(a.astype(BF16), b.astype(BF16), dims, preferred_element_type=F32)


_NT = (((1,), (1,)), ((), ()))
_TN = (((0,), (0,)), ((), ()))


def _rms_scale(x):
    return lax.rsqrt(jnp.mean(x * x, axis=-1, keepdims=True) + EPS)


def _silu(x):
    h = 0.5 * x
    return h + h * jnp.tanh(h)


def _rope_slab(y, cos, sin_signed):
    lane = lax.broadcasted_iota(jnp.int32, y.shape, 1)
    half = HD // 2
    up = pltpu.roll(y, half, 1)
    down = pltpu.roll(y, LANES - half, 1)
    swapped = jnp.where((lane % HD) < half, down, up)
    return y * cos + swapped * sin_signed


def _project(lhs, w_ref, res_ref, cos_ref, sin_ref, o_ref, *, rope_slabs, n_slabs):
    if not rope_slabs:
        y = jnp.dot(lhs, w_ref[...].astype(BF16), preferred_element_type=F32)
        if res_ref is not None:
            y = y + res_ref[...]
        o_ref[...] = y.astype(o_ref.dtype)
        return
    cos = cos_ref[...]
    sin = sin_ref[...]
    n_chunk = 2 if n_slabs % 2 == 0 else 1
    for c0 in range(0, n_slabs, n_chunk):
        cols = slice(c0 * LANES, (c0 + n_chunk) * LANES)
        y = jnp.dot(lhs, w_ref[:, cols].astype(BF16), preferred_element_type=F32)
        if res_ref is not None:
            y = y + res_ref[:, cols]
        for s in range(n_chunk):
            ys = y[:, s * LANES:(s + 1) * LANES]
            if c0 + s < rope_slabs:
                ys = _rope_slab(ys, cos, sin)
            o_ref[:, (c0 + s) * LANES:(c0 + s + 1) * LANES] = ys.astype(o_ref.dtype)


def _matmul_kernel(*refs, n_x, n_res, split_tile, row_axis, has_norm, rope_slabs, n_slabs, resident_w):
    it = iter(refs)
    x_refs = [next(it) for _ in range(n_x)]
    g_ref = next(it) if has_norm else None
    w_ref = next(it)
    res_refs = [next(it) for _ in range(n_res)]
    cos_ref = next(it) if rope_slabs else None
    sin_ref = next(it) if rope_slabs else None
    o_ref = next(it)
    xn_ref = next(it) if has_norm else None
    wb_ref = next(it) if resident_w else None

    if resident_w:
        @pl.when(pl.program_id(row_axis) == 0)
        def _():
            wb_ref[...] = w_ref[...].astype(BF16)
        w_ref = wb_ref

    if has_norm:
        @pl.when(pl.program_id(1) == 0)
        def _():
            x = x_refs[0][...]
            xn_ref[...] = (x * _rms_scale(x) * g_ref[...]).astype(BF16)

    def emit(lhs_ref, res_ref):
        _project(lhs_ref[...], w_ref, res_ref, cos_ref, sin_ref, o_ref, rope_slabs=rope_slabs, n_slabs=n_slabs)

    if n_x == 1:
        emit(xn_ref if has_norm else x_refs[0], res_refs[0] if n_res else None)
    else:
        i = pl.program_id(row_axis)
        pl.when(i < split_tile)(functools.partial(emit, x_refs[0], res_refs[0] if n_res else None))
        pl.when(i >= split_tile)(functools.partial(emit, x_refs[1], res_refs[-1] if n_res else None))


def _matmul(xs, w, *, gain=None, res=None, rope=None, rope_cols=0, out_dtype=F32, tm, tn, name,
            weights_outer=False):
    xs = xs if isinstance(xs, (tuple, list)) else (xs,)
    ress = () if res is None else (res if isinstance(res, (tuple, list)) else (res,))
    k = xs[0].shape[1]
    m = sum(x.shape[0] for x in xs)
    n = (w.shape[1] // tn) * tn
    assert all(x.shape[0] % tm == 0 for x in xs) and w.shape[0] == k
    has_norm = gain is not None
    assert not (has_norm and len(xs) > 1)
    assert len(ress) <= len(xs) and all(r.shape[0] == x.shape[0] for r, x in zip(ress, xs) if len(ress) > 1)
    split_tile = xs[0].shape[0] // tm
    rope_slabs = 0
    if rope is not None:
        assert rope_cols == n or tn == n
        rope_slabs = min(rope_cols, tn) // LANES
    assert not (has_norm and weights_outer)
    resident_w = weights_outer and n == tn and w.dtype != BF16

    def spec(shape, f):
        return pl.BlockSpec(shape, (lambda a, b: f(b, a)) if weights_outer else f)

    def row_specs(arrays, width, col):
        if len(arrays) == 1:
            return [spec((tm, width), lambda i, j: (i, col(j)))]
        return [spec((tm, width), lambda i, j: (jnp.minimum(i, split_tile - 1), col(j))),
                spec((tm, width), lambda i, j: (jnp.maximum(i - split_tile, 0), col(j)))]

    in_specs = row_specs(xs, k, lambda j: 0)
    args = list(xs)
    if has_norm:
        in_specs.append(spec((1, k), lambda i, j: (0, 0)))
        args.append(gain)
    in_specs.append(pl.BlockSpec((k, tn), lambda a, b: (0, 0), pipeline_mode=pl.Buffered(1)) if resident_w
                    else spec((k, tn), lambda i, j: (0, j)))
    args.append(w)
    in_specs += row_specs(ress, tn, lambda j: j) if ress else []
    args += list(ress)
    if rope_slabs:
        in_specs += [spec((tm, LANES), lambda i, j: (i, 0))] * 2
        args += list(rope)
    kern = functools.partial(_matmul_kernel, n_x=len(xs), n_res=len(ress), split_tile=split_tile,
                             row_axis=1 if weights_outer else 0, has_norm=has_norm,
                             rope_slabs=rope_slabs, n_slabs=tn // LANES, resident_w=resident_w)
    grid = (n // tn, m // tm) if weights_outer else (m // tm, n // tn)
    return pl.pallas_call(
        kern,
        grid=grid,
        in_specs=in_specs,
        out_specs=spec((tm, tn), lambda i, j: (i, j)),
        out_shape=jax.ShapeDtypeStruct((m, n), out_dtype),
        scratch_shapes=([pltpu.VMEM((tm, k), BF16)] if has_norm else [])
                       + ([pltpu.VMEM((k, tn), BF16)] if resident_w else []),
        compiler_params=pltpu.CompilerParams(
            dimension_semantics=("arbitrary" if resident_w else "parallel", "arbitrary"),
            vmem_limit_bytes=BIG_VMEM_LIMIT if resident_w else VMEM_LIMIT),
        name=name,
    )(*args)


def _mlp_kernel(*refs, tm, n_tiles, split_row, final_norm, prefetch_step):
    x_hbm, g_ref, wu_ref, wd_ref = refs[:4]
    fg_ref = refs[4] if final_norm else None
    *out_refs, acc_ref, hn_ref, sem_in, sem_out = refs[5 if final_norm else 4:]
    i = pl.program_id(0)
    f = pl.program_id(1)
    slot = i % 2

    def fetch(t):
        return pltpu.make_async_copy(x_hbm.at[pl.ds(t * tm, tm)], acc_ref.at[t % 2], sem_in.at[t % 2])

    def writebacks(t):
        r0, r1 = t * tm, (t + 1) * tm
        s = t % 2
        cps = []
        if r0 < split_row:
            n = min(r1, split_row) - r0
            cps.append(pltpu.make_async_copy(acc_ref.at[s, 0:n], out_refs[0].at[r0:r0 + n], sem_out.at[s, 0]))
        if r1 > split_row:
            a = max(r0, split_row)
            cps.append(pltpu.make_async_copy(acc_ref.at[s, a - r0:tm],
                                             out_refs[1].at[a - split_row:r1 - split_row], sem_out.at[s, 1]))
        return cps

    def start_tile(t):
        for cp in writebacks(t):
            cp.start()

    def wait_tile(t):
        for cp in writebacks(t):
            cp.wait()

    @pl.when(f == 0)
    def _():
        @pl.when(i == 0)
        def _():
            fetch(i).start()
        fetch(i).wait()
        x = acc_ref[slot]
        hn_ref[...] = (x * _rms_scale(x) * g_ref[...]).astype(BF16)

    u = jnp.dot(hn_ref[...], wu_ref[...].astype(BF16), preferred_element_type=F32)
    a = jnp.square(jnp.maximum(u, 0.0)).astype(BF16)
    acc_ref[slot] += jnp.dot(a, wd_ref[...].astype(BF16), preferred_element_type=F32)

    @pl.when((f == prefetch_step) & (i + 1 < n_tiles))
    def _():
        for t in range(n_tiles - 2):
            pl.when(i == t + 1)(functools.partial(wait_tile, t))
        fetch(i + 1).start()

    @pl.when(f == pl.num_programs(1) - 1)
    def _():
        if final_norm:
            y = acc_ref[slot]
            acc_ref[slot] = y * _rms_scale(y) * fg_ref[...]
        for t in range(n_tiles):
            pl.when(i == t)(functools.partial(start_tile, t))

        @pl.when(i == n_tiles - 1)
        def _():
            for t in range(max(n_tiles - 2, 0), n_tiles):
                wait_tile(t)


def _mlp(x, gain, w_up, w_down, layer, *, final_gain=None, split_row=None, tm, tf, name):
    m, d = x.shape
    dff = w_up.shape[2]
    n_tiles, nf = m // tm, dff // tf
    final_norm = final_gain is not None
    prefetch_step = 1
    assert m % tm == 0 and dff % tf == 0 and nf > prefetch_step + 1
    assert split_row is None or (split_row % SUBLANES == 0 and 0 < split_row < m)
    in_specs = [
        pl.BlockSpec(memory_space=pl.ANY),
        pl.BlockSpec((1, d), lambda i, f: (0, 0)),
        pl.BlockSpec((None, d, tf), lambda i, f: (layer, 0, f)),
        pl.BlockSpec((None, tf, d), lambda i, f: (layer, f, 0)),
    ]
    args = [x, gain, w_up, w_down]
    if final_norm:
        in_specs.append(pl.BlockSpec((1, d), lambda i, f: (0, 0)))
        args.append(final_gain)
    out_rows = [m] if split_row is None else [split_row, m - split_row]
    outs = pl.pallas_call(
        functools.partial(_mlp_kernel, tm=tm, n_tiles=n_tiles, split_row=out_rows[0], final_norm=final_norm,
                          prefetch_step=prefetch_step),
        grid=(n_tiles, nf),
        in_specs=in_specs,
        out_specs=[pl.BlockSpec(memory_space=pl.ANY)] * len(out_rows),
        out_shape=[jax.ShapeDtypeStruct((r, d), F32) for r in out_rows],
        scratch_shapes=[pltpu.VMEM((2, tm, d), F32), pltpu.VMEM((tm, d), BF16),
                        pltpu.SemaphoreType.DMA((2,)), pltpu.SemaphoreType.DMA((2, 2))],
        compiler_params=pltpu.CompilerParams(
            dimension_semantics=("arbitrary", "arbitrary"), vmem_limit_bytes=BIG_VMEM_LIMIT),
        name=name,
    )(*args)
    return outs[0] if split_row is None else outs


def _gdn_kernel(xq_ref, xk_ref, xv_ref, z_ref, gates_ref, alog_ref, dt_ref,
                pq_ref, pk_ref, pv_ref, wq_ref, wk_ref, wv_ref, onorm_ref, s0_ref,
                o_ref, s_ref, hq_ref, hk_ref, hv_ref, padq_ref, padk_ref, padv_ref, *, c, hb, nseq, nsub):
    n = pl.program_id(2)
    halo = SUBLANES
    seqs = range(nseq)
    rows_seq = nsub * c

    @pl.when(n == 0)
    def _():
        for pad_ref, prev_ref in ((padq_ref, pq_ref), (padk_ref, pk_ref), (padv_ref, pv_ref)):
            for sq in seqs:
                pad_ref[sq, 0:halo, :] = jnp.zeros((halo, pad_ref.shape[2]), F32)
                pad_ref[sq, halo - (CONV_W - 1):halo, :] = prev_ref[sq]
        s_ref[...] = s0_ref[...]

    def conv(x_ref, pad_ref, w_ref, hist_ref, sq):
        pad_ref[sq, halo:halo + rows_seq, :] = x_ref[sq * rows_seq:(sq + 1) * rows_seq, :]
        xp = pad_ref[sq]
        x1 = pltpu.roll(xp, 1, 0)
        a = xp * w_ref[3:4, :] + x1 * w_ref[2:3, :]
        b = xp * w_ref[1:2, :] + x1 * w_ref[0:1, :]
        acc = (a + pltpu.roll(b, 2, 0))[halo:, :]
        tail = xp[rows_seq:rows_seq + halo, :]
        pad_ref[sq, 0:halo, :] = tail
        hist_ref[sq] = tail
        return _silu(acc)

    cq = [conv(xq_ref, padq_ref, wq_ref, hq_ref, sq) for sq in seqs]
    ck = [conv(xk_ref, padk_ref, wk_ref, hk_ref, sq) for sq in seqs]
    cv = [conv(xv_ref, padv_ref, wv_ref, hv_ref, sq) for sq in seqs]

    row = lax.broadcasted_iota(jnp.int32, (c, c), 0)
    col = lax.broadcasted_iota(jnp.int32, (c, c), 1)
    incl = row >= col
    strict = row > col
    tril = jnp.where(incl, 1.0, 0.0).astype(F32)
    r128 = lax.broadcasted_iota(jnp.int32, (LANES, LANES), 0)
    c128 = lax.broadcasted_iota(jnp.int32, (LANES, LANES), 1)
    eye128 = jnp.where(r128 == c128, 1.0, 0.0).astype(F32)
    blocks = [(sq, sb) for sq in seqs for sb in range(nsub)]
    brow = {blk: slice(blk[0] * rows_seq + blk[1] * c, blk[0] * rows_seq + (blk[1] + 1) * c) for blk in blocks}
    beta_all, gc_all, gc_t = {}, {}, {}
    for blk in blocks:
        gates = gates_ref[brow[blk], :]
        beta_all[blk] = 0.5 + 0.5 * jnp.tanh(0.5 * gates)
        ga = gates + dt_ref[...]
        softplus = jnp.maximum(ga, 0.0) + jnp.log(1.0 + jnp.exp(-jnp.abs(ga)))
        g_all = -jnp.exp(alog_ref[...]) * softplus
        gc_all[blk] = jnp.dot(tril, g_all, preferred_element_type=F32,
                              precision=lax.Precision.HIGHEST)
        gc_t[blk] = lax.dot_general(eye128, gc_all[blk], _NT, preferred_element_type=F32,
                                    precision=lax.Precision.HIGHEST)

    units = [(blk, h) for blk in blocks for h in range(hb)]
    un = range(len(units))
    sls = [slice(h * LANES, (h + 1) * LANES) for _, h in units]
    srow = [slice(blk[1] * c, (blk[1] + 1) * c) for blk, _ in units]
    q = [cq[blk[0]][srow[u], sls[u]] for u, (blk, h) in enumerate(units)]
    k = [ck[blk[0]][srow[u], sls[u]] for u, (blk, h) in enumerate(units)]
    v = [cv[blk[0]][srow[u], sls[u]] for u, (blk, h) in enumerate(units)]
    q = [x * (lax.rsqrt(jnp.sum(x * x, axis=-1, keepdims=True) + EPS) * (DK ** -0.5)) for x in q]
    k = [x * lax.rsqrt(jnp.sum(x * x, axis=-1, keepdims=True) + EPS) for x in k]
    bcol = [beta_all[blk][:, h:h + 1] for blk, h in units]
    gcol = [gc_all[blk][:, hb + h:hb + h + 1] for blk, h in units]
    grow = [gc_t[blk][hb + h:hb + h + 1, :] for blk, h in units]
    glast = [gc_all[blk][c - 1:c, hb + h:hb + h + 1] for blk, h in units]
    decay = [jnp.exp(jnp.where(incl, gcol[u] - grow[u], -jnp.inf)) for u in un]
    kb = [k[u] * bcol[u] for u in un]
    eye = jnp.where(row == col, 1.0, 0.0).astype(F32)
    qkk = [_dot(jnp.concatenate([q[u], kb[u]], axis=0), k[u], _NT) for u in un]
    qk = [qkk[u][:c] * decay[u] for u in un]
    p = [-jnp.where(strict, qkk[u][c:] * decay[u], 0.0) for u in un]
    t = [eye + x for x in p]
    levels = max(int(math.ceil(math.log2(c))) - 1, 0)
    if levels:
        p = [_dot(x, x) for x in p]
    for _ in range(1, levels):
        r = [_dot(jnp.concatenate([p[u], t[u]], axis=0), p[u]) for u in un]
        t = [t[u] + r[u][c:] for u in un]
        p = [r[u][:c] for u in un]
    if levels:
        t = [t[u] + _dot(t[u], p[u]) for u in un]
    egc = [jnp.exp(x) for x in gcol]
    rhs = [jnp.concatenate([v[u] * bcol[u], kb[u] * egc[u]], axis=-1) for u in un]
    sol = [_dot(t[u], rhs[u]) for u in un]
    wqe = [jnp.concatenate([sol[u][:, DV:], q[u] * egc[u]], axis=0) for u in un]
    kd = [k[u] * jnp.exp(glast[u] - gcol[u]) for u in un]
    elast = [jnp.exp(x) for x in glast]

    onorm = onorm_ref[...]
    s = {(sq, h): s_ref[sq, h] for sq in seqs for h in range(hb)}
    for sb in range(nsub):
        cur = [u for u in un if units[u][0][1] == sb]
        key = {u: (units[u][0][0], units[u][1]) for u in cur}
        wq = {u: _dot(wqe[u], s[key[u]]) for u in cur}
        v_new = {u: sol[u][:, :DV] - wq[u][:c] for u in cur}
        o = {u: wq[u][c:] + _dot(qk[u], v_new[u]) for u in cur}
        for u in cur:
            s[key[u]] = s[key[u]] * elast[u] + _dot(kd[u], v_new[u], _TN)
        for u in cur:
            oh = o[u] * _rms_scale(o[u]) * onorm
            rows = brow[units[u][0]]
            o_ref[rows, sls[u]] = (oh * _silu(z_ref[rows, sls[u]])).astype(o_ref.dtype)
    for (sq, h), val in s.items():
        s_ref[sq, h] = val


def _gdn(qkvz, gates, alog_row, dt_row, conv_prev, conv_w, onorm_row, s0,
         *, batch, length, c, hb, nseq, nsub, name):
    nblk = length // (nsub * c)
    ng = HA // hb
    hw = hb * LANES
    rows_step = nseq * nsub * c
    assert length % (nsub * c) == 0 and c >= SUBLANES and CONV_W == 4
    assert batch % nseq == 0 and (nseq == 1 or nblk == 1)

    def rows(b, g, n):
        return b * nblk + n

    x_spec = lambda part: pl.BlockSpec((rows_step, hw), lambda b, g, n: (rows(b, g, n), part * ng + g))
    prev_spec = lambda part: pl.BlockSpec((nseq, CONV_W - 1, hw), lambda b, g, n: (b, 0, part * ng + g))
    w_spec = lambda part: pl.BlockSpec((CONV_W, hw), lambda b, g, n: (0, part * ng + g))
    gate_row_spec = pl.BlockSpec((1, LANES), lambda b, g, n: (0, g))
    hist_spec = pl.BlockSpec((nseq, SUBLANES, hw), lambda b, g, n: (b, 0, g))
    hist_shape = jax.ShapeDtypeStruct((batch, SUBLANES, HA * LANES), F32)
    state_spec = pl.BlockSpec((nseq, hb, DK, DV), lambda b, g, n: (b, g, 0, 0))
    in_specs = [
        x_spec(0), x_spec(1), x_spec(2), x_spec(3),
        pl.BlockSpec((rows_step, LANES), lambda b, g, n: (rows(b, g, n), g)),
        gate_row_spec, gate_row_spec,
        prev_spec(0), prev_spec(1), prev_spec(2),
        w_spec(0), w_spec(1), w_spec(2),
        pl.BlockSpec((1, LANES), lambda b, g, n: (0, 0)),
        state_spec,
    ]
    out_specs = [
        pl.BlockSpec((rows_step, hw), lambda b, g, n: (b * nblk + n, g)),
        state_spec,
        hist_spec, hist_spec, hist_spec,
    ]
    return pl.pallas_call(
        functools.partial(_gdn_kernel, c=c, hb=hb, nseq=nseq, nsub=nsub),
        grid=(batch // nseq, ng, nblk),
        in_specs=in_specs,
        out_specs=out_specs,
        out_shape=[jax.ShapeDtypeStruct((batch * length, HA * DV), BF16),
                   jax.ShapeDtypeStruct((batch, HA, DK, DV), F32),
                   hist_shape, hist_shape, hist_shape],
        scratch_shapes=[pltpu.VMEM((nseq, nsub * c + SUBLANES, hw), F32)] * 3,
        compiler_params=pltpu.CompilerParams(
            dimension_semantics=("parallel", "parallel", "arbitrary"),
            vmem_limit_bytes=VMEM_LIMIT),
        name=name,
    )(qkvz, qkvz, qkvz, qkvz, gates, alog_row, dt_row,
      conv_prev, conv_prev, conv_prev, conv_w, conv_w, conv_w, onorm_row, s0)


def _attn_kernel(sinks_ref, q_ref, kp_ref, vp_ref, kc_ref, vc_ref, o_ref, *, tq, nb, chunk_mask,
                 prev_transposed):
    i = pl.program_id(1)
    nk = WINDOW + tq
    lane_q = lax.broadcasted_iota(jnp.int32, (tq, LANES), 1)
    if chunk_mask:
        qc = lax.broadcasted_iota(jnp.int32, (tq, nk), 0) // CHUNK
        kc = lax.broadcasted_iota(jnp.int32, (tq, nk), 1) // CHUNK
        w_ch = WINDOW // CHUNK
        first_kc = jnp.where(i > 0, 0, w_ch)
        valid = (kc >= jnp.maximum(qc, first_kc)) & (kc <= qc + w_ch)
    scale = HD ** -0.5
    assert math.log2(HD) % 2 == 0
    per_slab = LANES // HD
    group = N_Q // N_KV
    slabs_per_kv = group // per_slab
    half_masks = [(lane_q >= r * HD) & (lane_q < (r + 1) * HD) for r in range(per_slab)]

    units = [(b, h) for b in range(nb) for h in range(N_KV)]
    kdup, vdup, qs = [], [], []
    for b, h in units:
        ksl = slice((h // per_slab) * LANES, (h // per_slab + 1) * LANES)

        def dup(prev_ref, cur_ref):
            if prev_transposed:
                xt = prev_ref[b, h]
                front = [jnp.concatenate([xt, xt], axis=0).T]
                slab = cur_ref[b * tq:(b + 1) * tq, ksl]
            else:
                front = []
                slab = jnp.concatenate([prev_ref[b * WINDOW:(b + 1) * WINDOW, ksl],
                                        cur_ref[b * tq:(b + 1) * tq, ksl]], axis=0)
            lane = lax.broadcasted_iota(jnp.int32, slab.shape, 1)
            first = (lane < HD) == (h % per_slab == 0)
            own = jnp.where(first, slab, pltpu.roll(slab, HD, 1))
            return jnp.concatenate(front + [own], axis=0).astype(BF16)

        kdup.append(dup(kp_ref, kc_ref))
        vdup.append(dup(vp_ref, vc_ref))
        pieces = []
        for s in range(slabs_per_kv):
            slab = h * slabs_per_kv + s
            q2 = q_ref[b * tq:(b + 1) * tq, slab * LANES:(slab + 1) * LANES]
            pieces += [jnp.where(mk, q2, jnp.zeros_like(q2)) for mk in half_masks]
        qs.append(jnp.concatenate(pieces, axis=0) * scale)
    sc = [lax.dot_general(qs[u], kdup[u], _NT, preferred_element_type=F32) for u in range(len(units))]
    ps = []
    for u, (b, h) in enumerate(units):
        blocks = []
        for r in range(group):
            sr = sc[u][r * tq:(r + 1) * tq]
            if chunk_mask:
                sr = jnp.where(valid, sr, -jnp.inf)
            sk = sinks_ref[h * group + r]
            mx = jnp.maximum(jnp.max(sr, axis=-1, keepdims=True), sk)
            e = jnp.exp(sr - mx)
            den = jnp.sum(e, axis=-1, keepdims=True) + jnp.exp(sk - mx)
            blocks.append((e / den).astype(BF16))
        ps.append(jnp.concatenate(blocks, axis=0))
    pv = [jnp.dot(ps[u], vdup[u], preferred_element_type=F32) for u in range(len(units))]
    for u, (b, h) in enumerate(units):
        for s in range(slabs_per_kv):
            slab = h * slabs_per_kv + s
            halves = [pv[u][(s * per_slab + r) * tq:(s * per_slab + r + 1) * tq] for r in range(per_slab)]
            out = halves[-1]
            for r in range(per_slab - 2, -1, -1):
                out = jnp.where(half_masks[r], halves[r], out)
            o_ref[b * tq:(b + 1) * tq, slab * LANES:(slab + 1) * LANES] = out.astype(o_ref.dtype)


def _attention(sinks, q, kprev, vprev, kprev_col, vprev_col, kv, *, row0, batch, length,
               tq, nb, chunk_mask, prev_transposed, name):
    nq = length // tq
    kvw = N_KV * HD
    assert row0 % (nb * tq) == 0 and length % tq == 0 and batch % nb == 0 and (nb == 1 or nq == 1)
    rb0 = row0 // (nb * tq)

    def cur(b, i):
        return rb0 + b * nq + i

    def prev(b, i):
        return b * nq + jnp.maximum(i - 1, 0)

    in_specs = [
        pl.BlockSpec(memory_space=pltpu.SMEM),
        pl.BlockSpec((nb * tq, N_Q * HD), lambda b, i: (cur(b, i), 0)),
        *([pl.BlockSpec((nb, N_KV, HD, WINDOW), lambda b, i: (b, 0, 0, 0))] * 2 if prev_transposed else
          [pl.BlockSpec((nb * WINDOW, kvw), lambda b, i: (prev(b, i), kprev_col)),
           pl.BlockSpec((nb * WINDOW, kvw), lambda b, i: (prev(b, i), vprev_col))]),
        pl.BlockSpec((nb * tq, kvw), lambda b, i: (cur(b, i), 0)),
        pl.BlockSpec((nb * tq, kvw), lambda b, i: (cur(b, i), 1)),
    ]
    return pl.pallas_call(
        functools.partial(_attn_kernel, tq=tq, nb=nb, chunk_mask=chunk_mask,
                          prev_transposed=prev_transposed),
        grid=(batch // nb, nq),
        in_specs=in_specs,
        out_specs=pl.BlockSpec((nb * tq, N_Q * HD), lambda b, i: (b * nq + i, 0)),
        out_shape=jax.ShapeDtypeStruct((batch * length, N_Q * HD), BF16),
        compiler_params=pltpu.CompilerParams(
            dimension_semantics=("parallel", "arbitrary"), vmem_limit_bytes=VMEM_LIMIT),
        name=name,
    )(sinks, q, kprev, vprev, kv, kv)


def _rope_tables(pos):
    half = HD // 2
    inv = 1.0 / (ROPE_THETA ** (jnp.arange(half, dtype=F32) / half))
    ang = pos.astype(F32)[:, None] * jnp.tile(inv, LANES // half)[None, :]
    sign = jnp.tile(jnp.concatenate([-jnp.ones((half,), F32), jnp.ones((half,), F32)]), LANES // HD)
    return jnp.cos(ang), jnp.sin(ang) * sign[None, :]


def _gate_layout(t, hb):
    lead = t.shape[:-1]
    ng = HA // hb
    if ng == 1:
        return jnp.pad(t, [(0, 0)] * len(lead) + [(0, LANES - 2 * HA)])
    beta = t[..., :HA].reshape(lead + (ng, hb))
    dec = t[..., HA:].reshape(lead + (ng, hb))
    pad = jnp.zeros(lead + (ng, LANES - 2 * hb), t.dtype)
    return jnp.concatenate([beta, dec, pad], axis=-1).reshape(lead + (ng * LANES,))


def kernel(x_prompt, x_sample, cache_conv, state_gdn, cache_k, cache_v, attn_norm, mlp_norm, final_norm, a_w_in, a_conv_w, a_log, a_dt_bias, a_o_norm, a_w_out, kv_norm, w_kv, b_w_q, b_sinks, b_w_o, w_up, w_down):
    bp, lp, d = x_prompt.shape
    bs, ls, _ = x_sample.shape
    mp, ms = bp * lp, bs * ls
    m = mp + ms
    hb = GDN_HEADS_PER_STEP
    dqk, dvw = HA * DK, HA * DV
    conv_ch = 2 * dqk + dvw
    keep = CONV_W - 1
    assert a_w_in.shape[0] == 1 and b_w_q.shape[0] == 1 and lp % WINDOW == 0 and ls >= keep

    xp = x_prompt.reshape(mp, d)
    xs = x_sample.reshape(ms, d)
    tm = _pick_tile(m, ROW_TILES)
    tm2 = _pick_tile(math.gcd(mp, ms), ROW_TILES[1:])
    tm_p = _pick_tile(mp, PROMPT_ROW_TILES)
    tm_s = _pick_tile(ms, ROW_TILES[1:])

    w_in = a_w_in[0].astype(BF16)
    w_gate = _gate_layout(w_in[:, conv_ch + dvw:], hb)
    g0 = attn_norm[0].reshape(1, d)
    tn_in = _pick_tile(conv_ch + dvw, IN_PROJ_COL_TILES)
    zeros_gate = jnp.zeros((HA,), F32)
    alog_row = _gate_layout(jnp.concatenate([zeros_gate, a_log[0]]), hb).reshape(1, -1)
    dt_row = _gate_layout(jnp.concatenate([zeros_gate, a_dt_bias[0]]), hb).reshape(1, -1)
    onorm_row = a_o_norm[0].reshape(1, DV)
    c_p, c_s = min(CHUNK, lp), min(CHUNK, ls)
    streams = (("prompt", xp, tm_p, bp, lp, c_p, 1, _pick_tile(lp // c_p, (GDN_PROMPT_BLOCKS_PER_STEP, 1)),
                jnp.zeros((bp, keep, conv_ch), F32), jnp.zeros((bp, HA, DK, DV), F32)),
               ("sample", xs, tm_s, bs, ls, c_s, _pick_tile(bs, (GDN_SAMPLE_SEQS_PER_STEP, 1)) if ls == c_s else 1,
                1, cache_conv[0], state_gdn[0]))
    mixed = []
    for tag, xr, tmr, nbatch, length, c_blk, nseq, nsub, conv_prev, s0 in streams:
        qkvz = _matmul(xr, w_in, gain=g0, tm=tmr, tn=tn_in, name="gdn_in_proj_" + tag)
        gates = _matmul(xr, w_gate, gain=g0, tm=tmr, tn=w_gate.shape[1], name="gdn_gate_proj_" + tag)
        mixed.append(_gdn(qkvz, gates, alog_row, dt_row, conv_prev, a_conv_w[0], onorm_row, s0,
                          batch=nbatch, length=length, c=c_blk, hb=hb, nseq=nseq, nsub=nsub,
                          name="gdn_" + tag))
    (o_p, gdn_p, *hist_p), (o_s, gdn_s, *hist_s) = mixed
    tn_o = _pick_tile(d, COL_TILES)
    x = _matmul((o_p, o_s), a_w_out[0], res=(xp, xs), tm=tm2, tn=d, weights_outer=True,
                name="gdn_out_proj")
    tf = _pick_tile(w_up.shape[2], COL_TILES)
    x = _mlp(x, mlp_norm[0].reshape(1, d), w_up, w_down, 0, tm=tm, tf=tf, name="mlp0")

    cos_p, sin_p = _rope_tables(jnp.arange(lp))
    cos_s, sin_s = _rope_tables(PAST_LEN + jnp.arange(ls))
    rope = (jnp.concatenate([jnp.tile(cos_p, (bp, 1)), jnp.tile(cos_s, (bs, 1))], axis=0),
            jnp.concatenate([jnp.tile(sin_p, (bp, 1)), jnp.tile(sin_s, (bs, 1))], axis=0))
    kvw = N_KV * HD
    kv = _matmul(x, w_kv, gain=kv_norm.reshape(1, d), rope=rope, rope_cols=kvw,
                 tm=tm, tn=2 * kvw, name="kv_proj")
    q = _matmul(x, b_w_q[0], gain=attn_norm[1].reshape(1, d), rope=rope,
                rope_cols=N_Q * HD, out_dtype=BF16, tm=tm, tn=tn_o, name="q_proj")
    sinks = b_sinks[0].astype(F32)
    a_p = _attention(sinks, q, kv, kv, 0, 1, kv, row0=0, batch=bp, length=lp, tq=WINDOW, nb=1,
                     chunk_mask=True, prev_transposed=False, name="attn_prompt")
    nb = _pick_tile(bs, (ATTN_SAMPLE_BATCHES_PER_STEP, 2, 1))
    a_s = _attention(sinks, q, jnp.transpose(cache_k, (0, 2, 3, 1)), jnp.transpose(cache_v, (0, 2, 3, 1)),
                     0, 0, kv, row0=mp, batch=bs, length=ls, tq=ls, nb=nb,
                     chunk_mask=False, prev_transposed=True, name="attn_sample")
    x = _matmul((a_p, a_s), b_w_o[0], res=x, tm=tm2, tn=d, weights_outer=True, name="attn_out_proj")
    y_p, y_s = _mlp(x, mlp_norm[1].reshape(1, d), w_up, w_down, 1, final_gain=final_norm.reshape(1, d),
                    split_row=mp, tm=tm, tf=tf, name="mlp1")

    y_prompt = y_p.reshape(bp, lp, d)
    y_sample = y_s.reshape(bs, ls, d)
    conv_p = jnp.concatenate([t[:, SUBLANES - keep:] for t in hist_p], axis=-1)[None]
    conv_s = jnp.concatenate([t[:, SUBLANES - keep:] for t in hist_s], axis=-1)[None]
    kv_p = jnp.stack([kv[(b + 1) * lp - WINDOW:(b + 1) * lp] for b in range(bp)])
    kv_s = kv[mp:].reshape(bs, ls, 2 * kvw)
    k_p = kv_p[..., :kvw].reshape(bp, WINDOW, N_KV, HD)
    v_p = kv_p[..., kvw:].reshape(bp, WINDOW, N_KV, HD)
    k_s = kv_s[..., :kvw].reshape(bs, ls, N_KV, HD)
    v_s = kv_s[..., kvw:].reshape(bs, ls, N_KV, HD)
    return (y_prompt, y_sample, conv_p, gdn_p[None], k_p, v_p, conv_s, gdn_s[None], k_s, v_s)
```

```python
import functools
import math

import jax
import jax.numpy as jnp
from jax import lax
from jax.experimental import pallas as pl
from jax.experimental.pallas import tpu as pltpu

F32 = jnp.float32
BF16 = jnp.bfloat16

EPS = 1e-6
CHUNK = 64
WINDOW = 128
PAST_LEN = 4096
ROPE_THETA = 10000.0
HA, DK, DV = 16, 128, 128
N_Q, N_KV, HD = 32, 4, 64
CONV_W = 4
LANES = 128
SUBLANES = 8
VMEM_LIMIT = 56 * 1024 * 1024
BIG_VMEM_LIMIT = 60 * 1024 * 1024
GDN_HEADS_PER_STEP = 16
GDN_SAMPLE_SEQS_PER_STEP = 4
GDN_PROMPT_BLOCKS_PER_STEP = 4
ATTN_SAMPLE_BATCHES_PER_STEP = 4
ROW_TILES = (1088, 512, 256, 128, 64, 32, 16)
PROMPT_ROW_TILES = (1024,) + ROW_TILES[1:]
COL_TILES = (1024, 512, 256, 128)
IN_PROJ_COL_TILES = (2048,) + COL_TILES


def _pick_tile(n, candidates):
    for c in candidates:
        if n % c == 0:
            return c
    return n


def _dot(a, b, dims=(((1,), (0,)), ((), ()))):
    return lax.dot_general(a.astype(BF16), b.astype(BF16), dims, preferred_element_type=F32)


_NT = (((1,), (1,)), ((), ()))
_TN = (((0,), (0,)), ((), ()))


def _rms_scale(x):
    return lax.rsqrt(jnp.mean(x * x, axis=-1, keepdims=True) + EPS)


def _silu(x):
    h = 0.5 * x
    return h + h * jnp.tanh(h)


def _rope_slab(y, cos, sin_signed):
    lane = lax.broadcasted_iota(jnp.int32, y.shape, 1)
    half = HD // 2
    up = pltpu.roll(y, half, 1)
    down = pltpu.roll(y, LANES - half, 1)
    swapped = jnp.where((lane % HD) < half, down, up)
    return y * cos + swapped * sin_signed


def _project(lhs, w_ref, res_ref, cos_ref, sin_ref, o_ref, *, rope_slabs, n_slabs):
    if not rope_slabs:
        y = jnp.dot(lhs, w_ref[...].astype(BF16), preferred_element_type=F32)
        if res_ref is not None:
            y = y + res_ref[...]
        o_ref[...] = y.astype(o_ref.dtype)
        return
    cos = cos_ref[...]
    sin = sin_ref[...]
    n_chunk = 2 if n_slabs % 2 == 0 else 1
    for c0 in range(0, n_slabs, n_chunk):
        cols = slice(c0 * LANES, (c0 + n_chunk) * LANES)
        y = jnp.dot(lhs, w_ref[:, cols].astype(BF16), preferred_element_type=F32)
        if res_ref is not None:
            y = y + res_ref[:, cols]
        for s in range(n_chunk):
            ys = y[:, s * LANES:(s + 1) * LANES]
            if c0 + s < rope_slabs:
                ys = _rope_slab(ys, cos, sin)
            o_ref[:, (c0 + s) * LANES:(c0 + s + 1) * LANES] = ys.astype(o_ref.dtype)


def _matmul_kernel(*refs, n_x, n_res, split_tile, row_axis, has_norm, rope_slabs, n_slabs, resident_w,
                   has_tail):
    it = iter(refs)
    x_refs = [next(it) for _ in range(n_x)]
    g_ref = next(it) if has_norm else None
    w_ref = next(it)
    wt_ref = next(it) if has_tail else None
    res_refs = [next(it) for _ in range(n_res)]
    cos_ref = next(it) if rope_slabs else None
    sin_ref = next(it) if rope_slabs else None
    o_ref = next(it)
    ot_ref = next(it) if has_tail else None
    xn_ref = next(it) if has_norm else None
    wb_ref = next(it) if resident_w else None

    if resident_w:
        @pl.when(pl.program_id(row_axis) == 0)
        def _():
            wb_ref[...] = w_ref[...].astype(BF16)
        w_ref = wb_ref

    if has_norm:
        @pl.when(pl.program_id(1) == 0)
        def _():
            x = x_refs[0][...]
            xn_ref[...] = (x * _rms_scale(x) * g_ref[...]).astype(BF16)

    if has_tail:
        @pl.when(pl.program_id(1) == pl.num_programs(1) - 1)
        def _():
            ot_ref[...] = jnp.dot(xn_ref[...], wt_ref[...].astype(BF16), preferred_element_type=F32)

    def emit(lhs_ref, res_ref):
        _project(lhs_ref[...], w_ref, res_ref, cos_ref, sin_ref, o_ref, rope_slabs=rope_slabs, n_slabs=n_slabs)

    if n_x == 1:
        emit(xn_ref if has_norm else x_refs[0], res_refs[0] if n_res else None)
    else:
        i = pl.program_id(row_axis)
        pl.when(i < split_tile)(functools.partial(emit, x_refs[0], res_refs[0] if n_res else None))
        pl.when(i >= split_tile)(functools.partial(emit, x_refs[1], res_refs[-1] if n_res else None))


def _matmul(xs, w, *, gain=None, res=None, rope=None, rope_cols=0, out_dtype=F32, tm, tn, name,
            weights_outer=False, w_tail=None):
    xs = xs if isinstance(xs, (tuple, list)) else (xs,)
    ress = () if res is None else (res if isinstance(res, (tuple, list)) else (res,))
    k = xs[0].shape[1]
    m = sum(x.shape[0] for x in xs)
    n = (w.shape[1] // tn) * tn
    assert all(x.shape[0] % tm == 0 for x in xs) and w.shape[0] == k
    has_norm = gain is not None
    assert not (has_norm and len(xs) > 1)
    assert len(ress) <= len(xs) and all(r.shape[0] == x.shape[0] for r, x in zip(ress, xs) if len(ress) > 1)
    split_tile = xs[0].shape[0] // tm
    rope_slabs = 0
    if rope is not None:
        assert rope_cols == n or tn == n
        rope_slabs = min(rope_cols, tn) // LANES
    assert not (has_norm and weights_outer)
    resident_w = weights_outer and n == tn and w.dtype != BF16

    def spec(shape, f):
        return pl.BlockSpec(shape, (lambda a, b: f(b, a)) if weights_outer else f)

    def row_specs(arrays, width, col):
        if len(arrays) == 1:
            return [spec((tm, width), lambda i, j: (i, col(j)))]
        return [spec((tm, width), lambda i, j: (jnp.minimum(i, split_tile - 1), col(j))),
                spec((tm, width), lambda i, j: (jnp.maximum(i - split_tile, 0), col(j)))]

    in_specs = row_specs(xs, k, lambda j: 0)
    args = list(xs)
    if has_norm:
        in_specs.append(spec((1, k), lambda i, j: (0, 0)))
        args.append(gain)
    in_specs.append(pl.BlockSpec((k, tn), lambda a, b: (0, 0), pipeline_mode=pl.Buffered(1)) if resident_w
                    else spec((k, tn), lambda i, j: (0, j)))
    args.append(w)
    has_tail = w_tail is not None
    assert not has_tail or (has_norm and w_tail.shape[1] % LANES == 0)
    if has_tail:
        in_specs.append(spec(w_tail.shape, lambda i, j: (0, 0)))
        args.append(w_tail)
    in_specs += row_specs(ress, tn, lambda j: j) if ress else []
    args += list(ress)
    if rope_slabs:
        in_specs += [spec((tm, LANES), lambda i, j: (i, 0))] * 2
        args += list(rope)
    kern = functools.partial(_matmul_kernel, n_x=len(xs), n_res=len(ress), split_tile=split_tile,
                             row_axis=1 if weights_outer else 0, has_norm=has_norm,
                             rope_slabs=rope_slabs, n_slabs=tn // LANES, resident_w=resident_w,
                             has_tail=has_tail)
    grid = (n // tn, m // tm) if weights_outer else (m // tm, n // tn)
    out_specs = [spec((tm, tn), lambda i, j: (i, j))]
    out_shape = [jax.ShapeDtypeStruct((m, n), out_dtype)]
    if has_tail:
        out_specs.append(spec((tm, w_tail.shape[1]), lambda i, j: (i, 0)))
        out_shape.append(jax.ShapeDtypeStruct((m, w_tail.shape[1]), F32))
    outs = pl.pallas_call(
        kern,
        grid=grid,
        in_specs=in_specs,
        out_specs=out_specs,
        out_shape=out_shape,
        scratch_shapes=([pltpu.VMEM((tm, k), BF16)] if has_norm else [])
                       + ([pltpu.VMEM((k, tn), BF16)] if resident_w else []),
        compiler_params=pltpu.CompilerParams(
            dimension_semantics=("arbitrary" if resident_w else "parallel", "arbitrary"),
            vmem_limit_bytes=BIG_VMEM_LIMIT if resident_w or has_tail else VMEM_LIMIT),
        name=name,
    )(*args)
    return tuple(outs) if has_tail else outs[0]


def _mlp_kernel(*refs, tm, n_tiles, split_row, final_norm, prefetch_step):
    x_hbm, g_ref, wu_ref, wd_ref = refs[:4]
    fg_ref = refs[4] if final_norm else None
    *out_refs, acc_ref, hn_ref, sem_in, sem_out = refs[5 if final_norm else 4:]
    i = pl.program_id(0)
    f = pl.program_id(1)
    slot = i % 2

    def fetch(t):
        return pltpu.make_async_copy(x_hbm.at[pl.ds(t * tm, tm)], acc_ref.at[t % 2], sem_in.at[t % 2])

    def writebacks(t):
        r0, r1 = t * tm, (t + 1) * tm
        s = t % 2
        cps = []
        if r0 < split_row:
            n = min(r1, split_row) - r0
            cps.append(pltpu.make_async_copy(acc_ref.at[s, 0:n], out_refs[0].at[r0:r0 + n], sem_out.at[s, 0]))
        if r1 > split_row:
            a = max(r0, split_row)
            cps.append(pltpu.make_async_copy(acc_ref.at[s, a - r0:tm],
                                             out_refs[1].at[a - split_row:r1 - split_row], sem_out.at[s, 1]))
        return cps

    def start_tile(t):
        for cp in writebacks(t):
            cp.start()

    def wait_tile(t):
        for cp in writebacks(t):
            cp.wait()

    @pl.when(f == 0)
    def _():
        @pl.when(i == 0)
        def _():
            fetch(i).start()
        fetch(i).wait()
        x = acc_ref[slot]
        hn_ref[...] = (x * _rms_scale(x) * g_ref[...]).astype(BF16)

    u = jnp.dot(hn_ref[...], wu_ref[...].astype(BF16), preferred_element_type=F32)
    a = jnp.square(jnp.maximum(u, 0.0)).astype(BF16)
    acc_ref[slot] += jnp.dot(a, wd_ref[...].astype(BF16), preferred_element_type=F32)

    @pl.when((f == prefetch_step) & (i + 1 < n_tiles))
    def _():
        for t in range(n_tiles - 2):
            pl.when(i == t + 1)(functools.partial(wait_tile, t))
        fetch(i + 1).start()

    @pl.when(f == pl.num_programs(1) - 1)
    def _():
        if final_norm:
            y = acc_ref[slot]
            acc_ref[slot] = y * _rms_scale(y) * fg_ref[...]
        for t in range(n_tiles):
            pl.when(i == t)(functools.partial(start_tile, t))

        @pl.when(i == n_tiles - 1)
        def _():
            for t in range(max(n_tiles - 2, 0), n_tiles):
                wait_tile(t)


def _mlp(x, gain, w_up, w_down, layer, *, final_gain=None, split_row=None, tm, tf, name):
    m, d = x.shape
    dff = w_up.shape[2]
    n_tiles, nf = m // tm, dff // tf
    final_norm = final_gain is not None
    prefetch_step = 1
    assert m % tm == 0 and dff % tf == 0 and nf > prefetch_step + 1
    assert split_row is None or (split_row % SUBLANES == 0 and 0 < split_row < m)
    in_specs = [
        pl.BlockSpec(memory_space=pl.ANY),
        pl.BlockSpec((1, d), lambda i, f: (0, 0)),
        pl.BlockSpec((None, d, tf), lambda i, f: (layer, 0, f)),
        pl.BlockSpec((None, tf, d), lambda i, f: (layer, f, 0)),
    ]
    args = [x, gain, w_up, w_down]
    if final_norm:
        in_specs.append(pl.BlockSpec((1, d), lambda i, f: (0, 0)))
        args.append(final_gain)
    out_rows = [m] if split_row is None else [split_row, m - split_row]
    outs = pl.pallas_call(
        functools.partial(_mlp_kernel, tm=tm, n_tiles=n_tiles, split_row=out_rows[0], final_norm=final_norm,
                          prefetch_step=prefetch_step),
        grid=(n_tiles, nf),
        in_specs=in_specs,
        out_specs=[pl.BlockSpec(memory_space=pl.ANY)] * len(out_rows),
        out_shape=[jax.ShapeDtypeStruct((r, d), F32) for r in out_rows],
        scratch_shapes=[pltpu.VMEM((2, tm, d), F32), pltpu.VMEM((tm, d), BF16),
                        pltpu.SemaphoreType.DMA((2,)), pltpu.SemaphoreType.DMA((2, 2))],
        compiler_params=pltpu.CompilerParams(
            dimension_semantics=("arbitrary", "arbitrary"), vmem_limit_bytes=BIG_VMEM_LIMIT),
        name=name,
    )(*args)
    return outs[0] if split_row is None else outs


def _gdn_kernel(xq_ref, xk_ref, xv_ref, z_ref, gates_ref, alog_ref, dt_ref,
                pq_ref, pk_ref, pv_ref, wq_ref, wk_ref, wv_ref, onorm_ref, s0_ref,
                o_ref, s_ref, hq_ref, hk_ref, hv_ref, padq_ref, padk_ref, padv_ref, *, c, hb, nseq, nsub):
    n = pl.program_id(2)
    halo = SUBLANES
    seqs = range(nseq)
    rows_seq = nsub * c

    @pl.when(n == 0)
    def _():
        for pad_ref, prev_ref in ((padq_ref, pq_ref), (padk_ref, pk_ref), (padv_ref, pv_ref)):
            for sq in seqs:
                pad_ref[sq, 0:halo, :] = jnp.zeros((halo, pad_ref.shape[2]), F32)
                pad_ref[sq, halo - (CONV_W - 1):halo, :] = prev_ref[sq]
        s_ref[...] = s0_ref[...]

    def conv(x_ref, pad_ref, w_ref, hist_ref, sq):
        pad_ref[sq, halo:halo + rows_seq, :] = x_ref[sq * rows_seq:(sq + 1) * rows_seq, :]
        xp = pad_ref[sq]
        x1 = pltpu.roll(xp, 1, 0)
        a = xp * w_ref[3:4, :] + x1 * w_ref[2:3, :]
        b = xp * w_ref[1:2, :] + x1 * w_ref[0:1, :]
        acc = (a + pltpu.roll(b, 2, 0))[halo:, :]
        tail = xp[rows_seq:rows_seq + halo, :]
        pad_ref[sq, 0:halo, :] = tail
        hist_ref[sq] = tail
        return _silu(acc)

    cq = [conv(xq_ref, padq_ref, wq_ref, hq_ref, sq) for sq in seqs]
    ck = [conv(xk_ref, padk_ref, wk_ref, hk_ref, sq) for sq in seqs]
    cv = [conv(xv_ref, padv_ref, wv_ref, hv_ref, sq) for sq in seqs]

    row = lax.broadcasted_iota(jnp.int32, (c, c), 0)
    col = lax.broadcasted_iota(jnp.int32, (c, c), 1)
    incl = row >= col
    strict = row > col
    tril = jnp.where(incl, 1.0, 0.0).astype(F32)
    r128 = lax.broadcasted_iota(jnp.int32, (LANES, LANES), 0)
    c128 = lax.broadcasted_iota(jnp.int32, (LANES, LANES), 1)
    eye128 = jnp.where(r128 == c128, 1.0, 0.0).astype(F32)
    blocks = [(sq, sb) for sq in seqs for sb in range(nsub)]
    brow = {blk: slice(blk[0] * rows_seq + blk[1] * c, blk[0] * rows_seq + (blk[1] + 1) * c) for blk in blocks}
    beta_all, gc_all, gc_t = {}, {}, {}
    for blk in blocks:
        gates = gates_ref[brow[blk], :]
        beta_all[blk] = 0.5 + 0.5 * jnp.tanh(0.5 * gates)
        ga = gates + dt_ref[...]
        softplus = jnp.maximum(ga, 0.0) + jnp.log(1.0 + jnp.exp(-jnp.abs(ga)))
        g_all = -jnp.exp(alog_ref[...]) * softplus
        gc_all[blk] = jnp.dot(tril, g_all, preferred_element_type=F32,
                              precision=lax.Precision.HIGHEST)
        gc_t[blk] = lax.dot_general(eye128, gc_all[blk], _NT, preferred_element_type=F32,
                                    precision=lax.Precision.HIGHEST)

    units = [(blk, h) for blk in blocks for h in range(hb)]
    un = range(len(units))
    sls = [slice(h * LANES, (h + 1) * LANES) for _, h in units]
    srow = [slice(blk[1] * c, (blk[1] + 1) * c) for blk, _ in units]
    q = [cq[blk[0]][srow[u], sls[u]] for u, (blk, h) in enumerate(units)]
    k = [ck[blk[0]][srow[u], sls[u]] for u, (blk, h) in enumerate(units)]
    v = [cv[blk[0]][srow[u], sls[u]] for u, (blk, h) in enumerate(units)]
    q = [x * (lax.rsqrt(jnp.sum(x * x, axis=-1, keepdims=True) + EPS) * (DK ** -0.5)) for x in q]
    k = [x * lax.rsqrt(jnp.sum(x * x, axis=-1, keepdims=True) + EPS) for x in k]
    bcol = [beta_all[blk][:, h:h + 1] for blk, h in units]
    gcol = [gc_all[blk][:, hb + h:hb + h + 1] for blk, h in units]
    grow = [gc_t[blk][hb + h:hb + h + 1, :] for blk, h in units]
    glast = [gc_all[blk][c - 1:c, hb + h:hb + h + 1] for blk, h in units]
    decay = [jnp.exp(jnp.where(incl, gcol[u] - grow[u], -jnp.inf)) for u in un]
    kb = [k[u] * bcol[u] for u in un]
    eye = jnp.where(row == col, 1.0, 0.0).astype(F32)
    qkk = [_dot(jnp.concatenate([q[u], kb[u]], axis=0), k[u], _NT) for u in un]
    qk = [qkk[u][:c] * decay[u] for u in un]
    p = [-jnp.where(strict, qkk[u][c:] * decay[u], 0.0) for u in un]
    t = [eye + x for x in p]
    levels = max(int(math.ceil(math.log2(c))) - 1, 0)
    if levels:
        p = [_dot(x, x) for x in p]
    for _ in range(1, levels):
        r = [_dot(jnp.concatenate([p[u], t[u]], axis=0), p[u]) for u in un]
        t = [t[u] + r[u][c:] for u in un]
        p = [r[u][:c] for u in un]
    if levels:
        t = [t[u] + _dot(t[u], p[u]) for u in un]
    egc = [jnp.exp(x) for x in gcol]
    rhs = [jnp.concatenate([v[u] * bcol[u], kb[u] * egc[u]], axis=-1) for u in un]
    sol = [_dot(t[u], rhs[u]) for u in un]
    wqe = [jnp.concatenate([sol[u][:, DV:], q[u] * egc[u]], axis=0) for u in un]
    kd = [k[u] * jnp.exp(glast[u] - gcol[u]) for u in un]
    elast = [jnp.exp(x) for x in glast]

    onorm = onorm_ref[...]
    s = {(sq, h): s_ref[sq, h] for sq in seqs for h in range(hb)}
    for sb in range(nsub):
        cur = [u for u in un if units[u][0][1] == sb]
        key = {u: (units[u][0][0], units[u][1]) for u in cur}
        wq = {u: _dot(wqe[u], s[key[u]]) for u in cur}
        v_new = {u: sol[u][:, :DV] - wq[u][:c] for u in cur}
        o = {u: wq[u][c:] + _dot(qk[u], v_new[u]) for u in cur}
        for u in cur:
            s[key[u]] = s[key[u]] * elast[u] + _dot(kd[u], v_new[u], _TN)
        for u in cur:
            oh = o[u] * _rms_scale(o[u]) * onorm
            rows = brow[units[u][0]]
            o_ref[rows, sls[u]] = (oh * _silu(z_ref[rows, sls[u]])).astype(o_ref.dtype)
    for (sq, h), val in s.items():
        s_ref[sq, h] = val


def _gdn(qkvz, gates, alog_row, dt_row, conv_prev, conv_w, onorm_row, s0,
         *, batch, length, c, hb, nseq, nsub, name):
    nblk = length // (nsub * c)
    ng = HA // hb
    hw = hb * LANES
    rows_step = nseq * nsub * c
    assert length % (nsub * c) == 0 and c >= SUBLANES and CONV_W == 4
    assert batch % nseq == 0 and (nseq == 1 or nblk == 1)

    def rows(b, g, n):
        return b * nblk + n

    x_spec = lambda part: pl.BlockSpec((rows_step, hw), lambda b, g, n: (rows(b, g, n), part * ng + g))
    prev_spec = lambda part: pl.BlockSpec((nseq, CONV_W - 1, hw), lambda b, g, n: (b, 0, part * ng + g))
    w_spec = lambda part: pl.BlockSpec((CONV_W, hw), lambda b, g, n: (0, part * ng + g))
    gate_row_spec = pl.BlockSpec((1, LANES), lambda b, g, n: (0, g))
    hist_spec = pl.BlockSpec((nseq, SUBLANES, hw), lambda b, g, n: (b, 0, g))
    hist_shape = jax.ShapeDtypeStruct((batch, SUBLANES, HA * LANES), F32)
    state_spec = pl.BlockSpec((nseq, hb, DK, DV), lambda b, g, n: (b, g, 0, 0))
    in_specs = [
        x_spec(0), x_spec(1), x_spec(2), x_spec(3),
        pl.BlockSpec((rows_step, LANES), lambda b, g, n: (rows(b, g, n), g)),
        gate_row_spec, gate_row_spec,
        prev_spec(0), prev_spec(1), prev_spec(2),
        w_spec(0), w_spec(1), w_spec(2),
        pl.BlockSpec((1, LANES), lambda b, g, n: (0, 0)),
        state_spec,
    ]
    out_specs = [
        pl.BlockSpec((rows_step, hw), lambda b, g, n: (b * nblk + n, g)),
        state_spec,
        hist_spec, hist_spec, hist_spec,
    ]
    return pl.pallas_call(
        functools.partial(_gdn_kernel, c=c, hb=hb, nseq=nseq, nsub=nsub),
        grid=(batch // nseq, ng, nblk),
        in_specs=in_specs,
        out_specs=out_specs,
        out_shape=[jax.ShapeDtypeStruct((batch * length, HA * DV), BF16),
                   jax.ShapeDtypeStruct((batch, HA, DK, DV), F32),
                   hist_shape, hist_shape, hist_shape],
        scratch_shapes=[pltpu.VMEM((nseq, nsub * c + SUBLANES, hw), F32)] * 3,
        compiler_params=pltpu.CompilerParams(
            dimension_semantics=("parallel", "parallel", "arbitrary"),
            vmem_limit_bytes=VMEM_LIMIT),
        name=name,
    )(qkvz, qkvz, qkvz, qkvz, gates, alog_row, dt_row,
      conv_prev, conv_prev, conv_prev, conv_w, conv_w, conv_w, onorm_row, s0)


def _attn_kernel(sinks_ref, q_ref, kp_ref, vp_ref, kc_ref, vc_ref, o_ref, *, tq, nb, chunk_mask,
                 prev_transposed):
    i = pl.program_id(1)
    nk = WINDOW + tq
    lane_q = lax.broadcasted_iota(jnp.int32, (tq, LANES), 1)
    if chunk_mask:
        qc = lax.broadcasted_iota(jnp.int32, (tq, nk), 0) // CHUNK
        kc = lax.broadcasted_iota(jnp.int32, (tq, nk), 1) // CHUNK
        w_ch = WINDOW // CHUNK
        first_kc = jnp.where(i > 0, 0, w_ch)
        valid = (kc >= jnp.maximum(qc, first_kc)) & (kc <= qc + w_ch)
    scale = HD ** -0.5
    assert math.log2(HD) % 2 == 0
    per_slab = LANES // HD
    group = N_Q // N_KV
    slabs_per_kv = group // per_slab
    half_masks = [(lane_q >= r * HD) & (lane_q < (r + 1) * HD) for r in range(per_slab)]

    units = [(b, h) for b in range(nb) for h in range(N_KV)]
    kdup, vdup, qs = [], [], []
    for b, h in units:
        ksl = slice((h // per_slab) * LANES, (h // per_slab + 1) * LANES)

        def dup(prev_ref, cur_ref):
            if prev_transposed:
                xt = prev_ref[b, h]
                front = [jnp.concatenate([xt, xt], axis=0).T]
                slab = cur_ref[b * tq:(b + 1) * tq, ksl]
            else:
                front = []
                slab = jnp.concatenate([prev_ref[b * WINDOW:(b + 1) * WINDOW, ksl],
                                        cur_ref[b * tq:(b + 1) * tq, ksl]], axis=0)
            lane = lax.broadcasted_iota(jnp.int32, slab.shape, 1)
            first = (lane < HD) == (h % per_slab == 0)
            own = jnp.where(first, slab, pltpu.roll(slab, HD, 1))
            return jnp.concatenate(front + [own], axis=0).astype(BF16)

        kdup.append(dup(kp_ref, kc_ref))
        vdup.append(dup(vp_ref, vc_ref))
        pieces = []
        for s in range(slabs_per_kv):
            slab = h * slabs_per_kv + s
            q2 = q_ref[b * tq:(b + 1) * tq, slab * LANES:(slab + 1) * LANES]
            pieces += [jnp.where(mk, q2, jnp.zeros_like(q2)) for mk in half_masks]
        qs.append(jnp.concatenate(pieces, axis=0) * scale)
    sc = [lax.dot_general(qs[u], kdup[u], _NT, preferred_element_type=F32) for u in range(len(units))]
    ps = []
    for u, (b, h) in enumerate(units):
        blocks = []
        for r in range(group):
            sr = sc[u][r * tq:(r + 1) * tq]
            if chunk_mask:
                sr = jnp.where(valid, sr, -jnp.inf)
            sk = sinks_ref[h * group + r]
            mx = jnp.maximum(jnp.max(sr, axis=-1, keepdims=True), sk)
            e = jnp.exp(sr - mx)
            den = jnp.sum(e, axis=-1, keepdims=True) + jnp.exp(sk - mx)
            blocks.append((e / den).astype(BF16))
        ps.append(jnp.concatenate(blocks, axis=0))
    pv = [jnp.dot(ps[u], vdup[u], preferred_element_type=F32) for u in range(len(units))]
    for u, (b, h) in enumerate(units):
        for s in range(slabs_per_kv):
            slab = h * slabs_per_kv + s
            halves = [pv[u][(s * per_slab + r) * tq:(s * per_slab + r + 1) * tq] for r in range(per_slab)]
            out = halves[-1]
            for r in range(per_slab - 2, -1, -1):
                out = jnp.where(half_masks[r], halves[r], out)
            o_ref[b * tq:(b + 1) * tq, slab * LANES:(slab + 1) * LANES] = out.astype(o_ref.dtype)


def _attention(sinks, q, kprev, vprev, kprev_col, vprev_col, kv, *, row0, batch, length,
               tq, nb, chunk_mask, prev_transposed, name):
    nq = length // tq
    kvw = N_KV * HD
    assert row0 % (nb * tq) == 0 and length % tq == 0 and batch % nb == 0 and (nb == 1 or nq == 1)
    rb0 = row0 // (nb * tq)

    def cur(b, i):
        return rb0 + b * nq + i

    def prev(b, i):
        return b * nq + jnp.maximum(i - 1, 0)

    in_specs = [
        pl.BlockSpec(memory_space=pltpu.SMEM),
        pl.BlockSpec((nb * tq, N_Q * HD), lambda b, i: (cur(b, i), 0)),
        *([pl.BlockSpec((nb, N_KV, HD, WINDOW), lambda b, i: (b, 0, 0, 0))] * 2 if prev_transposed else
          [pl.BlockSpec((nb * WINDOW, kvw), lambda b, i: (prev(b, i), kprev_col)),
           pl.BlockSpec((nb * WINDOW, kvw), lambda b, i: (prev(b, i), vprev_col))]),
        pl.BlockSpec((nb * tq, kvw), lambda b, i: (cur(b, i), 0)),
        pl.BlockSpec((nb * tq, kvw), lambda b, i: (cur(b, i), 1)),
    ]
    return pl.pallas_call(
        functools.partial(_attn_kernel, tq=tq, nb=nb, chunk_mask=chunk_mask,
                          prev_transposed=prev_transposed),
        grid=(batch // nb, nq),
        in_specs=in_specs,
        out_specs=pl.BlockSpec((nb * tq, N_Q * HD), lambda b, i: (b * nq + i, 0)),
        out_shape=jax.ShapeDtypeStruct((batch * length, N_Q * HD), BF16),
        compiler_params=pltpu.CompilerParams(
            dimension_semantics=("parallel", "arbitrary"), vmem_limit_bytes=VMEM_LIMIT),
        name=name,
    )(sinks, q, kprev, vprev, kv, kv)


def _rope_tables(pos):
    half = HD // 2
    inv = 1.0 / (ROPE_THETA ** (jnp.arange(half, dtype=F32) / half))
    ang = pos.astype(F32)[:, None] * jnp.tile(inv, LANES // half)[None, :]
    sign = jnp.tile(jnp.concatenate([-jnp.ones((half,), F32), jnp.ones((half,), F32)]), LANES // HD)
    return jnp.cos(ang), jnp.sin(ang) * sign[None, :]


def _gate_layout(t, hb):
    lead = t.shape[:-1]
    ng = HA // hb
    if ng == 1:
        return jnp.pad(t, [(0, 0)] * len(lead) + [(0, LANES - 2 * HA)])
    beta = t[..., :HA].reshape(lead + (ng, hb))
    dec = t[..., HA:].reshape(lead + (ng, hb))
    pad = jnp.zeros(lead + (ng, LANES - 2 * hb), t.dtype)
    return jnp.concatenate([beta, dec, pad], axis=-1).reshape(lead + (ng * LANES,))


def kernel(x_prompt, x_sample, cache_conv, state_gdn, cache_k, cache_v, attn_norm, mlp_norm, final_norm, a_w_in, a_conv_w, a_log, a_dt_bias, a_o_norm, a_w_out, kv_norm, w_kv, b_w_q, b_sinks, b_w_o, w_up, w_down):
    bp, lp, d = x_prompt.shape
    bs, ls, _ = x_sample.shape
    mp, ms = bp * lp, bs * ls
    m = mp + ms
    hb = GDN_HEADS_PER_STEP
    dqk, dvw = HA * DK, HA * DV
    conv_ch = 2 * dqk + dvw
    keep = CONV_W - 1
    assert a_w_in.shape[0] == 1 and b_w_q.shape[0] == 1 and lp % WINDOW == 0 and ls >= keep

    xp = x_prompt.reshape(mp, d)
    xs = x_sample.reshape(ms, d)
    tm = _pick_tile(m, ROW_TILES)
    tm2 = _pick_tile(math.gcd(mp, ms), ROW_TILES[1:])
    tm_p = _pick_tile(mp, PROMPT_ROW_TILES)
    tm_s = _pick_tile(ms, ROW_TILES[1:])

    w_in = a_w_in[0].astype(BF16)
    w_gate = _gate_layout(w_in[:, conv_ch + dvw:], hb)
    g0 = attn_norm[0].reshape(1, d)
    tn_in = _pick_tile(conv_ch + dvw, IN_PROJ_COL_TILES)
    zeros_gate = jnp.zeros((HA,), F32)
    alog_row = _gate_layout(jnp.concatenate([zeros_gate, a_log[0]]), hb).reshape(1, -1)
    dt_row = _gate_layout(jnp.concatenate([zeros_gate, a_dt_bias[0]]), hb).reshape(1, -1)
    onorm_row = a_o_norm[0].reshape(1, DV)
    c_p, c_s = min(CHUNK, lp), min(CHUNK, ls)
    streams = (("prompt", xp, tm_p, bp, lp, c_p, 1, _pick_tile(lp // c_p, (GDN_PROMPT_BLOCKS_PER_STEP, 1)),
                jnp.zeros((bp, keep, conv_ch), F32), jnp.zeros((bp, HA, DK, DV), F32)),
               ("sample", xs, tm_s, bs, ls, c_s, _pick_tile(bs, (GDN_SAMPLE_SEQS_PER_STEP, 1)) if ls == c_s else 1,
                1, cache_conv[0], state_gdn[0]))
    mixed = []
    for tag, xr, tmr, nbatch, length, c_blk, nseq, nsub, conv_prev, s0 in streams:
        qkvz, gates = _matmul(xr, w_in, gain=g0, tm=tmr, tn=tn_in, w_tail=w_gate, name="gdn_in_proj_" + tag)
        mixed.append(_gdn(qkvz, gates, alog_row, dt_row, conv_prev, a_conv_w[0], onorm_row, s0,
                          batch=nbatch, length=length, c=c_blk, hb=hb, nseq=nseq, nsub=nsub,
                          name="gdn_" + tag))
    (o_p, gdn_p, *hist_p), (o_s, gdn_s, *hist_s) = mixed
    tn_o = _pick_tile(d, COL_TILES)
    x = _matmul((o_p, o_s), a_w_out[0], res=(xp, xs), tm=tm2, tn=d, weights_outer=True,
                name="gdn_out_proj")
    tf = _pick_tile(w_up.shape[2], COL_TILES)
    x = _mlp(x, mlp_norm[0].reshape(1, d), w_up, w_down, 0, tm=tm, tf=tf, name="mlp0")

    cos_p, sin_p = _rope_tables(jnp.arange(lp))
    cos_s, sin_s = _rope_tables(PAST_LEN + jnp.arange(ls))
    rope = (jnp.concatenate([jnp.tile(cos_p, (bp, 1)), jnp.tile(cos_s, (bs, 1))], axis=0),
            jnp.concatenate([jnp.tile(sin_p, (bp, 1)), jnp.tile(sin_s, (bs, 1))], axis=0))
    kvw = N_KV * HD
    kv = _matmul(x, w_kv, gain=kv_norm.reshape(1, d), rope=rope, rope_cols=kvw,
                 tm=tm, tn=2 * kvw, name="kv_proj")
    q = _matmul(x, b_w_q[0], gain=attn_norm[1].reshape(1, d), rope=rope,
                rope_cols=N_Q * HD, out_dtype=BF16, tm=tm, tn=tn_o, name="q_proj")
    sinks = b_sinks[0].astype(F32)
    a_p = _attention(sinks, q, kv, kv, 0, 1, kv, row0=0, batch=bp, length=lp, tq=WINDOW, nb=1,
                     chunk_mask=True, prev_transposed=False, name="attn_prompt")
    nb = _pick_tile(bs, (ATTN_SAMPLE_BATCHES_PER_STEP, 2, 1))
    a_s = _attention(sinks, q, jnp.transpose(cache_k, (0, 2, 3, 1)), jnp.transpose(cache_v, (0, 2, 3, 1)),
                     0, 0, kv, row0=mp, batch=bs, length=ls, tq=ls, nb=nb,
                     chunk_mask=False, prev_transposed=True, name="attn_sample")
    x = _matmul((a_p, a_s), b_w_o[0], res=x, tm=tm2, tn=d, weights_outer=True, name="attn_out_proj")
    y_p, y_s = _mlp(x, mlp_norm[1].reshape(1, d), w_up, w_down, 1, final_gain=final_norm.reshape(1, d),
                    split_row=mp, tm=tm, tf=tf, name="mlp1")

    y_prompt = y_p.reshape(bp, lp, d)
    y_sample = y_s.reshape(bs, ls, d)
    conv_p = jnp.concatenate([t[:, SUBLANES - keep:] for t in hist_p], axis=-1)[None]
    conv_s = jnp.concatenate([t[:, SUBLANES - keep:] for t in hist_s], axis=-1)[None]
    kv_p = jnp.stack([kv[(b + 1) * lp - WINDOW:(b + 1) * lp] for b in range(bp)])
    kv_s = kv[mp:].reshape(bs, ls, 2 * kvw)
    k_p = kv_p[..., :kvw].reshape(bp, WINDOW, N_KV, HD)
    v_p = kv_p[..., kvw:].reshape(bp, WINDOW, N_KV, HD)
    k_s = kv_s[..., :kvw].reshape(bs, ls, N_KV, HD)
    v_s = kv_s[..., kvw:].reshape(bs, ls, N_KV, HD)
    return (y_prompt, y_sample, conv_p, gdn_p[None], k_p, v_p, conv_s, gdn_s[None], k_s, v_s)
```

```python
import functools
import math

import jax
import jax.numpy as jnp
from jax import lax
from jax.experimental import pallas as pl
from jax.experimental.pallas import tpu as pltpu

F32 = jnp.float32
BF16 = jnp.bfloat16

EPS = 1e-6
CHUNK = 64
WINDOW = 128
PAST_LEN = 4096
ROPE_THETA = 10000.0
HA, DK, DV = 16, 128, 128
N_Q, N_KV, HD = 32, 4, 64
CONV_W = 4
LANES = 128
SUBLANES = 8
VMEM_LIMIT = 56 * 1024 * 1024
BIG_VMEM_LIMIT = 60 * 1024 * 1024
GDN_HEADS_PER_STEP = 16
GDN_SAMPLE_SEQS_PER_STEP = 4
GDN_PROMPT_BLOCKS_PER_STEP = 4
ATTN_SAMPLE_BATCHES_PER_STEP = 2
ROW_TILES = (1088, 512, 256, 128, 64, 32, 16)
PROMPT_ROW_TILES = (1024,) + ROW_TILES[1:]
COL_TILES = (1024, 512, 256, 128)
IN_PROJ_COL_TILES = (2048,) + COL_TILES


def _pick_tile(n, candidates):
    for c in candidates:
        if n % c == 0:
            return c
    return n


def _dot(a, b, dims=(((1,), (0,)), ((), ()))):
    return lax.dot_general(a.astype(BF16), b.astype(BF16), dims, preferred_element_type=F32)


_NT = (((1,), (1,)), ((), ()))
_TN = (((0,), (0,)), ((), ()))


def _rms_scale(x):
    return lax.rsqrt(jnp.mean(x * x, axis=-1, keepdims=True) + EPS)


def _silu(x):
    h = 0.5 * x
    return h + h * jnp.tanh(h)


def _rope_slab(y, cos, sin_signed):
    lane = lax.broadcasted_iota(jnp.int32, y.shape, 1)
    half = HD // 2
    up = pltpu.roll(y, half, 1)
    down = pltpu.roll(y, LANES - half, 1)
    swapped = jnp.where((lane % HD) < half, down, up)
    return y * cos + swapped * sin_signed


def _project(lhs, w_ref, res_ref, cos_ref, sin_ref, o_ref, *, rope_slabs, n_slabs):
    if not rope_slabs:
        y = jnp.dot(lhs, w_ref[...].astype(BF16), preferred_element_type=F32)
        if res_ref is not None:
            y = y + res_ref[...]
        o_ref[...] = y.astype(o_ref.dtype)
        return
    cos = cos_ref[...]
    sin = sin_ref[...]
    n_chunk = 2 if n_slabs % 2 == 0 else 1
    for c0 in range(0, n_slabs, n_chunk):
        cols = slice(c0 * LANES, (c0 + n_chunk) * LANES)
        y = jnp.dot(lhs, w_ref[:, cols].astype(BF16), preferred_element_type=F32)
        if res_ref is not None:
            y = y + res_ref[:, cols]
        for s in range(n_chunk):
            ys = y[:, s * LANES:(s + 1) * LANES]
            if c0 + s < rope_slabs:
                ys = _rope_slab(ys, cos, sin)
            o_ref[:, (c0 + s) * LANES:(c0 + s + 1) * LANES] = ys.astype(o_ref.dtype)


def _matmul_kernel(*refs, n_x, n_res, split_tile, row_axis, has_norm, rope_slabs, n_slabs, resident_w,
                   has_tail):
    it = iter(refs)
    x_refs = [next(it) for _ in range(n_x)]
    g_ref = next(it) if has_norm else None
    w_ref = next(it)
    wt_ref = next(it) if has_tail else None
    res_refs = [next(it) for _ in range(n_res)]
    cos_ref = next(it) if rope_slabs else None
    sin_ref = next(it) if rope_slabs else None
    o_ref = next(it)
    ot_ref = next(it) if has_tail else None
    xn_ref = next(it) if has_norm else None
    wb_ref = next(it) if resident_w else None

    if resident_w:
        @pl.when(pl.program_id(row_axis) == 0)
        def _():
            wb_ref[...] = w_ref[...].astype(BF16)
        w_ref = wb_ref

    if has_norm:
        @pl.when(pl.program_id(1) == 0)
        def _():
            x = x_refs[0][...]
            xn_ref[...] = (x * _rms_scale(x) * g_ref[...]).astype(BF16)

    if has_tail:
        @pl.when(pl.program_id(1) == pl.num_programs(1) - 1)
        def _():
            ot_ref[...] = jnp.dot(xn_ref[...], wt_ref[...].astype(BF16), preferred_element_type=F32)

    def emit(lhs_ref, res_ref):
        _project(lhs_ref[...], w_ref, res_ref, cos_ref, sin_ref, o_ref, rope_slabs=rope_slabs, n_slabs=n_slabs)

    if n_x == 1:
        emit(xn_ref if has_norm else x_refs[0], res_refs[0] if n_res else None)
    else:
        i = pl.program_id(row_axis)
        pl.when(i < split_tile)(functools.partial(emit, x_refs[0], res_refs[0] if n_res else None))
        pl.when(i >= split_tile)(functools.partial(emit, x_refs[1], res_refs[-1] if n_res else None))


def _matmul(xs, w, *, gain=None, res=None, rope=None, rope_cols=0, out_dtype=F32, tm, tn, name,
            weights_outer=False, w_tail=None):
    xs = xs if isinstance(xs, (tuple, list)) else (xs,)
    ress = () if res is None else (res if isinstance(res, (tuple, list)) else (res,))
    k = xs[0].shape[1]
    m = sum(x.shape[0] for x in xs)
    n = (w.shape[1] // tn) * tn
    assert all(x.shape[0] % tm == 0 for x in xs) and w.shape[0] == k
    has_norm = gain is not None
    assert not (has_norm and len(xs) > 1)
    assert len(ress) <= len(xs) and all(r.shape[0] == x.shape[0] for r, x in zip(ress, xs) if len(ress) > 1)
    split_tile = xs[0].shape[0] // tm
    rope_slabs = 0
    if rope is not None:
        assert rope_cols == n or tn == n
        rope_slabs = min(rope_cols, tn) // LANES
    assert not (has_norm and weights_outer)
    resident_w = weights_outer and n == tn and w.dtype != BF16

    def spec(shape, f):
        return pl.BlockSpec(shape, (lambda a, b: f(b, a)) if weights_outer else f)

    def row_specs(arrays, width, col):
        if len(arrays) == 1:
            return [spec((tm, width), lambda i, j: (i, col(j)))]
        return [spec((tm, width), lambda i, j: (jnp.minimum(i, split_tile - 1), col(j))),
                spec((tm, width), lambda i, j: (jnp.maximum(i - split_tile, 0), col(j)))]

    in_specs = row_specs(xs, k, lambda j: 0)
    args = list(xs)
    if has_norm:
        in_specs.append(spec((1, k), lambda i, j: (0, 0)))
        args.append(gain)
    in_specs.append(pl.BlockSpec((k, tn), lambda a, b: (0, 0), pipeline_mode=pl.Buffered(1)) if resident_w
                    else spec((k, tn), lambda i, j: (0, j)))
    args.append(w)
    has_tail = w_tail is not None
    assert not has_tail or (has_norm and w_tail.shape[1] % LANES == 0)
    if has_tail:
        in_specs.append(spec(w_tail.shape, lambda i, j: (0, 0)))
        args.append(w_tail)
    in_specs += row_specs(ress, tn, lambda j: j) if ress else []
    args += list(ress)
    if rope_slabs:
        in_specs += [spec((tm, LANES), lambda i, j: (i, 0))] * 2
        args += list(rope)
    kern = functools.partial(_matmul_kernel, n_x=len(xs), n_res=len(ress), split_tile=split_tile,
                             row_axis=1 if weights_outer else 0, has_norm=has_norm,
                             rope_slabs=rope_slabs, n_slabs=tn // LANES, resident_w=resident_w,
                             has_tail=has_tail)
    grid = (n // tn, m // tm) if weights_outer else (m // tm, n // tn)
    out_specs = [spec((tm, tn), lambda i, j: (i, j))]
    out_shape = [jax.ShapeDtypeStruct((m, n), out_dtype)]
    if has_tail:
        out_specs.append(spec((tm, w_tail.shape[1]), lambda i, j: (i, 0)))
        out_shape.append(jax.ShapeDtypeStruct((m, w_tail.shape[1]), F32))
    outs = pl.pallas_call(
        kern,
        grid=grid,
        in_specs=in_specs,
        out_specs=out_specs,
        out_shape=out_shape,
        scratch_shapes=([pltpu.VMEM((tm, k), BF16)] if has_norm else [])
                       + ([pltpu.VMEM((k, tn), BF16)] if resident_w else []),
        compiler_params=pltpu.CompilerParams(
            dimension_semantics=("arbitrary" if resident_w else "parallel", "arbitrary"),
            vmem_limit_bytes=BIG_VMEM_LIMIT if resident_w or has_tail else VMEM_LIMIT),
        name=name,
    )(*args)
    return tuple(outs) if has_tail else outs[0]


def _mlp_kernel(*refs, tm, n_tiles, split_row, final_norm, prefetch_step):
    x_hbm, g_ref, wu_ref, wd_ref = refs[:4]
    fg_ref = refs[4] if final_norm else None
    *out_refs, acc_ref, hn_ref, sem_in, sem_out = refs[5 if final_norm else 4:]
    i = pl.program_id(0)
    f = pl.program_id(1)
    slot = i % 2

    def fetch(t):
        return pltpu.make_async_copy(x_hbm.at[pl.ds(t * tm, tm)], acc_ref.at[t % 2], sem_in.at[t % 2])

    def writebacks(t):
        r0, r1 = t * tm, (t + 1) * tm
        s = t % 2
        cps = []
        if r0 < split_row:
            n = min(r1, split_row) - r0
            cps.append(pltpu.make_async_copy(acc_ref.at[s, 0:n], out_refs[0].at[r0:r0 + n], sem_out.at[s, 0]))
        if r1 > split_row:
            a = max(r0, split_row)
            cps.append(pltpu.make_async_copy(acc_ref.at[s, a - r0:tm],
                                             out_refs[1].at[a - split_row:r1 - split_row], sem_out.at[s, 1]))
        return cps

    def start_tile(t):
        for cp in writebacks(t):
            cp.start()

    def wait_tile(t):
        for cp in writebacks(t):
            cp.wait()

    @pl.when(f == 0)
    def _():
        @pl.when(i == 0)
        def _():
            fetch(i).start()
        fetch(i).wait()
        x = acc_ref[slot]
        hn_ref[...] = (x * _rms_scale(x) * g_ref[...]).astype(BF16)

    u = jnp.dot(hn_ref[...], wu_ref[...].astype(BF16), preferred_element_type=F32)
    a = jnp.square(jnp.maximum(u, 0.0)).astype(BF16)
    acc_ref[slot] += jnp.dot(a, wd_ref[...].astype(BF16), preferred_element_type=F32)

    @pl.when((f == prefetch_step) & (i + 1 < n_tiles))
    def _():
        for t in range(n_tiles - 2):
            pl.when(i == t + 1)(functools.partial(wait_tile, t))
        fetch(i + 1).start()

    @pl.when(f == pl.num_programs(1) - 1)
    def _():
        if final_norm:
            y = acc_ref[slot]
            acc_ref[slot] = y * _rms_scale(y) * fg_ref[...]
        for t in range(n_tiles):
            pl.when(i == t)(functools.partial(start_tile, t))

        @pl.when(i == n_tiles - 1)
        def _():
            for t in range(max(n_tiles - 2, 0), n_tiles):
                wait_tile(t)


def _mlp(x, gain, w_up, w_down, layer, *, final_gain=None, split_row=None, tm, tf, name):
    m, d = x.shape
    dff = w_up.shape[2]
    n_tiles, nf = m // tm, dff // tf
    final_norm = final_gain is not None
    prefetch_step = 1
    assert m % tm == 0 and dff % tf == 0 and nf > prefetch_step + 1
    assert split_row is None or (split_row % SUBLANES == 0 and 0 < split_row < m)
    in_specs = [
        pl.BlockSpec(memory_space=pl.ANY),
        pl.BlockSpec((1, d), lambda i, f: (0, 0)),
        pl.BlockSpec((None, d, tf), lambda i, f: (layer, 0, f)),
        pl.BlockSpec((None, tf, d), lambda i, f: (layer, f, 0)),
    ]
    args = [x, gain, w_up, w_down]
    if final_norm:
        in_specs.append(pl.BlockSpec((1, d), lambda i, f: (0, 0)))
        args.append(final_gain)
    out_rows = [m] if split_row is None else [split_row, m - split_row]
    outs = pl.pallas_call(
        functools.partial(_mlp_kernel, tm=tm, n_tiles=n_tiles, split_row=out_rows[0], final_norm=final_norm,
                          prefetch_step=prefetch_step),
        grid=(n_tiles, nf),
        in_specs=in_specs,
        out_specs=[pl.BlockSpec(memory_space=pl.ANY)] * len(out_rows),
        out_shape=[jax.ShapeDtypeStruct((r, d), F32) for r in out_rows],
        scratch_shapes=[pltpu.VMEM((2, tm, d), F32), pltpu.VMEM((tm, d), BF16),
                        pltpu.SemaphoreType.DMA((2,)), pltpu.SemaphoreType.DMA((2, 2))],
        compiler_params=pltpu.CompilerParams(
            dimension_semantics=("arbitrary", "arbitrary"), vmem_limit_bytes=BIG_VMEM_LIMIT),
        name=name,
    )(*args)
    return outs[0] if split_row is None else outs


def _gdn_kernel(xq_ref, xk_ref, xv_ref, z_ref, gates_ref, alog_ref, dt_ref,
                pq_ref, pk_ref, pv_ref, wq_ref, wk_ref, wv_ref, onorm_ref, s0_ref,
                o_ref, s_ref, hq_ref, hk_ref, hv_ref, padq_ref, padk_ref, padv_ref, *, c, hb, nseq, nsub):
    n = pl.program_id(2)
    halo = SUBLANES
    seqs = range(nseq)
    rows_seq = nsub * c

    @pl.when(n == 0)
    def _():
        for pad_ref, prev_ref in ((padq_ref, pq_ref), (padk_ref, pk_ref), (padv_ref, pv_ref)):
            for sq in seqs:
                pad_ref[sq, 0:halo, :] = jnp.zeros((halo, pad_ref.shape[2]), F32)
                pad_ref[sq, halo - (CONV_W - 1):halo, :] = prev_ref[sq]
        s_ref[...] = s0_ref[...]

    def conv(x_ref, pad_ref, w_ref, hist_ref, sq):
        pad_ref[sq, halo:halo + rows_seq, :] = x_ref[sq * rows_seq:(sq + 1) * rows_seq, :]
        xp = pad_ref[sq]
        x1 = pltpu.roll(xp, 1, 0)
        a = xp * w_ref[3:4, :] + x1 * w_ref[2:3, :]
        b = xp * w_ref[1:2, :] + x1 * w_ref[0:1, :]
        acc = (a + pltpu.roll(b, 2, 0))[halo:, :]
        tail = xp[rows_seq:rows_seq + halo, :]
        pad_ref[sq, 0:halo, :] = tail
        hist_ref[sq] = tail
        return _silu(acc)

    cq = [conv(xq_ref, padq_ref, wq_ref, hq_ref, sq) for sq in seqs]
    ck = [conv(xk_ref, padk_ref, wk_ref, hk_ref, sq) for sq in seqs]
    cv = [conv(xv_ref, padv_ref, wv_ref, hv_ref, sq) for sq in seqs]

    row = lax.broadcasted_iota(jnp.int32, (c, c), 0)
    col = lax.broadcasted_iota(jnp.int32, (c, c), 1)
    incl = row >= col
    strict = row > col
    tril = jnp.where(incl, 1.0, 0.0).astype(F32)
    r128 = lax.broadcasted_iota(jnp.int32, (LANES, LANES), 0)
    c128 = lax.broadcasted_iota(jnp.int32, (LANES, LANES), 1)
    eye128 = jnp.where(r128 == c128, 1.0, 0.0).astype(F32)
    blocks = [(sq, sb) for sq in seqs for sb in range(nsub)]
    brow = {blk: slice(blk[0] * rows_seq + blk[1] * c, blk[0] * rows_seq + (blk[1] + 1) * c) for blk in blocks}
    beta_all, gc_all, gc_t = {}, {}, {}
    for blk in blocks:
        gates = gates_ref[brow[blk], :]
        beta_all[blk] = 0.5 + 0.5 * jnp.tanh(0.5 * gates)
        ga = gates + dt_ref[...]
        softplus = jnp.maximum(ga, 0.0) + jnp.log(1.0 + jnp.exp(-jnp.abs(ga)))
        g_all = -jnp.exp(alog_ref[...]) * softplus
        gc_all[blk] = jnp.dot(tril, g_all, preferred_element_type=F32,
                              precision=lax.Precision.HIGHEST)
        gc_t[blk] = lax.dot_general(eye128, gc_all[blk], _NT, preferred_element_type=F32,
                                    precision=lax.Precision.HIGHEST)

    units = [(blk, h) for blk in blocks for h in range(hb)]
    un = range(len(units))
    sls = [slice(h * LANES, (h + 1) * LANES) for _, h in units]
    srow = [slice(blk[1] * c, (blk[1] + 1) * c) for blk, _ in units]
    q = [cq[blk[0]][srow[u], sls[u]] for u, (blk, h) in enumerate(units)]
    k = [ck[blk[0]][srow[u], sls[u]] for u, (blk, h) in enumerate(units)]
    v = [cv[blk[0]][srow[u], sls[u]] for u, (blk, h) in enumerate(units)]
    q = [x * (lax.rsqrt(jnp.sum(x * x, axis=-1, keepdims=True) + EPS) * (DK ** -0.5)) for x in q]
    k = [x * lax.rsqrt(jnp.sum(x * x, axis=-1, keepdims=True) + EPS) for x in k]
    bcol = [beta_all[blk][:, h:h + 1] for blk, h in units]
    gcol = [gc_all[blk][:, hb + h:hb + h + 1] for blk, h in units]
    grow = [gc_t[blk][hb + h:hb + h + 1, :] for blk, h in units]
    glast = [gc_all[blk][c - 1:c, hb + h:hb + h + 1] for blk, h in units]
    decay = [jnp.exp(jnp.where(incl, gcol[u] - grow[u], -jnp.inf)) for u in un]
    kb = [k[u] * bcol[u] for u in un]
    eye = jnp.where(row == col, 1.0, 0.0).astype(F32)
    qkk = [_dot(jnp.concatenate([q[u], kb[u]], axis=0), k[u], _NT) for u in un]
    qk = [qkk[u][:c] * decay[u] for u in un]
    p = [-jnp.where(strict, qkk[u][c:] * decay[u], 0.0) for u in un]
    t = [eye + x for x in p]
    levels = max(int(math.ceil(math.log2(c))) - 1, 0)
    if levels:
        p = [_dot(x, x) for x in p]
    for _ in range(1, levels):
        r = [_dot(jnp.concatenate([p[u], t[u]], axis=0), p[u]) for u in un]
        t = [t[u] + r[u][c:] for u in un]
        p = [r[u][:c] for u in un]
    if levels:
        t = [t[u] + _dot(t[u], p[u]) for u in un]
    egc = [jnp.exp(x) for x in gcol]
    rhs = [jnp.concatenate([v[u] * bcol[u], kb[u] * egc[u]], axis=-1) for u in un]
    sol = [_dot(t[u], rhs[u]) for u in un]
    wqe = [jnp.concatenate([sol[u][:, DV:], q[u] * egc[u]], axis=0) for u in un]
    kd = [k[u] * jnp.exp(glast[u] - gcol[u]) for u in un]
    elast = [jnp.exp(x) for x in glast]

    onorm = onorm_ref[...]
    s = {(sq, h): s_ref[sq, h] for sq in seqs for h in range(hb)}
    for sb in range(nsub):
        cur = [u for u in un if units[u][0][1] == sb]
        key = {u: (units[u][0][0], units[u][1]) for u in cur}
        wq = {u: _dot(wqe[u], s[key[u]]) for u in cur}
        v_new = {u: sol[u][:, :DV] - wq[u][:c] for u in cur}
        o = {u: wq[u][c:] + _dot(qk[u], v_new[u]) for u in cur}
        for u in cur:
            s[key[u]] = s[key[u]] * elast[u] + _dot(kd[u], v_new[u], _TN)
        for u in cur:
            oh = o[u] * _rms_scale(o[u]) * onorm
            rows = brow[units[u][0]]
            o_ref[rows, sls[u]] = (oh * _silu(z_ref[rows, sls[u]])).astype(o_ref.dtype)
    for (sq, h), val in s.items():
        s_ref[sq, h] = val


def _gdn(qkvz, gates, alog_row, dt_row, conv_prev, conv_w, onorm_row, s0,
         *, batch, length, c, hb, nseq, nsub, name):
    nblk = length // (nsub * c)
    ng = HA // hb
    hw = hb * LANES
    rows_step = nseq * nsub * c
    assert length % (nsub * c) == 0 and c >= SUBLANES and CONV_W == 4
    assert batch % nseq == 0 and (nseq == 1 or nblk == 1)

    def rows(b, g, n):
        return b * nblk + n

    x_spec = lambda part: pl.BlockSpec((rows_step, hw), lambda b, g, n: (rows(b, g, n), part * ng + g))
    prev_spec = lambda part: pl.BlockSpec((nseq, CONV_W - 1, hw), lambda b, g, n: (b, 0, part * ng + g))
    w_spec = lambda part: pl.BlockSpec((CONV_W, hw), lambda b, g, n: (0, part * ng + g))
    gate_row_spec = pl.BlockSpec((1, LANES), lambda b, g, n: (0, g))
    hist_spec = pl.BlockSpec((nseq, SUBLANES, hw), lambda b, g, n: (b, 0, g))
    hist_shape = jax.ShapeDtypeStruct((batch, SUBLANES, HA * LANES), F32)
    state_spec = pl.BlockSpec((nseq, hb, DK, DV), lambda b, g, n: (b, g, 0, 0))
    in_specs = [
        x_spec(0), x_spec(1), x_spec(2), x_spec(3),
        pl.BlockSpec((rows_step, LANES), lambda b, g, n: (rows(b, g, n), g)),
        gate_row_spec, gate_row_spec,
        prev_spec(0), prev_spec(1), prev_spec(2),
        w_spec(0), w_spec(1), w_spec(2),
        pl.BlockSpec((1, LANES), lambda b, g, n: (0, 0)),
        state_spec,
    ]
    out_specs = [
        pl.BlockSpec((rows_step, hw), lambda b, g, n: (b * nblk + n, g)),
        state_spec,
        hist_spec, hist_spec, hist_spec,
    ]
    return pl.pallas_call(
        functools.partial(_gdn_kernel, c=c, hb=hb, nseq=nseq, nsub=nsub),
        grid=(batch // nseq, ng, nblk),
        in_specs=in_specs,
        out_specs=out_specs,
        out_shape=[jax.ShapeDtypeStruct((batch * length, HA * DV), BF16),
                   jax.ShapeDtypeStruct((batch, HA, DK, DV), F32),
                   hist_shape, hist_shape, hist_shape],
        scratch_shapes=[pltpu.VMEM((nseq, nsub * c + SUBLANES, hw), F32)] * 3,
        compiler_params=pltpu.CompilerParams(
            dimension_semantics=("parallel", "parallel", "arbitrary"),
            vmem_limit_bytes=VMEM_LIMIT),
        name=name,
    )(qkvz, qkvz, qkvz, qkvz, gates, alog_row, dt_row,
      conv_prev, conv_prev, conv_prev, conv_w, conv_w, conv_w, onorm_row, s0)


def _attn_kernel(sinks_ref, q_ref, kp_ref, vp_ref, kc_ref, vc_ref, o_ref, *, tq, nb, chunk_mask,
                 prev_transposed):
    i = pl.program_id(1)
    nk = WINDOW + tq
    lane_q = lax.broadcasted_iota(jnp.int32, (tq, LANES), 1)
    if chunk_mask:
        qc = lax.broadcasted_iota(jnp.int32, (tq, nk), 0) // CHUNK
        kc = lax.broadcasted_iota(jnp.int32, (tq, nk), 1) // CHUNK
        w_ch = WINDOW // CHUNK
        first_kc = jnp.where(i > 0, 0, w_ch)
        valid = (kc >= jnp.maximum(qc, first_kc)) & (kc <= qc + w_ch)
    scale = HD ** -0.5
    assert math.log2(HD) % 2 == 0
    per_slab = LANES // HD
    group = N_Q // N_KV
    slabs_per_kv = group // per_slab
    half_masks = [(lane_q >= r * HD) & (lane_q < (r + 1) * HD) for r in range(per_slab)]

    units = [(b, h) for b in range(nb) for h in range(N_KV)]
    kdup, vdup, qs = [], [], []
    for b, h in units:
        ksl = slice((h // per_slab) * LANES, (h // per_slab + 1) * LANES)

        def dup(prev_ref, cur_ref):
            if prev_transposed:
                xt = prev_ref[b, h]
                front = [jnp.concatenate([xt, xt], axis=0).T]
                slab = cur_ref[b * tq:(b + 1) * tq, ksl]
            else:
                front = []
                slab = jnp.concatenate([prev_ref[b * WINDOW:(b + 1) * WINDOW, ksl],
                                        cur_ref[b * tq:(b + 1) * tq, ksl]], axis=0)
            lane = lax.broadcasted_iota(jnp.int32, slab.shape, 1)
            first = (lane < HD) == (h % per_slab == 0)
            own = jnp.where(first, slab, pltpu.roll(slab, HD, 1))
            return jnp.concatenate(front + [own], axis=0).astype(BF16)

        kdup.append(dup(kp_ref, kc_ref))
        vdup.append(dup(vp_ref, vc_ref))
        pieces = []
        for s in range(slabs_per_kv):
            slab = h * slabs_per_kv + s
            q2 = q_ref[b * tq:(b + 1) * tq, slab * LANES:(slab + 1) * LANES]
            pieces += [jnp.where(mk, q2, jnp.zeros_like(q2)) for mk in half_masks]
        qs.append(jnp.concatenate(pieces, axis=0) * scale)
    sc = [lax.dot_general(qs[u], kdup[u], _NT, preferred_element_type=F32) for u in range(len(units))]
    ps = []
    for u, (b, h) in enumerate(units):
        blocks = []
        for r in range(group):
            sr = sc[u][r * tq:(r + 1) * tq]
            if chunk_mask:
                sr = jnp.where(valid, sr, -jnp.inf)
            sk = sinks_ref[h * group + r]
            mx = jnp.maximum(jnp.max(sr, axis=-1, keepdims=True), sk)
            e = jnp.exp(sr - mx)
            den = jnp.sum(e, axis=-1, keepdims=True) + jnp.exp(sk - mx)
            blocks.append((e / den).astype(BF16))
        ps.append(jnp.concatenate(blocks, axis=0))
    pv = [jnp.dot(ps[u], vdup[u], preferred_element_type=F32) for u in range(len(units))]
    for u, (b, h) in enumerate(units):
        for s in range(slabs_per_kv):
            slab = h * slabs_per_kv + s
            halves = [pv[u][(s * per_slab + r) * tq:(s * per_slab + r + 1) * tq] for r in range(per_slab)]
            out = halves[-1]
            for r in range(per_slab - 2, -1, -1):
                out = jnp.where(half_masks[r], halves[r], out)
            o_ref[b * tq:(b + 1) * tq, slab * LANES:(slab + 1) * LANES] = out.astype(o_ref.dtype)


def _attention(sinks, q, kprev, vprev, kprev_col, vprev_col, kv, *, row0, batch, length,
               tq, nb, chunk_mask, prev_transposed, name):
    nq = length // tq
    kvw = N_KV * HD
    assert row0 % (nb * tq) == 0 and length % tq == 0 and batch % nb == 0 and (nb == 1 or nq == 1)
    rb0 = row0 // (nb * tq)

    def cur(b, i):
        return rb0 + b * nq + i

    def prev(b, i):
        return b * nq + jnp.maximum(i - 1, 0)

    in_specs = [
        pl.BlockSpec(memory_space=pltpu.SMEM),
        pl.BlockSpec((nb * tq, N_Q * HD), lambda b, i: (cur(b, i), 0)),
        *([pl.BlockSpec((nb, N_KV, HD, WINDOW), lambda b, i: (b, 0, 0, 0))] * 2 if prev_transposed else
          [pl.BlockSpec((nb * WINDOW, kvw), lambda b, i: (prev(b, i), kprev_col)),
           pl.BlockSpec((nb * WINDOW, kvw), lambda b, i: (prev(b, i), vprev_col))]),
        pl.BlockSpec((nb * tq, kvw), lambda b, i: (cur(b, i), 0)),
        pl.BlockSpec((nb * tq, kvw), lambda b, i: (cur(b, i), 1)),
    ]
    return pl.pallas_call(
        functools.partial(_attn_kernel, tq=tq, nb=nb, chunk_mask=chunk_mask,
                          prev_transposed=prev_transposed),
        grid=(batch // nb, nq),
        in_specs=in_specs,
        out_specs=pl.BlockSpec((nb * tq, N_Q * HD), lambda b, i: (b * nq + i, 0)),
        out_shape=jax.ShapeDtypeStruct((batch * length, N_Q * HD), BF16),
        compiler_params=pltpu.CompilerParams(
            dimension_semantics=("parallel", "arbitrary"), vmem_limit_bytes=VMEM_LIMIT),
        name=name,
    )(sinks, q, kprev, vprev, kv, kv)


def _rope_tables(pos):
    half = HD // 2
    inv = 1.0 / (ROPE_THETA ** (jnp.arange(half, dtype=F32) / half))
    ang = pos.astype(F32)[:, None] * jnp.tile(inv, LANES // half)[None, :]
    sign = jnp.tile(jnp.concatenate([-jnp.ones((half,), F32), jnp.ones((half,), F32)]), LANES // HD)
    return jnp.cos(ang), jnp.sin(ang) * sign[None, :]


def _gate_layout(t, hb):
    lead = t.shape[:-1]
    ng = HA // hb
    if ng == 1:
        return jnp.pad(t, [(0, 0)] * len(lead) + [(0, LANES - 2 * HA)])
    beta = t[..., :HA].reshape(lead + (ng, hb))
    dec = t[..., HA:].reshape(lead + (ng, hb))
    pad = jnp.zeros(lead + (ng, LANES - 2 * hb), t.dtype)
    return jnp.concatenate([beta, dec, pad], axis=-1).reshape(lead + (ng * LANES,))


def kernel(x_prompt, x_sample, cache_conv, state_gdn, cache_k, cache_v, attn_norm, mlp_norm, final_norm, a_w_in, a_conv_w, a_log, a_dt_bias, a_o_norm, a_w_out, kv_norm, w_kv, b_w_q, b_sinks, b_w_o, w_up, w_down):
    bp, lp, d = x_prompt.shape
    bs, ls, _ = x_sample.shape
    mp, ms = bp * lp, bs * ls
    m = mp + ms
    hb = GDN_HEADS_PER_STEP
    dqk, dvw = HA * DK, HA * DV
    conv_ch = 2 * dqk + dvw
    keep = CONV_W - 1
    assert a_w_in.shape[0] == 1 and b_w_q.shape[0] == 1 and lp % WINDOW == 0 and ls >= keep

    xp = x_prompt.reshape(mp, d)
    xs = x_sample.reshape(ms, d)
    tm = _pick_tile(m, ROW_TILES)
    tm2 = _pick_tile(math.gcd(mp, ms), ROW_TILES[1:])
    tm_p = _pick_tile(mp, PROMPT_ROW_TILES)
    tm_s = _pick_tile(ms, ROW_TILES[1:])

    w_in = a_w_in[0].astype(BF16)
    w_gate = _gate_layout(w_in[:, conv_ch + dvw:], hb)
    g0 = attn_norm[0].reshape(1, d)
    tn_in = _pick_tile(conv_ch + dvw, IN_PROJ_COL_TILES)
    zeros_gate = jnp.zeros((HA,), F32)
    alog_row = _gate_layout(jnp.concatenate([zeros_gate, a_log[0]]), hb).reshape(1, -1)
    dt_row = _gate_layout(jnp.concatenate([zeros_gate, a_dt_bias[0]]), hb).reshape(1, -1)
    onorm_row = a_o_norm[0].reshape(1, DV)
    c_p, c_s = min(CHUNK, lp), min(CHUNK, ls)
    streams = (("prompt", xp, tm_p, bp, lp, c_p, 1, _pick_tile(lp // c_p, (GDN_PROMPT_BLOCKS_PER_STEP, 1)),
                jnp.zeros((bp, keep, conv_ch), F32), jnp.zeros((bp, HA, DK, DV), F32)),
               ("sample", xs, tm_s, bs, ls, c_s, _pick_tile(bs, (GDN_SAMPLE_SEQS_PER_STEP, 1)) if ls == c_s else 1,
                1, cache_conv[0], state_gdn[0]))
    mixed = []
    for tag, xr, tmr, nbatch, length, c_blk, nseq, nsub, conv_prev, s0 in streams:
        qkvz, gates = _matmul(xr, w_in, gain=g0, tm=tmr, tn=tn_in, w_tail=w_gate, name="gdn_in_proj_" + tag)
        mixed.append(_gdn(qkvz, gates, alog_row, dt_row, conv_prev, a_conv_w[0], onorm_row, s0,
                          batch=nbatch, length=length, c=c_blk, hb=hb, nseq=nseq, nsub=nsub,
                          name="gdn_" + tag))
    (o_p, gdn_p, *hist_p), (o_s, gdn_s, *hist_s) = mixed
    tn_o = _pick_tile(d, COL_TILES)
    x = _matmul((o_p, o_s), a_w_out[0], res=(xp, xs), tm=tm2, tn=d, weights_outer=True,
                name="gdn_out_proj")
    tf = _pick_tile(w_up.shape[2], COL_TILES)
    x = _mlp(x, mlp_norm[0].reshape(1, d), w_up, w_down, 0, tm=tm, tf=tf, name="mlp0")

    cos_p, sin_p = _rope_tables(jnp.arange(lp))
    cos_s, sin_s = _rope_tables(PAST_LEN + jnp.arange(ls))
    rope = (jnp.concatenate([jnp.tile(cos_p, (bp, 1)), jnp.tile(cos_s, (bs, 1))], axis=0),
            jnp.concatenate([jnp.tile(sin_p, (bp, 1)), jnp.tile(sin_s, (bs, 1))], axis=0))
    kvw = N_KV * HD
    kv = _matmul(x, w_kv, gain=kv_norm.reshape(1, d), rope=rope, rope_cols=kvw,
                 tm=tm, tn=2 * kvw, name="kv_proj")
    q = _matmul(x, b_w_q[0], gain=attn_norm[1].reshape(1, d), rope=rope,
                rope_cols=N_Q * HD, out_dtype=BF16, tm=tm, tn=tn_o, name="q_proj")
    sinks = b_sinks[0].astype(F32)
    a_p = _attention(sinks, q, kv, kv, 0, 1, kv, row0=0, batch=bp, length=lp, tq=WINDOW, nb=1,
                     chunk_mask=True, prev_transposed=False, name="attn_prompt")
    nb = _pick_tile(bs, (ATTN_SAMPLE_BATCHES_PER_STEP, 2, 1))
    a_s = _attention(sinks, q, jnp.transpose(cache_k, (0, 2, 3, 1)), jnp.transpose(cache_v, (0, 2, 3, 1)),
                     0, 0, kv, row0=mp, batch=bs, length=ls, tq=ls, nb=nb,
                     chunk_mask=False, prev_transposed=True, name="attn_sample")
    x = _matmul((a_p, a_s), b_w_o[0], res=x, tm=tm2, tn=d, weights_outer=True, name="attn_out_proj")
    y_p, y_s = _mlp(x, mlp_norm[1].reshape(1, d), w_up, w_down, 1, final_gain=final_norm.reshape(1, d),
                    split_row=mp, tm=tm, tf=tf, name="mlp1")

    y_prompt = y_p.reshape(bp, lp, d)
    y_sample = y_s.reshape(bs, ls, d)
    conv_p = jnp.concatenate([t[:, SUBLANES - keep:] for t in hist_p], axis=-1)[None]
    conv_s = jnp.concatenate([t[:, SUBLANES - keep:] for t in hist_s], axis=-1)[None]
    kv_p = jnp.stack([kv[(b + 1) * lp - WINDOW:(b + 1) * lp] for b in range(bp)])
    kv_s = kv[mp:].reshape(bs, ls, 2 * kvw)
    k_p = kv_p[..., :kvw].reshape(bp, WINDOW, N_KV, HD)
    v_p = kv_p[..., kvw:].reshape(bp, WINDOW, N_KV, HD)
    k_s = kv_s[..., :kvw].reshape(bs, ls, N_KV, HD)
    v_s = kv_s[..., kvw:].reshape(bs, ls, N_KV, HD)
    return (y_prompt, y_sample, conv_p, gdn_p[None], k_p, v_p, conv_s, gdn_s[None], k_s, v_s)
```
